```python
import math
import jax
import jax.numpy as jnp
from jax import lax
import numpy as np

D_MODEL = 1024
BATCH = 4
SEQ = 4096
DEPTH = 1
DEC_BATCH = 32
DEC_SEQ = 32
PAST_LEN = 2048

CHUNK = 64
N_MEM = 256
EPS = 1e-6
RET_HEADS = 4
RET_DK = D_MODEL // 8
RET_DV = D_MODEL // 8
RET_QK = RET_HEADS * RET_DK
RET_WIDTH = RET_HEADS * RET_DV
ROPE_BASE = 10000.0
SSM_WIDTH = D_MODEL // 2
SSM_GROUP = 16
SSM_GROUPS = SSM_WIDTH // SSM_GROUP
SSM_STATE = 64
X_HEADS = 4
X_HD = D_MODEL // 8
X_WIDTH = X_HEADS * X_HD
N_BRANCH = 3
N_EXPERTS = 32
TOP_K = 4
D_FF = D_MODEL
SWIGLU_ALPHA = 1.702
SWIGLU_LIMIT = 7.0
EXPERT_BLOCK = 128

kernel_name = "hybrid_retention_s5_moe_stream_step"


def rms_norm(x, w):
    xf = x.astype(jnp.float32)
    y = xf * lax.rsqrt(jnp.mean(xf * xf, axis=-1, keepdims=True) + EPS)
    return (y * w.astype(jnp.float32)).astype(x.dtype)


def rope(x, pos):
    d = x.shape[-1]
    half = d // 2
    inv = jnp.exp(-math.log(ROPE_BASE) * 2.0 * jnp.arange(half, dtype=jnp.float32) / d)
    ang = pos.astype(jnp.float32)[:, None] * inv[None, :]
    cos = jnp.cos(ang)[None, :, None, :]
    sin = jnp.sin(ang)[None, :, None, :]
    xf = x.astype(jnp.float32)
    x1, x2 = xf[..., :half], xf[..., half:]
    return jnp.concatenate([x1 * cos - x2 * sin, x1 * sin + x2 * cos], axis=-1)


def retention(q, k, v, s0):
    lc = q.shape[2]
    log_g = jnp.log(1.0 - jnp.exp2(-5.0 - jnp.arange(RET_HEADS, dtype=jnp.float32)))
    idx = jnp.arange(lc, dtype=jnp.float32)
    dist = jnp.abs(idx[:, None] - idx[None, :])
    decay_intra = jnp.exp(log_g[:, None, None] * dist)
    scores = jnp.einsum('bcihd,bcjhd->bchij', q, k) * decay_intra
    intra = jnp.einsum('bchij,bcjhe->bcihe', scores, v)
    k_w = jnp.exp(log_g[None, :] * (lc - 1.0 - idx)[:, None])
    kv = jnp.einsum('bcihd,bcihe->cbhde', k * k_w[:, :, None], v)
    chunk_decay = jnp.exp(log_g * lc)[None, :, None, None]

    def step(s, kv_c):
        return chunk_decay * s + kv_c, s

    s_last, s_before = lax.scan(step, s0, kv)
    q_w = jnp.exp(log_g[None, :] * (idx + 1.0)[:, None])
    cross = jnp.einsum('bcihd,cbhde->bcihe', q * q_w[:, :, None], s_before)
    return intra + cross, s_last


def complex_affine_combine(e1, e2):
    a1r, a1i, b1r, b1i = e1
    a2r, a2i, b2r, b2i = e2
    ar = a2r * a1r - a2i * a1i
    ai = a2r * a1i + a2i * a1r
    br = a2r * b1r - a2i * b1i + b2r
    bi = a2r * b1i + a2i * b1r + b2i
    return ar, ai, br, bi


def s5(u, h0_re, h0_im, lam_re, lam_im, log_dt, b_re, b_im, c_re, c_im, d_skip):
    f32 = jnp.float32
    bsz, L, _ = u.shape
    uf = u.astype(f32).reshape(bsz, L, SSM_GROUPS, SSM_GROUP)
    dt = jnp.exp(log_dt.astype(f32))[:, None]
    lr, li = lam_re.astype(f32), lam_im.astype(f32)
    mag = jnp.exp(lr * dt)
    a_re, a_im = mag * jnp.cos(li * dt), mag * jnp.sin(li * dt)
    den = lr * lr + li * li
    nr, ni = a_re - 1.0, a_im
    co_re = (nr * lr + ni * li) / den
    co_im = (ni * lr - nr * li) / den
    br, bi = b_re.astype(f32), b_im.astype(f32)
    bb_re = co_re[..., None] * br - co_im[..., None] * bi
    bb_im = co_re[..., None] * bi + co_im[..., None] * br
    bu_re = jnp.einsum('blgp,gnp->blgn', uf, bb_re)
    bu_im = jnp.einsum('blgp,gnp->blgn', uf, bb_im)
    h0r, h0i = h0_re.astype(f32), h0_im.astype(f32)
    bu_re = bu_re.at[:, 0].add(a_re * h0r - a_im * h0i)
    bu_im = bu_im.at[:, 0].add(a_re * h0i + a_im * h0r)
    ar = jnp.broadcast_to(a_re, bu_re.shape)
    ai = jnp.broadcast_to(a_im, bu_im.shape)
    _, _, hr, hi = lax.associative_scan(complex_affine_combine, (ar, ai, bu_re, bu_im), axis=1)
    y = (jnp.einsum('blgn,gpn->blgp', hr, c_re.astype(f32))
         - jnp.einsum('blgn,gpn->blgp', hi, c_im.astype(f32))
         + d_skip.astype(f32).reshape(SSM_GROUPS, SSM_GROUP) * uf)
    return y.reshape(bsz, L, SSM_WIDTH), hr[:, -1], hi[:, -1]


def mem_kv(mem, mem_norm, w_mem_kv):
    bsz, m, _ = mem.shape
    kv = rms_norm(mem, mem_norm) @ w_mem_kv
    k, v = jnp.split(kv, 2, axis=-1)
    return k.reshape(bsz, m, X_HEADS, X_HD), v.reshape(bsz, m, X_HEADS, X_HD)


def cross_attend(xq, mem_k, mem_v):
    s = jnp.einsum('blhd,bmhd->bhlm', xq, mem_k).astype(jnp.float32) * (X_HD ** -0.5)
    p = jax.nn.softmax(s, axis=-1).astype(mem_v.dtype)
    return jnp.einsum('bhlm,bmhd->blhd', p, mem_v)


def moe(x, w_router, b_router, w_gate_up, b_gate_up, w_down, b_down):
    bsz, L, dm = x.shape
    t = bsz * L
    xf = x.reshape(t, dm)
    logits = (xf @ w_router).astype(jnp.float32) + b_router.astype(jnp.float32)
    top_val, top_idx = lax.top_k(logits, TOP_K)
    gate_w = jax.nn.softmax(top_val, axis=-1)
    a = t * TOP_K
    flat_e = top_idx.reshape(a).astype(jnp.int32)
    order = jnp.argsort(flat_e).astype(jnp.int32)
    sorted_e = flat_e[order]
    counts = jnp.bincount(flat_e, length=N_EXPERTS).astype(jnp.int32)
    starts = jnp.cumsum(counts) - counts
    padded = ((counts + EXPERT_BLOCK - 1) // EXPERT_BLOCK) * EXPERT_BLOCK
    pad_ends = jnp.cumsum(padded)
    pad_starts = pad_ends - padded
    rank = jnp.arange(a, dtype=jnp.int32) - starts[sorted_e]
    dest_sorted = pad_starts[sorted_e] + rank
    n_blocks = -(-(a + N_EXPERTS * (EXPERT_BLOCK - 1)) // EXPERT_BLOCK)
    n_rows = n_blocks * EXPERT_BLOCK
    row_token = jnp.full((n_rows,), t, jnp.int32).at[dest_sorted].set(order // TOP_K)
    block_e = jnp.minimum(
        jnp.searchsorted(pad_ends, jnp.arange(n_blocks, dtype=jnp.int32) * EXPERT_BLOCK, side='right'),
        N_EXPERTS - 1).astype(jnp.int32)
    x_pad = jnp.concatenate([xf, jnp.zeros((1, dm), xf.dtype)], axis=0)
    x_blocks = x_pad[row_token].reshape(n_blocks, EXPERT_BLOCK, dm)

    def expert_block(args):
        xb, e = args
        h = xb @ w_gate_up[e] + b_gate_up[e]
        gate, up = jnp.split(h, 2, axis=-1)
        gate = jnp.minimum(gate, SWIGLU_LIMIT)
        up = jnp.clip(up, -SWIGLU_LIMIT, SWIGLU_LIMIT)
        act = (up + 1.0) * gate * jax.nn.sigmoid(SWIGLU_ALPHA * gate)
        return act @ w_down[e] + b_down[e]

    y_rows = lax.map(expert_block, (x_blocks, block_e)).reshape(n_rows, dm)
    dest = jnp.zeros((a,), jnp.int32).at[order].set(dest_sorted)
    y_assign = y_rows[dest].reshape(t, TOP_K, dm)
    out = jnp.einsum('tkd,tk->td', y_assign, gate_w.astype(y_assign.dtype))
    return out.reshape(bsz, L, dm)


def layer(x, pos, mem_k, mem_v, s_ret, h_re, h_im, lp):
    f32 = jnp.float32
    bsz, L, _ = x.shape
    xn = rms_norm(x, lp["norm_mix"])
    z = xn @ lp["w_in"]
    cuts = np.cumsum([RET_QK, RET_QK, RET_WIDTH, RET_WIDTH, SSM_WIDTH, X_WIDTH]).tolist()
    q, k, v, g, u, xq, gate_logits = jnp.split(z, cuts, axis=-1)
    lc = min(CHUNK, L)
    nc = L // lc
    qr = rope(q.reshape(bsz, L, RET_HEADS, RET_DK), pos) * (RET_DK ** -0.5)
    kr = rope(k.reshape(bsz, L, RET_HEADS, RET_DK), pos)
    vr = v.astype(f32).reshape(bsz, nc, lc, RET_HEADS, RET_DV)
    o, s_ret_new = retention(qr.reshape(bsz, nc, lc, RET_HEADS, RET_DK),
                             kr.reshape(bsz, nc, lc, RET_HEADS, RET_DK), vr, s_ret.astype(f32))
    o = o.reshape(bsz, L, RET_HEADS, RET_DV)
    mu = jnp.mean(o, axis=-1, keepdims=True)
    var = jnp.mean(jnp.square(o - mu), axis=-1, keepdims=True)
    o = (o - mu) * lax.rsqrt(var + EPS) * lp["ret_gn"].astype(f32).reshape(RET_HEADS, RET_DV)
    o = (o.reshape(bsz, L, RET_WIDTH) * jax.nn.silu(g.astype(f32))).astype(x.dtype)
    ret_branch = o @ lp["w_ret_o"]
    y, h_re_new, h_im_new = s5(u, h_re, h_im, lp["ssm_lam_re"], lp["ssm_lam_im"], lp["ssm_log_dt"],
                               lp["ssm_b_re"], lp["ssm_b_im"], lp["ssm_c_re"], lp["ssm_c_im"], lp["ssm_d"])
    zg = jax.nn.gelu(y).astype(x.dtype) @ lp["w_ssm_glu"]
    ga, gb = jnp.split(zg, 2, axis=-1)
    ssm_branch = (ga * jax.nn.sigmoid(gb)) @ lp["w_ssm_o"]
    xo = cross_attend(xq.reshape(bsz, L, X_HEADS, X_HD), mem_k, mem_v)
    x_branch = xo.reshape(bsz, L, X_WIDTH) @ lp["w_x_o"]
    g_ret, g_ssm, g_x = jnp.split(jax.nn.sigmoid(gate_logits), N_BRANCH, axis=-1)
    merged = g_ret * ret_branch + g_ssm * ssm_branch + g_x * x_branch
    h = x + merged @ lp["w_out"]
    h = h + moe(rms_norm(h, lp["norm_ffn"]), lp["w_router"], lp["b_router"], lp["w_gate_up"],
                lp["b_gate_up"], lp["w_down"], lp["b_down"])
    return h, s_ret_new, h_re_new, h_im_new


def setup_inputs(seed: int = 0) -> dict:
    key = jax.random.key(seed)
    ks = iter(jax.random.split(key, 48))
    f32 = jnp.float32

    def nrm(shape, scale):
        return scale * jax.random.normal(next(ks), shape, f32)

    G, N, P = SSM_GROUPS, SSM_STATE, SSM_GROUP
    in_cols = 2 * RET_QK + 2 * RET_WIDTH + SSM_WIDTH + X_WIDTH + N_BRANCH * D_MODEL
    d = D_MODEL
    return {
        "x_prompt": nrm((BATCH, SEQ, d), 1.0),
        "x_sample": nrm((DEC_BATCH, DEC_SEQ, d), 1.0),
        "cache_mem_k": nrm((DEPTH, DEC_BATCH, N_MEM, X_HEADS, X_HD), 1.0),
        "cache_mem_v": nrm((DEPTH, DEC_BATCH, N_MEM, X_HEADS, X_HD), 1.0),
        "state_ret": nrm((DEPTH, DEC_BATCH, RET_HEADS, RET_DK, RET_DV), 2.0),
        "state_ssm_re": nrm((DEPTH, DEC_BATCH, G, N), 1.0),
        "state_ssm_im": nrm((DEPTH, DEC_BATCH, G, N), 1.0),
        "mem_prompt": nrm((BATCH, N_MEM, d), 1.0),
        "norm_mix": 1.0 + nrm((DEPTH, d), 0.02),
        "w_in": nrm((DEPTH, d, in_cols), d ** -0.5),
        "ret_gn": 1.0 + nrm((DEPTH, RET_WIDTH), 0.02),
        "w_ret_o": nrm((DEPTH, RET_WIDTH, d), RET_WIDTH ** -0.5),
        "ssm_lam_re": -0.5 + nrm((DEPTH, G, N), 0.01),
        "ssm_lam_im": math.pi * jnp.arange(N, dtype=f32) + nrm((DEPTH, G, N), 0.01),
        "ssm_log_dt": jax.random.uniform(next(ks), (DEPTH, G), f32, math.log(1e-3), math.log(1e-1)),
        "ssm_b_re": nrm((DEPTH, G, N, P), (2.0 * P) ** -0.5),
        "ssm_b_im": nrm((DEPTH, G, N, P), (2.0 * P) ** -0.5),
        "ssm_c_re": nrm((DEPTH, G, P, N), (2.0 * N) ** -0.5),
        "ssm_c_im": nrm((DEPTH, G, P, N), (2.0 * N) ** -0.5),
        "ssm_d": nrm((DEPTH, SSM_WIDTH), 1.0),
        "w_ssm_glu": nrm((DEPTH, SSM_WIDTH, 2 * SSM_WIDTH), SSM_WIDTH ** -0.5),
        "w_ssm_o": nrm((DEPTH, SSM_WIDTH, d), SSM_WIDTH ** -0.5),
        "mem_norm": 1.0 + nrm((DEPTH, d), 0.02),
        "w_mem_kv": nrm((DEPTH, d, 2 * X_WIDTH), d ** -0.5),
        "w_x_o": nrm((DEPTH, X_WIDTH, d), X_WIDTH ** -0.5),
        "w_out": nrm((DEPTH, d, d), d ** -0.5),
        "norm_ffn": 1.0 + nrm((DEPTH, d), 0.02),
        "w_router": nrm((DEPTH, d, N_EXPERTS), d ** -0.5),
        "b_router": nrm((DEPTH, N_EXPERTS), 0.01),
        "w_gate_up": nrm((DEPTH, N_EXPERTS, d, 2 * D_FF), d ** -0.5),
        "b_gate_up": nrm((DEPTH, N_EXPERTS, 2 * D_FF), 0.01),
        "w_down": nrm((DEPTH, N_EXPERTS, D_FF, d), D_FF ** -0.5),
        "b_down": nrm((DEPTH, N_EXPERTS, d), 0.01),
        "final_norm": 1.0 + nrm((d,), 0.02),
    }


def reference(x_prompt, x_sample, cache_mem_k, cache_mem_v, state_ret, state_ssm_re, state_ssm_im,
              mem_prompt, norm_mix, w_in, ret_gn, w_ret_o, ssm_lam_re, ssm_lam_im, ssm_log_dt,
              ssm_b_re, ssm_b_im, ssm_c_re, ssm_c_im, ssm_d, w_ssm_glu, w_ssm_o, mem_norm, w_mem_kv,
              w_x_o, w_out, norm_ffn, w_router, b_router, w_gate_up, b_gate_up, w_down, b_down,
              final_norm):
    bp = x_prompt.shape[0]
    pos_p = jnp.arange(x_prompt.shape[1], dtype=jnp.int32)
    pos_s = PAST_LEN + jnp.arange(x_sample.shape[1], dtype=jnp.int32)
    zero_ret = jnp.zeros((bp, RET_HEADS, RET_DK, RET_DV), jnp.float32)
    zero_ssm = jnp.zeros((bp, SSM_GROUPS, SSM_STATE), jnp.float32)
    hp, hs = x_prompt, x_sample
    ret_p, sre_p, sim_p, mk_p, mv_p = [], [], [], [], []
    ret_s, sre_s, sim_s = [], [], []
    for l in range(DEPTH):
        lp = {
            "norm_mix": norm_mix[l], "w_in": w_in[l], "ret_gn": ret_gn[l], "w_ret_o": w_ret_o[l],
            "ssm_lam_re": ssm_lam_re[l], "ssm_lam_im": ssm_lam_im[l], "ssm_log_dt": ssm_log_dt[l],
            "ssm_b_re": ssm_b_re[l], "ssm_b_im": ssm_b_im[l], "ssm_c_re": ssm_c_re[l],
            "ssm_c_im": ssm_c_im[l], "ssm_d": ssm_d[l], "w_ssm_glu": w_ssm_glu[l],
            "w_ssm_o": w_ssm_o[l], "w_x_o": w_x_o[l], "w_out": w_out[l], "norm_ffn": norm_ffn[l],
            "w_router": w_router[l], "b_router": b_router[l], "w_gate_up": w_gate_up[l],
            "b_gate_up": b_gate_up[l], "w_down": w_down[l], "b_down": b_down[l],
        }
        mk, mv = mem_kv(mem_prompt, mem_norm[l], w_mem_kv[l])
        hp, sr, shr, shi = layer(hp, pos_p, mk, mv, zero_ret, zero_ssm, zero_ssm, lp)
        ret_p.append(sr)
        sre_p.append(shr)
        sim_p.append(shi)
        mk_p.append(mk)
        mv_p.append(mv)
        hs, sr2, shr2, shi2 = layer(hs, pos_s, cache_mem_k[l], cache_mem_v[l], state_ret[l],
                                    state_ssm_re[l], state_ssm_im[l], lp)
        ret_s.append(sr2)
        sre_s.append(shr2)
        sim_s.append(shi2)
    y_prompt = rms_norm(hp, final_norm)
    y_sample = rms_norm(hs, final_norm)
    return (y_prompt, y_sample, jnp.stack(ret_p), jnp.stack(sre_p), jnp.stack(sim_p),
            jnp.stack(mk_p), jnp.stack(mv_p), jnp.stack(ret_s), jnp.stack(sre_s), jnp.stack(sim_s))
```

```python
import functools
import math

import jax
import jax.numpy as jnp
import numpy as np
from jax import lax
from jax.experimental import pallas as pl
from jax.experimental.pallas import tpu as pltpu

F32 = jnp.float32
BF16 = jnp.bfloat16

EPS = 1e-6
CHUNK = 64
PAST_LEN = 2048
ROPE_BASE = 10000.0
RET_HEADS = 4
X_HEADS = 4
HEAD_DIM = 128
SSM_GROUP = 16
SSM_STATE = 64
TOP_K = 4
SWIGLU_ALPHA = 1.702
SWIGLU_LIMIT = 7.0

VMEM_LIMIT = 52 * 1024 * 1024
S5_CHUNK = 16
ROW_TILE = 256
FFN_BLOCK = 256
NT_DIMS = (((1,), (1,)), ((), ()))
TN_DIMS = (((0,), (0,)), ((), ()))


def _params(n_axes=1):
    return pltpu.CompilerParams(dimension_semantics=("arbitrary",) * n_axes,
                                vmem_limit_bytes=VMEM_LIMIT)


def _resident(shape):
    nd = len(shape)
    return pl.BlockSpec(shape, lambda *_: (0,) * nd, pipeline_mode=pl.Buffered(1))


def _rms(x, w):
    return x * lax.rsqrt(jnp.mean(x * x, axis=-1, keepdims=True) + EPS) * w


def _norm_matmul_kernel(x_ref, nw_ref, w_ref, o_ref, *, n_chunk):
    xb = _rms(x_ref[...], nw_ref[...]).astype(BF16)
    for n0 in range(0, o_ref.shape[1], n_chunk):
        o_ref[:, n0:n0 + n_chunk] = jnp.dot(
            xb, w_ref[:, n0:n0 + n_chunk], preferred_element_type=F32).astype(o_ref.dtype)


def norm_matmul(x, nw, w, out_dtype):
    t, d = x.shape
    n = w.shape[1]
    return pl.pallas_call(
        functools.partial(_norm_matmul_kernel, n_chunk=min(n, 1024)),
        grid=(t // ROW_TILE,),
        in_specs=[pl.BlockSpec((ROW_TILE, d), lambda i: (i, 0)), _resident((1, d)), _resident((d, n))],
        out_specs=pl.BlockSpec((ROW_TILE, n), lambda i: (i, 0)),
        out_shape=jax.ShapeDtypeStruct((t, n), out_dtype),
        compiler_params=_params(), name="norm_matmul",
    )(x, nw.reshape(1, d), w)


def s5_tables(lam_re, lam_im, log_dt, b_re, b_im, c_re, c_im, d_skip):
    g, n, p = b_re.shape
    s = S5_CHUNK
    hi = lax.Precision.HIGHEST
    dt = jnp.exp(log_dt)[:, None]
    a_re = jnp.exp(lam_re * dt) * jnp.cos(lam_im * dt)
    a_im = jnp.exp(lam_re * dt) * jnp.sin(lam_im * dt)
    den = lam_re * lam_re + lam_im * lam_im
    nr, ni = a_re - 1.0, a_im
    co_re = (nr * lam_re + ni * lam_im) / den
    co_im = (ni * lam_re - nr * lam_im) / den
    bb_re = co_re[..., None] * b_re - co_im[..., None] * b_im
    bb_im = co_re[..., None] * b_im + co_im[..., None] * b_re
    tau = jnp.arange(s + 1, dtype=F32)[:, None, None]
    pw_mag = jnp.exp(lam_re * dt * tau)
    pw_re = pw_mag * jnp.cos(lam_im * dt * tau)
    pw_im = pw_mag * jnp.sin(lam_im * dt * tau)
    ca_re = c_re[None] * pw_re[:, :, None, :] - c_im[None] * pw_im[:, :, None, :]
    ca_im = c_re[None] * pw_im[:, :, None, :] + c_im[None] * pw_re[:, :, None, :]
    kq = (jnp.einsum('tgpn,gnq->gtqp', ca_re[:s], bb_re, precision=hi)
          - jnp.einsum('tgpn,gnq->gtqp', ca_im[:s], bb_im, precision=hi))
    ts = np.arange(s)
    lag = ts[None, :] - ts[:, None]
    m5 = kq[:, np.clip(lag, 0, s - 1)] * jnp.asarray(lag >= 0, F32)[None, :, :, None, None]
    m = m5.transpose(0, 1, 3, 2, 4).reshape(g, s * p, s * p)
    rev = s - 1 - ts
    w_re = pw_re[rev][:, :, :, None] * bb_re[None] - pw_im[rev][:, :, :, None] * bb_im[None]
    w_im = pw_re[rev][:, :, :, None] * bb_im[None] + pw_im[rev][:, :, :, None] * bb_re[None]
    w_re = w_re.transpose(1, 0, 3, 2).reshape(g, s * p, n)
    w_im = w_im.transpose(1, 0, 3, 2).reshape(g, s * p, n)
    v_re = ca_re[1:].transpose(1, 3, 0, 2).reshape(g, n, s * p)
    v_im = -ca_im[1:].transpose(1, 3, 0, 2).reshape(g, n, s * p)

    z_m = jnp.zeros((g // 2, s * p, s * p), F32)
    z_w = jnp.zeros((g // 2, s * p, n), F32)
    z_v = jnp.zeros((g // 2, n, s * p), F32)
    m0, m1 = m[0::2], m[1::2]
    mw = jnp.concatenate([
        jnp.concatenate([m0, z_m, w_re[0::2], z_w, w_im[0::2], z_w], axis=2),
        jnp.concatenate([z_m, m1, z_w, w_re[1::2], z_w, w_im[1::2]], axis=2)], axis=1)
    v = jnp.concatenate([
        jnp.concatenate([v_re[0::2], z_v], axis=2), jnp.concatenate([z_v, v_re[1::2]], axis=2),
        jnp.concatenate([v_im[0::2], z_v], axis=2), jnp.concatenate([z_v, v_im[1::2]], axis=2)], axis=1)
    a_s_re = pw_re[s].reshape(1, g * n)
    a_s_im = pw_im[s].reshape(1, g * n)
    dtab = jnp.broadcast_to(d_skip.reshape(g // 2, 2, 1, p), (g // 2, 2, s, p)).reshape(g // 2, 1, 2 * s * p)
    return mw.astype(BF16), v.astype(BF16), a_s_re, a_s_im, dtab


def _s5a_kernel(u_ref, mw_ref, yi_ref, ire_ref, iim_ref):
    r = jnp.dot(u_ref[0], mw_ref[0], preferred_element_type=F32)
    w = yi_ref.shape[2]
    yi_ref[0] = r[:, :w]
    ire_ref[...] = r[:, w:w + 128]
    iim_ref[...] = r[:, w + 128:w + 256]


def _s5scan_kernel(ire_ref, iim_ref, ar_ref, ai_ref, h0r_ref, h0i_ref,
                   hpr_ref, hpi_ref, hfr_ref, hfi_ref):
    nb, nc, _ = ire_ref.shape
    ar, ai = ar_ref[...], ai_ref[...]

    def body(c, carry):
        out = []
        for b in range(nb):
            hr, hi = carry[2 * b], carry[2 * b + 1]
            hpr_ref[b, pl.ds(c, 1), :] = hr
            hpi_ref[b, pl.ds(c, 1), :] = hi
            out.append(ar * hr - ai * hi + ire_ref[b, pl.ds(c, 1), :])
            out.append(ar * hi + ai * hr + iim_ref[b, pl.ds(c, 1), :])
        return tuple(out)

    init = []
    for b in range(nb):
        init += [h0r_ref[b], h0i_ref[b]]
    fin = lax.fori_loop(0, nc, body, tuple(init))
    for b in range(nb):
        hfr_ref[b] = fin[2 * b]
        hfi_ref[b] = fin[2 * b + 1]


def _s5b_kernel(hpr_ref, hpi_ref, v_ref, yi_ref, u_ref, d_ref, y_ref):
    y = yi_ref[0] + d_ref[0] * u_ref[0].astype(F32)
    y += jnp.dot(hpr_ref[...].astype(BF16), v_ref[0, :128, :], preferred_element_type=F32)
    y += jnp.dot(hpi_ref[...].astype(BF16), v_ref[0, 128:, :], preferred_element_type=F32)
    y_ref[0] = y


def s5_apply(u, h0_re, h0_im, tables):
    mw, v, a_re, a_im, dtab = tables
    bsz, length, width = u.shape
    gp = mw.shape[0]
    s = S5_CHUNK
    nc = length // s
    rows = bsz * nc
    lanes = a_re.shape[1]
    pw = mw.shape[1]
    up = u.reshape(bsz, nc, s, gp, 2, SSM_GROUP).transpose(3, 0, 1, 4, 2, 5).reshape(gp, rows, pw)
    yi, inj_re, inj_im = pl.pallas_call(
        _s5a_kernel, grid=(gp,),
        in_specs=[pl.BlockSpec((1, rows, pw), lambda g: (g, 0, 0)),
                  pl.BlockSpec((1, pw, pw + 256), lambda g: (g, 0, 0))],
        out_specs=[pl.BlockSpec((1, rows, pw), lambda g: (g, 0, 0)),
                   pl.BlockSpec((rows, 128), lambda g: (0, g)),
                   pl.BlockSpec((rows, 128), lambda g: (0, g))],
        out_shape=[jax.ShapeDtypeStruct((gp, rows, pw), F32),
                   jax.ShapeDtypeStruct((rows, lanes), F32),
                   jax.ShapeDtypeStruct((rows, lanes), F32)],
        compiler_params=_params(), name="s5_chunk_in",
    )(up, mw)

    sb, lw = 4, 512
    seq_spec = pl.BlockSpec((sb, nc, lw), lambda b, l: (b, 0, l))
    vec_spec = pl.BlockSpec((sb, 1, lw), lambda b, l: (b, 0, l))
    tab_spec = pl.BlockSpec((1, lw), lambda b, l: (0, l))
    hp_re, hp_im, hf_re, hf_im = pl.pallas_call(
        _s5scan_kernel, grid=(bsz // sb, lanes // lw),
        in_specs=[seq_spec, seq_spec, tab_spec, tab_spec, vec_spec, vec_spec],
        out_specs=[seq_spec, seq_spec, vec_spec, vec_spec],
        out_shape=[jax.ShapeDtypeStruct((bsz, nc, lanes), F32)] * 2
        + [jax.ShapeDtypeStruct((bsz, 1, lanes), F32)] * 2,
        compiler_params=_params(2), name="s5_scan",
    )(inj_re.reshape(bsz, nc, lanes), inj_im.reshape(bsz, nc, lanes), a_re, a_im,
      h0_re.reshape(bsz, 1, lanes), h0_im.reshape(bsz, 1, lanes))

    yp = pl.pallas_call(
        _s5b_kernel, grid=(gp,),
        in_specs=[pl.BlockSpec((rows, 128), lambda g: (0, g)),
                  pl.BlockSpec((rows, 128), lambda g: (0, g)),
                  pl.BlockSpec((1, 256, pw), lambda g: (g, 0, 0)),
                  pl.BlockSpec((1, rows, pw), lambda g: (g, 0, 0)),
                  pl.BlockSpec((1, rows, pw), lambda g: (g, 0, 0)),
                  pl.BlockSpec((1, 1, pw), lambda g: (g, 0, 0))],
        out_specs=pl.BlockSpec((1, rows, pw), lambda g: (g, 0, 0)),
        out_shape=jax.ShapeDtypeStruct((gp, rows, pw), F32),
        compiler_params=_params(), name="s5_chunk_out",
    )(hp_re.reshape(rows, lanes), hp_im.reshape(rows, lanes), v, yi, up, dtab)
    y = yp.reshape(gp, bsz, nc, 2, s, SSM_GROUP).transpose(1, 2, 4, 0, 3, 5).reshape(bsz, length, width)
    g = 2 * gp
    return y, hf_re.reshape(bsz, g, SSM_STATE), hf_im.reshape(bsz, g, SSM_STATE)


def _retention_gammas():
    return 1.0 - np.exp2(-5.0 - np.arange(RET_HEADS, dtype=np.float64))


def retention_tables(tile, chunk):
    gam = _retention_gammas()[:, None, None]
    i = np.arange(tile)[:, None]
    j = np.arange(tile)[None, :]
    same = (i // chunk) == (j // chunk)
    earlier = (j // chunk) < (i // chunk)
    dist = np.where(same, np.abs(i - j), np.where(earlier, i - j, 0))
    dmask = np.where(same | earlier, gam ** dist[None], 0.0)
    qw = np.broadcast_to((gam[:, :, 0] ** (np.arange(tile) + 1.0))[:, :, None], (RET_HEADS, tile, HEAD_DIM))
    kw = np.broadcast_to((gam[:, :, 0] ** (tile - 1.0 - np.arange(tile)))[:, :, None], (RET_HEADS, tile, HEAD_DIM))
    return (jnp.asarray(dmask, F32), jnp.asarray(qw, F32), jnp.asarray(kw, F32),
            tuple(float(x) for x in _retention_gammas() ** tile))


def rope_tables(pos):
    half = HEAD_DIM // 2
    inv = jnp.exp(-math.log(ROPE_BASE) * 2.0 * jnp.arange(half, dtype=F32) / HEAD_DIM)
    ang = pos.astype(F32)[:, None] * inv[None, :]
    cos, sin = jnp.cos(ang), jnp.sin(ang)
    cosf = jnp.concatenate([cos, cos], axis=1)
    sinf = jnp.concatenate([-sin, sin], axis=1)
    return cosf, sinf


def _mixer_kernel(x_ref, zq_ref, xq_ref, gl_ref, y_ref, cq_ref, sq_ref, ck_ref, sk_ref,
                  dm_ref, qw_ref, kw_ref, mk_ref, mv_ref, s0_ref, gn_ref,
                  wro_ref, wglu_ref, wso_ref, wxo_ref, wout_ref,
                  h_ref, sout_ref, s_scr, o_scr, xo_scr, *, nb, tl, tile_decay):
    lt = pl.program_id(1)
    hd = HEAD_DIM
    qk = RET_HEADS * hd

    @pl.when(lt == 0)
    def _():
        s_scr[...] = s0_ref[...]

    cq, sq, ck, sk = cq_ref[...], sq_ref[...], ck_ref[...], sk_ref[...]
    for n in range(nb):
        rows = slice(n * tl, (n + 1) * tl)
        for h in range(RET_HEADS):
            c0 = h * hd
            q = zq_ref[rows, c0:c0 + hd].astype(F32)
            k = zq_ref[rows, qk + c0:qk + c0 + hd].astype(F32)
            v = zq_ref[rows, 2 * qk + c0:2 * qk + c0 + hd]
            g = zq_ref[rows, 3 * qk + c0:3 * qk + c0 + hd].astype(F32)
            qr = q * cq + pltpu.roll(q, hd // 2, 1) * sq
            kr = k * ck + pltpu.roll(k, hd // 2, 1) * sk
            sc = lax.dot_general(qr.astype(BF16), kr.astype(BF16), NT_DIMS,
                                 preferred_element_type=F32) * dm_ref[h]
            o = jnp.dot(sc.astype(BF16), v, preferred_element_type=F32)
            s_old = s_scr[n, h]
            o += jnp.dot((qr * qw_ref[h]).astype(BF16), s_old.astype(BF16), preferred_element_type=F32)
            kv = lax.dot_general((kr * kw_ref[h]).astype(BF16), v, TN_DIMS, preferred_element_type=F32)
            s_scr[n, h] = tile_decay[h] * s_old + kv
            d = o - jnp.mean(o, axis=-1, keepdims=True)
            on = d * lax.rsqrt(jnp.mean(d * d, axis=-1, keepdims=True) + EPS) * gn_ref[:, c0:c0 + hd]
            o_scr[rows, c0:c0 + hd] = (on * (g * jax.nn.sigmoid(g))).astype(BF16)
            mkh = mk_ref[n, :, c0:c0 + hd].astype(BF16)
            mvh = mv_ref[n, :, c0:c0 + hd].astype(BF16)
            s = lax.dot_general(xq_ref[rows, c0:c0 + hd], mkh, NT_DIMS,
                                preferred_element_type=F32) * (hd ** -0.5)
            e = jnp.exp(s - jnp.max(s, axis=-1, keepdims=True))
            p = e / jnp.sum(e, axis=-1, keepdims=True)
            xo_scr[rows, c0:c0 + hd] = jnp.dot(p.astype(BF16), mvh, preferred_element_type=F32).astype(BF16)

    ret = jnp.dot(o_scr[...], wro_ref[...], preferred_element_type=F32)
    zg = jnp.dot(jax.nn.gelu(y_ref[...]).astype(BF16), wglu_ref[...], preferred_element_type=F32)
    half = zg.shape[1] // 2
    glu = (zg[:, :half] * jax.nn.sigmoid(zg[:, half:])).astype(BF16)
    ssm = jnp.dot(glu, wso_ref[...], preferred_element_type=F32)
    xb = jnp.dot(xo_scr[...], wxo_ref[...], preferred_element_type=F32)
    dm = ret.shape[1]
    merged = (jax.nn.sigmoid(gl_ref[:, :dm].astype(F32)) * ret
              + jax.nn.sigmoid(gl_ref[:, dm:2 * dm].astype(F32)) * ssm
              + jax.nn.sigmoid(gl_ref[:, 2 * dm:].astype(F32)) * xb)
    h_ref[...] = x_ref[...] + jnp.dot(merged.astype(BF16), wout_ref[...], preferred_element_type=F32)

    @pl.when(lt == pl.num_programs(1) - 1)
    def _():
        sout_ref[...] = s_scr[...]


def mixer(x, z, y_ssm, pos, mem_k, mem_v, s0, ret_gn, w_ret_o, w_ssm_glu, w_ssm_o, w_x_o, w_out, *, nb, tl):
    bsz, length, dm = x.shape
    chunk = min(CHUNK, length)
    nl = length // tl
    rows = nb * tl
    qk = RET_HEADS * HEAD_DIM
    sw = y_ssm.shape[2]
    xw = X_HEADS * HEAD_DIM
    gate_col = (4 * qk + sw + xw)
    assert gate_col % (3 * dm) == 0 and (4 * qk + sw) % xw == 0
    dmask, qw, kw, tile_decay = retention_tables(tl, chunk)
    cosf, sinf = rope_tables(pos)
    scale = HEAD_DIM ** -0.5
    row_map = lambda b, l: (b * nl + l, 0)
    tab_map = lambda b, l: (l, 0)
    st_spec = pl.BlockSpec((nb, RET_HEADS, HEAD_DIM, HEAD_DIM), lambda b, l: (b, 0, 0, 0))
    mem_spec = pl.BlockSpec((nb, mem_k.shape[1], xw), lambda b, l: (b, 0, 0))
    h, s_out = pl.pallas_call(
        functools.partial(_mixer_kernel, nb=nb, tl=tl, tile_decay=tile_decay),
        grid=(bsz // nb, nl),
        in_specs=[pl.BlockSpec((rows, dm), row_map),
                  pl.BlockSpec((rows, 4 * qk), row_map),
                  pl.BlockSpec((rows, xw), lambda b, l: (b * nl + l, (4 * qk + sw) // xw)),
                  pl.BlockSpec((rows, 3 * dm), lambda b, l: (b * nl + l, gate_col // (3 * dm))),
                  pl.BlockSpec((rows, sw), row_map),
                  pl.BlockSpec((tl, HEAD_DIM), tab_map), pl.BlockSpec((tl, HEAD_DIM), tab_map),
                  pl.BlockSpec((tl, HEAD_DIM), tab_map), pl.BlockSpec((tl, HEAD_DIM), tab_map),
                  _resident(dmask.shape), _resident(qw.shape), _resident(kw.shape),
                  mem_spec, mem_spec, st_spec, _resident((1, qk)),
                  _resident(w_ret_o.shape), _resident(w_ssm_glu.shape), _resident(w_ssm_o.shape),
                  _resident(w_x_o.shape), _resident(w_out.shape)],
        out_specs=[pl.BlockSpec((rows, dm), row_map), st_spec],
        out_shape=[jax.ShapeDtypeStruct((bsz * length, dm), F32),
                   jax.ShapeDtypeStruct(s0.shape, F32)],
        scratch_shapes=[pltpu.VMEM((nb, RET_HEADS, HEAD_DIM, HEAD_DIM), F32),
                        pltpu.VMEM((rows, qk), BF16), pltpu.VMEM((rows, xw), BF16)],
        compiler_params=_params(2), name="mixer",
    )(x.reshape(bsz * length, dm), z, z, z, y_ssm.reshape(bsz * length, sw),
      cosf * scale, sinf * scale, cosf, sinf, dmask, qw, kw,
      mem_k, mem_v, s0, ret_gn.reshape(1, qk), w_ret_o, w_ssm_glu, w_ssm_o, w_x_o, w_out)
    return h, s_out


def _router_kernel(h_ref, nw_ref, wrt_ref, br_ref, tri_ref, xn_ref, idx_ref, gw_ref, rank_ref, cnt_ref, cnt_scr):
    @pl.when(pl.program_id(0) == 0)
    def _():
        cnt_scr[...] = jnp.zeros_like(cnt_scr)

    xn = _rms(h_ref[...], nw_ref[...])
    xn_ref[...] = xn
    logits = lax.dot_general(wrt_ref[...], xn, NT_DIMS, precision=lax.Precision.HIGHEST,
                             preferred_element_type=F32) + br_ref[...]
    ne = logits.shape[0]
    iota = lax.broadcasted_iota(jnp.int32, logits.shape, 0)
    rest = logits
    sel = jnp.zeros(logits.shape, jnp.bool_)
    vals, idxs = [], []
    for _ in range(TOP_K):
        m = jnp.max(rest, axis=0, keepdims=True)
        ix = jnp.min(jnp.where(rest == m, iota, ne), axis=0, keepdims=True)
        hit = iota == ix
        vals.append(m)
        idxs.append(ix)
        sel = jnp.logical_or(sel, hit)
        rest = jnp.where(hit, -jnp.inf, rest)
    es = [jnp.exp(v - vals[0]) for v in vals]
    tot = es[0] + es[1] + es[2] + es[3]
    before = cnt_scr[...] + jnp.dot(sel.astype(BF16), tri_ref[...], preferred_element_type=F32)
    for k in range(TOP_K):
        idx_ref[k:k + 1, :] = idxs[k]
        gw_ref[k:k + 1, :] = es[k] / tot
        rank_ref[k:k + 1, :] = jnp.sum(jnp.where(iota == idxs[k], before, 0.0), axis=0,
                                       keepdims=True).astype(jnp.int32)
    cnt_scr[...] += jnp.sum(sel.astype(F32), axis=1, keepdims=True)
    cnt_ref[...] = jnp.broadcast_to(cnt_scr[...], cnt_ref.shape)


def router(h, norm_ffn, w_router, b_router):
    t, dm = h.shape
    ne = w_router.shape[1]
    tt = ROW_TILE
    tri = jnp.asarray(np.triu(np.ones((tt, tt), np.float32), k=1), BF16)
    tok_spec = pl.BlockSpec((TOP_K, tt), lambda i: (0, i))
    return pl.pallas_call(
        _router_kernel, grid=(t // tt,),
        in_specs=[pl.BlockSpec((tt, dm), lambda i: (i, 0)), _resident((1, dm)), _resident((ne, dm)),
                  _resident((ne, 1)), _resident((tt, tt))],
        out_specs=[pl.BlockSpec((tt, dm), lambda i: (i, 0)), tok_spec, tok_spec, tok_spec,
                   pl.BlockSpec((ne, 128), lambda i: (0, 0))],
        out_shape=[jax.ShapeDtypeStruct((t, dm), F32), jax.ShapeDtypeStruct((TOP_K, t), jnp.int32),
                   jax.ShapeDtypeStruct((TOP_K, t), F32), jax.ShapeDtypeStruct((TOP_K, t), jnp.int32),
                   jax.ShapeDtypeStruct((ne, 128), F32)],
        scratch_shapes=[pltpu.VMEM((ne, 1), F32)],
        compiler_params=_params(), name="router",
    )(h, norm_ffn.reshape(1, dm), w_router.T, b_router.reshape(ne, 1), tri)


def _dispatch_kernel(dest_ref, xn_ref, xs_in_ref, xs_ref, sem):
    del xs_in_ref
    tt = xn_ref.shape[0]

    def row_copy(t, k):
        return pltpu.make_async_copy(xn_ref.at[pl.ds(t, 1)],
                                     xs_ref.at[pl.ds(dest_ref[0, 0, k * tt + t], 1)], sem)

    def issue(t, carry):
        for k in range(TOP_K):
            row_copy(t, k).start()
        return carry

    def drain(t, carry):
        for k in range(TOP_K):
            row_copy(t, k).wait()
        return carry

    lax.fori_loop(0, tt, issue, 0)
    lax.fori_loop(0, tt, drain, 0)


def dispatch(xn, dest_tiles, n_rows):
    t, dm = xn.shape
    tt = ROW_TILE
    return pl.pallas_call(
        _dispatch_kernel, grid=(t // tt,),
        in_specs=[pl.BlockSpec((1, 1, TOP_K * tt), lambda i: (i, 0, 0), memory_space=pltpu.SMEM),
                  pl.BlockSpec((tt, dm), lambda i: (i, 0)),
                  pl.BlockSpec(memory_space=pl.ANY)],
        out_specs=pl.BlockSpec(memory_space=pl.ANY),
        out_shape=jax.ShapeDtypeStruct((n_rows, dm), F32),
        scratch_shapes=[pltpu.SemaphoreType.DMA(())],
        input_output_aliases={2: 0},
        compiler_params=_params(), name="dispatch",
    )(dest_tiles, xn, jnp.zeros((n_rows, dm), F32))


def _ffn_kernel(be_ref, first_ref, blk_ref, used_ref, xs_ref, wgu_ref, bgu_ref, wd_ref, bd_ref,
                y_ref, wgu_bf, wd_bf):
    del be_ref, blk_ref
    i = pl.program_id(0)

    @pl.when(first_ref[i] == 1)
    def _():
        wgu_bf[...] = wgu_ref[0].astype(BF16)
        wd_bf[...] = wd_ref[0].astype(BF16)

    @pl.when(i < used_ref[0])
    def _():
        hgu = jnp.dot(xs_ref[...].astype(BF16), wgu_bf[...], preferred_element_type=F32) + bgu_ref[0]
        ff = hgu.shape[1] // 2
        gate = jnp.minimum(hgu[:, :ff], SWIGLU_LIMIT)
        up = jnp.clip(hgu[:, ff:], -SWIGLU_LIMIT, SWIGLU_LIMIT)
        act = (up + 1.0) * gate * jax.nn.sigmoid(SWIGLU_ALPHA * gate)
        y_ref[...] = jnp.dot(act.astype(BF16), wd_bf[...], preferred_element_type=F32) + bd_ref[0]

    @pl.when(i >= used_ref[0])
    def _():
        y_ref[...] = jnp.zeros_like(y_ref)


def expert_ffn(xs, block_e, block_first, block_idx, n_used, w_gate_up, b_gate_up, w_down, b_down):
    n_rows, dm = xs.shape
    ne, _, ff2 = w_gate_up.shape
    bm = FFN_BLOCK
    grid_spec = pltpu.PrefetchScalarGridSpec(
        num_scalar_prefetch=4, grid=(n_rows // bm,),
        in_specs=[pl.BlockSpec((bm, dm), lambda i, be, bf, bi, nu: (bi[i], 0)),
                  pl.BlockSpec((1, dm, ff2), lambda i, be, bf, bi, nu: (be[i], 0, 0)),
                  pl.BlockSpec((1, 1, ff2), lambda i, be, bf, bi, nu: (be[i], 0, 0)),
                  pl.BlockSpec((1, ff2 // 2, dm), lambda i, be, bf, bi, nu: (be[i], 0, 0)),
                  pl.BlockSpec((1, 1, dm), lambda i, be, bf, bi, nu: (be[i], 0, 0))],
        out_specs=pl.BlockSpec((bm, dm), lambda i, be, bf, bi, nu: (i, 0)),
        scratch_shapes=[pltpu.VMEM((dm, ff2), BF16), pltpu.VMEM((ff2 // 2, dm), BF16)])
    return pl.pallas_call(
        _ffn_kernel, grid_spec=grid_spec,
        out_shape=jax.ShapeDtypeStruct((n_rows, dm), F32),
        compiler_params=_params(), name="expert_ffn",
    )(block_e, block_first, block_idx, n_used, xs, w_gate_up, b_gate_up.reshape(ne, 1, ff2),
      w_down, b_down.reshape(ne, 1, dm))


def _combine_kernel(dest_ref, gw_ref, h_ref, fn_ref, yr_ref, out_ref, buf, sem):
    tt = h_ref.shape[0]

    def row_copy(t, k):
        return pltpu.make_async_copy(yr_ref.at[pl.ds(dest_ref[0, 0, k * tt + t], 1)],
                                     buf.at[k, pl.ds(t, 1)], sem)

    def issue(t, carry):
        for k in range(TOP_K):
            row_copy(t, k).start()
        return carry

    def drain(t, carry):
        for k in range(TOP_K):
            row_copy(t, k).wait()
        return carry

    lax.fori_loop(0, tt, issue, 0)
    lax.fori_loop(0, tt, drain, 0)
    acc = h_ref[...]
    for k in range(TOP_K):
        acc += gw_ref[:, k:k + 1] * buf[k]
    out_ref[...] = _rms(acc, fn_ref[...])


def combine(dest_tiles, gw_t, h, final_norm, y_rows):
    t, dm = h.shape
    tt = ROW_TILE
    return pl.pallas_call(
        _combine_kernel, grid=(t // tt,),
        in_specs=[pl.BlockSpec((1, 1, TOP_K * tt), lambda i: (i, 0, 0), memory_space=pltpu.SMEM),
                  pl.BlockSpec((tt, TOP_K), lambda i: (i, 0)),
                  pl.BlockSpec((tt, dm), lambda i: (i, 0)), _resident((1, dm)),
                  pl.BlockSpec(memory_space=pl.ANY)],
        out_specs=pl.BlockSpec((tt, dm), lambda i: (i, 0)),
        out_shape=jax.ShapeDtypeStruct((t, dm), F32),
        scratch_shapes=[pltpu.VMEM((TOP_K, tt, dm), F32), pltpu.SemaphoreType.DMA(())],
        compiler_params=_params(), name="combine",
    )(dest_tiles, gw_t, h, final_norm.reshape(1, dm), y_rows)


def moe_and_final_norm(h, norm_ffn, w_router, b_router, w_gate_up, b_gate_up, w_down, b_down, final_norm):
    t, dm = h.shape
    ne = w_router.shape[1]
    bm = FFN_BLOCK
    tt = ROW_TILE
    xn, idx, gw, rank, cnt = router(h, norm_ffn, w_router, b_router)
    counts = cnt[:, 0].astype(jnp.int32)
    padded = ((counts + bm - 1) // bm) * bm
    pad_ends = jnp.cumsum(padded)
    pad_starts = pad_ends - padded
    n_blocks = -(-(t * TOP_K + ne * (bm - 1)) // bm)
    n_used = pad_ends[-1] // bm
    blocks = jnp.arange(n_blocks, dtype=jnp.int32)
    block_idx = jnp.minimum(blocks, n_used - 1)
    block_e = jnp.minimum(jnp.searchsorted(pad_ends, block_idx * bm, side='right'), ne - 1).astype(jnp.int32)
    block_first = ((block_idx * bm == pad_starts[block_e]) & (blocks < n_used)).astype(jnp.int32)
    dest = pad_starts[idx] + rank
    dest_tiles = dest.reshape(TOP_K, t // tt, tt).transpose(1, 0, 2).reshape(t // tt, 1, TOP_K * tt)
    xs = dispatch(xn, dest_tiles, n_blocks * bm)
    y_rows = expert_ffn(xs, block_e, block_first, block_idx.astype(jnp.int32),
                        n_used.reshape(1).astype(jnp.int32), w_gate_up, b_gate_up, w_down, b_down)
    return combine(dest_tiles, gw.T, h, final_norm, y_rows)


def kernel(x_prompt, x_sample, cache_mem_k, cache_mem_v, state_ret, state_ssm_re, state_ssm_im, mem_prompt, norm_mix, w_in, ret_gn, w_ret_o, ssm_lam_re, ssm_lam_im, ssm_log_dt, ssm_b_re, ssm_b_im, ssm_c_re, ssm_c_im, ssm_d, w_ssm_glu, w_ssm_o, mem_norm, w_mem_kv, w_x_o, w_out, norm_ffn, w_router, b_router, w_gate_up, b_gate_up, w_down, b_down, final_norm):
    assert norm_mix.shape[0] == 1, "single-layer step"
    bp, lp, dm = x_prompt.shape
    bs, ls, _ = x_sample.shape
    n_mem = mem_prompt.shape[1]
    xw = X_HEADS * HEAD_DIM
    qk = RET_HEADS * HEAD_DIM
    sw = ssm_d.shape[1]
    g = ssm_lam_re.shape[1]

    w_in_b = w_in[0].astype(BF16)
    tables = s5_tables(ssm_lam_re[0], ssm_lam_im[0], ssm_log_dt[0], ssm_b_re[0], ssm_b_im[0],
                       ssm_c_re[0], ssm_c_im[0], ssm_d[0])
    mix_w = (ret_gn[0], w_ret_o[0].astype(BF16), w_ssm_glu[0].astype(BF16), w_ssm_o[0].astype(BF16),
             w_x_o[0].astype(BF16), w_out[0].astype(BF16))

    kv = norm_matmul(mem_prompt.reshape(bp * n_mem, dm), mem_norm[0], w_mem_kv[0].astype(BF16), F32)
    mk_p = kv[:, :xw].reshape(bp, n_mem, xw)
    mv_p = kv[:, xw:].reshape(bp, n_mem, xw)

    def group(x, pos, mem_k, mem_v, s_ret, h_re, h_im, nb, tl):
        bsz, length, _ = x.shape
        z = norm_matmul(x.reshape(bsz * length, dm), norm_mix[0], w_in_b, BF16)
        u = z[:, 4 * qk:4 * qk + sw].reshape(bsz, length, sw)
        y, hf_re, hf_im = s5_apply(u, h_re, h_im, tables)
        h, s_new = mixer(x, z, y, pos, mem_k, mem_v, s_ret, *mix_w, nb=nb, tl=tl)
        return h, s_new, hf_re, hf_im

    zero_ret = jnp.zeros((bp, RET_HEADS, HEAD_DIM, HEAD_DIM), F32)
    zero_ssm = jnp.zeros((bp, g, SSM_STATE), F32)
    h_p, ret_p, sre_p, sim_p = group(x_prompt, jnp.arange(lp, dtype=jnp.int32), mk_p, mv_p,
                                     zero_ret, zero_ssm, zero_ssm, 1, ROW_TILE)
    h_s, ret_s, sre_s, sim_s = group(x_sample, PAST_LEN + jnp.arange(ls, dtype=jnp.int32),
                                     cache_mem_k[0].reshape(bs, n_mem, xw), cache_mem_v[0].reshape(bs, n_mem, xw),
                                     state_ret[0], state_ssm_re[0], state_ssm_im[0], ROW_TILE // ls, ls)

    y_all = moe_and_final_norm(jnp.concatenate([h_p, h_s], axis=0), norm_ffn[0], w_router[0], b_router[0],
                               w_gate_up[0], b_gate_up[0], w_down[0], b_down[0], final_norm)
    y_prompt = y_all[:bp * lp].reshape(bp, lp, dm)
    y_sample = y_all[bp * lp:].reshape(bs, ls, dm)
    return (y_prompt, y_sample, ret_p[None], sre_p[None], sim_p[None],
            mk_p.reshape(1, bp, n_mem, X_HEADS, HEAD_DIM), mv_p.reshape(1, bp, n_mem, X_HEADS, HEAD_DIM),
            ret_s[None], sre_s[None], sim_s[None])
```

```python
import functools
import math

import jax
import jax.numpy as jnp
import numpy as np
from jax import lax
from jax.experimental import pallas as pl
from jax.experimental.pallas import tpu as pltpu

F32 = jnp.float32
BF16 = jnp.bfloat16

EPS = 1e-6
CHUNK = 64
PAST_LEN = 2048
ROPE_BASE = 10000.0
RET_HEADS = 4
X_HEADS = 4
HEAD_DIM = 128
SSM_GROUP = 16
SSM_STATE = 64
TOP_K = 4
SWIGLU_ALPHA = 1.702
SWIGLU_LIMIT = 7.0

VMEM_LIMIT = 52 * 1024 * 1024
S5_CHUNK = 8
S5_LANES = 128
ROW_TILE = 256
FFN_BLOCK = 256
NT_DIMS = (((1,), (1,)), ((), ()))
TN_DIMS = (((0,), (0,)), ((), ()))


def _params(n_axes=1):
    return pltpu.CompilerParams(dimension_semantics=("arbitrary",) * n_axes,
                                vmem_limit_bytes=VMEM_LIMIT)


def _resident(shape):
    nd = len(shape)
    return pl.BlockSpec(shape, lambda *_: (0,) * nd, pipeline_mode=pl.Buffered(1))


def _rms(x, w):
    return x * lax.rsqrt(jnp.mean(x * x, axis=-1, keepdims=True) + EPS) * w


def _norm_matmul_kernel(x_ref, nw_ref, w_ref, o_ref, *f32_refs, n_chunk, f32_cols):
    xb = _rms(x_ref[...], nw_ref[...]).astype(BF16)
    for n0 in range(0, o_ref.shape[1], n_chunk):
        r = jnp.dot(xb, w_ref[:, n0:n0 + n_chunk], preferred_element_type=F32)
        o_ref[:, n0:n0 + n_chunk] = r.astype(o_ref.dtype)
        if f32_cols is not None and n0 <= f32_cols[0] and f32_cols[1] <= n0 + n_chunk:
            f32_refs[0][...] = r[:, f32_cols[0] - n0:f32_cols[1] - n0]


def norm_matmul(x, nw, w, out_dtype, f32_cols=None):
    t, d = x.shape
    n = w.shape[1]
    n_chunk = min(n, 1024)
    out_specs = [pl.BlockSpec((ROW_TILE, n), lambda i: (i, 0))]
    out_shape = [jax.ShapeDtypeStruct((t, n), out_dtype)]
    if f32_cols is not None:
        lo, hi = f32_cols
        assert lo // n_chunk == (hi - 1) // n_chunk
        out_specs.append(pl.BlockSpec((ROW_TILE, hi - lo), lambda i: (i, 0)))
        out_shape.append(jax.ShapeDtypeStruct((t, hi - lo), F32))
    out = pl.pallas_call(
        functools.partial(_norm_matmul_kernel, n_chunk=n_chunk, f32_cols=f32_cols),
        grid=(t // ROW_TILE,),
        in_specs=[pl.BlockSpec((ROW_TILE, d), lambda i: (i, 0)), _resident((1, d)), _resident((d, n))],
        out_specs=out_specs, out_shape=out_shape,
        compiler_params=_params(), name="norm_matmul",
    )(x, nw.reshape(1, d), w)
    return out if f32_cols is not None else out[0]


def s5_tables(lam_re, lam_im, log_dt, b_re, b_im, c_re, c_im, d_skip):
    g, n, p = b_re.shape
    s = S5_CHUNK
    gl = S5_LANES // p
    j = g // gl
    hi = lax.Precision.HIGHEST
    dt = jnp.exp(log_dt)[:, None]
    a_re = jnp.exp(lam_re * dt) * jnp.cos(lam_im * dt)
    a_im = jnp.exp(lam_re * dt) * jnp.sin(lam_im * dt)
    den = lam_re * lam_re + lam_im * lam_im
    nr, ni = a_re - 1.0, a_im
    co_re = (nr * lam_re + ni * lam_im) / den
    co_im = (ni * lam_re - nr * lam_im) / den
    bb_re = co_re[..., None] * b_re - co_im[..., None] * b_im
    bb_im = co_re[..., None] * b_im + co_im[..., None] * b_re
    tau = jnp.arange(s + 1, dtype=F32)[:, None, None]
    pw_mag = jnp.exp(lam_re * dt * tau)
    pw_re = pw_mag * jnp.cos(lam_im * dt * tau)
    pw_im = pw_mag * jnp.sin(lam_im * dt * tau)
    ca_re = c_re[None] * pw_re[:, :, None, :] - c_im[None] * pw_im[:, :, None, :]
    ca_im = c_re[None] * pw_im[:, :, None, :] + c_im[None] * pw_re[:, :, None, :]
    kq = (jnp.einsum('tgpn,gnq->gtqp', ca_re[:s], bb_re, precision=hi)
          - jnp.einsum('tgpn,gnq->gtqp', ca_im[:s], bb_im, precision=hi))
    ts = np.arange(s)
    lag_onehot = (ts[None, None, :] - ts[None, :, None] == ts[:, None, None]).astype(np.float32)
    eye = jnp.eye(gl, dtype=F32)
    m5 = jnp.einsum('gxqp,xst->gsqtp', kq, lag_onehot, precision=hi)
    m = m5.reshape(j, gl, s, p, s, p).transpose(0, 2, 1, 3, 4, 5)
    m = (m[:, :, :, :, :, None, :] * eye[None, None, :, None, None, :, None]).reshape(j, s * gl * p, s * gl * p)
    rev = s - 1 - ts
    w_re = pw_re[rev][:, :, :, None] * bb_re[None] - pw_im[rev][:, :, :, None] * bb_im[None]
    w_im = pw_re[rev][:, :, :, None] * bb_im[None] + pw_im[rev][:, :, :, None] * bb_re[None]

    def w_block(x):
        x = x.reshape(s, j, gl, n, p).transpose(1, 0, 2, 4, 3)
        return (x[:, :, :, :, None, :] * eye[None, None, :, None, :, None]).reshape(j, s * gl * p, gl * n)

    def v_block(x):
        x = x.reshape(s, j, gl, p, n).transpose(1, 2, 4, 0, 3)
        return (x[:, :, :, :, None, :] * eye[None, :, None, None, :, None]).reshape(j, gl * n, s * gl * p)

    w = jnp.concatenate([w_block(w_re), w_block(w_im)], axis=2)
    v = jnp.concatenate([v_block(ca_re[1:]), v_block(-ca_im[1:])], axis=1)
    a_s_re = pw_re[s].reshape(1, g * n)
    a_s_im = pw_im[s].reshape(1, g * n)
    dtab = jnp.broadcast_to(d_skip.reshape(j, 1, 1, gl * p), (j, 1, s, gl * p)).reshape(j, 1, s * gl * p)
    return m.astype(BF16), w.astype(BF16), v.astype(BF16), a_s_re, a_s_im, dtab


def _s5_flat(u_ref):
    return jnp.concatenate([u_ref[:, t, :] for t in range(u_ref.shape[1])], axis=1)


def _s5a_kernel(u_ref, w_ref, ire_ref, iim_ref):
    r = jnp.dot(_s5_flat(u_ref).astype(BF16), w_ref[0], preferred_element_type=F32)
    half = r.shape[1] // 2
    ire_ref[...] = r[:, :half]
    iim_ref[...] = r[:, half:]


def _s5scan_kernel(ire_ref, iim_ref, ar_ref, ai_ref, h0r_ref, h0i_ref,
                   hpr_ref, hpi_ref, hfr_ref, hfi_ref):
    nb, nc, _ = ire_ref.shape
    ar, ai = ar_ref[...], ai_ref[...]

    def body(c, carry):
        out = []
        for b in range(nb):
            hr, hi = carry[2 * b], carry[2 * b + 1]
            hpr_ref[b, pl.ds(c, 1), :] = hr
            hpi_ref[b, pl.ds(c, 1), :] = hi
            out.append(ar * hr - ai * hi + ire_ref[b, pl.ds(c, 1), :])
            out.append(ar * hi + ai * hr + iim_ref[b, pl.ds(c, 1), :])
        return tuple(out)

    init = []
    for b in range(nb):
        init += [h0r_ref[b], h0i_ref[b]]
    fin = lax.fori_loop(0, nc, body, tuple(init))
    for b in range(nb):
        hfr_ref[b] = fin[2 * b]
        hfi_ref[b] = fin[2 * b + 1]


def _s5b_kernel(u_ref, hpr_ref, hpi_ref, m_ref, v_ref, d_ref, y_ref):
    uf = _s5_flat(u_ref)
    half = hpr_ref.shape[1]
    y = d_ref[0] * uf + jnp.dot(uf.astype(BF16), m_ref[0], preferred_element_type=F32)
    y += jnp.dot(hpr_ref[...].astype(BF16), v_ref[0, :half, :], preferred_element_type=F32)
    y += jnp.dot(hpi_ref[...].astype(BF16), v_ref[0, half:, :], preferred_element_type=F32)
    lanes = y_ref.shape[2]
    for t in range(y_ref.shape[1]):
        y_ref[:, t, :] = y[:, t * lanes:(t + 1) * lanes]


def s5_apply(u, bsz, h0_re, h0_im, tables):
    m, w, v, a_re, a_im, dtab = tables
    tokens, width = u.shape
    nj = m.shape[0]
    s = S5_CHUNK
    rows = tokens // s
    nc = rows // bsz
    lanes = a_re.shape[1]
    half = w.shape[2] // 2
    fl = m.shape[1]
    rt = min(rows, 512)
    u3 = u.reshape(rows, s, width)
    u_spec = pl.BlockSpec((rt, s, S5_LANES), lambda j, r: (r, 0, j))
    st_spec = pl.BlockSpec((rt, half), lambda j, r: (r, j))
    tab_spec = pl.BlockSpec((1, fl, fl), lambda j, r: (j, 0, 0))
    inj_re, inj_im = pl.pallas_call(
        _s5a_kernel, grid=(nj, rows // rt),
        in_specs=[u_spec, tab_spec],
        out_specs=[st_spec, st_spec],
        out_shape=[jax.ShapeDtypeStruct((rows, lanes), F32)] * 2,
        compiler_params=_params(2), name="s5_chunk_in",
    )(u3, w)

    sb, lw = 4, 512
    seq_spec = pl.BlockSpec((sb, nc, lw), lambda b, l: (b, 0, l))
    vec_spec = pl.BlockSpec((sb, 1, lw), lambda b, l: (b, 0, l))
    atab_spec = pl.BlockSpec((1, lw), lambda b, l: (0, l))
    hp_re, hp_im, hf_re, hf_im = pl.pallas_call(
        _s5scan_kernel, grid=(bsz // sb, lanes // lw),
        in_specs=[seq_spec, seq_spec, atab_spec, atab_spec, vec_spec, vec_spec],
        out_specs=[seq_spec, seq_spec, vec_spec, vec_spec],
        out_shape=[jax.ShapeDtypeStruct((bsz, nc, lanes), F32)] * 2
        + [jax.ShapeDtypeStruct((bsz, 1, lanes), F32)] * 2,
        compiler_params=_params(2), name="s5_scan",
    )(inj_re.reshape(bsz, nc, lanes), inj_im.reshape(bsz, nc, lanes), a_re, a_im,
      h0_re.reshape(bsz, 1, lanes), h0_im.reshape(bsz, 1, lanes))

    y3 = pl.pallas_call(
        _s5b_kernel, grid=(nj, rows // rt),
        in_specs=[u_spec, st_spec, st_spec, tab_spec, tab_spec,
                  pl.BlockSpec((1, 1, fl), lambda j, r: (j, 0, 0))],
        out_specs=u_spec,
        out_shape=jax.ShapeDtypeStruct((rows, s, width), F32),
        compiler_params=_params(2), name="s5_chunk_out",
    )(u3, hp_re.reshape(rows, lanes), hp_im.reshape(rows, lanes), m, v, dtab)
    g = lanes // SSM_STATE
    return y3.reshape(tokens, width), hf_re.reshape(bsz, g, SSM_STATE), hf_im.reshape(bsz, g, SSM_STATE)


def _retention_gammas():
    return 1.0 - np.exp2(-5.0 - np.arange(RET_HEADS, dtype=np.float64))


def retention_tables(tile, chunk):
    gam = _retention_gammas()[:, None, None]
    i = np.arange(tile)[:, None]
    j = np.arange(tile)[None, :]
    same = (i // chunk) == (j // chunk)
    earlier = (j // chunk) < (i // chunk)
    dist = np.where(same, np.abs(i - j), np.where(earlier, i - j, 0))
    dmask = np.where(same | earlier, gam ** dist[None], 0.0)
    qw = np.broadcast_to((gam[:, :, 0] ** (np.arange(tile) + 1.0))[:, :, None], (RET_HEADS, tile, HEAD_DIM))
    kw = np.broadcast_to((gam[:, :, 0] ** (tile - 1.0 - np.arange(tile)))[:, :, None], (RET_HEADS, tile, HEAD_DIM))
    return (jnp.asarray(dmask, F32), jnp.asarray(qw, F32), jnp.asarray(kw, F32),
            tuple(float(x) for x in _retention_gammas() ** tile))


def rope_tables(pos):
    half = HEAD_DIM // 2
    inv = jnp.exp(-math.log(ROPE_BASE) * 2.0 * jnp.arange(half, dtype=F32) / HEAD_DIM)
    ang = pos.astype(F32)[:, None] * inv[None, :]
    cos, sin = jnp.cos(ang), jnp.sin(ang)
    cosf = jnp.concatenate([cos, cos], axis=1)
    sinf = jnp.concatenate([-sin, sin], axis=1)
    return cosf, sinf


def _mixer_kernel(x_ref, zq_ref, xq_ref, gl_ref, y_ref, cq_ref, sq_ref, ck_ref, sk_ref,
                  dm_ref, qw_ref, kw_ref, mk_ref, mv_ref, s0_ref, gn_ref,
                  wro_ref, wglu_ref, wso_ref, wxo_ref, wout_ref,
                  h_ref, sout_ref, s_scr, o_scr, xo_scr, *, nb, tl, tile_decay):
    lt = pl.program_id(1)
    hd = HEAD_DIM
    qk = RET_HEADS * hd

    @pl.when(lt == 0)
    def _():
        s_scr[...] = s0_ref[...]

    cq, sq, ck, sk = cq_ref[...], sq_ref[...], ck_ref[...], sk_ref[...]
    for n in range(nb):
        rows = slice(n * tl, (n + 1) * tl)
        for h in range(RET_HEADS):
            c0 = h * hd
            q = zq_ref[rows, c0:c0 + hd].astype(F32)
            k = zq_ref[rows, qk + c0:qk + c0 + hd].astype(F32)
            v = zq_ref[rows, 2 * qk + c0:2 * qk + c0 + hd]
            g = zq_ref[rows, 3 * qk + c0:3 * qk + c0 + hd].astype(F32)
            qr = q * cq + pltpu.roll(q, hd // 2, 1) * sq
            kr = k * ck + pltpu.roll(k, hd // 2, 1) * sk
            sc = lax.dot_general(qr.astype(BF16), kr.astype(BF16), NT_DIMS,
                                 preferred_element_type=F32) * dm_ref[h]
            o = jnp.dot(sc.astype(BF16), v, preferred_element_type=F32)
            s_old = s_scr[n, h]
            o += jnp.dot((qr * qw_ref[h]).astype(BF16), s_old.astype(BF16), preferred_element_type=F32)
            kv = lax.dot_general((kr * kw_ref[h]).astype(BF16), v, TN_DIMS, preferred_element_type=F32)
            s_scr[n, h] = tile_decay[h] * s_old + kv
            d = o - jnp.mean(o, axis=-1, keepdims=True)
            on = d * lax.rsqrt(jnp.mean(d * d, axis=-1, keepdims=True) + EPS) * gn_ref[:, c0:c0 + hd]
            o_scr[rows, c0:c0 + hd] = (on * (g * jax.nn.sigmoid(g))).astype(BF16)
            mkh = mk_ref[n, :, c0:c0 + hd].astype(BF16)
            mvh = mv_ref[n, :, c0:c0 + hd].astype(BF16)
            s = lax.dot_general(xq_ref[rows, c0:c0 + hd], mkh, NT_DIMS,
                                preferred_element_type=F32) * (hd ** -0.5)
            e = jnp.exp(s - jnp.max(s, axis=-1, keepdims=True))
            p = e / jnp.sum(e, axis=-1, keepdims=True)
            xo_scr[rows, c0:c0 + hd] = jnp.dot(p.astype(BF16), mvh, preferred_element_type=F32).astype(BF16)

    ret = jnp.dot(o_scr[...], wro_ref[...], preferred_element_type=F32)
    zg = jnp.dot(jax.nn.gelu(y_ref[...]).astype(BF16), wglu_ref[...], preferred_element_type=F32)
    half = zg.shape[1] // 2
    glu = (zg[:, :half] * jax.nn.sigmoid(zg[:, half:])).astype(BF16)
    ssm = jnp.dot(glu, wso_ref[...], preferred_element_type=F32)
    xb = jnp.dot(xo_scr[...], wxo_ref[...], preferred_element_type=F32)
    dm = ret.shape[1]
    merged = (jax.nn.sigmoid(gl_ref[:, :dm].astype(F32)) * ret
              + jax.nn.sigmoid(gl_ref[:, dm:2 * dm].astype(F32)) * ssm
              + jax.nn.sigmoid(gl_ref[:, 2 * dm:].astype(F32)) * xb)
    h_ref[...] = x_ref[...] + jnp.dot(merged.astype(BF16), wout_ref[...], preferred_element_type=F32)

    @pl.when(lt == pl.num_programs(1) - 1)
    def _():
        sout_ref[...] = s_scr[...]


def mixer(x, z, y_ssm, pos, mem_k, mem_v, s0, ret_gn, w_ret_o, w_ssm_glu, w_ssm_o, w_x_o, w_out, *, nb, tl):
    bsz, length, dm = x.shape
    chunk = min(CHUNK, length)
    nl = length // tl
    rows = nb * tl
    qk = RET_HEADS * HEAD_DIM
    sw = y_ssm.shape[1]
    xw = X_HEADS * HEAD_DIM
    gate_col = (4 * qk + sw + xw)
    assert gate_col % (3 * dm) == 0 and (4 * qk + sw) % xw == 0
    dmask, qw, kw, tile_decay = retention_tables(tl, chunk)
    cosf, sinf = rope_tables(pos)
    scale = HEAD_DIM ** -0.5
    row_map = lambda b, l: (b * nl + l, 0)
    tab_map = lambda b, l: (l, 0)
    st_spec = pl.BlockSpec((nb, RET_HEADS, HEAD_DIM, HEAD_DIM), lambda b, l: (b, 0, 0, 0))
    mem_spec = pl.BlockSpec((nb, mem_k.shape[1], xw), lambda b, l: (b, 0, 0))
    h, s_out = pl.pallas_call(
        functools.partial(_mixer_kernel, nb=nb, tl=tl, tile_decay=tile_decay),
        grid=(bsz // nb, nl),
        in_specs=[pl.BlockSpec((rows, dm), row_map),
                  pl.BlockSpec((rows, 4 * qk), row_map),
                  pl.BlockSpec((rows, xw), lambda b, l: (b * nl + l, (4 * qk + sw) // xw)),
                  pl.BlockSpec((rows, 3 * dm), lambda b, l: (b * nl + l, gate_col // (3 * dm))),
                  pl.BlockSpec((rows, sw), row_map),
                  pl.BlockSpec((tl, HEAD_DIM), tab_map), pl.BlockSpec((tl, HEAD_DIM), tab_map),
                  pl.BlockSpec((tl, HEAD_DIM), tab_map), pl.BlockSpec((tl, HEAD_DIM), tab_map),
                  _resident(dmask.shape), _resident(qw.shape), _resident(kw.shape),
                  mem_spec, mem_spec, st_spec, _resident((1, qk)),
                  _resident(w_ret_o.shape), _resident(w_ssm_glu.shape), _resident(w_ssm_o.shape),
                  _resident(w_x_o.shape), _resident(w_out.shape)],
        out_specs=[pl.BlockSpec((rows, dm), row_map), st_spec],
        out_shape=[jax.ShapeDtypeStruct((bsz * length, dm), F32),
                   jax.ShapeDtypeStruct(s0.shape, F32)],
        scratch_shapes=[pltpu.VMEM((nb, RET_HEADS, HEAD_DIM, HEAD_DIM), F32),
                        pltpu.VMEM((rows, qk), BF16), pltpu.VMEM((rows, xw), BF16)],
        compiler_params=_params(2), name="mixer",
    )(x.reshape(bsz * length, dm), z, z, z, y_ssm,
      cosf * scale, sinf * scale, cosf, sinf, dmask, qw, kw,
      mem_k, mem_v, s0, ret_gn.reshape(1, qk), w_ret_o, w_ssm_glu, w_ssm_o, w_x_o, w_out)
    return h, s_out


def _router_kernel(h_ref, nw_ref, wrt_ref, br_ref, tri_ref, xn_ref, idx_ref, gw_ref, rank_ref, cnt_ref, cnt_scr):
    @pl.when(pl.program_id(0) == 0)
    def _():
        cnt_scr[...] = jnp.zeros_like(cnt_scr)

    xn = _rms(h_ref[...], nw_ref[...])
    xn_ref[...] = xn
    logits = lax.dot_general(wrt_ref[...], xn, NT_DIMS, precision=lax.Precision.HIGHEST,
                             preferred_element_type=F32) + br_ref[...]
    ne = logits.shape[0]
    iota = lax.broadcasted_iota(jnp.int32, logits.shape, 0)
    rest = logits
    sel = jnp.zeros(logits.shape, jnp.bool_)
    vals, idxs = [], []
    for _ in range(TOP_K):
        m = jnp.max(rest, axis=0, keepdims=True)
        ix = jnp.min(jnp.where(rest == m, iota, ne), axis=0, keepdims=True)
        hit = iota == ix
        vals.append(m)
        idxs.append(ix)
        sel = jnp.logical_or(sel, hit)
        rest = jnp.where(hit, -jnp.inf, rest)
    es = [jnp.exp(v - vals[0]) for v in vals]
    tot = es[0] + es[1] + es[2] + es[3]
    before = cnt_scr[...] + jnp.dot(sel.astype(BF16), tri_ref[...], preferred_element_type=F32)
    for k in range(TOP_K):
        idx_ref[k:k + 1, :] = idxs[k]
        gw_ref[k:k + 1, :] = es[k] / tot
        rank_ref[k:k + 1, :] = jnp.sum(jnp.where(iota == idxs[k], before, 0.0), axis=0,
                                       keepdims=True).astype(jnp.int32)
    cnt_scr[...] += jnp.sum(sel.astype(F32), axis=1, keepdims=True)
    cnt_ref[...] = jnp.broadcast_to(cnt_scr[...], cnt_ref.shape)


def router(h, norm_ffn, w_router, b_router):
    t, dm = h.shape
    ne = w_router.shape[1]
    tt = ROW_TILE
    tri = jnp.asarray(np.triu(np.ones((tt, tt), np.float32), k=1), BF16)
    tok_spec = pl.BlockSpec((TOP_K, tt), lambda i: (0, i))
    return pl.pallas_call(
        _router_kernel, grid=(t // tt,),
        in_specs=[pl.BlockSpec((tt, dm), lambda i: (i, 0)), _resident((1, dm)), _resident((ne, dm)),
                  _resident((ne, 1)), _resident((tt, tt))],
        out_specs=[pl.BlockSpec((tt, dm), lambda i: (i, 0)), tok_spec, tok_spec, tok_spec,
                   pl.BlockSpec((ne, 128), lambda i: (0, 0))],
        out_shape=[jax.ShapeDtypeStruct((t, dm), F32), jax.ShapeDtypeStruct((TOP_K, t), jnp.int32),
                   jax.ShapeDtypeStruct((TOP_K, t), F32), jax.ShapeDtypeStruct((TOP_K, t), jnp.int32),
                   jax.ShapeDtypeStruct((ne, 128), F32)],
        scratch_shapes=[pltpu.VMEM((ne, 1), F32)],
        compiler_params=_params(), name="router",
    )(h, norm_ffn.reshape(1, dm), w_router.T, b_router.reshape(ne, 1), tri)


def _dispatch_kernel(dest_ref, xn_ref, xs_in_ref, xs_ref, sem):
    del xs_in_ref
    tt = xn_ref.shape[0]

    def row_copy(t, k):
        return pltpu.make_async_copy(xn_ref.at[pl.ds(t, 1)],
                                     xs_ref.at[pl.ds(dest_ref[0, 0, k * tt + t], 1)], sem)

    def issue(t, carry):
        for k in range(TOP_K):
            row_copy(t, k).start()
        return carry

    def drain(t, carry):
        for k in range(TOP_K):
            row_copy(t, k).wait()
        return carry

    lax.fori_loop(0, tt, issue, 0)
    lax.fori_loop(0, tt, drain, 0)


def dispatch(xn, dest_tiles, n_rows):
    t, dm = xn.shape
    tt = ROW_TILE
    return pl.pallas_call(
        _dispatch_kernel, grid=(t // tt,),
        in_specs=[pl.BlockSpec((1, 1, TOP_K * tt), lambda i: (i, 0, 0), memory_space=pltpu.SMEM),
                  pl.BlockSpec((tt, dm), lambda i: (i, 0)),
                  pl.BlockSpec(memory_space=pl.ANY)],
        out_specs=pl.BlockSpec(memory_space=pl.ANY),
        out_shape=jax.ShapeDtypeStruct((n_rows, dm), F32),
        scratch_shapes=[pltpu.SemaphoreType.DMA(())],
        input_output_aliases={2: 0},
        compiler_params=_params(), name="dispatch",
    )(dest_tiles, xn, jnp.zeros((n_rows, dm), F32))


def _ffn_kernel(be_ref, first_ref, blk_ref, used_ref, xs_ref, wgu_ref, bgu_ref, wd_ref, bd_ref,
                y_ref, wgu_bf, wd_bf):
    del be_ref, blk_ref
    i = pl.program_id(0)

    @pl.when(first_ref[i] == 1)
    def _():
        wgu_bf[...] = wgu_ref[0].astype(BF16)
        wd_bf[...] = wd_ref[0].astype(BF16)

    @pl.when(i < used_ref[0])
    def _():
        hgu = jnp.dot(xs_ref[...].astype(BF16), wgu_bf[...], preferred_element_type=F32) + bgu_ref[0]
        ff = hgu.shape[1] // 2
        gate = jnp.minimum(hgu[:, :ff], SWIGLU_LIMIT)
        up = jnp.clip(hgu[:, ff:], -SWIGLU_LIMIT, SWIGLU_LIMIT)
        act = (up + 1.0) * gate * jax.nn.sigmoid(SWIGLU_ALPHA * gate)
        y_ref[...] = jnp.dot(act.astype(BF16), wd_bf[...], preferred_element_type=F32) + bd_ref[0]

    @pl.when(i >= used_ref[0])
    def _():
        y_ref[...] = jnp.zeros_like(y_ref)


def expert_ffn(xs, block_e, block_first, block_idx, n_used, w_gate_up, b_gate_up, w_down, b_down):
    n_rows, dm = xs.shape
    ne, _, ff2 = w_gate_up.shape
    bm = FFN_BLOCK
    grid_spec = pltpu.PrefetchScalarGridSpec(
        num_scalar_prefetch=4, grid=(n_rows // bm,),
        in_specs=[pl.BlockSpec((bm, dm), lambda i, be, bf, bi, nu: (bi[i], 0)),
                  pl.BlockSpec((1, dm, ff2), lambda i, be, bf, bi, nu: (be[i], 0, 0)),
                  pl.BlockSpec((1, 1, ff2), lambda i, be, bf, bi, nu: (be[i], 0, 0)),
                  pl.BlockSpec((1, ff2 // 2, dm), lambda i, be, bf, bi, nu: (be[i], 0, 0)),
                  pl.BlockSpec((1, 1, dm), lambda i, be, bf, bi, nu: (be[i], 0, 0))],
        out_specs=pl.BlockSpec((bm, dm), lambda i, be, bf, bi, nu: (i, 0)),
        scratch_shapes=[pltpu.VMEM((dm, ff2), BF16), pltpu.VMEM((ff2 // 2, dm), BF16)])
    return pl.pallas_call(
        _ffn_kernel, grid_spec=grid_spec,
        out_shape=jax.ShapeDtypeStruct((n_rows, dm), F32),
        compiler_params=_params(), name="expert_ffn",
    )(block_e, block_first, block_idx, n_used, xs, w_gate_up, b_gate_up.reshape(ne, 1, ff2),
      w_down, b_down.reshape(ne, 1, dm))


def _combine_kernel(dest_ref, gw_ref, h_ref, fn_ref, yr_ref, out_ref, buf, sem):
    tt = h_ref.shape[0]

    def row_copy(t, k):
        return pltpu.make_async_copy(yr_ref.at[pl.ds(dest_ref[0, 0, k * tt + t], 1)],
                                     buf.at[k, pl.ds(t, 1)], sem)

    def issue(t, carry):
        for k in range(TOP_K):
            row_copy(t, k).start()
        return carry

    def drain(t, carry):
        for k in range(TOP_K):
            row_copy(t, k).wait()
        return carry

    lax.fori_loop(0, tt, issue, 0)
    lax.fori_loop(0, tt, drain, 0)
    acc = h_ref[...]
    for k in range(TOP_K):
        acc += gw_ref[:, k:k + 1] * buf[k]
    out_ref[...] = _rms(acc, fn_ref[...])


def combine(dest_tiles, gw_t, h, final_norm, y_rows):
    t, dm = h.shape
    tt = ROW_TILE
    return pl.pallas_call(
        _combine_kernel, grid=(t // tt,),
        in_specs=[pl.BlockSpec((1, 1, TOP_K * tt), lambda i: (i, 0, 0), memory_space=pltpu.SMEM),
                  pl.BlockSpec((tt, TOP_K), lambda i: (i, 0)),
                  pl.BlockSpec((tt, dm), lambda i: (i, 0)), _resident((1, dm)),
                  pl.BlockSpec(memory_space=pl.ANY)],
        out_specs=pl.BlockSpec((tt, dm), lambda i: (i, 0)),
        out_shape=jax.ShapeDtypeStruct((t, dm), F32),
        scratch_shapes=[pltpu.VMEM((TOP_K, tt, dm), F32), pltpu.SemaphoreType.DMA(())],
        compiler_params=_params(), name="combine",
    )(dest_tiles, gw_t, h, final_norm.reshape(1, dm), y_rows)


def moe_and_final_norm(h, norm_ffn, w_router, b_router, w_gate_up, b_gate_up, w_down, b_down, final_norm):
    t, dm = h.shape
    ne = w_router.shape[1]
    bm = FFN_BLOCK
    tt = ROW_TILE
    xn, idx, gw, rank, cnt = router(h, norm_ffn, w_router, b_router)
    counts = cnt[:, 0].astype(jnp.int32)
    padded = ((counts + bm - 1) // bm) * bm
    pad_ends = jnp.cumsum(padded)
    pad_starts = pad_ends - padded
    n_blocks = -(-(t * TOP_K + ne * (bm - 1)) // bm)
    n_used = pad_ends[-1] // bm
    blocks = jnp.arange(n_blocks, dtype=jnp.int32)
    block_idx = jnp.minimum(blocks, n_used - 1)
    block_e = jnp.minimum(jnp.sum((pad_ends[None, :] <= (block_idx * bm)[:, None]).astype(jnp.int32), axis=1),
                          ne - 1)
    block_first = (jnp.any(block_idx[:, None] * bm == pad_starts[None, :], axis=1)
                   & (blocks < n_used)).astype(jnp.int32)
    experts = jnp.arange(ne, dtype=jnp.int32)[:, None, None]
    dest = rank + jnp.sum(jnp.where(idx[None] == experts, pad_starts[:, None, None], 0), axis=0)
    dest_tiles = dest.reshape(TOP_K, t // tt, tt).transpose(1, 0, 2).reshape(t // tt, 1, TOP_K * tt)
    xs = dispatch(xn, dest_tiles, n_blocks * bm)
    y_rows = expert_ffn(xs, block_e, block_first, block_idx.astype(jnp.int32),
                        n_used.reshape(1).astype(jnp.int32), w_gate_up, b_gate_up, w_down, b_down)
    return combine(dest_tiles, gw.T, h, final_norm, y_rows)


def kernel(x_prompt, x_sample, cache_mem_k, cache_mem_v, state_ret, state_ssm_re, state_ssm_im, mem_prompt, norm_mix, w_in, ret_gn, w_ret_o, ssm_lam_re, ssm_lam_im, ssm_log_dt, ssm_b_re, ssm_b_im, ssm_c_re, ssm_c_im, ssm_d, w_ssm_glu, w_ssm_o, mem_norm, w_mem_kv, w_x_o, w_out, norm_ffn, w_router, b_router, w_gate_up, b_gate_up, w_down, b_down, final_norm):
    assert norm_mix.shape[0] == 1, "single-layer step"
    bp, lp, dm = x_prompt.shape
    bs, ls, _ = x_sample.shape
    n_mem = mem_prompt.shape[1]
    xw = X_HEADS * HEAD_DIM
    qk = RET_HEADS * HEAD_DIM
    sw = ssm_d.shape[1]
    g = ssm_lam_re.shape[1]

    w_in_b = w_in[0].astype(BF16)
    tables = s5_tables(ssm_lam_re[0], ssm_lam_im[0], ssm_log_dt[0], ssm_b_re[0], ssm_b_im[0],
                       ssm_c_re[0], ssm_c_im[0], ssm_d[0])
    mix_w = (ret_gn[0], w_ret_o[0].astype(BF16), w_ssm_glu[0].astype(BF16), w_ssm_o[0].astype(BF16),
             w_x_o[0].astype(BF16), w_out[0].astype(BF16))

    kv = norm_matmul(mem_prompt.reshape(bp * n_mem, dm), mem_norm[0], w_mem_kv[0].astype(BF16), F32)
    mk_p = kv[:, :xw].reshape(bp, n_mem, xw)
    mv_p = kv[:, xw:].reshape(bp, n_mem, xw)

    def group(x, pos, mem_k, mem_v, s_ret, h_re, h_im, nb, tl):
        bsz, length, _ = x.shape
        z, u = norm_matmul(x.reshape(bsz * length, dm), norm_mix[0], w_in_b, BF16,
                           f32_cols=(4 * qk, 4 * qk + sw))
        y, hf_re, hf_im = s5_apply(u, bsz, h_re, h_im, tables)
        h, s_new = mixer(x, z, y, pos, mem_k, mem_v, s_ret, *mix_w, nb=nb, tl=tl)
        return h, s_new, hf_re, hf_im

    zero_ret = jnp.zeros((bp, RET_HEADS, HEAD_DIM, HEAD_DIM), F32)
    zero_ssm = jnp.zeros((bp, g, SSM_STATE), F32)
    h_p, ret_p, sre_p, sim_p = group(x_prompt, jnp.arange(lp, dtype=jnp.int32), mk_p, mv_p,
                                     zero_ret, zero_ssm, zero_ssm, 1, ROW_TILE)
    h_s, ret_s, sre_s, sim_s = group(x_sample, PAST_LEN + jnp.arange(ls, dtype=jnp.int32),
                                     cache_mem_k[0].reshape(bs, n_mem, xw), cache_mem_v[0].reshape(bs, n_mem, xw),
                                     state_ret[0], state_ssm_re[0], state_ssm_im[0], ROW_TILE // ls, ls)

    y_all = moe_and_final_norm(jnp.concatenate([h_p, h_s], axis=0), norm_ffn[0], w_router[0], b_router[0],
                               w_gate_up[0], b_gate_up[0], w_down[0], b_down[0], final_norm)
    y_prompt = y_all[:bp * lp].reshape(bp, lp, dm)
    y_sample = y_all[bp * lp:].reshape(bs, ls, dm)
    return (y_prompt, y_sample, ret_p[None], sre_p[None], sim_p[None],
            mk_p.reshape(1, bp, n_mem, X_HEADS, HEAD_DIM), mv_p.reshape(1, bp, n_mem, X_HEADS, HEAD_DIM),
            ret_s[None], sre_s[None], sim_s[None])
```

```python
import functools
import math

import jax
import jax.numpy as jnp
import numpy as np
from jax import lax
from jax.experimental import pallas as pl
from jax.experimental.pallas import tpu as pltpu

F32 = jnp.float32
BF16 = jnp.bfloat16

EPS = 1e-6
CHUNK = 64
PAST_LEN = 2048
ROPE_BASE = 10000.0
RET_HEADS = 4
X_HEADS = 4
HEAD_DIM = 128
SSM_GROUP = 16
SSM_STATE = 64
TOP_K = 4
SWIGLU_ALPHA = 1.702
SWIGLU_LIMIT = 7.0

VMEM_LIMIT = 52 * 1024 * 1024
S5_CHUNK = 8
S5_LANES = 128
ROW_TILE = 256
FFN_BLOCK = 256
ROW_ALIGN = 8
NT_DIMS = (((1,), (1,)), ((), ()))
TN_DIMS = (((0,), (0,)), ((), ()))


def _params(n_axes=1):
    return pltpu.CompilerParams(dimension_semantics=("arbitrary",) * n_axes,
                                vmem_limit_bytes=VMEM_LIMIT)


def _resident(shape):
    nd = len(shape)
    return pl.BlockSpec(shape, lambda *_: (0,) * nd, pipeline_mode=pl.Buffered(1))


def _rms(x, w):
    return x * lax.rsqrt(jnp.mean(x * x, axis=-1, keepdims=True) + EPS) * w


def _norm_matmul_kernel(x_ref, nw_ref, w_ref, o_ref, *f32_refs, n_chunk, f32_cols):
    xb = _rms(x_ref[...], nw_ref[...]).astype(BF16)
    for n0 in range(0, o_ref.shape[1], n_chunk):
        r = jnp.dot(xb, w_ref[:, n0:n0 + n_chunk], preferred_element_type=F32)
        o_ref[:, n0:n0 + n_chunk] = r.astype(o_ref.dtype)
        if f32_cols is not None and n0 <= f32_cols[0] and f32_cols[1] <= n0 + n_chunk:
            f32_refs[0][...] = r[:, f32_cols[0] - n0:f32_cols[1] - n0]


def norm_matmul(x, nw, w, out_dtype, f32_cols=None):
    t, d = x.shape
    n = w.shape[1]
    n_chunk = min(n, 1024)
    out_specs = [pl.BlockSpec((ROW_TILE, n), lambda i: (i, 0))]
    out_shape = [jax.ShapeDtypeStruct((t, n), out_dtype)]
    if f32_cols is not None:
        lo, hi = f32_cols
        assert lo // n_chunk == (hi - 1) // n_chunk
        out_specs.append(pl.BlockSpec((ROW_TILE, hi - lo), lambda i: (i, 0)))
        out_shape.append(jax.ShapeDtypeStruct((t, hi - lo), F32))
    out = pl.pallas_call(
        functools.partial(_norm_matmul_kernel, n_chunk=n_chunk, f32_cols=f32_cols),
        grid=(t // ROW_TILE,),
        in_specs=[pl.BlockSpec((ROW_TILE, d), lambda i: (i, 0)), _resident((1, d)), _resident((d, n))],
        out_specs=out_specs, out_shape=out_shape,
        compiler_params=_params(), name="norm_matmul",
    )(x, nw.reshape(1, d), w)
    return out if f32_cols is not None else out[0]


def s5_tables(lam_re, lam_im, log_dt, b_re, b_im, c_re, c_im, d_skip):
    g, n, p = b_re.shape
    s = S5_CHUNK
    gl = S5_LANES // p
    j = g // gl
    hi = lax.Precision.HIGHEST
    dt = jnp.exp(log_dt)[:, None]
    a_re = jnp.exp(lam_re * dt) * jnp.cos(lam_im * dt)
    a_im = jnp.exp(lam_re * dt) * jnp.sin(lam_im * dt)
    den = lam_re * lam_re + lam_im * lam_im
    nr, ni = a_re - 1.0, a_im
    co_re = (nr * lam_re + ni * lam_im) / den
    co_im = (ni * lam_re - nr * lam_im) / den
    bb_re = co_re[..., None] * b_re - co_im[..., None] * b_im
    bb_im = co_re[..., None] * b_im + co_im[..., None] * b_re
    tau = jnp.arange(s + 1, dtype=F32)[:, None, None]
    pw_mag = jnp.exp(lam_re * dt * tau)
    pw_re = pw_mag * jnp.cos(lam_im * dt * tau)
    pw_im = pw_mag * jnp.sin(lam_im * dt * tau)
    ca_re = c_re[None] * pw_re[:, :, None, :] - c_im[None] * pw_im[:, :, None, :]
    ca_im = c_re[None] * pw_im[:, :, None, :] + c_im[None] * pw_re[:, :, None, :]
    kq = (jnp.einsum('tgpn,gnq->gtqp', ca_re[:s], bb_re, precision=hi)
          - jnp.einsum('tgpn,gnq->gtqp', ca_im[:s], bb_im, precision=hi))
    ts = np.arange(s)
    lag_onehot = (ts[None, None, :] - ts[None, :, None] == ts[:, None, None]).astype(np.float32)
    eye = jnp.eye(gl, dtype=F32)
    m5 = jnp.einsum('gxqp,xst->gsqtp', kq, lag_onehot, precision=hi)
    m = m5.reshape(j, gl, s, p, s, p).transpose(0, 2, 1, 3, 4, 5)
    m = (m[:, :, :, :, :, None, :] * eye[None, None, :, None, None, :, None]).reshape(j, s * gl * p, s * gl * p)
    rev = s - 1 - ts
    w_re = pw_re[rev][:, :, :, None] * bb_re[None] - pw_im[rev][:, :, :, None] * bb_im[None]
    w_im = pw_re[rev][:, :, :, None] * bb_im[None] + pw_im[rev][:, :, :, None] * bb_re[None]

    def w_block(x):
        x = x.reshape(s, j, gl, n, p).transpose(1, 0, 2, 4, 3)
        return (x[:, :, :, :, None, :] * eye[None, None, :, None, :, None]).reshape(j, s * gl * p, gl * n)

    def v_block(x):
        x = x.reshape(s, j, gl, p, n).transpose(1, 2, 4, 0, 3)
        return (x[:, :, :, :, None, :] * eye[None, :, None, None, :, None]).reshape(j, gl * n, s * gl * p)

    w = jnp.concatenate([w_block(w_re), w_block(w_im)], axis=2)
    v = jnp.concatenate([v_block(ca_re[1:]), v_block(-ca_im[1:])], axis=1)
    a_s_re = pw_re[s].reshape(1, g * n)
    a_s_im = pw_im[s].reshape(1, g * n)
    dtab = jnp.broadcast_to(d_skip.reshape(j, 1, 1, gl * p), (j, 1, s, gl * p)).reshape(j, 1, s * gl * p)
    return m.astype(BF16), w.astype(BF16), v.astype(BF16), a_s_re, a_s_im, dtab


def _s5_flat(u_ref):
    return jnp.concatenate([u_ref[:, t, :] for t in range(u_ref.shape[1])], axis=1)


def _s5a_kernel(u_ref, w_ref, ire_ref, iim_ref):
    r = jnp.dot(_s5_flat(u_ref).astype(BF16), w_ref[0], preferred_element_type=F32)
    half = r.shape[1] // 2
    ire_ref[...] = r[:, :half]
    iim_ref[...] = r[:, half:]


def _s5scan_kernel(ire_ref, iim_ref, ar_ref, ai_ref, h0r_ref, h0i_ref,
                   hpr_ref, hpi_ref, hfr_ref, hfi_ref):
    nb, nc, _ = ire_ref.shape
    ar, ai = ar_ref[...], ai_ref[...]

    def body(c, carry):
        out = []
        for b in range(nb):
            hr, hi = carry[2 * b], carry[2 * b + 1]
            hpr_ref[b, pl.ds(c, 1), :] = hr
            hpi_ref[b, pl.ds(c, 1), :] = hi
            out.append(ar * hr - ai * hi + ire_ref[b, pl.ds(c, 1), :])
            out.append(ar * hi + ai * hr + iim_ref[b, pl.ds(c, 1), :])
        return tuple(out)

    init = []
    for b in range(nb):
        init += [h0r_ref[b], h0i_ref[b]]
    fin = lax.fori_loop(0, nc, body, tuple(init))
    for b in range(nb):
        hfr_ref[b] = fin[2 * b]
        hfi_ref[b] = fin[2 * b + 1]


def _s5b_kernel(u_ref, hpr_ref, hpi_ref, m_ref, v_ref, d_ref, y_ref):
    uf = _s5_flat(u_ref)
    half = hpr_ref.shape[1]
    y = d_ref[0] * uf + jnp.dot(uf.astype(BF16), m_ref[0], preferred_element_type=F32)
    y += jnp.dot(hpr_ref[...].astype(BF16), v_ref[0, :half, :], preferred_element_type=F32)
    y += jnp.dot(hpi_ref[...].astype(BF16), v_ref[0, half:, :], preferred_element_type=F32)
    lanes = y_ref.shape[2]
    for t in range(y_ref.shape[1]):
        y_ref[:, t, :] = y[:, t * lanes:(t + 1) * lanes]


def s5_apply(u, bsz, h0_re, h0_im, tables):
    m, w, v, a_re, a_im, dtab = tables
    tokens, width = u.shape
    nj = m.shape[0]
    s = S5_CHUNK
    rows = tokens // s
    nc = rows // bsz
    lanes = a_re.shape[1]
    half = w.shape[2] // 2
    fl = m.shape[1]
    rt = min(rows, 512)
    u3 = u.reshape(rows, s, width)
    u_spec = pl.BlockSpec((rt, s, S5_LANES), lambda j, r: (r, 0, j))
    st_spec = pl.BlockSpec((rt, half), lambda j, r: (r, j))
    tab_spec = pl.BlockSpec((1, fl, fl), lambda j, r: (j, 0, 0))
    inj_re, inj_im = pl.pallas_call(
        _s5a_kernel, grid=(nj, rows // rt),
        in_specs=[u_spec, tab_spec],
        out_specs=[st_spec, st_spec],
        out_shape=[jax.ShapeDtypeStruct((rows, lanes), F32)] * 2,
        compiler_params=_params(2), name="s5_chunk_in",
    )(u3, w)

    sb, lw = 4, 512
    seq_spec = pl.BlockSpec((sb, nc, lw), lambda b, l: (b, 0, l))
    vec_spec = pl.BlockSpec((sb, 1, lw), lambda b, l: (b, 0, l))
    atab_spec = pl.BlockSpec((1, lw), lambda b, l: (0, l))
    hp_re, hp_im, hf_re, hf_im = pl.pallas_call(
        _s5scan_kernel, grid=(bsz // sb, lanes // lw),
        in_specs=[seq_spec, seq_spec, atab_spec, atab_spec, vec_spec, vec_spec],
        out_specs=[seq_spec, seq_spec, vec_spec, vec_spec],
        out_shape=[jax.ShapeDtypeStruct((bsz, nc, lanes), F32)] * 2
        + [jax.ShapeDtypeStruct((bsz, 1, lanes), F32)] * 2,
        compiler_params=_params(2), name="s5_scan",
    )(inj_re.reshape(bsz, nc, lanes), inj_im.reshape(bsz, nc, lanes), a_re, a_im,
      h0_re.reshape(bsz, 1, lanes), h0_im.reshape(bsz, 1, lanes))

    y3 = pl.pallas_call(
        _s5b_kernel, grid=(nj, rows // rt),
        in_specs=[u_spec, st_spec, st_spec, tab_spec, tab_spec,
                  pl.BlockSpec((1, 1, fl), lambda j, r: (j, 0, 0))],
        out_specs=u_spec,
        out_shape=jax.ShapeDtypeStruct((rows, s, width), F32),
        compiler_params=_params(2), name="s5_chunk_out",
    )(u3, hp_re.reshape(rows, lanes), hp_im.reshape(rows, lanes), m, v, dtab)
    g = lanes // SSM_STATE
    return y3.reshape(tokens, width), hf_re.reshape(bsz, g, SSM_STATE), hf_im.reshape(bsz, g, SSM_STATE)


def _retention_gammas():
    return 1.0 - np.exp2(-5.0 - np.arange(RET_HEADS, dtype=np.float64))


def retention_tables(tile, chunk):
    gam = _retention_gammas()[:, None, None]
    i = np.arange(tile)[:, None]
    j = np.arange(tile)[None, :]
    same = (i // chunk) == (j // chunk)
    earlier = (j // chunk) < (i // chunk)
    dist = np.where(same, np.abs(i - j), np.where(earlier, i - j, 0))
    dmask = np.where(same | earlier, gam ** dist[None], 0.0)
    qw = np.broadcast_to((gam[:, :, 0] ** (np.arange(tile) + 1.0))[:, :, None], (RET_HEADS, tile, HEAD_DIM))
    kw = np.broadcast_to((gam[:, :, 0] ** (tile - 1.0 - np.arange(tile)))[:, :, None], (RET_HEADS, tile, HEAD_DIM))
    return (jnp.asarray(dmask, F32), jnp.asarray(qw, F32), jnp.asarray(kw, F32),
            tuple(float(x) for x in _retention_gammas() ** tile))


def rope_tables(pos):
    half = HEAD_DIM // 2
    inv = jnp.exp(-math.log(ROPE_BASE) * 2.0 * jnp.arange(half, dtype=F32) / HEAD_DIM)
    ang = pos.astype(F32)[:, None] * inv[None, :]
    cos, sin = jnp.cos(ang), jnp.sin(ang)
    cosf = jnp.concatenate([cos, cos], axis=1)
    sinf = jnp.concatenate([-sin, sin], axis=1)
    return cosf, sinf


def _mixer_kernel(x_ref, zq_ref, xq_ref, gl_ref, y_ref, cq_ref, sq_ref, ck_ref, sk_ref,
                  dm_ref, qw_ref, kw_ref, mk_ref, mv_ref, s0_ref, gn_ref,
                  wro_ref, wglu_ref, wso_ref, wxo_ref, wout_ref,
                  h_ref, sout_ref, s_scr, o_scr, xo_scr, *, nb, tl, tile_decay):
    lt = pl.program_id(1)
    hd = HEAD_DIM
    qk = RET_HEADS * hd

    @pl.when(lt == 0)
    def _():
        s_scr[...] = s0_ref[...]

    cq, sq, ck, sk = cq_ref[...], sq_ref[...], ck_ref[...], sk_ref[...]
    for n in range(nb):
        rows = slice(n * tl, (n + 1) * tl)
        for h in range(RET_HEADS):
            c0 = h * hd
            q = zq_ref[rows, c0:c0 + hd].astype(F32)
            k = zq_ref[rows, qk + c0:qk + c0 + hd].astype(F32)
            v = zq_ref[rows, 2 * qk + c0:2 * qk + c0 + hd]
            g = zq_ref[rows, 3 * qk + c0:3 * qk + c0 + hd].astype(F32)
            qr = q * cq + pltpu.roll(q, hd // 2, 1) * sq
            kr = k * ck + pltpu.roll(k, hd // 2, 1) * sk
            sc = lax.dot_general(qr.astype(BF16), kr.astype(BF16), NT_DIMS,
                                 preferred_element_type=F32) * dm_ref[h]
            o = jnp.dot(sc.astype(BF16), v, preferred_element_type=F32)
            s_old = s_scr[n, h]
            o += jnp.dot((qr * qw_ref[h]).astype(BF16), s_old.astype(BF16), preferred_element_type=F32)
            kv = lax.dot_general((kr * kw_ref[h]).astype(BF16), v, TN_DIMS, preferred_element_type=F32)
            s_scr[n, h] = tile_decay[h] * s_old + kv
            d = o - jnp.mean(o, axis=-1, keepdims=True)
            on = d * lax.rsqrt(jnp.mean(d * d, axis=-1, keepdims=True) + EPS) * gn_ref[:, c0:c0 + hd]
            o_scr[rows, c0:c0 + hd] = (on * (g * jax.nn.sigmoid(g))).astype(BF16)
            mkh = mk_ref[n, :, c0:c0 + hd].astype(BF16)
            mvh = mv_ref[n, :, c0:c0 + hd].astype(BF16)
            s = lax.dot_general(xq_ref[rows, c0:c0 + hd], mkh, NT_DIMS,
                                preferred_element_type=F32) * (hd ** -0.5)
            e = jnp.exp(s - jnp.max(s, axis=-1, keepdims=True))
            p = e / jnp.sum(e, axis=-1, keepdims=True)
            xo_scr[rows, c0:c0 + hd] = jnp.dot(p.astype(BF16), mvh, preferred_element_type=F32).astype(BF16)

    ret = jnp.dot(o_scr[...], wro_ref[...], preferred_element_type=F32)
    zg = jnp.dot(jax.nn.gelu(y_ref[...]).astype(BF16), wglu_ref[...], preferred_element_type=F32)
    half = zg.shape[1] // 2
    glu = (zg[:, :half] * jax.nn.sigmoid(zg[:, half:])).astype(BF16)
    ssm = jnp.dot(glu, wso_ref[...], preferred_element_type=F32)
    xb = jnp.dot(xo_scr[...], wxo_ref[...], preferred_element_type=F32)
    dm = ret.shape[1]
    merged = (jax.nn.sigmoid(gl_ref[:, :dm].astype(F32)) * ret
              + jax.nn.sigmoid(gl_ref[:, dm:2 * dm].astype(F32)) * ssm
              + jax.nn.sigmoid(gl_ref[:, 2 * dm:].astype(F32)) * xb)
    h_ref[...] = x_ref[...] + jnp.dot(merged.astype(BF16), wout_ref[...], preferred_element_type=F32)

    @pl.when(lt == pl.num_programs(1) - 1)
    def _():
        sout_ref[...] = s_scr[...]


def mixer(x, z, y_ssm, pos, mem_k, mem_v, s0, ret_gn, w_ret_o, w_ssm_glu, w_ssm_o, w_x_o, w_out, *, nb, tl):
    bsz, length, dm = x.shape
    chunk = min(CHUNK, length)
    nl = length // tl
    rows = nb * tl
    qk = RET_HEADS * HEAD_DIM
    sw = y_ssm.shape[1]
    xw = X_HEADS * HEAD_DIM
    gate_col = (4 * qk + sw + xw)
    assert gate_col % (3 * dm) == 0 and (4 * qk + sw) % xw == 0
    dmask, qw, kw, tile_decay = retention_tables(tl, chunk)
    cosf, sinf = rope_tables(pos)
    scale = HEAD_DIM ** -0.5
    row_map = lambda b, l: (b * nl + l, 0)
    tab_map = lambda b, l: (l, 0)
    st_spec = pl.BlockSpec((nb, RET_HEADS, HEAD_DIM, HEAD_DIM), lambda b, l: (b, 0, 0, 0))
    mem_spec = pl.BlockSpec((nb, mem_k.shape[1], xw), lambda b, l: (b, 0, 0))
    h, s_out = pl.pallas_call(
        functools.partial(_mixer_kernel, nb=nb, tl=tl, tile_decay=tile_decay),
        grid=(bsz // nb, nl),
        in_specs=[pl.BlockSpec((rows, dm), row_map),
                  pl.BlockSpec((rows, 4 * qk), row_map),
                  pl.BlockSpec((rows, xw), lambda b, l: (b * nl + l, (4 * qk + sw) // xw)),
                  pl.BlockSpec((rows, 3 * dm), lambda b, l: (b * nl + l, gate_col // (3 * dm))),
                  pl.BlockSpec((rows, sw), row_map),
                  pl.BlockSpec((tl, HEAD_DIM), tab_map), pl.BlockSpec((tl, HEAD_DIM), tab_map),
                  pl.BlockSpec((tl, HEAD_DIM), tab_map), pl.BlockSpec((tl, HEAD_DIM), tab_map),
                  _resident(dmask.shape), _resident(qw.shape), _resident(kw.shape),
                  mem_spec, mem_spec, st_spec, _resident((1, qk)),
                  _resident(w_ret_o.shape), _resident(w_ssm_glu.shape), _resident(w_ssm_o.shape),
                  _resident(w_x_o.shape), _resident(w_out.shape)],
        out_specs=[pl.BlockSpec((rows, dm), row_map), st_spec],
        out_shape=[jax.ShapeDtypeStruct((bsz * length, dm), F32),
                   jax.ShapeDtypeStruct(s0.shape, F32)],
        scratch_shapes=[pltpu.VMEM((nb, RET_HEADS, HEAD_DIM, HEAD_DIM), F32),
                        pltpu.VMEM((rows, qk), BF16), pltpu.VMEM((rows, xw), BF16)],
        compiler_params=_params(2), name="mixer",
    )(x.reshape(bsz * length, dm), z, z, z, y_ssm,
      cosf * scale, sinf * scale, cosf, sinf, dmask, qw, kw,
      mem_k, mem_v, s0, ret_gn.reshape(1, qk), w_ret_o, w_ssm_glu, w_ssm_o, w_x_o, w_out)
    return h, s_out


def _router_kernel(hp_ref, hs_ref, nw_ref, wrt_ref, br_ref, tri_ref, low_ref,
                   xn_ref, gw_ref, crow_ref, cnt_ref, *, prompt_tiles):
    h = jnp.where(pl.program_id(0) < prompt_tiles, hp_ref[...], hs_ref[...])
    xn = _rms(h, nw_ref[...])
    xn_ref[...] = xn.astype(BF16)
    logits = lax.dot_general(wrt_ref[...], xn, NT_DIMS, precision=lax.Precision.HIGHEST,
                             preferred_element_type=F32) + br_ref[...]
    ne = logits.shape[0]
    iota = lax.broadcasted_iota(jnp.int32, logits.shape, 0)
    rest = logits
    sel = jnp.zeros(logits.shape, jnp.bool_)
    vals, idxs = [], []
    for _ in range(TOP_K):
        m = jnp.max(rest, axis=0, keepdims=True)
        ix = jnp.min(jnp.where(rest == m, iota, ne), axis=0, keepdims=True)
        hit = iota == ix
        vals.append(m)
        idxs.append(ix)
        sel = jnp.logical_or(sel, hit)
        rest = jnp.where(hit, -jnp.inf, rest)
    es = [jnp.exp(v - vals[0]) for v in vals]
    tot = es[0] + es[1] + es[2] + es[3]
    before = jnp.dot(sel.astype(BF16), tri_ref[...], preferred_element_type=F32)
    cnt = jnp.sum(sel.astype(F32), axis=1, keepdims=True)
    seg = jnp.floor((cnt + (ROW_ALIGN - 1.0)) * (1.0 / ROW_ALIGN)) * ROW_ALIGN
    start = jnp.dot(low_ref[...], jnp.broadcast_to(seg, before.shape), precision=lax.Precision.HIGHEST,
                    preferred_element_type=F32)
    place = start + before
    for k in range(TOP_K):
        gw_ref[k:k + 1, :] = es[k] / tot
        crow_ref[k:k + 1, :] = jnp.sum(jnp.where(iota == idxs[k], place, 0.0), axis=0,
                                       keepdims=True).astype(jnp.int32)
    cnt_ref[...] = jnp.broadcast_to(cnt, cnt_ref.shape)


def router(h_p, h_s, norm_ffn, w_router, b_router):
    dm = h_p.shape[1]
    ne = w_router.shape[1]
    tt = ROW_TILE
    npt, nst = h_p.shape[0] // tt, h_s.shape[0] // tt
    t = (npt + nst) * tt
    tri = jnp.asarray(np.triu(np.ones((tt, tt), np.float32), k=1), BF16)
    low = jnp.asarray(np.tril(np.ones((ne, ne), np.float32), k=-1))
    tok_spec = pl.BlockSpec((TOP_K, tt), lambda i: (0, i))
    return pl.pallas_call(
        functools.partial(_router_kernel, prompt_tiles=npt), grid=(npt + nst,),
        in_specs=[pl.BlockSpec((tt, dm), lambda i: (jnp.minimum(i, npt - 1), 0)),
                  pl.BlockSpec((tt, dm), lambda i: (jnp.maximum(i - npt, 0), 0)),
                  _resident((1, dm)), _resident((ne, dm)), _resident((ne, 1)), _resident((tt, tt)),
                  _resident((ne, ne))],
        out_specs=[pl.BlockSpec((tt, dm), lambda i: (i, 0)), tok_spec, tok_spec,
                   pl.BlockSpec((ne, 128), lambda i: (i, 0))],
        out_shape=[jax.ShapeDtypeStruct((t, dm), BF16),
                   jax.ShapeDtypeStruct((TOP_K, t), F32), jax.ShapeDtypeStruct((TOP_K, t), jnp.int32),
                   jax.ShapeDtypeStruct(((npt + nst) * ne, 128), F32)],
        compiler_params=_params(), name="router",
    )(h_p, h_s, norm_ffn.reshape(1, dm), w_router.T, b_router.reshape(ne, 1), tri, low)


def _pack_bf16_pairs(x):
    n = x.shape[1] // 2
    lo = lax.bitcast_convert_type(x[:, :n].astype(BF16).astype(F32), jnp.int32)
    hi = lax.bitcast_convert_type(x[:, n:].astype(BF16).astype(F32), jnp.int32)
    return lax.shift_right_logical(lo, 16) | (hi & -65536)


def _unpack_bf16_pairs(u):
    lo = lax.bitcast_convert_type(lax.shift_left(u, 16), F32).astype(BF16)
    hi = lax.bitcast_convert_type(u & -65536, F32).astype(BF16)
    return lo, hi


def _segment_loop(tile, ne, nch_ref, fn):
    def per_expert(e, carry):
        base = tile * ne + e

        def per_chunk(j, c2):
            fn(base, j)
            return c2

        lax.fori_loop(0, nch_ref[base], per_chunk, 0)
        return carry

    lax.fori_loop(0, ne, per_expert, 0)


def _dispatch_kernel(seg_ref, off_ref, nch_ref, tch_ref, zs_ref, zn_ref, used_ref,
                     xn_ref, crow_ref, xs_ref, cbuf, zbuf, sems, zsem, *, ne, bm):
    i = pl.program_id(0)
    nt = pl.num_programs(0)
    slot = lax.rem(i, 2)
    tt = xn_ref.shape[0]
    cr = cbuf.shape[1]
    ra = ROW_ALIGN
    n_blocks = xs_ref.shape[0] // bm

    def chunk_copy(sl, src_row, dst_row):
        return pltpu.make_async_copy(cbuf.at[sl, pl.ds(src_row, ra)], xs_ref.at[pl.ds(dst_row, ra)], sems.at[sl])

    def wait_chunks(sl, n):
        def body(c, carry):
            chunk_copy(sl, 0, 0).wait()
            return carry
        lax.fori_loop(0, n, body, 0)

    def tail_copy(e, j):
        return pltpu.make_async_copy(zbuf.at[pl.ds(0, ra)],
                                     xs_ref.at[pl.ds(pl.multiple_of(zs_ref[e] + ra * j, ra), ra)], zsem)

    def block_copy(b):
        return pltpu.make_async_copy(zbuf, xs_ref.at[pl.ds(pl.multiple_of(b * bm, bm), bm)], zsem)

    def zero_fill(start):
        def per_expert(e, carry):
            def per_chunk(j, c2):
                (tail_copy(e, j).start() if start else tail_copy(e, j).wait())
                return c2
            lax.fori_loop(0, zn_ref[e], per_chunk, 0)
            return carry
        lax.fori_loop(0, ne, per_expert, 0)

        def per_block(b, carry):
            (block_copy(b).start() if start else block_copy(b).wait())
            return carry
        lax.fori_loop(used_ref[0], n_blocks, per_block, 0)

    @pl.when(i == 0)
    def _():
        zbuf[...] = jnp.zeros_like(zbuf)
        zero_fill(True)
        zero_fill(False)

    @pl.when(i >= 2)
    def _():
        wait_chunks(slot, tch_ref[jnp.maximum(i - 2, 0)])

    crow = crow_ref[...]
    rows = lax.broadcasted_iota(jnp.int32, (cr, tt), 0)
    hit = rows == crow[0:1, :]
    for k in range(1, TOP_K):
        hit = jnp.logical_or(hit, rows == crow[k:k + 1, :])
    packed = _pack_bf16_pairs(jnp.dot(jnp.where(hit, 1.0, 0.0).astype(BF16), xn_ref[...],
                                      preferred_element_type=F32))
    cbuf[slot] = packed

    def start_chunk(base, j):
        chunk_copy(slot, pl.multiple_of(off_ref[base] + ra * j, ra),
                   pl.multiple_of(seg_ref[base] + ra * j, ra)).start()

    _segment_loop(i, ne, nch_ref, start_chunk)

    @pl.when(i == nt - 1)
    def _():
        wait_chunks(slot, tch_ref[i])
        wait_chunks(1 - slot, jnp.where(nt >= 2, tch_ref[jnp.maximum(i - 1, 0)], 0))


def dispatch(xn, crow, tables, n_rows, ne):
    t, dm = xn.shape
    tt = ROW_TILE
    bm = FFN_BLOCK
    cr = _compact_rows(ne)
    grid_spec = pltpu.PrefetchScalarGridSpec(
        num_scalar_prefetch=7, grid=(t // tt,),
        in_specs=[pl.BlockSpec((tt, dm), lambda i, *_: (i, 0)),
                  pl.BlockSpec((TOP_K, tt), lambda i, *_: (0, i))],
        out_specs=pl.BlockSpec(memory_space=pl.ANY),
        scratch_shapes=[pltpu.VMEM((2, cr, dm // 2), jnp.int32), pltpu.VMEM((bm, dm // 2), jnp.int32),
                        pltpu.SemaphoreType.DMA((2,)), pltpu.SemaphoreType.DMA(())])
    return pl.pallas_call(
        functools.partial(_dispatch_kernel, ne=ne, bm=bm), grid_spec=grid_spec,
        out_shape=jax.ShapeDtypeStruct((n_rows, dm // 2), jnp.int32),
        compiler_params=_params(), name="dispatch",
    )(*tables, xn, crow)


def _ffn_kernel(be_ref, first_ref, blk_ref, used_ref, xs_ref, wgu_ref, bgu_ref, wd_ref, bd_ref,
                y_ref, wgu_bf, wd_bf):
    del be_ref, blk_ref
    i = pl.program_id(0)

    @pl.when(first_ref[i] == 1)
    def _():
        wgu_bf[...] = wgu_ref[0].astype(BF16)
        wd_bf[...] = wd_ref[0].astype(BF16)

    @pl.when(i < used_ref[0])
    def _():
        x_lo, x_hi = _unpack_bf16_pairs(xs_ref[...])
        kh = x_lo.shape[1]
        hgu = (jnp.dot(x_lo, wgu_bf[:kh, :], preferred_element_type=F32)
               + jnp.dot(x_hi, wgu_bf[kh:, :], preferred_element_type=F32) + bgu_ref[0])
        ff = hgu.shape[1] // 2
        gate = jnp.minimum(hgu[:, :ff], SWIGLU_LIMIT)
        up = jnp.clip(hgu[:, ff:], -SWIGLU_LIMIT, SWIGLU_LIMIT)
        act = (up + 1.0) * gate * jax.nn.sigmoid(SWIGLU_ALPHA * gate)
        y_ref[...] = _pack_bf16_pairs(jnp.dot(act.astype(BF16), wd_bf[...], preferred_element_type=F32)
                                      + bd_ref[0])

    @pl.when(i >= used_ref[0])
    def _():
        y_ref[...] = jnp.zeros_like(y_ref)


def expert_ffn(xs, block_e, block_first, block_idx, n_used, w_gate_up, b_gate_up, w_down, b_down):
    n_rows = xs.shape[0]
    ne, dm, ff2 = w_gate_up.shape
    bm = FFN_BLOCK
    grid_spec = pltpu.PrefetchScalarGridSpec(
        num_scalar_prefetch=4, grid=(n_rows // bm,),
        in_specs=[pl.BlockSpec((bm, dm // 2), lambda i, be, bf, bi, nu: (bi[i], 0)),
                  pl.BlockSpec((1, dm, ff2), lambda i, be, bf, bi, nu: (be[i], 0, 0)),
                  pl.BlockSpec((1, 1, ff2), lambda i, be, bf, bi, nu: (be[i], 0, 0)),
                  pl.BlockSpec((1, ff2 // 2, dm), lambda i, be, bf, bi, nu: (be[i], 0, 0)),
                  pl.BlockSpec((1, 1, dm), lambda i, be, bf, bi, nu: (be[i], 0, 0))],
        out_specs=pl.BlockSpec((bm, dm // 2), lambda i, be, bf, bi, nu: (i, 0)),
        scratch_shapes=[pltpu.VMEM((dm, ff2), BF16), pltpu.VMEM((ff2 // 2, dm), BF16)])
    return pl.pallas_call(
        _ffn_kernel, grid_spec=grid_spec,
        out_shape=jax.ShapeDtypeStruct((n_rows, dm // 2), jnp.int32),
        compiler_params=_params(), name="expert_ffn",
    )(block_e, block_first, block_idx, n_used, xs, w_gate_up, b_gate_up.reshape(ne, 1, ff2),
      w_down, b_down.reshape(ne, 1, dm))


def _combine_kernel(seg_ref, off_ref, nch_ref, tch_ref,
                    crow_ref, gw_ref, hp_ref, hs_ref, fn_ref, yr_ref, yp_ref, ys_ref, ybuf, sems,
                    *, ne, prompt_tiles):
    i = pl.program_id(0)
    nt = pl.num_programs(0)
    slot = lax.rem(i, 2)
    tt = hp_ref.shape[0]
    cr = ybuf.shape[1]
    ra = ROW_ALIGN

    def chunk_copy(sl, src_row, dst_row):
        return pltpu.make_async_copy(yr_ref.at[pl.ds(src_row, ra)], ybuf.at[sl, pl.ds(dst_row, ra)], sems.at[sl])

    def fetch(tile, sl):
        def start_chunk(base, j):
            chunk_copy(sl, pl.multiple_of(seg_ref[base] + ra * j, ra),
                       pl.multiple_of(off_ref[base] + ra * j, ra)).start()
        _segment_loop(tile, ne, nch_ref, start_chunk)

    @pl.when(i == 0)
    def _():
        ybuf[...] = jnp.zeros_like(ybuf)
        fetch(0, 0)

    @pl.when(i + 1 < nt)
    def _():
        fetch(i + 1, 1 - slot)

    def wait_one(c, carry):
        chunk_copy(slot, 0, 0).wait()
        return carry
    lax.fori_loop(0, tch_ref[i], wait_one, 0)

    y_lo, y_hi = _unpack_bf16_pairs(ybuf[slot])
    cols = lax.broadcasted_iota(jnp.int32, (tt, cr), 1)
    q = jnp.zeros((tt, cr), F32)
    for k in range(TOP_K):
        q += jnp.where(cols == crow_ref[:, k:k + 1], gw_ref[:, k:k + 1], 0.0)
    qb = q.astype(BF16)
    moe = jnp.concatenate([jnp.dot(qb, y_lo, preferred_element_type=F32),
                           jnp.dot(qb, y_hi, preferred_element_type=F32)], axis=1)
    h = jnp.where(i < prompt_tiles, hp_ref[...], hs_ref[...])
    out = _rms(h + moe, fn_ref[...])

    @pl.when(i < prompt_tiles)
    def _():
        yp_ref[...] = out

    @pl.when(i >= prompt_tiles)
    def _():
        ys_ref[...] = out


def combine(crow_t, gw_t, h_p, h_s, final_norm, y_rows, tables, ne):
    dm = h_p.shape[1]
    tt = ROW_TILE
    npt, nst = h_p.shape[0] // tt, h_s.shape[0] // tt
    cr = _compact_rows(ne)
    p_map = lambda i, *_: (jnp.minimum(i, npt - 1), 0)
    s_map = lambda i, *_: (jnp.maximum(i - npt, 0), 0)
    grid_spec = pltpu.PrefetchScalarGridSpec(
        num_scalar_prefetch=4, grid=(npt + nst,),
        in_specs=[pl.BlockSpec((tt, TOP_K), lambda i, *_: (i, 0)),
                  pl.BlockSpec((tt, TOP_K), lambda i, *_: (i, 0)),
                  pl.BlockSpec((tt, dm), p_map), pl.BlockSpec((tt, dm), s_map),
                  pl.BlockSpec((1, dm), lambda i, *_: (0, 0)),
                  pl.BlockSpec(memory_space=pl.ANY)],
        out_specs=[pl.BlockSpec((tt, dm), p_map), pl.BlockSpec((tt, dm), s_map)],
        scratch_shapes=[pltpu.VMEM((2, cr, dm // 2), jnp.int32), pltpu.SemaphoreType.DMA((2,))])
    return pl.pallas_call(
        functools.partial(_combine_kernel, ne=ne, prompt_tiles=npt), grid_spec=grid_spec,
        out_shape=[jax.ShapeDtypeStruct(h_p.shape, F32), jax.ShapeDtypeStruct(h_s.shape, F32)],
        compiler_params=_params(), name="combine",
    )(*tables, crow_t, gw_t, h_p, h_s, final_norm.reshape(1, dm), y_rows)


def _compact_rows(ne):
    return -(-(TOP_K * ROW_TILE + ne * (ROW_ALIGN - 1)) // 128) * 128


def moe_and_final_norm(h_p, h_s, norm_ffn, w_router, b_router, w_gate_up, b_gate_up, w_down, b_down, final_norm):
    ne = w_router.shape[1]
    bm = FFN_BLOCK
    ra = ROW_ALIGN
    xn, gw, crow, cnt = router(h_p, h_s, norm_ffn, w_router, b_router)
    t = xn.shape[0]
    nt = t // ROW_TILE
    seg = -(-cnt[:, 0].astype(jnp.int32).reshape(nt, ne) // ra) * ra
    seg_before = jnp.cumsum(seg, axis=0) - seg
    rows_e = jnp.sum(seg, axis=0)
    padded = -(-rows_e // bm) * bm
    pad_ends = jnp.cumsum(padded)
    pad_starts = pad_ends - padded
    n_blocks = -(-(t * TOP_K + nt * ne * (ra - 1) + ne * (bm - 1)) // bm)
    n_used = jnp.maximum(pad_ends[-1] // bm, 1)
    blocks = jnp.arange(n_blocks, dtype=jnp.int32)
    block_idx = jnp.minimum(blocks, n_used - 1)
    block_e = jnp.minimum(jnp.sum((pad_ends[None, :] <= (block_idx * bm)[:, None]).astype(jnp.int32), axis=1),
                          ne - 1)
    block_first = (jnp.any(block_idx[:, None] * bm == pad_starts[None, :], axis=1)
                   & (blocks < n_used)).astype(jnp.int32)
    seg_tables = ((pad_starts[None, :] + seg_before).reshape(-1),
                  (jnp.cumsum(seg, axis=1) - seg).reshape(-1),
                  (seg // ra).reshape(-1), jnp.sum(seg // ra, axis=1))
    fill_tables = (pad_starts + rows_e, (padded - rows_e) // ra, n_used.reshape(1))
    to_i32 = lambda xs: tuple(x.astype(jnp.int32) for x in xs)
    xs = dispatch(xn, crow, to_i32(seg_tables + fill_tables), n_blocks * bm, ne)
    y_rows = expert_ffn(xs, block_e.astype(jnp.int32), block_first, block_idx.astype(jnp.int32),
                        n_used.reshape(1).astype(jnp.int32), w_gate_up, b_gate_up, w_down, b_down)
    return combine(crow.T, gw.T, h_p, h_s, final_norm, y_rows, to_i32(seg_tables), ne)


def kernel(x_prompt, x_sample, cache_mem_k, cache_mem_v, state_ret, state_ssm_re, state_ssm_im, mem_prompt, norm_mix, w_in, ret_gn, w_ret_o, ssm_lam_re, ssm_lam_im, ssm_log_dt, ssm_b_re, ssm_b_im, ssm_c_re, ssm_c_im, ssm_d, w_ssm_glu, w_ssm_o, mem_norm, w_mem_kv, w_x_o, w_out, norm_ffn, w_router, b_router, w_gate_up, b_gate_up, w_down, b_down, final_norm):
    assert norm_mix.shape[0] == 1, "single-layer step"
    bp, lp, dm = x_prompt.shape
    bs, ls, _ = x_sample.shape
    n_mem = mem_prompt.shape[1]
    xw = X_HEADS * HEAD_DIM
    qk = RET_HEADS * HEAD_DIM
    sw = ssm_d.shape[1]
    g = ssm_lam_re.shape[1]

    w_in_b = w_in[0].astype(BF16)
    tables = s5_tables(ssm_lam_re[0], ssm_lam_im[0], ssm_log_dt[0], ssm_b_re[0], ssm_b_im[0],
                       ssm_c_re[0], ssm_c_im[0], ssm_d[0])
    mix_w = (ret_gn[0], w_ret_o[0].astype(BF16), w_ssm_glu[0].astype(BF16), w_ssm_o[0].astype(BF16),
             w_x_o[0].astype(BF16), w_out[0].astype(BF16))

    kv = norm_matmul(mem_prompt.reshape(bp * n_mem, dm), mem_norm[0], w_mem_kv[0].astype(BF16), F32)
    mk_p = kv[:, :xw].reshape(bp, n_mem, xw)
    mv_p = kv[:, xw:].reshape(bp, n_mem, xw)

    def group(x, pos, mem_k, mem_v, s_ret, h_re, h_im, nb, tl):
        bsz, length, _ = x.shape
        z, u = norm_matmul(x.reshape(bsz * length, dm), norm_mix[0], w_in_b, BF16,
                           f32_cols=(4 * qk, 4 * qk + sw))
        y, hf_re, hf_im = s5_apply(u, bsz, h_re, h_im, tables)
        h, s_new = mixer(x, z, y, pos, mem_k, mem_v, s_ret, *mix_w, nb=nb, tl=tl)
        return h, s_new, hf_re, hf_im

    zero_ret = jnp.zeros((bp, RET_HEADS, HEAD_DIM, HEAD_DIM), F32)
    zero_ssm = jnp.zeros((bp, g, SSM_STATE), F32)
    h_p, ret_p, sre_p, sim_p = group(x_prompt, jnp.arange(lp, dtype=jnp.int32), mk_p, mv_p,
                                     zero_ret, zero_ssm, zero_ssm, 1, ROW_TILE)
    h_s, ret_s, sre_s, sim_s = group(x_sample, PAST_LEN + jnp.arange(ls, dtype=jnp.int32),
                                     cache_mem_k[0].reshape(bs, n_mem, xw), cache_mem_v[0].reshape(bs, n_mem, xw),
                                     state_ret[0], state_ssm_re[0], state_ssm_im[0], ROW_TILE // ls, ls)

    y_p, y_s = moe_and_final_norm(h_p, h_s, norm_ffn[0], w_router[0], b_router[0],
                                  w_gate_up[0], b_gate_up[0], w_down[0], b_down[0], final_norm)
    return (y_p.reshape(bp, lp, dm), y_s.reshape(bs, ls, dm), ret_p[None], sre_p[None], sim_p[None],
            mk_p.reshape(1, bp, n_mem, X_HEADS, HEAD_DIM), mv_p.reshape(1, bp, n_mem, X_HEADS, HEAD_DIM),
            ret_s[None], sre_s[None], sim_s[None])
```

```python
import functools
import math

import jax
import jax.numpy as jnp
import numpy as np
from jax import lax
from jax.experimental import pallas as pl
from jax.experimental.pallas import tpu as pltpu

F32 = jnp.float32
BF16 = jnp.bfloat16

EPS = 1e-6
CHUNK = 64
PAST_LEN = 2048
ROPE_BASE = 10000.0
RET_HEADS = 4
X_HEADS = 4
HEAD_DIM = 128
SSM_GROUP = 16
SSM_STATE = 64
TOP_K = 4
SWIGLU_ALPHA = 1.702
SWIGLU_LIMIT = 7.0

VMEM_LIMIT = 52 * 1024 * 1024
S5_CHUNK = 8
S5_LANES = 128
ROW_TILE = 256
FFN_BLOCK = 256
ROW_ALIGN = 8
NT_DIMS = (((1,), (1,)), ((), ()))
TN_DIMS = (((0,), (0,)), ((), ()))


def _params(n_axes=1):
    return pltpu.CompilerParams(dimension_semantics=("arbitrary",) * n_axes,
                                vmem_limit_bytes=VMEM_LIMIT)


def _resident(shape):
    nd = len(shape)
    return pl.BlockSpec(shape, lambda *_: (0,) * nd, pipeline_mode=pl.Buffered(1))


def _rms(x, w):
    return x * lax.rsqrt(jnp.mean(x * x, axis=-1, keepdims=True) + EPS) * w


def _norm_matmul_kernel(x_ref, nw_ref, w_ref, o_ref, *f32_refs, n_chunk, f32_cols):
    xb = _rms(x_ref[...], nw_ref[...]).astype(BF16)
    for n0 in range(0, o_ref.shape[1], n_chunk):
        r = jnp.dot(xb, w_ref[:, n0:n0 + n_chunk], preferred_element_type=F32)
        o_ref[:, n0:n0 + n_chunk] = r.astype(o_ref.dtype)
        if f32_cols is not None and n0 <= f32_cols[0] and f32_cols[1] <= n0 + n_chunk:
            f32_refs[0][...] = r[:, f32_cols[0] - n0:f32_cols[1] - n0]


def norm_matmul(x, nw, w, out_dtype, f32_cols=None):
    t, d = x.shape
    n = w.shape[1]
    n_chunk = min(n, 1024)
    out_specs = [pl.BlockSpec((ROW_TILE, n), lambda i: (i, 0))]
    out_shape = [jax.ShapeDtypeStruct((t, n), out_dtype)]
    if f32_cols is not None:
        lo, hi = f32_cols
        assert lo // n_chunk == (hi - 1) // n_chunk
        out_specs.append(pl.BlockSpec((ROW_TILE, hi - lo), lambda i: (i, 0)))
        out_shape.append(jax.ShapeDtypeStruct((t, hi - lo), F32))
    out = pl.pallas_call(
        functools.partial(_norm_matmul_kernel, n_chunk=n_chunk, f32_cols=f32_cols),
        grid=(t // ROW_TILE,),
        in_specs=[pl.BlockSpec((ROW_TILE, d), lambda i: (i, 0)), _resident((1, d)), _resident((d, n))],
        out_specs=out_specs, out_shape=out_shape,
        compiler_params=_params(), name="norm_matmul",
    )(x, nw.reshape(1, d), w)
    return out if f32_cols is not None else out[0]


def s5_tables(lam_re, lam_im, log_dt, b_re, b_im, c_re, c_im, d_skip):
    g, n, p = b_re.shape
    s = S5_CHUNK
    gl = S5_LANES // p
    j = g // gl
    hi = lax.Precision.HIGHEST
    dt = jnp.exp(log_dt)[:, None]
    a_re = jnp.exp(lam_re * dt) * jnp.cos(lam_im * dt)
    a_im = jnp.exp(lam_re * dt) * jnp.sin(lam_im * dt)
    den = lam_re * lam_re + lam_im * lam_im
    nr, ni = a_re - 1.0, a_im
    co_re = (nr * lam_re + ni * lam_im) / den
    co_im = (ni * lam_re - nr * lam_im) / den
    bb_re = co_re[..., None] * b_re - co_im[..., None] * b_im
    bb_im = co_re[..., None] * b_im + co_im[..., None] * b_re
    tau = jnp.arange(s + 1, dtype=F32)[:, None, None]
    pw_mag = jnp.exp(lam_re * dt * tau)
    pw_re = pw_mag * jnp.cos(lam_im * dt * tau)
    pw_im = pw_mag * jnp.sin(lam_im * dt * tau)
    ca_re = c_re[None] * pw_re[:, :, None, :] - c_im[None] * pw_im[:, :, None, :]
    ca_im = c_re[None] * pw_im[:, :, None, :] + c_im[None] * pw_re[:, :, None, :]
    kq = (jnp.einsum('tgpn,gnq->gtqp', ca_re[:s], bb_re, precision=hi)
          - jnp.einsum('tgpn,gnq->gtqp', ca_im[:s], bb_im, precision=hi))
    ts = np.arange(s)
    lag_onehot = (ts[None, None, :] - ts[None, :, None] == ts[:, None, None]).astype(np.float32)
    rev = s - 1 - ts
    w_re = pw_re[rev][:, :, :, None] * bb_re[None] - pw_im[rev][:, :, :, None] * bb_im[None]
    w_im = pw_re[rev][:, :, :, None] * bb_im[None] + pw_im[rev][:, :, :, None] * bb_re[None]
    m_c = (jnp.einsum('gxqp,xst->gsqtp', kq, lag_onehot, precision=hi)
           .reshape(j, gl, s, p, s * p).transpose(0, 2, 1, 3, 4).reshape(j, s * gl * p, s * p))
    w_c = (jnp.stack([w_re, w_im]).reshape(2, s, j, gl, n, p).transpose(2, 1, 3, 5, 0, 4)
           .reshape(j, s * gl * p, 2 * n))
    v_c = (jnp.stack([ca_re[1:], -ca_im[1:]]).reshape(2, s, j, gl, p, n).transpose(2, 0, 3, 5, 1, 4)
           .reshape(j, 2 * gl * n, s * p))
    fl = s * gl * p
    c_io = np.arange(fl)
    c_st = np.arange(2 * gl * n)
    k_io = np.arange(s * p)
    k_st = np.arange(2 * n)
    spread_io = ((k_io[:, None] // p == c_io[None, :] // (gl * p)) & (k_io[:, None] % p == c_io[None, :] % p))
    spread_st = ((k_st[:, None] // n == c_st[None, :] // (gl * n)) & (k_st[:, None] % n == c_st[None, :] % n))
    grp_io = (c_io // p) % gl
    grp_st = (c_st // n) % gl

    def expand(compact, spread, row_grp, col_grp):
        full = jnp.einsum('jrk,kc->jrc', compact.astype(BF16), jnp.asarray(spread, BF16),
                          preferred_element_type=F32)
        return jnp.where(jnp.asarray(row_grp[:, None] == col_grp[None, :]), full, 0.0).astype(BF16)

    m = expand(m_c, spread_io, grp_io, grp_io)
    w = expand(w_c, spread_st, grp_io, grp_st)
    v = expand(v_c, spread_io, grp_st, grp_io)
    a_s_re = pw_re[s].reshape(1, g * n)
    a_s_im = pw_im[s].reshape(1, g * n)
    dtab = jnp.broadcast_to(d_skip.reshape(j, 1, 1, gl * p), (j, 1, s, gl * p)).reshape(j, 1, s * gl * p)
    return m, w, v, a_s_re, a_s_im, dtab


def _s5_flat(u_ref):
    return jnp.concatenate([u_ref[:, t, :] for t in range(u_ref.shape[1])], axis=1)


def _s5a_kernel(u_ref, w_ref, ire_ref, iim_ref):
    r = jnp.dot(_s5_flat(u_ref).astype(BF16), w_ref[0], preferred_element_type=F32)
    half = r.shape[1] // 2
    ire_ref[...] = r[:, :half]
    iim_ref[...] = r[:, half:]


def _s5scan_kernel(ire_ref, iim_ref, ar_ref, ai_ref, h0r_ref, h0i_ref,
                   hpr_ref, hpi_ref, hfr_ref, hfi_ref):
    nb, nc, _ = ire_ref.shape
    ar, ai = ar_ref[...], ai_ref[...]

    def body(c, carry):
        out = []
        for b in range(nb):
            hr, hi = carry[2 * b], carry[2 * b + 1]
            hpr_ref[b, pl.ds(c, 1), :] = hr
            hpi_ref[b, pl.ds(c, 1), :] = hi
            out.append(ar * hr - ai * hi + ire_ref[b, pl.ds(c, 1), :])
            out.append(ar * hi + ai * hr + iim_ref[b, pl.ds(c, 1), :])
        return tuple(out)

    init = []
    for b in range(nb):
        init += [h0r_ref[b], h0i_ref[b]]
    fin = lax.fori_loop(0, nc, body, tuple(init))
    for b in range(nb):
        hfr_ref[b] = fin[2 * b]
        hfi_ref[b] = fin[2 * b + 1]


def _s5b_kernel(u_ref, hpr_ref, hpi_ref, m_ref, v_ref, d_ref, y_ref):
    uf = _s5_flat(u_ref)
    half = hpr_ref.shape[1]
    y = d_ref[0] * uf + jnp.dot(uf.astype(BF16), m_ref[0], preferred_element_type=F32)
    y += jnp.dot(hpr_ref[...].astype(BF16), v_ref[0, :half, :], preferred_element_type=F32)
    y += jnp.dot(hpi_ref[...].astype(BF16), v_ref[0, half:, :], preferred_element_type=F32)
    lanes = y_ref.shape[2]
    for t in range(y_ref.shape[1]):
        y_ref[:, t, :] = y[:, t * lanes:(t + 1) * lanes]


def s5_apply(u, bsz, h0_re, h0_im, tables):
    m, w, v, a_re, a_im, dtab = tables
    tokens, width = u.shape
    nj = m.shape[0]
    s = S5_CHUNK
    rows = tokens // s
    nc = rows // bsz
    lanes = a_re.shape[1]
    half = w.shape[2] // 2
    fl = m.shape[1]
    rt = min(rows, 512)
    u3 = u.reshape(rows, s, width)
    u_spec = pl.BlockSpec((rt, s, S5_LANES), lambda j, r: (r, 0, j))
    st_spec = pl.BlockSpec((rt, half), lambda j, r: (r, j))
    tab_spec = pl.BlockSpec((1, fl, fl), lambda j, r: (j, 0, 0))
    inj_re, inj_im = pl.pallas_call(
        _s5a_kernel, grid=(nj, rows // rt),
        in_specs=[u_spec, tab_spec],
        out_specs=[st_spec, st_spec],
        out_shape=[jax.ShapeDtypeStruct((rows, lanes), F32)] * 2,
        compiler_params=_params(2), name="s5_chunk_in",
    )(u3, w)

    sb, lw = 4, 512
    seq_spec = pl.BlockSpec((sb, nc, lw), lambda b, l: (b, 0, l))
    vec_spec = pl.BlockSpec((sb, 1, lw), lambda b, l: (b, 0, l))
    atab_spec = pl.BlockSpec((1, lw), lambda b, l: (0, l))
    hp_re, hp_im, hf_re, hf_im = pl.pallas_call(
        _s5scan_kernel, grid=(bsz // sb, lanes // lw),
        in_specs=[seq_spec, seq_spec, atab_spec, atab_spec, vec_spec, vec_spec],
        out_specs=[seq_spec, seq_spec, vec_spec, vec_spec],
        out_shape=[jax.ShapeDtypeStruct((bsz, nc, lanes), F32)] * 2
        + [jax.ShapeDtypeStruct((bsz, 1, lanes), F32)] * 2,
        compiler_params=_params(2), name="s5_scan",
    )(inj_re.reshape(bsz, nc, lanes), inj_im.reshape(bsz, nc, lanes), a_re, a_im,
      h0_re.reshape(bsz, 1, lanes), h0_im.reshape(bsz, 1, lanes))

    y3 = pl.pallas_call(
        _s5b_kernel, grid=(nj, rows // rt),
        in_specs=[u_spec, st_spec, st_spec, tab_spec, tab_spec,
                  pl.BlockSpec((1, 1, fl), lambda j, r: (j, 0, 0))],
        out_specs=u_spec,
        out_shape=jax.ShapeDtypeStruct((rows, s, width), F32),
        compiler_params=_params(2), name="s5_chunk_out",
    )(u3, hp_re.reshape(rows, lanes), hp_im.reshape(rows, lanes), m, v, dtab)
    g = lanes // SSM_STATE
    return y3.reshape(tokens, width), hf_re.reshape(bsz, g, SSM_STATE), hf_im.reshape(bsz, g, SSM_STATE)


def _retention_gammas():
    return 1.0 - np.exp2(-5.0 - np.arange(RET_HEADS, dtype=np.float64))


def retention_tables(tile, chunk):
    gam = _retention_gammas()[:, None, None]
    i = np.arange(tile)[:, None]
    j = np.arange(tile)[None, :]
    same = (i // chunk) == (j // chunk)
    earlier = (j // chunk) < (i // chunk)
    dist = np.where(same, np.abs(i - j), np.where(earlier, i - j, 0))
    dmask = np.where(same | earlier, gam ** dist[None], 0.0)
    qw = np.broadcast_to((gam[:, :, 0] ** (np.arange(tile) + 1.0))[:, :, None], (RET_HEADS, tile, HEAD_DIM))
    kw = np.broadcast_to((gam[:, :, 0] ** (tile - 1.0 - np.arange(tile)))[:, :, None], (RET_HEADS, tile, HEAD_DIM))
    return (jnp.asarray(dmask, F32), jnp.asarray(qw, F32), jnp.asarray(kw, F32),
            tuple(float(x) for x in _retention_gammas() ** tile))


def rope_tables(pos):
    half = HEAD_DIM // 2
    inv = jnp.exp(-math.log(ROPE_BASE) * 2.0 * jnp.arange(half, dtype=F32) / HEAD_DIM)
    ang = pos.astype(F32)[:, None] * inv[None, :]
    cos, sin = jnp.cos(ang), jnp.sin(ang)
    cosf = jnp.concatenate([cos, cos], axis=1)
    sinf = jnp.concatenate([-sin, sin], axis=1)
    return cosf, sinf


def _mixer_kernel(x_ref, zq_ref, xq_ref, gl_ref, y_ref, cq_ref, sq_ref, ck_ref, sk_ref,
                  dm_ref, qw_ref, kw_ref, mk_ref, mv_ref, s0_ref, gn_ref,
                  wro_ref, wglu_ref, wso_ref, wxo_ref, wout_ref,
                  h_ref, sout_ref, s_scr, o_scr, xo_scr, *, nb, tl, tile_decay):
    lt = pl.program_id(1)
    hd = HEAD_DIM
    qk = RET_HEADS * hd

    @pl.when(lt == 0)
    def _():
        s_scr[...] = s0_ref[...]

    cq, sq, ck, sk = cq_ref[...], sq_ref[...], ck_ref[...], sk_ref[...]
    for n in range(nb):
        rows = slice(n * tl, (n + 1) * tl)
        for h in range(RET_HEADS):
            c0 = h * hd
            q = zq_ref[rows, c0:c0 + hd].astype(F32)
            k = zq_ref[rows, qk + c0:qk + c0 + hd].astype(F32)
            v = zq_ref[rows, 2 * qk + c0:2 * qk + c0 + hd]
            g = zq_ref[rows, 3 * qk + c0:3 * qk + c0 + hd].astype(F32)
            qr = q * cq + pltpu.roll(q, hd // 2, 1) * sq
            kr = k * ck + pltpu.roll(k, hd // 2, 1) * sk
            sc = lax.dot_general(qr.astype(BF16), kr.astype(BF16), NT_DIMS,
                                 preferred_element_type=F32) * dm_ref[h]
            o = jnp.dot(sc.astype(BF16), v, preferred_element_type=F32)
            s_old = s_scr[n, h]
            o += jnp.dot((qr * qw_ref[h]).astype(BF16), s_old.astype(BF16), preferred_element_type=F32)
            kv = lax.dot_general((kr * kw_ref[h]).astype(BF16), v, TN_DIMS, preferred_element_type=F32)
            s_scr[n, h] = tile_decay[h] * s_old + kv
            d = o - jnp.mean(o, axis=-1, keepdims=True)
            on = d * lax.rsqrt(jnp.mean(d * d, axis=-1, keepdims=True) + EPS) * gn_ref[:, c0:c0 + hd]
            o_scr[rows, c0:c0 + hd] = (on * (g * jax.nn.sigmoid(g))).astype(BF16)
            mem_rows = pl.ds(h, mk_ref.shape[1] // X_HEADS, stride=X_HEADS)
            mkh = mk_ref[n, mem_rows, :].astype(BF16)
            mvh = mv_ref[n, mem_rows, :].astype(BF16)
            s = lax.dot_general(xq_ref[rows, c0:c0 + hd], mkh, NT_DIMS,
                                preferred_element_type=F32) * (hd ** -0.5)
            e = jnp.exp(s - jnp.max(s, axis=-1, keepdims=True))
            p = e / jnp.sum(e, axis=-1, keepdims=True)
            xo_scr[rows, c0:c0 + hd] = jnp.dot(p.astype(BF16), mvh, preferred_element_type=F32).astype(BF16)

    ret = jnp.dot(o_scr[...], wro_ref[...], preferred_element_type=F32)
    zg = jnp.dot(jax.nn.gelu(y_ref[...]).astype(BF16), wglu_ref[...], preferred_element_type=F32)
    half = zg.shape[1] // 2
    glu = (zg[:, :half] * jax.nn.sigmoid(zg[:, half:])).astype(BF16)
    ssm = jnp.dot(glu, wso_ref[...], preferred_element_type=F32)
    xb = jnp.dot(xo_scr[...], wxo_ref[...], preferred_element_type=F32)
    dm = ret.shape[1]
    merged = (jax.nn.sigmoid(gl_ref[:, :dm].astype(F32)) * ret
              + jax.nn.sigmoid(gl_ref[:, dm:2 * dm].astype(F32)) * ssm
              + jax.nn.sigmoid(gl_ref[:, 2 * dm:].astype(F32)) * xb)
    h_ref[...] = x_ref[...] + jnp.dot(merged.astype(BF16), wout_ref[...], preferred_element_type=F32)

    @pl.when(lt == pl.num_programs(1) - 1)
    def _():
        sout_ref[...] = s_scr[...]


def mixer(x, z, y_ssm, pos, mem_k, mem_v, s0, ret_gn, w_ret_o, w_ssm_glu, w_ssm_o, w_x_o, w_out, *, nb, tl):
    bsz, length, dm = x.shape
    chunk = min(CHUNK, length)
    nl = length // tl
    rows = nb * tl
    qk = RET_HEADS * HEAD_DIM
    sw = y_ssm.shape[1]
    xw = X_HEADS * HEAD_DIM
    gate_col = (4 * qk + sw + xw)
    assert gate_col % (3 * dm) == 0 and (4 * qk + sw) % xw == 0
    dmask, qw, kw, tile_decay = retention_tables(tl, chunk)
    cosf, sinf = rope_tables(pos)
    scale = HEAD_DIM ** -0.5
    row_map = lambda b, l: (b * nl + l, 0)
    tab_map = lambda b, l: (l, 0)
    st_spec = pl.BlockSpec((nb, RET_HEADS, HEAD_DIM, HEAD_DIM), lambda b, l: (b, 0, 0, 0))
    mem_spec = pl.BlockSpec((nb,) + mem_k.shape[1:], lambda b, l: (b,) + (0,) * (mem_k.ndim - 1))
    h, s_out = pl.pallas_call(
        functools.partial(_mixer_kernel, nb=nb, tl=tl, tile_decay=tile_decay),
        grid=(bsz // nb, nl),
        in_specs=[pl.BlockSpec((rows, dm), row_map),
                  pl.BlockSpec((rows, 4 * qk), row_map),
                  pl.BlockSpec((rows, xw), lambda b, l: (b * nl + l, (4 * qk + sw) // xw)),
                  pl.BlockSpec((rows, 3 * dm), lambda b, l: (b * nl + l, gate_col // (3 * dm))),
                  pl.BlockSpec((rows, sw), row_map),
                  pl.BlockSpec((tl, HEAD_DIM), tab_map), pl.BlockSpec((tl, HEAD_DIM), tab_map),
                  pl.BlockSpec((tl, HEAD_DIM), tab_map), pl.BlockSpec((tl, HEAD_DIM), tab_map),
                  _resident(dmask.shape), _resident(qw.shape), _resident(kw.shape),
                  mem_spec, mem_spec, st_spec, _resident((1, qk)),
                  _resident(w_ret_o.shape), _resident(w_ssm_glu.shape), _resident(w_ssm_o.shape),
                  _resident(w_x_o.shape), _resident(w_out.shape)],
        out_specs=[pl.BlockSpec((rows, dm), row_map), st_spec],
        out_shape=[jax.ShapeDtypeStruct((bsz * length, dm), F32),
                   jax.ShapeDtypeStruct(s0.shape, F32)],
        scratch_shapes=[pltpu.VMEM((nb, RET_HEADS, HEAD_DIM, HEAD_DIM), F32),
                        pltpu.VMEM((rows, qk), BF16), pltpu.VMEM((rows, xw), BF16)],
        compiler_params=_params(2), name="mixer",
    )(x.reshape(bsz * length, dm), z, z, z, y_ssm,
      cosf * scale, sinf * scale, cosf, sinf, dmask, qw, kw,
      mem_k, mem_v, s0, ret_gn.reshape(1, qk), w_ret_o, w_ssm_glu, w_ssm_o, w_x_o, w_out)
    return h, s_out


def _router_kernel(hp_ref, hs_ref, nw_ref, wrt_ref, br_ref, tri_ref, low_ref,
                   xn_ref, gw_ref, crow_ref, cnt_ref, *, prompt_tiles):
    h = jnp.where(pl.program_id(0) < prompt_tiles, hp_ref[...], hs_ref[...])
    xn = _rms(h, nw_ref[...])
    xn_ref[...] = xn.astype(BF16)
    logits = lax.dot_general(wrt_ref[...], xn, NT_DIMS, precision=lax.Precision.HIGHEST,
                             preferred_element_type=F32) + br_ref[...]
    ne = logits.shape[0]
    iota = lax.broadcasted_iota(jnp.int32, logits.shape, 0)
    rest = logits
    sel = jnp.zeros(logits.shape, jnp.bool_)
    vals, idxs = [], []
    for _ in range(TOP_K):
        m = jnp.max(rest, axis=0, keepdims=True)
        ix = jnp.min(jnp.where(rest == m, iota, ne), axis=0, keepdims=True)
        hit = iota == ix
        vals.append(m)
        idxs.append(ix)
        sel = jnp.logical_or(sel, hit)
        rest = jnp.where(hit, -jnp.inf, rest)
    es = [jnp.exp(v - vals[0]) for v in vals]
    tot = es[0] + es[1] + es[2] + es[3]
    before = jnp.dot(sel.astype(BF16), tri_ref[...], preferred_element_type=F32)
    cnt = jnp.sum(sel.astype(F32), axis=1, keepdims=True)
    seg = jnp.floor((cnt + (ROW_ALIGN - 1.0)) * (1.0 / ROW_ALIGN)) * ROW_ALIGN
    start = jnp.dot(low_ref[...], jnp.broadcast_to(seg, before.shape), precision=lax.Precision.HIGHEST,
                    preferred_element_type=F32)
    place = start + before
    for k in range(TOP_K):
        gw_ref[k:k + 1, :] = es[k] / tot
        crow_ref[k:k + 1, :] = jnp.sum(jnp.where(iota == idxs[k], place, 0.0), axis=0,
                                       keepdims=True).astype(jnp.int32)
    cnt_ref[...] = jnp.broadcast_to(cnt, cnt_ref.shape)


def router(h_p, h_s, norm_ffn, w_router, b_router):
    dm = h_p.shape[1]
    ne = w_router.shape[1]
    tt = ROW_TILE
    npt, nst = h_p.shape[0] // tt, h_s.shape[0] // tt
    t = (npt + nst) * tt
    tri = jnp.asarray(np.triu(np.ones((tt, tt), np.float32), k=1), BF16)
    low = jnp.asarray(np.tril(np.ones((ne, ne), np.float32), k=-1))
    tok_spec = pl.BlockSpec((TOP_K, tt), lambda i: (0, i))
    return pl.pallas_call(
        functools.partial(_router_kernel, prompt_tiles=npt), grid=(npt + nst,),
        in_specs=[pl.BlockSpec((tt, dm), lambda i: (jnp.minimum(i, npt - 1), 0)),
                  pl.BlockSpec((tt, dm), lambda i: (jnp.maximum(i - npt, 0), 0)),
                  _resident((1, dm)), _resident((ne, dm)), _resident((ne, 1)), _resident((tt, tt)),
                  _resident((ne, ne))],
        out_specs=[pl.BlockSpec((tt, dm), lambda i: (i, 0)), tok_spec, tok_spec,
                   pl.BlockSpec((ne, 128), lambda i: (i, 0))],
        out_shape=[jax.ShapeDtypeStruct((t, dm), BF16),
                   jax.ShapeDtypeStruct((TOP_K, t), F32), jax.ShapeDtypeStruct((TOP_K, t), jnp.int32),
                   jax.ShapeDtypeStruct(((npt + nst) * ne, 128), F32)],
        compiler_params=_params(), name="router",
    )(h_p, h_s, norm_ffn.reshape(1, dm), w_router.T, b_router.reshape(ne, 1), tri, low)


def _pack_bf16_pairs(x):
    n = x.shape[1] // 2
    lo = lax.bitcast_convert_type(x[:, :n].astype(BF16).astype(F32), jnp.int32)
    hi = lax.bitcast_convert_type(x[:, n:].astype(BF16).astype(F32), jnp.int32)
    return lax.shift_right_logical(lo, 16) | (hi & -65536)


def _unpack_bf16_pairs(u):
    lo = lax.bitcast_convert_type(lax.shift_left(u, 16), F32).astype(BF16)
    hi = lax.bitcast_convert_type(u & -65536, F32).astype(BF16)
    return lo, hi


def _split_count(n, fn):
    def quad(jq, carry):
        fn(4 * jq, 4)
        return carry

    lax.fori_loop(0, n // 4, quad, 0)

    @pl.when(n % 4 >= 2)
    def _():
        fn(n // 4 * 4, 2)

    @pl.when(n % 2 == 1)
    def _():
        fn(n // 2 * 2, 1)


def _segment_loop(tile, ne, nch_ref, fn):
    def per_expert(e, carry):
        base = tile * ne + e
        _split_count(nch_ref[base], lambda j, m: fn(base, ROW_ALIGN * j, ROW_ALIGN * m))
        return carry

    lax.fori_loop(0, ne, per_expert, 0)


def _dispatch_kernel(seg_ref, off_ref, nch_ref, tch_ref, zs_ref, zn_ref, used_ref,
                     xn_ref, crow_ref, xs_ref, cbuf, zbuf, sems, zsem, *, ne, bm):
    i = pl.program_id(0)
    nt = pl.num_programs(0)
    slot = lax.rem(i, 2)
    tt = xn_ref.shape[0]
    cr = cbuf.shape[1]
    ra = ROW_ALIGN
    n_blocks = xs_ref.shape[0] // bm

    def chunk_copy(sl, src_row, dst_row, rows):
        return pltpu.make_async_copy(cbuf.at[sl, pl.ds(src_row, rows)], xs_ref.at[pl.ds(dst_row, rows)],
                                     sems.at[sl])

    def wait_chunks(sl, n):
        _split_count(n, lambda j, m: chunk_copy(sl, 0, 0, ra * m).wait())

    def tail_copy(e, j):
        return pltpu.make_async_copy(zbuf.at[pl.ds(0, ra)],
                                     xs_ref.at[pl.ds(pl.multiple_of(zs_ref[e] + ra * j, ra), ra)], zsem)

    def block_copy(b):
        return pltpu.make_async_copy(zbuf, xs_ref.at[pl.ds(pl.multiple_of(b * bm, bm), bm)], zsem)

    def zero_fill(start):
        def per_expert(e, carry):
            def per_chunk(j, c2):
                (tail_copy(e, j).start() if start else tail_copy(e, j).wait())
                return c2
            lax.fori_loop(0, zn_ref[e], per_chunk, 0)
            return carry
        lax.fori_loop(0, ne, per_expert, 0)

        def per_block(b, carry):
            (block_copy(b).start() if start else block_copy(b).wait())
            return carry
        lax.fori_loop(used_ref[0], n_blocks, per_block, 0)

    @pl.when(i == 0)
    def _():
        zbuf[...] = jnp.zeros_like(zbuf)
        zero_fill(True)
        zero_fill(False)

    @pl.when(i >= 2)
    def _():
        wait_chunks(slot, tch_ref[jnp.maximum(i - 2, 0)])

    crow = crow_ref[...]
    rows = lax.broadcasted_iota(jnp.int32, (cr, tt), 0)
    hit = rows == crow[0:1, :]
    for k in range(1, TOP_K):
        hit = jnp.logical_or(hit, rows == crow[k:k + 1, :])
    packed = _pack_bf16_pairs(jnp.dot(jnp.where(hit, 1.0, 0.0).astype(BF16), xn_ref[...],
                                      preferred_element_type=F32))
    cbuf[slot] = packed

    def start_piece(base, row, rows):
        chunk_copy(slot, pl.multiple_of(off_ref[base] + row, ra),
                   pl.multiple_of(seg_ref[base] + row, ra), rows).start()

    _segment_loop(i, ne, nch_ref, start_piece)

    @pl.when(i == nt - 1)
    def _():
        wait_chunks(slot, tch_ref[i])
        wait_chunks(1 - slot, jnp.where(nt >= 2, tch_ref[jnp.maximum(i - 1, 0)], 0))


def dispatch(xn, crow, tables, n_rows, ne):
    t, dm = xn.shape
    tt = ROW_TILE
    bm = FFN_BLOCK
    cr = _compact_rows(ne)
    grid_spec = pltpu.PrefetchScalarGridSpec(
        num_scalar_prefetch=7, grid=(t // tt,),
        in_specs=[pl.BlockSpec((tt, dm), lambda i, *_: (i, 0)),
                  pl.BlockSpec((TOP_K, tt), lambda i, *_: (0, i))],
        out_specs=pl.BlockSpec(memory_space=pl.ANY),
        scratch_shapes=[pltpu.VMEM((2, cr, dm // 2), jnp.int32), pltpu.VMEM((bm, dm // 2), jnp.int32),
                        pltpu.SemaphoreType.DMA((2,)), pltpu.SemaphoreType.DMA(())])
    return pl.pallas_call(
        functools.partial(_dispatch_kernel, ne=ne, bm=bm), grid_spec=grid_spec,
        out_shape=jax.ShapeDtypeStruct((n_rows, dm // 2), jnp.int32),
        compiler_params=_params(), name="dispatch",
    )(*tables, xn, crow)


def _ffn_kernel(be_ref, first_ref, blk_ref, used_ref, next_ref, wslot_ref,
                xs_ref, wgu_hbm, bgu_ref, wd_hbm, bd_ref, y_ref, wgu_f32, wd_f32, wgu_bf, wd_bf, sems):
    del blk_ref
    i = pl.program_id(0)

    def weight_copies(e, sl):
        return (pltpu.make_async_copy(wgu_hbm.at[e], wgu_f32.at[sl], sems.at[0, sl]),
                pltpu.make_async_copy(wd_hbm.at[e], wd_f32.at[sl], sems.at[1, sl]))

    @pl.when(first_ref[i] == 1)
    def _():
        sl = wslot_ref[i]

        @pl.when(i == 0)
        def _():
            for c in weight_copies(be_ref[i], sl):
                c.start()

        for c in weight_copies(be_ref[i], sl):
            c.wait()
        wgu_bf[...] = wgu_f32[sl].astype(BF16)
        wd_bf[...] = wd_f32[sl].astype(BF16)

        @pl.when(next_ref[i] >= 0)
        def _():
            for c in weight_copies(next_ref[i], 1 - sl):
                c.start()

    @pl.when(i < used_ref[0])
    def _():
        x_lo, x_hi = _unpack_bf16_pairs(xs_ref[...])
        kh = x_lo.shape[1]
        hgu = (jnp.dot(x_lo, wgu_bf[:kh, :], preferred_element_type=F32)
               + jnp.dot(x_hi, wgu_bf[kh:, :], preferred_element_type=F32) + bgu_ref[0])
        ff = hgu.shape[1] // 2
        gate = jnp.minimum(hgu[:, :ff], SWIGLU_LIMIT)
        up = jnp.clip(hgu[:, ff:], -SWIGLU_LIMIT, SWIGLU_LIMIT)
        act = (up + 1.0) * gate * jax.nn.sigmoid(SWIGLU_ALPHA * gate)
        y_ref[...] = _pack_bf16_pairs(jnp.dot(act.astype(BF16), wd_bf[...], preferred_element_type=F32)
                                      + bd_ref[0])

    @pl.when(i >= used_ref[0])
    def _():
        y_ref[...] = jnp.zeros_like(y_ref)


def expert_ffn(xs, block_tables, w_gate_up, b_gate_up, w_down, b_down):
    n_rows = xs.shape[0]
    ne, dm, ff2 = w_gate_up.shape
    bm = FFN_BLOCK
    grid_spec = pltpu.PrefetchScalarGridSpec(
        num_scalar_prefetch=6, grid=(n_rows // bm,),
        in_specs=[pl.BlockSpec((bm, dm // 2), lambda i, be, bf, bi, *_: (bi[i], 0)),
                  pl.BlockSpec(memory_space=pl.ANY),
                  pl.BlockSpec((1, 1, ff2), lambda i, be, *_: (be[i], 0, 0)),
                  pl.BlockSpec(memory_space=pl.ANY),
                  pl.BlockSpec((1, 1, dm), lambda i, be, *_: (be[i], 0, 0))],
        out_specs=pl.BlockSpec((bm, dm // 2), lambda i, *_: (i, 0)),
        scratch_shapes=[pltpu.VMEM((2, dm, ff2), F32), pltpu.VMEM((2, ff2 // 2, dm), F32),
                        pltpu.VMEM((dm, ff2), BF16), pltpu.VMEM((ff2 // 2, dm), BF16),
                        pltpu.SemaphoreType.DMA((2, 2))])
    return pl.pallas_call(
        _ffn_kernel, grid_spec=grid_spec,
        out_shape=jax.ShapeDtypeStruct((n_rows, dm // 2), jnp.int32),
        compiler_params=_params(), name="expert_ffn",
    )(*block_tables, xs, w_gate_up, b_gate_up.reshape(ne, 1, ff2), w_down, b_down.reshape(ne, 1, dm))


def _combine_kernel(seg_ref, off_ref, nch_ref, tch_ref,
                    crow_ref, gw_ref, hp_ref, hs_ref, fn_ref, yr_ref, yp_ref, ys_ref, ybuf, sems,
                    *, ne, prompt_tiles):
    i = pl.program_id(0)
    nt = pl.num_programs(0)
    slot = lax.rem(i, 2)
    tt = hp_ref.shape[0]
    cr = ybuf.shape[1]
    ra = ROW_ALIGN

    def chunk_copy(sl, src_row, dst_row, rows):
        return pltpu.make_async_copy(yr_ref.at[pl.ds(src_row, rows)], ybuf.at[sl, pl.ds(dst_row, rows)],
                                     sems.at[sl])

    def fetch(tile, sl):
        def start_piece(base, row, rows):
            chunk_copy(sl, pl.multiple_of(seg_ref[base] + row, ra),
                       pl.multiple_of(off_ref[base] + row, ra), rows).start()
        _segment_loop(tile, ne, nch_ref, start_piece)

    @pl.when(i == 0)
    def _():
        ybuf[...] = jnp.zeros_like(ybuf)
        fetch(0, 0)

    @pl.when(i + 1 < nt)
    def _():
        fetch(i + 1, 1 - slot)

    _split_count(tch_ref[i], lambda j, m: chunk_copy(slot, 0, 0, ra * m).wait())

    y_lo, y_hi = _unpack_bf16_pairs(ybuf[slot])
    cols = lax.broadcasted_iota(jnp.int32, (tt, cr), 1)
    q = jnp.zeros((tt, cr), F32)
    for k in range(TOP_K):
        q += jnp.where(cols == crow_ref[:, k:k + 1], gw_ref[:, k:k + 1], 0.0)
    qb = q.astype(BF16)
    moe = jnp.concatenate([jnp.dot(qb, y_lo, preferred_element_type=F32),
                           jnp.dot(qb, y_hi, preferred_element_type=F32)], axis=1)
    h = jnp.where(i < prompt_tiles, hp_ref[...], hs_ref[...])
    out = _rms(h + moe, fn_ref[...])

    @pl.when(i < prompt_tiles)
    def _():
        yp_ref[...] = out

    @pl.when(i >= prompt_tiles)
    def _():
        ys_ref[...] = out


def combine(crow_t, gw_t, h_p, h_s, final_norm, y_rows, tables, ne):
    dm = h_p.shape[1]
    tt = ROW_TILE
    npt, nst = h_p.shape[0] // tt, h_s.shape[0] // tt
    cr = _compact_rows(ne)
    p_map = lambda i, *_: (jnp.minimum(i, npt - 1), 0)
    s_map = lambda i, *_: (jnp.maximum(i - npt, 0), 0)
    grid_spec = pltpu.PrefetchScalarGridSpec(
        num_scalar_prefetch=4, grid=(npt + nst,),
        in_specs=[pl.BlockSpec((tt, TOP_K), lambda i, *_: (i, 0)),
                  pl.BlockSpec((tt, TOP_K), lambda i, *_: (i, 0)),
                  pl.BlockSpec((tt, dm), p_map), pl.BlockSpec((tt, dm), s_map),
                  pl.BlockSpec((1, dm), lambda i, *_: (0, 0)),
                  pl.BlockSpec(memory_space=pl.ANY)],
        out_specs=[pl.BlockSpec((tt, dm), p_map), pl.BlockSpec((tt, dm), s_map)],
        scratch_shapes=[pltpu.VMEM((2, cr, dm // 2), jnp.int32), pltpu.SemaphoreType.DMA((2,))])
    return pl.pallas_call(
        functools.partial(_combine_kernel, ne=ne, prompt_tiles=npt), grid_spec=grid_spec,
        out_shape=[jax.ShapeDtypeStruct(h_p.shape, F32), jax.ShapeDtypeStruct(h_s.shape, F32)],
        compiler_params=_params(), name="combine",
    )(*tables, crow_t, gw_t, h_p, h_s, final_norm.reshape(1, dm), y_rows)


def _compact_rows(ne):
    return -(-(TOP_K * ROW_TILE + ne * (ROW_ALIGN - 1)) // 128) * 128


def moe_and_final_norm(h_p, h_s, norm_ffn, w_router, b_router, w_gate_up, b_gate_up, w_down, b_down, final_norm):
    ne = w_router.shape[1]
    bm = FFN_BLOCK
    ra = ROW_ALIGN
    xn, gw, crow, cnt = router(h_p, h_s, norm_ffn, w_router, b_router)
    t = xn.shape[0]
    nt = t // ROW_TILE
    seg = -(-cnt[:, 0].astype(jnp.int32).reshape(nt, ne) // ra) * ra
    seg_before = jnp.cumsum(seg, axis=0) - seg
    rows_e = jnp.sum(seg, axis=0)
    padded = -(-rows_e // bm) * bm
    pad_ends = jnp.cumsum(padded)
    pad_starts = pad_ends - padded
    n_blocks = -(-(t * TOP_K + nt * ne * (ra - 1) + ne * (bm - 1)) // bm)
    n_used = jnp.maximum(pad_ends[-1] // bm, 1)
    blocks = jnp.arange(n_blocks, dtype=jnp.int32)
    block_idx = jnp.minimum(blocks, n_used - 1)
    block_e = jnp.minimum(jnp.sum((pad_ends[None, :] <= (block_idx * bm)[:, None]).astype(jnp.int32), axis=1),
                          ne - 1)
    block_first = (jnp.any(block_idx[:, None] * bm == pad_starts[None, :], axis=1)
                   & (blocks < n_used)).astype(jnp.int32)
    experts = jnp.arange(ne, dtype=jnp.int32)
    active = padded > 0
    later = jnp.where(active, experts, ne)
    next_active = jnp.concatenate([lax.cummin(later, reverse=True)[1:], jnp.full((1,), ne, jnp.int32)])
    next_active = jnp.where(next_active < ne, next_active, -1)
    slot_e = (jnp.cumsum(active.astype(jnp.int32)) - active.astype(jnp.int32)) % 2
    is_e = block_e[:, None] == experts[None, :]
    block_tables = (block_e, block_first, block_idx, n_used.reshape(1),
                    jnp.sum(jnp.where(is_e, next_active[None, :], 0), axis=1),
                    jnp.sum(jnp.where(is_e, slot_e[None, :], 0), axis=1))
    seg_tables = ((pad_starts[None, :] + seg_before).reshape(-1),
                  (jnp.cumsum(seg, axis=1) - seg).reshape(-1),
                  (seg // ra).reshape(-1), jnp.sum(seg // ra, axis=1))
    fill_tables = (pad_starts + rows_e, (padded - rows_e) // ra, n_used.reshape(1))
    to_i32 = lambda xs: tuple(x.astype(jnp.int32) for x in xs)
    xs = dispatch(xn, crow, to_i32(seg_tables + fill_tables), n_blocks * bm, ne)
    y_rows = expert_ffn(xs, to_i32(block_tables), w_gate_up, b_gate_up, w_down, b_down)
    return combine(crow.T, gw.T, h_p, h_s, final_norm, y_rows, to_i32(seg_tables), ne)


def kernel(x_prompt, x_sample, cache_mem_k, cache_mem_v, state_ret, state_ssm_re, state_ssm_im, mem_prompt, norm_mix, w_in, ret_gn, w_ret_o, ssm_lam_re, ssm_lam_im, ssm_log_dt, ssm_b_re, ssm_b_im, ssm_c_re, ssm_c_im, ssm_d, w_ssm_glu, w_ssm_o, mem_norm, w_mem_kv, w_x_o, w_out, norm_ffn, w_router, b_router, w_gate_up, b_gate_up, w_down, b_down, final_norm):
    assert norm_mix.shape[0] == 1, "single-layer step"
    bp, lp, dm = x_prompt.shape
    bs, ls, _ = x_sample.shape
    n_mem = mem_prompt.shape[1]
    xw = X_HEADS * HEAD_DIM
    qk = RET_HEADS * HEAD_DIM
    sw = ssm_d.shape[1]
    g = ssm_lam_re.shape[1]

    w_in_b = w_in[0].astype(BF16)
    tables = s5_tables(ssm_lam_re[0], ssm_lam_im[0], ssm_log_dt[0], ssm_b_re[0], ssm_b_im[0],
                       ssm_c_re[0], ssm_c_im[0], ssm_d[0])
    mix_w = (ret_gn[0], w_ret_o[0].astype(BF16), w_ssm_glu[0].astype(BF16), w_ssm_o[0].astype(BF16),
             w_x_o[0].astype(BF16), w_out[0].astype(BF16))

    kv = norm_matmul(mem_prompt.reshape(bp * n_mem, dm), mem_norm[0], w_mem_kv[0].astype(BF16), F32)
    mk_p = kv[:, :xw].reshape(bp, n_mem * X_HEADS, HEAD_DIM)
    mv_p = kv[:, xw:].reshape(bp, n_mem * X_HEADS, HEAD_DIM)

    def group(x, pos, mem_k, mem_v, s_ret, h_re, h_im, nb, tl):
        bsz, length, _ = x.shape
        z, u = norm_matmul(x.reshape(bsz * length, dm), norm_mix[0], w_in_b, BF16,
                           f32_cols=(4 * qk, 4 * qk + sw))
        y, hf_re, hf_im = s5_apply(u, bsz, h_re, h_im, tables)
        h, s_new = mixer(x, z, y, pos, mem_k, mem_v, s_ret, *mix_w, nb=nb, tl=tl)
        return h, s_new, hf_re, hf_im

    zero_ret = jnp.zeros((bp, RET_HEADS, HEAD_DIM, HEAD_DIM), F32)
    zero_ssm = jnp.zeros((bp, g, SSM_STATE), F32)
    h_p, ret_p, sre_p, sim_p = group(x_prompt, jnp.arange(lp, dtype=jnp.int32), mk_p, mv_p,
                                     zero_ret, zero_ssm, zero_ssm, 1, ROW_TILE)
    h_s, ret_s, sre_s, sim_s = group(x_sample, PAST_LEN + jnp.arange(ls, dtype=jnp.int32),
                                     cache_mem_k[0].reshape(bs, n_mem * X_HEADS, HEAD_DIM),
                                     cache_mem_v[0].reshape(bs, n_mem * X_HEADS, HEAD_DIM),
                                     state_ret[0], state_ssm_re[0], state_ssm_im[0], ROW_TILE // ls, ls)

    y_p, y_s = moe_and_final_norm(h_p, h_s, norm_ffn[0], w_router[0], b_router[0],
                                  w_gate_up[0], b_gate_up[0], w_down[0], b_down[0], final_norm)
    return (y_p.reshape(bp, lp, dm), y_s.reshape(bs, ls, dm), ret_p[None], sre_p[None], sim_p[None],
            mk_p.reshape(1, bp, n_mem, X_HEADS, HEAD_DIM), mv_p.reshape(1, bp, n_mem, X_HEADS, HEAD_DIM),
            ret_s[None], sre_s[None], sim_s[None])
```

```python
import functools
import math

import jax
import jax.numpy as jnp
import numpy as np
from jax import lax
from jax.experimental import pallas as pl
from jax.experimental.pallas import tpu as pltpu

F32 = jnp.float32
BF16 = jnp.bfloat16

EPS = 1e-6
CHUNK = 64
PAST_LEN = 2048
ROPE_BASE = 10000.0
RET_HEADS = 4
X_HEADS = 4
HEAD_DIM = 128
SSM_GROUP = 16
SSM_STATE = 64
TOP_K = 4
SWIGLU_ALPHA = 1.702
SWIGLU_LIMIT = 7.0

VMEM_LIMIT = 52 * 1024 * 1024
S5_CHUNK = 8
S5_LANES = 128
ROW_TILE = 256
FFN_BLOCK = 256
ROW_ALIGN = 8
MIX_COLS = 256
NT_DIMS = (((1,), (1,)), ((), ()))
TN_DIMS = (((0,), (0,)), ((), ()))


def _params(n_axes=1):
    return pltpu.CompilerParams(dimension_semantics=("arbitrary",) * n_axes,
                                vmem_limit_bytes=VMEM_LIMIT)


def _resident(shape):
    nd = len(shape)
    return pl.BlockSpec(shape, lambda *_: (0,) * nd, pipeline_mode=pl.Buffered(1))


def _rms(x, w):
    return x * lax.rsqrt(jnp.mean(x * x, axis=-1, keepdims=True) + EPS) * w


def _sigmoid(x):
    return 0.5 * jnp.tanh(0.5 * x) + 0.5


_ACTIVATIONS = {"sigmoid": _sigmoid, "silu": lambda v: v * _sigmoid(v)}


def _norm_matmul_kernel(x_ref, nw_ref, w_ref, o_ref, *f32_refs, n_chunk, f32_cols, acts):
    xb = _rms(x_ref[...], nw_ref[...]).astype(BF16)
    for n0 in range(0, o_ref.shape[1], n_chunk):
        r = jnp.dot(xb, w_ref[:, n0:n0 + n_chunk], preferred_element_type=F32)
        if f32_cols is not None and n0 <= f32_cols[0] and f32_cols[1] <= n0 + n_chunk:
            f32_refs[0][...] = r[:, f32_cols[0] - n0:f32_cols[1] - n0]
        cuts = sorted({n0, n0 + n_chunk} | {c for lo, hi, _ in acts for c in (lo, hi) if n0 < c < n0 + n_chunk})
        for a, b in zip(cuts[:-1], cuts[1:]):
            piece = r[:, a - n0:b - n0]
            for lo, hi, kind in acts:
                if lo <= a and b <= hi:
                    piece = _ACTIVATIONS[kind](piece)
            o_ref[:, a:b] = piece.astype(o_ref.dtype)


def norm_matmul(x, nw, w, out_dtype, f32_cols=None, acts=()):
    t, d = x.shape
    n = w.shape[1]
    n_chunk = min(n, 1024)
    out_specs = [pl.BlockSpec((ROW_TILE, n), lambda i: (i, 0))]
    out_shape = [jax.ShapeDtypeStruct((t, n), out_dtype)]
    if f32_cols is not None:
        lo, hi = f32_cols
        assert lo // n_chunk == (hi - 1) // n_chunk
        out_specs.append(pl.BlockSpec((ROW_TILE, hi - lo), lambda i: (i, 0)))
        out_shape.append(jax.ShapeDtypeStruct((t, hi - lo), F32))
    out = pl.pallas_call(
        functools.partial(_norm_matmul_kernel, n_chunk=n_chunk, f32_cols=f32_cols, acts=tuple(acts)),
        grid=(t // ROW_TILE,),
        in_specs=[pl.BlockSpec((ROW_TILE, d), lambda i: (i, 0)), _resident((1, d)), _resident((d, n))],
        out_specs=out_specs, out_shape=out_shape,
        compiler_params=_params(), name="norm_matmul",
    )(x, nw.reshape(1, d), w)
    return out if f32_cols is not None else out[0]


def s5_tables(lam_re, lam_im, log_dt, b_re, b_im, c_re, c_im, d_skip):
    g, n, p = b_re.shape
    s = S5_CHUNK
    gl = S5_LANES // p
    j = g // gl
    hi = lax.Precision.HIGHEST
    dt = jnp.exp(log_dt)[:, None]
    a_re = jnp.exp(lam_re * dt) * jnp.cos(lam_im * dt)
    a_im = jnp.exp(lam_re * dt) * jnp.sin(lam_im * dt)
    den = lam_re * lam_re + lam_im * lam_im
    nr, ni = a_re - 1.0, a_im
    co_re = (nr * lam_re + ni * lam_im) / den
    co_im = (ni * lam_re - nr * lam_im) / den
    bb_re = co_re[..., None] * b_re - co_im[..., None] * b_im
    bb_im = co_re[..., None] * b_im + co_im[..., None] * b_re
    tau = jnp.arange(s + 1, dtype=F32)[:, None, None]
    pw_mag = jnp.exp(lam_re * dt * tau)
    pw_re = pw_mag * jnp.cos(lam_im * dt * tau)
    pw_im = pw_mag * jnp.sin(lam_im * dt * tau)
    ca_re = c_re[None] * pw_re[:, :, None, :] - c_im[None] * pw_im[:, :, None, :]
    ca_im = c_re[None] * pw_im[:, :, None, :] + c_im[None] * pw_re[:, :, None, :]
    kq = (jnp.einsum('tgpn,gnq->gtqp', ca_re[:s], bb_re, precision=hi)
          - jnp.einsum('tgpn,gnq->gtqp', ca_im[:s], bb_im, precision=hi))
    ts = np.arange(s)
    lag_onehot = (ts[None, None, :] - ts[None, :, None] == ts[:, None, None]).astype(np.float32)
    rev = s - 1 - ts
    w_re = pw_re[rev][:, :, :, None] * bb_re[None] - pw_im[rev][:, :, :, None] * bb_im[None]
    w_im = pw_re[rev][:, :, :, None] * bb_im[None] + pw_im[rev][:, :, :, None] * bb_re[None]
    m_c = (jnp.einsum('gxqp,xst->gsqtp', kq, lag_onehot, precision=hi)
           .reshape(j, gl, s, p, s * p).transpose(0, 2, 1, 3, 4).reshape(j, s * gl * p, s * p))
    w_c = (jnp.stack([w_re, w_im]).reshape(2, s, j, gl, n, p).transpose(2, 1, 3, 5, 0, 4)
           .reshape(j, s * gl * p, 2 * n))
    v_c = (jnp.stack([ca_re[1:], -ca_im[1:]]).reshape(2, s, j, gl, p, n).transpose(2, 0, 3, 5, 1, 4)
           .reshape(j, 2 * gl * n, s * p))
    fl = s * gl * p
    c_io = np.arange(fl)
    c_st = np.arange(2 * gl * n)
    k_io = np.arange(s * p)
    k_st = np.arange(2 * n)
    spread_io = ((k_io[:, None] // p == c_io[None, :] // (gl * p)) & (k_io[:, None] % p == c_io[None, :] % p))
    spread_st = ((k_st[:, None] // n == c_st[None, :] // (gl * n)) & (k_st[:, None] % n == c_st[None, :] % n))
    grp_io = (c_io // p) % gl
    grp_st = (c_st // n) % gl

    def expand(compact, spread, row_grp, col_grp):
        full = jnp.einsum('jrk,kc->jrc', compact.astype(BF16), jnp.asarray(spread, BF16),
                          preferred_element_type=F32)
        return jnp.where(jnp.asarray(row_grp[:, None] == col_grp[None, :]), full, 0.0).astype(BF16)

    m = expand(m_c, spread_io, grp_io, grp_io)
    w = expand(w_c, spread_st, grp_io, grp_st)
    v = expand(v_c, spread_io, grp_st, grp_io)
    a_s_re = pw_re[s].reshape(1, g * n)
    a_s_im = pw_im[s].reshape(1, g * n)
    dtab = jnp.broadcast_to(d_skip.reshape(j, 1, 1, gl * p), (j, 1, s, gl * p)).reshape(j, 1, s * gl * p)
    return m, w, v, a_s_re, a_s_im, dtab


def _s5_flat(u_ref):
    return jnp.concatenate([u_ref[:, t, :] for t in range(u_ref.shape[1])], axis=1)


def _s5a_kernel(u_ref, w_ref, ire_ref, iim_ref):
    r = jnp.dot(_s5_flat(u_ref).astype(BF16), w_ref[0], preferred_element_type=F32)
    half = r.shape[1] // 2
    ire_ref[...] = r[:, :half]
    iim_ref[...] = r[:, half:]


def _s5scan_kernel(ire_ref, iim_ref, ar_ref, ai_ref, h0r_ref, h0i_ref,
                   hpr_ref, hpi_ref, hfr_ref, hfi_ref):
    nb, nc, _ = ire_ref.shape
    ar, ai = ar_ref[...], ai_ref[...]

    def body(c, carry):
        out = []
        for b in range(nb):
            hr, hi = carry[2 * b], carry[2 * b + 1]
            hpr_ref[b, pl.ds(c, 1), :] = hr
            hpi_ref[b, pl.ds(c, 1), :] = hi
            out.append(ar * hr - ai * hi + ire_ref[b, pl.ds(c, 1), :])
            out.append(ar * hi + ai * hr + iim_ref[b, pl.ds(c, 1), :])
        return tuple(out)

    init = []
    for b in range(nb):
        init += [h0r_ref[b], h0i_ref[b]]
    fin = lax.fori_loop(0, nc, body, tuple(init))
    for b in range(nb):
        hfr_ref[b] = fin[2 * b]
        hfi_ref[b] = fin[2 * b + 1]


def _s5b_kernel(u_ref, hpr_ref, hpi_ref, m_ref, v_ref, d_ref, y_ref):
    uf = _s5_flat(u_ref)
    half = hpr_ref.shape[1]
    y = d_ref[0] * uf + jnp.dot(uf.astype(BF16), m_ref[0], preferred_element_type=F32)
    y += jnp.dot(hpr_ref[...].astype(BF16), v_ref[0, :half, :], preferred_element_type=F32)
    y += jnp.dot(hpi_ref[...].astype(BF16), v_ref[0, half:, :], preferred_element_type=F32)
    lanes = y_ref.shape[2]
    for t in range(y_ref.shape[1]):
        y_ref[:, t, :] = y[:, t * lanes:(t + 1) * lanes]


def s5_apply(u, bsz, h0_re, h0_im, tables):
    m, w, v, a_re, a_im, dtab = tables
    tokens, width = u.shape
    nj = m.shape[0]
    s = S5_CHUNK
    rows = tokens // s
    nc = rows // bsz
    lanes = a_re.shape[1]
    half = w.shape[2] // 2
    fl = m.shape[1]
    rt = min(rows, 512)
    u3 = u.reshape(rows, s, width)
    u_spec = pl.BlockSpec((rt, s, S5_LANES), lambda j, r: (r, 0, j))
    st_spec = pl.BlockSpec((rt, half), lambda j, r: (r, j))
    tab_spec = pl.BlockSpec((1, fl, fl), lambda j, r: (j, 0, 0))
    inj_re, inj_im = pl.pallas_call(
        _s5a_kernel, grid=(nj, rows // rt),
        in_specs=[u_spec, tab_spec],
        out_specs=[st_spec, st_spec],
        out_shape=[jax.ShapeDtypeStruct((rows, lanes), F32)] * 2,
        compiler_params=_params(2), name="s5_chunk_in",
    )(u3, w)

    sb, lw = 4, 512
    seq_spec = pl.BlockSpec((sb, nc, lw), lambda b, l: (b, 0, l))
    vec_spec = pl.BlockSpec((sb, 1, lw), lambda b, l: (b, 0, l))
    atab_spec = pl.BlockSpec((1, lw), lambda b, l: (0, l))
    hp_re, hp_im, hf_re, hf_im = pl.pallas_call(
        _s5scan_kernel, grid=(bsz // sb, lanes // lw),
        in_specs=[seq_spec, seq_spec, atab_spec, atab_spec, vec_spec, vec_spec],
        out_specs=[seq_spec, seq_spec, vec_spec, vec_spec],
        out_shape=[jax.ShapeDtypeStruct((bsz, nc, lanes), F32)] * 2
        + [jax.ShapeDtypeStruct((bsz, 1, lanes), F32)] * 2,
        compiler_params=_params(2), name="s5_scan",
    )(inj_re.reshape(bsz, nc, lanes), inj_im.reshape(bsz, nc, lanes), a_re, a_im,
      h0_re.reshape(bsz, 1, lanes), h0_im.reshape(bsz, 1, lanes))

    y3 = pl.pallas_call(
        _s5b_kernel, grid=(nj, rows // rt),
        in_specs=[u_spec, st_spec, st_spec, tab_spec, tab_spec,
                  pl.BlockSpec((1, 1, fl), lambda j, r: (j, 0, 0))],
        out_specs=u_spec,
        out_shape=jax.ShapeDtypeStruct((rows, s, width), F32),
        compiler_params=_params(2), name="s5_chunk_out",
    )(u3, hp_re.reshape(rows, lanes), hp_im.reshape(rows, lanes), m, v, dtab)
    g = lanes // SSM_STATE
    return y3.reshape(tokens, width), hf_re.reshape(bsz, g, SSM_STATE), hf_im.reshape(bsz, g, SSM_STATE)


def _retention_gammas():
    return 1.0 - np.exp2(-5.0 - np.arange(RET_HEADS, dtype=np.float64))


def retention_tables(tile, chunk):
    gam = _retention_gammas()[:, None, None]
    i = np.arange(tile)[:, None]
    j = np.arange(tile)[None, :]
    same = (i // chunk) == (j // chunk)
    earlier = (j // chunk) < (i // chunk)
    dist = np.where(same, np.abs(i - j), np.where(earlier, i - j, 0))
    dmask = np.where(same | earlier, gam ** dist[None], 0.0)
    qw = np.broadcast_to((gam[:, :, 0] ** (np.arange(tile) + 1.0))[:, :, None], (RET_HEADS, tile, HEAD_DIM))
    kw = np.broadcast_to((gam[:, :, 0] ** (tile - 1.0 - np.arange(tile)))[:, :, None], (RET_HEADS, tile, HEAD_DIM))
    return (jnp.asarray(dmask, F32), jnp.asarray(qw, F32), jnp.asarray(kw, F32),
            tuple(float(x) for x in _retention_gammas() ** tile))


def rope_tables(pos):
    half = HEAD_DIM // 2
    inv = jnp.exp(-math.log(ROPE_BASE) * 2.0 * jnp.arange(half, dtype=F32) / HEAD_DIM)
    ang = pos.astype(F32)[:, None] * inv[None, :]
    cos, sin = jnp.cos(ang), jnp.sin(ang)
    cosf = jnp.concatenate([cos, cos], axis=1)
    sinf = jnp.concatenate([-sin, sin], axis=1)
    return cosf, sinf


def _mixer_kernel(x_ref, zq_ref, xq_ref, gl_ref, y_ref, cq_ref, sq_ref, ck_ref, sk_ref,
                  dm_ref, qw_ref, kw_ref, mk_ref, mv_ref, s0_ref, gn_ref,
                  wro_ref, wglu_ref, wso_ref, wxo_ref, wout_ref,
                  h_ref, sout_ref, s_scr, o_scr, xo_scr, glu_scr, mg_scr, *, nb, tl, tile_decay):
    lt = pl.program_id(1)
    hd = HEAD_DIM
    qk = RET_HEADS * hd

    @pl.when(lt == 0)
    def _():
        s_scr[...] = s0_ref[...]

    cq, sq, ck, sk = cq_ref[...], sq_ref[...], ck_ref[...], sk_ref[...]
    for n in range(nb):
        rows = slice(n * tl, (n + 1) * tl)
        for h in range(RET_HEADS):
            c0 = h * hd
            q = zq_ref[rows, c0:c0 + hd].astype(F32)
            k = zq_ref[rows, qk + c0:qk + c0 + hd].astype(F32)
            v = zq_ref[rows, 2 * qk + c0:2 * qk + c0 + hd]
            g = zq_ref[rows, 3 * qk + c0:3 * qk + c0 + hd].astype(F32)
            qr = q * cq + pltpu.roll(q, hd // 2, 1) * sq
            kr = k * ck + pltpu.roll(k, hd // 2, 1) * sk
            sc = lax.dot_general(qr.astype(BF16), kr.astype(BF16), NT_DIMS,
                                 preferred_element_type=F32) * dm_ref[h]
            o = jnp.dot(sc.astype(BF16), v, preferred_element_type=F32)
            s_old = s_scr[n, h]
            o += jnp.dot((qr * qw_ref[h]).astype(BF16), s_old.astype(BF16), preferred_element_type=F32)
            kv = lax.dot_general((kr * kw_ref[h]).astype(BF16), v, TN_DIMS, preferred_element_type=F32)
            s_scr[n, h] = tile_decay[h] * s_old + kv
            d = o - jnp.mean(o, axis=-1, keepdims=True)
            on = d * lax.rsqrt(jnp.mean(d * d, axis=-1, keepdims=True) + EPS) * gn_ref[:, c0:c0 + hd]
            o_scr[rows, c0:c0 + hd] = (on * g).astype(BF16)
            mem_rows = pl.ds(h, mk_ref.shape[1] // X_HEADS, stride=X_HEADS)
            mkh = mk_ref[n, mem_rows, :].astype(BF16)
            mvh = mv_ref[n, mem_rows, :].astype(BF16)
            s = lax.dot_general(xq_ref[rows, c0:c0 + hd], mkh, NT_DIMS,
                                preferred_element_type=F32) * (hd ** -0.5)
            e = jnp.exp(s - jnp.max(s, axis=-1, keepdims=True))
            p = e / jnp.sum(e, axis=-1, keepdims=True)
            xo_scr[rows, c0:c0 + hd] = jnp.dot(p.astype(BF16), mvh, preferred_element_type=F32).astype(BF16)

    cw = MIX_COLS
    yb = jax.nn.gelu(y_ref[...]).astype(BF16)
    half = wglu_ref.shape[1] // 2
    for c0 in range(0, half, cw):
        ga = jnp.dot(yb, wglu_ref[:, c0:c0 + cw], preferred_element_type=F32)
        gb = jnp.dot(yb, wglu_ref[:, half + c0:half + c0 + cw], preferred_element_type=F32)
        glu_scr[:, c0:c0 + cw] = (ga * _sigmoid(gb)).astype(BF16)
    dm = h_ref.shape[1]
    for c0 in range(0, dm, cw):
        cols = slice(c0, c0 + cw)
        ret = jnp.dot(o_scr[...], wro_ref[:, cols], preferred_element_type=F32)
        ssm = jnp.dot(glu_scr[...], wso_ref[:, cols], preferred_element_type=F32)
        xb = jnp.dot(xo_scr[...], wxo_ref[:, cols], preferred_element_type=F32)
        merged = (gl_ref[:, c0:c0 + cw].astype(F32) * ret
                  + gl_ref[:, dm + c0:dm + c0 + cw].astype(F32) * ssm
                  + gl_ref[:, 2 * dm + c0:2 * dm + c0 + cw].astype(F32) * xb)
        mg_scr[:, cols] = merged.astype(BF16)
    for c0 in range(0, dm, cw):
        cols = slice(c0, c0 + cw)
        h_ref[:, cols] = x_ref[:, cols] + jnp.dot(mg_scr[...], wout_ref[:, cols], preferred_element_type=F32)

    @pl.when(lt == pl.num_programs(1) - 1)
    def _():
        sout_ref[...] = s_scr[...]


def mixer(x, z, y_ssm, pos, mem_k, mem_v, s0, ret_gn, w_ret_o, w_ssm_glu, w_ssm_o, w_x_o, w_out, *, nb, tl):
    bsz, length, dm = x.shape
    chunk = min(CHUNK, length)
    nl = length // tl
    rows = nb * tl
    qk = RET_HEADS * HEAD_DIM
    sw = y_ssm.shape[1]
    xw = X_HEADS * HEAD_DIM
    gate_col = (4 * qk + sw + xw)
    assert gate_col % (3 * dm) == 0 and (4 * qk + sw) % xw == 0
    dmask, qw, kw, tile_decay = retention_tables(tl, chunk)
    cosf, sinf = rope_tables(pos)
    scale = HEAD_DIM ** -0.5
    row_map = lambda b, l: (b * nl + l, 0)
    tab_map = lambda b, l: (l, 0)
    st_spec = pl.BlockSpec((nb, RET_HEADS, HEAD_DIM, HEAD_DIM), lambda b, l: (b, 0, 0, 0))
    mem_spec = pl.BlockSpec((nb,) + mem_k.shape[1:], lambda b, l: (b,) + (0,) * (mem_k.ndim - 1))
    h, s_out = pl.pallas_call(
        functools.partial(_mixer_kernel, nb=nb, tl=tl, tile_decay=tile_decay),
        grid=(bsz // nb, nl),
        in_specs=[pl.BlockSpec((rows, dm), row_map),
                  pl.BlockSpec((rows, 4 * qk), row_map),
                  pl.BlockSpec((rows, xw), lambda b, l: (b * nl + l, (4 * qk + sw) // xw)),
                  pl.BlockSpec((rows, 3 * dm), lambda b, l: (b * nl + l, gate_col // (3 * dm))),
                  pl.BlockSpec((rows, sw), row_map),
                  pl.BlockSpec((tl, HEAD_DIM), tab_map), pl.BlockSpec((tl, HEAD_DIM), tab_map),
                  pl.BlockSpec((tl, HEAD_DIM), tab_map), pl.BlockSpec((tl, HEAD_DIM), tab_map),
                  _resident(dmask.shape), _resident(qw.shape), _resident(kw.shape),
                  mem_spec, mem_spec, st_spec, _resident((1, qk)),
                  _resident(w_ret_o.shape), _resident(w_ssm_glu.shape), _resident(w_ssm_o.shape),
                  _resident(w_x_o.shape), _resident(w_out.shape)],
        out_specs=[pl.BlockSpec((rows, dm), row_map), st_spec],
        out_shape=[jax.ShapeDtypeStruct((bsz * length, dm), F32),
                   jax.ShapeDtypeStruct(s0.shape, F32)],
        scratch_shapes=[pltpu.VMEM((nb, RET_HEADS, HEAD_DIM, HEAD_DIM), F32),
                        pltpu.VMEM((rows, qk), BF16), pltpu.VMEM((rows, xw), BF16),
                        pltpu.VMEM((rows, w_ssm_o.shape[0]), BF16), pltpu.VMEM((rows, dm), BF16)],
        compiler_params=_params(2), name="mixer",
    )(x.reshape(bsz * length, dm), z, z, z, y_ssm,
      cosf * scale, sinf * scale, cosf, sinf, dmask, qw, kw,
      mem_k, mem_v, s0, ret_gn.reshape(1, qk), w_ret_o, w_ssm_glu, w_ssm_o, w_x_o, w_out)
    return h, s_out


def _router_kernel(hp_ref, hs_ref, nw_ref, wrt_ref, br_ref, tri_ref, low_ref,
                   xn_ref, gw_ref, crow_ref, cnt_ref, *, prompt_tiles):
    h = jnp.where(pl.program_id(0) < prompt_tiles, hp_ref[...], hs_ref[...])
    xn = _rms(h, nw_ref[...])
    xn_ref[...] = xn.astype(BF16)
    logits = lax.dot_general(wrt_ref[...], xn, NT_DIMS, precision=lax.Precision.HIGHEST,
                             preferred_element_type=F32) + br_ref[...]
    ne = logits.shape[0]
    iota = lax.broadcasted_iota(jnp.int32, logits.shape, 0)
    rest = logits
    sel = jnp.zeros(logits.shape, jnp.bool_)
    vals, idxs = [], []
    for _ in range(TOP_K):
        m = jnp.max(rest, axis=0, keepdims=True)
        ix = jnp.min(jnp.where(rest == m, iota, ne), axis=0, keepdims=True)
        hit = iota == ix
        vals.append(m)
        idxs.append(ix)
        sel = jnp.logical_or(sel, hit)
        rest = jnp.where(hit, -jnp.inf, rest)
    es = [jnp.exp(v - vals[0]) for v in vals]
    tot = es[0] + es[1] + es[2] + es[3]
    before = jnp.dot(sel.astype(BF16), tri_ref[...], preferred_element_type=F32)
    cnt = jnp.sum(sel.astype(F32), axis=1, keepdims=True)
    seg = jnp.floor((cnt + (ROW_ALIGN - 1.0)) * (1.0 / ROW_ALIGN)) * ROW_ALIGN
    start = jnp.dot(low_ref[...], jnp.broadcast_to(seg, before.shape), precision=lax.Precision.HIGHEST,
                    preferred_element_type=F32)
    place = start + before
    for k in range(TOP_K):
        gw_ref[k:k + 1, :] = es[k] / tot
        crow_ref[k:k + 1, :] = jnp.sum(jnp.where(iota == idxs[k], place, 0.0), axis=0,
                                       keepdims=True).astype(jnp.int32)
    cnt_ref[...] = jnp.broadcast_to(cnt, cnt_ref.shape)


def router(h_p, h_s, norm_ffn, w_router, b_router):
    dm = h_p.shape[1]
    ne = w_router.shape[1]
    tt = ROW_TILE
    npt, nst = h_p.shape[0] // tt, h_s.shape[0] // tt
    t = (npt + nst) * tt
    tri = jnp.asarray(np.triu(np.ones((tt, tt), np.float32), k=1), BF16)
    low = jnp.asarray(np.tril(np.ones((ne, ne), np.float32), k=-1))
    tok_spec = pl.BlockSpec((TOP_K, tt), lambda i: (0, i))
    return pl.pallas_call(
        functools.partial(_router_kernel, prompt_tiles=npt), grid=(npt + nst,),
        in_specs=[pl.BlockSpec((tt, dm), lambda i: (jnp.minimum(i, npt - 1), 0)),
                  pl.BlockSpec((tt, dm), lambda i: (jnp.maximum(i - npt, 0), 0)),
                  _resident((1, dm)), _resident((ne, dm)), _resident((ne, 1)), _resident((tt, tt)),
                  _resident((ne, ne))],
        out_specs=[pl.BlockSpec((tt, dm), lambda i: (i, 0)), tok_spec, tok_spec,
                   pl.BlockSpec((ne, 128), lambda i: (i, 0))],
        out_shape=[jax.ShapeDtypeStruct((t, dm), BF16),
                   jax.ShapeDtypeStruct((TOP_K, t), F32), jax.ShapeDtypeStruct((TOP_K, t), jnp.int32),
                   jax.ShapeDtypeStruct(((npt + nst) * ne, 128), F32)],
        compiler_params=_params(), name="router",
    )(h_p, h_s, norm_ffn.reshape(1, dm), w_router.T, b_router.reshape(ne, 1), tri, low)


def _pack_bf16_pairs(x):
    n = x.shape[1] // 2
    lo = lax.bitcast_convert_type(x[:, :n].astype(BF16).astype(F32), jnp.int32)
    hi = lax.bitcast_convert_type(x[:, n:].astype(BF16).astype(F32), jnp.int32)
    return lax.shift_right_logical(lo, 16) | (hi & -65536)


def _unpack_bf16_pairs(u):
    lo = lax.bitcast_convert_type(lax.shift_left(u, 16), F32).astype(BF16)
    hi = lax.bitcast_convert_type(u & -65536, F32).astype(BF16)
    return lo, hi


def _split_count(n, fn):
    def quad(jq, carry):
        fn(4 * jq, 4)
        return carry

    lax.fori_loop(0, n // 4, quad, 0)

    @pl.when(n % 4 >= 2)
    def _():
        fn(n // 4 * 4, 2)

    @pl.when(n % 2 == 1)
    def _():
        fn(n // 2 * 2, 1)


def _chunk_loop(n, fn):
    def quad(jq, carry):
        for u in range(4):
            fn(4 * jq + u)
        return carry

    def single(j, carry):
        fn(j)
        return carry

    lax.fori_loop(0, n // 4, quad, 0)
    lax.fori_loop(n // 4 * 4, n, single, 0)


def _dispatch_kernel(dst_ref, tch_ref, zs_ref, zn_ref, used_ref,
                     xn_ref, crow_ref, xs_ref, cbuf, zbuf, sems, zsem, *, ne, bm):
    i = pl.program_id(0)
    nt = pl.num_programs(0)
    slot = lax.rem(i, 2)
    tt = xn_ref.shape[0]
    cr = cbuf.shape[1]
    ra = ROW_ALIGN
    n_blocks = xs_ref.shape[0] // bm

    def chunk_copy(sl, src_row, dst_row, rows):
        return pltpu.make_async_copy(cbuf.at[sl, pl.ds(src_row, rows)], xs_ref.at[pl.ds(dst_row, rows)],
                                     sems.at[sl])

    def wait_chunks(sl, n):
        _split_count(n, lambda j, m: chunk_copy(sl, 0, 0, ra * m).wait())

    def tail_copy(e, j):
        return pltpu.make_async_copy(zbuf.at[pl.ds(0, ra)],
                                     xs_ref.at[pl.ds(pl.multiple_of(zs_ref[e] + ra * j, ra), ra)], zsem)

    def block_copy(b):
        return pltpu.make_async_copy(zbuf, xs_ref.at[pl.ds(pl.multiple_of(b * bm, bm), bm)], zsem)

    def zero_fill(start):
        def per_expert(e, carry):
            def per_chunk(j, c2):
                (tail_copy(e, j).start() if start else tail_copy(e, j).wait())
                return c2
            lax.fori_loop(0, zn_ref[e], per_chunk, 0)
            return carry
        lax.fori_loop(0, ne, per_expert, 0)

        def per_block(b, carry):
            (block_copy(b).start() if start else block_copy(b).wait())
            return carry
        lax.fori_loop(used_ref[0], n_blocks, per_block, 0)

    @pl.when(i == 0)
    def _():
        zbuf[...] = jnp.zeros_like(zbuf)
        zero_fill(True)
        zero_fill(False)

    @pl.when(i >= 2)
    def _():
        wait_chunks(slot, tch_ref[jnp.maximum(i - 2, 0)])

    crow = crow_ref[...]
    rows = lax.broadcasted_iota(jnp.int32, (cr, tt), 0)
    hit = rows == crow[0:1, :]
    for k in range(1, TOP_K):
        hit = jnp.logical_or(hit, rows == crow[k:k + 1, :])
    packed = _pack_bf16_pairs(jnp.dot(jnp.where(hit, 1.0, 0.0).astype(BF16), xn_ref[...],
                                      preferred_element_type=F32))
    cbuf[slot] = packed

    chunks_per_tile = cr // ra
    _chunk_loop(tch_ref[i], lambda c: chunk_copy(
        slot, pl.multiple_of(c * ra, ra), pl.multiple_of(dst_ref[i * chunks_per_tile + c], ra), ra).start())

    @pl.when(i == nt - 1)
    def _():
        wait_chunks(slot, tch_ref[i])
        wait_chunks(1 - slot, jnp.where(nt >= 2, tch_ref[jnp.maximum(i - 1, 0)], 0))


def dispatch(xn, crow, tables, n_rows, ne):
    t, dm = xn.shape
    tt = ROW_TILE
    bm = FFN_BLOCK
    cr = _compact_rows(ne)
    grid_spec = pltpu.PrefetchScalarGridSpec(
        num_scalar_prefetch=5, grid=(t // tt,),
        in_specs=[pl.BlockSpec((tt, dm), lambda i, *_: (i, 0)),
                  pl.BlockSpec((TOP_K, tt), lambda i, *_: (0, i))],
        out_specs=pl.BlockSpec(memory_space=pl.ANY),
        scratch_shapes=[pltpu.VMEM((2, cr, dm // 2), jnp.int32), pltpu.VMEM((bm, dm // 2), jnp.int32),
                        pltpu.SemaphoreType.DMA((2,)), pltpu.SemaphoreType.DMA(())])
    return pl.pallas_call(
        functools.partial(_dispatch_kernel, ne=ne, bm=bm), grid_spec=grid_spec,
        out_shape=jax.ShapeDtypeStruct((n_rows, dm // 2), jnp.int32),
        compiler_params=_params(), name="dispatch",
    )(*tables, xn, crow)


def _ffn_kernel(be_ref, first_ref, blk_ref, used_ref, next_ref, wslot_ref,
                xs_ref, wgu_hbm, bgu_ref, wd_hbm, bd_ref, y_ref, wgu_f32, wd_f32, wgu_bf, wd_bf, sems):
    del blk_ref
    i = pl.program_id(0)

    def weight_copies(e, sl):
        return (pltpu.make_async_copy(wgu_hbm.at[e], wgu_f32.at[sl], sems.at[0, sl]),
                pltpu.make_async_copy(wd_hbm.at[e], wd_f32.at[sl], sems.at[1, sl]))

    @pl.when(first_ref[i] == 1)
    def _():
        sl = wslot_ref[i]

        @pl.when(i == 0)
        def _():
            for c in weight_copies(be_ref[i], sl):
                c.start()

        for c in weight_copies(be_ref[i], sl):
            c.wait()
        wgu_bf[...] = wgu_f32[sl].astype(BF16)
        wd_bf[...] = wd_f32[sl].astype(BF16)

        @pl.when(next_ref[i] >= 0)
        def _():
            for c in weight_copies(next_ref[i], 1 - sl):
                c.start()

    @pl.when(i < used_ref[0])
    def _():
        x_lo, x_hi = _unpack_bf16_pairs(xs_ref[...])
        kh = x_lo.shape[1]
        hgu = (jnp.dot(x_lo, wgu_bf[:kh, :], preferred_element_type=F32)
               + jnp.dot(x_hi, wgu_bf[kh:, :], preferred_element_type=F32) + bgu_ref[0])
        ff = hgu.shape[1] // 2
        gate = jnp.minimum(hgu[:, :ff], SWIGLU_LIMIT)
        up = jnp.clip(hgu[:, ff:], -SWIGLU_LIMIT, SWIGLU_LIMIT)
        act = (up + 1.0) * gate * jax.nn.sigmoid(SWIGLU_ALPHA * gate)
        y_ref[...] = _pack_bf16_pairs(jnp.dot(act.astype(BF16), wd_bf[...], preferred_element_type=F32)
                                      + bd_ref[0])

    @pl.when(i >= used_ref[0])
    def _():
        y_ref[...] = jnp.zeros_like(y_ref)


def expert_ffn(xs, block_tables, w_gate_up, b_gate_up, w_down, b_down):
    n_rows = xs.shape[0]
    ne, dm, ff2 = w_gate_up.shape
    bm = FFN_BLOCK
    grid_spec = pltpu.PrefetchScalarGridSpec(
        num_scalar_prefetch=6, grid=(n_rows // bm,),
        in_specs=[pl.BlockSpec((bm, dm // 2), lambda i, be, bf, bi, *_: (bi[i], 0)),
                  pl.BlockSpec(memory_space=pl.ANY),
                  pl.BlockSpec((1, 1, ff2), lambda i, be, *_: (be[i], 0, 0)),
                  pl.BlockSpec(memory_space=pl.ANY),
                  pl.BlockSpec((1, 1, dm), lambda i, be, *_: (be[i], 0, 0))],
        out_specs=pl.BlockSpec((bm, dm // 2), lambda i, *_: (i, 0)),
        scratch_shapes=[pltpu.VMEM((2, dm, ff2), F32), pltpu.VMEM((2, ff2 // 2, dm), F32),
                        pltpu.VMEM((dm, ff2), BF16), pltpu.VMEM((ff2 // 2, dm), BF16),
                        pltpu.SemaphoreType.DMA((2, 2))])
    return pl.pallas_call(
        _ffn_kernel, grid_spec=grid_spec,
        out_shape=jax.ShapeDtypeStruct((n_rows, dm // 2), jnp.int32),
        compiler_params=_params(), name="expert_ffn",
    )(*block_tables, xs, w_gate_up, b_gate_up.reshape(ne, 1, ff2), w_down, b_down.reshape(ne, 1, dm))


def _combine_kernel(dst_ref, tch_ref,
                    crow_ref, gw_ref, hp_ref, hs_ref, fn_ref, yr_ref, yp_ref, ys_ref, ybuf, sems,
                    *, prompt_tiles):
    i = pl.program_id(0)
    nt = pl.num_programs(0)
    slot = lax.rem(i, 2)
    tt = hp_ref.shape[0]
    cr = ybuf.shape[1]
    ra = ROW_ALIGN

    def chunk_copy(sl, src_row, dst_row, rows):
        return pltpu.make_async_copy(yr_ref.at[pl.ds(src_row, rows)], ybuf.at[sl, pl.ds(dst_row, rows)],
                                     sems.at[sl])

    def fetch(tile, sl):
        chunks_per_tile = cr // ra
        _chunk_loop(tch_ref[tile], lambda c: chunk_copy(
            sl, pl.multiple_of(dst_ref[tile * chunks_per_tile + c], ra), pl.multiple_of(c * ra, ra), ra).start())

    @pl.when(i == 0)
    def _():
        ybuf[...] = jnp.zeros_like(ybuf)
        fetch(0, 0)

    @pl.when(i + 1 < nt)
    def _():
        fetch(i + 1, 1 - slot)

    _split_count(tch_ref[i], lambda j, m: chunk_copy(slot, 0, 0, ra * m).wait())

    y_lo, y_hi = _unpack_bf16_pairs(ybuf[slot])
    cols = lax.broadcasted_iota(jnp.int32, (tt, cr), 1)
    q = jnp.zeros((tt, cr), F32)
    for k in range(TOP_K):
        q += jnp.where(cols == crow_ref[:, k:k + 1], gw_ref[:, k:k + 1], 0.0)
    qb = q.astype(BF16)
    moe = jnp.concatenate([jnp.dot(qb, y_lo, preferred_element_type=F32),
                           jnp.dot(qb, y_hi, preferred_element_type=F32)], axis=1)
    h = jnp.where(i < prompt_tiles, hp_ref[...], hs_ref[...])
    out = _rms(h + moe, fn_ref[...])

    @pl.when(i < prompt_tiles)
    def _():
        yp_ref[...] = out

    @pl.when(i >= prompt_tiles)
    def _():
        ys_ref[...] = out


def combine(crow_t, gw_t, h_p, h_s, final_norm, y_rows, tables, ne):
    dm = h_p.shape[1]
    tt = ROW_TILE
    npt, nst = h_p.shape[0] // tt, h_s.shape[0] // tt
    cr = _compact_rows(ne)
    p_map = lambda i, *_: (jnp.minimum(i, npt - 1), 0)
    s_map = lambda i, *_: (jnp.maximum(i - npt, 0), 0)
    grid_spec = pltpu.PrefetchScalarGridSpec(
        num_scalar_prefetch=2, grid=(npt + nst,),
        in_specs=[pl.BlockSpec((tt, TOP_K), lambda i, *_: (i, 0)),
                  pl.BlockSpec((tt, TOP_K), lambda i, *_: (i, 0)),
                  pl.BlockSpec((tt, dm), p_map), pl.BlockSpec((tt, dm), s_map),
                  pl.BlockSpec((1, dm), lambda i, *_: (0, 0)),
                  pl.BlockSpec(memory_space=pl.ANY)],
        out_specs=[pl.BlockSpec((tt, dm), p_map), pl.BlockSpec((tt, dm), s_map)],
        scratch_shapes=[pltpu.VMEM((2, cr, dm // 2), jnp.int32), pltpu.SemaphoreType.DMA((2,))])
    return pl.pallas_call(
        functools.partial(_combine_kernel, prompt_tiles=npt), grid_spec=grid_spec,
        out_shape=[jax.ShapeDtypeStruct(h_p.shape, F32), jax.ShapeDtypeStruct(h_s.shape, F32)],
        compiler_params=_params(), name="combine",
    )(*tables, crow_t, gw_t, h_p, h_s, final_norm.reshape(1, dm), y_rows)


def _compact_rows(ne):
    return -(-(TOP_K * ROW_TILE + ne * (ROW_ALIGN - 1)) // 128) * 128


def moe_and_final_norm(h_p, h_s, norm_ffn, w_router, b_router, w_gate_up, b_gate_up, w_down, b_down, final_norm):
    ne = w_router.shape[1]
    bm = FFN_BLOCK
    ra = ROW_ALIGN
    xn, gw, crow, cnt = router(h_p, h_s, norm_ffn, w_router, b_router)
    t = xn.shape[0]
    nt = t // ROW_TILE
    seg = -(-cnt[:, 0].astype(jnp.int32).reshape(nt, ne) // ra) * ra
    seg_before = jnp.cumsum(seg, axis=0) - seg
    rows_e = jnp.sum(seg, axis=0)
    padded = -(-rows_e // bm) * bm
    pad_ends = jnp.cumsum(padded)
    pad_starts = pad_ends - padded
    n_blocks = -(-(t * TOP_K + nt * ne * (ra - 1) + ne * (bm - 1)) // bm)
    n_used = jnp.maximum(pad_ends[-1] // bm, 1)
    blocks = jnp.arange(n_blocks, dtype=jnp.int32)
    block_idx = jnp.minimum(blocks, n_used - 1)
    block_e = jnp.minimum(jnp.sum((pad_ends[None, :] <= (block_idx * bm)[:, None]).astype(jnp.int32), axis=1),
                          ne - 1)
    block_first = (jnp.any(block_idx[:, None] * bm == pad_starts[None, :], axis=1)
                   & (blocks < n_used)).astype(jnp.int32)
    experts = jnp.arange(ne, dtype=jnp.int32)
    active = padded > 0
    later = jnp.where(active, experts, ne)
    next_active = jnp.concatenate([lax.cummin(later, reverse=True)[1:], jnp.full((1,), ne, jnp.int32)])
    next_active = jnp.where(next_active < ne, next_active, -1)
    slot_e = (jnp.cumsum(active.astype(jnp.int32)) - active.astype(jnp.int32)) % 2
    is_e = block_e[:, None] == experts[None, :]
    block_tables = (block_e, block_first, block_idx, n_used.reshape(1),
                    jnp.sum(jnp.where(is_e, next_active[None, :], 0), axis=1),
                    jnp.sum(jnp.where(is_e, slot_e[None, :], 0), axis=1))
    chunks = seg // ra
    chunk_end = jnp.cumsum(chunks, axis=1)
    c_ids = jnp.arange(_compact_rows(ne) // ra, dtype=jnp.int32)
    owner = jnp.sum((chunk_end[:, None, :] <= c_ids[None, :, None]).astype(jnp.int32), axis=2)
    seg_shift = pad_starts[None, :] + seg_before - ra * (chunk_end - chunks)
    dst = ra * c_ids[None, :] + jnp.sum(
        jnp.where(owner[:, :, None] == experts[None, None, :], seg_shift[:, None, :], 0), axis=2)
    seg_tables = (dst.reshape(-1), chunk_end[:, -1])
    fill_tables = (pad_starts + rows_e, (padded - rows_e) // ra, n_used.reshape(1))
    to_i32 = lambda xs: tuple(x.astype(jnp.int32) for x in xs)
    xs = dispatch(xn, crow, to_i32(seg_tables + fill_tables), n_blocks * bm, ne)
    y_rows = expert_ffn(xs, to_i32(block_tables), w_gate_up, b_gate_up, w_down, b_down)
    return combine(crow.T, gw.T, h_p, h_s, final_norm, y_rows, to_i32(seg_tables), ne)


def kernel(x_prompt, x_sample, cache_mem_k, cache_mem_v, state_ret, state_ssm_re, state_ssm_im, mem_prompt, norm_mix, w_in, ret_gn, w_ret_o, ssm_lam_re, ssm_lam_im, ssm_log_dt, ssm_b_re, ssm_b_im, ssm_c_re, ssm_c_im, ssm_d, w_ssm_glu, w_ssm_o, mem_norm, w_mem_kv, w_x_o, w_out, norm_ffn, w_router, b_router, w_gate_up, b_gate_up, w_down, b_down, final_norm):
    assert norm_mix.shape[0] == 1, "single-layer step"
    bp, lp, dm = x_prompt.shape
    bs, ls, _ = x_sample.shape
    n_mem = mem_prompt.shape[1]
    xw = X_HEADS * HEAD_DIM
    qk = RET_HEADS * HEAD_DIM
    sw = ssm_d.shape[1]
    g = ssm_lam_re.shape[1]

    w_in_b = w_in[0].astype(BF16)
    tables = s5_tables(ssm_lam_re[0], ssm_lam_im[0], ssm_log_dt[0], ssm_b_re[0], ssm_b_im[0],
                       ssm_c_re[0], ssm_c_im[0], ssm_d[0])
    mix_w = (ret_gn[0], w_ret_o[0].astype(BF16), w_ssm_glu[0].astype(BF16), w_ssm_o[0].astype(BF16),
             w_x_o[0].astype(BF16), w_out[0].astype(BF16))

    kv = norm_matmul(mem_prompt.reshape(bp * n_mem, dm), mem_norm[0], w_mem_kv[0].astype(BF16), F32)
    mk_p = kv[:, :xw].reshape(bp, n_mem * X_HEADS, HEAD_DIM)
    mv_p = kv[:, xw:].reshape(bp, n_mem * X_HEADS, HEAD_DIM)

    def group(x, pos, mem_k, mem_v, s_ret, h_re, h_im, nb, tl):
        bsz, length, _ = x.shape
        z, u = norm_matmul(x.reshape(bsz * length, dm), norm_mix[0], w_in_b, BF16,
                           f32_cols=(4 * qk, 4 * qk + sw),
                           acts=((3 * qk, 4 * qk, "silu"), (4 * qk + sw + xw, w_in_b.shape[1], "sigmoid")))
        y, hf_re, hf_im = s5_apply(u, bsz, h_re, h_im, tables)
        h, s_new = mixer(x, z, y, pos, mem_k, mem_v, s_ret, *mix_w, nb=nb, tl=tl)
        return h, s_new, hf_re, hf_im

    zero_ret = jnp.zeros((bp, RET_HEADS, HEAD_DIM, HEAD_DIM), F32)
    zero_ssm = jnp.zeros((bp, g, SSM_STATE), F32)
    h_p, ret_p, sre_p, sim_p = group(x_prompt, jnp.arange(lp, dtype=jnp.int32), mk_p, mv_p,
                                     zero_ret, zero_ssm, zero_ssm, 1, ROW_TILE)
    h_s, ret_s, sre_s, sim_s = group(x_sample, PAST_LEN + jnp.arange(ls, dtype=jnp.int32),
                                     cache_mem_k[0].reshape(bs, n_mem * X_HEADS, HEAD_DIM),
                                     cache_mem_v[0].reshape(bs, n_mem * X_HEADS, HEAD_DIM),
                                     state_ret[0], state_ssm_re[0], state_ssm_im[0], ROW_TILE // ls, ls)

    y_p, y_s = moe_and_final_norm(h_p, h_s, norm_ffn[0], w_router[0], b_router[0],
                                  w_gate_up[0], b_gate_up[0], w_down[0], b_down[0], final_norm)
    return (y_p.reshape(bp, lp, dm), y_s.reshape(bs, ls, dm), ret_p[None], sre_p[None], sim_p[None],
            mk_p.reshape(1, bp, n_mem, X_HEADS, HEAD_DIM), mv_p.reshape(1, bp, n_mem, X_HEADS, HEAD_DIM),
            ret_s[None], sre_s[None], sim_s[None])
```

```python
import functools
import math

import jax
import jax.numpy as jnp
import numpy as np
from jax import lax
from jax.experimental import pallas as pl
from jax.experimental.pallas import tpu as pltpu

F32 = jnp.float32
BF16 = jnp.bfloat16

EPS = 1e-6
CHUNK = 64
PAST_LEN = 2048
ROPE_BASE = 10000.0
RET_HEADS = 4
X_HEADS = 4
HEAD_DIM = 128
SSM_GROUP = 16
SSM_STATE = 64
TOP_K = 4
SWIGLU_ALPHA = 1.702
SWIGLU_LIMIT = 7.0

VMEM_LIMIT = 52 * 1024 * 1024
S5_CHUNK = 8
S5_LANES = 128
ROW_TILE = 256
FFN_BLOCK = 256
ROW_ALIGN = 8
MIX_COLS = 256
NT_DIMS = (((1,), (1,)), ((), ()))
TN_DIMS = (((0,), (0,)), ((), ()))


def _params(n_axes=1):
    return pltpu.CompilerParams(dimension_semantics=("arbitrary",) * n_axes,
                                vmem_limit_bytes=VMEM_LIMIT)


def _resident(shape):
    nd = len(shape)
    return pl.BlockSpec(shape, lambda *_: (0,) * nd, pipeline_mode=pl.Buffered(1))


def _rms(x, w):
    return x * lax.rsqrt(jnp.mean(x * x, axis=-1, keepdims=True) + EPS) * w


def _sigmoid(x):
    return 0.5 * jnp.tanh(0.5 * x) + 0.5


_ACTIVATIONS = {"sigmoid": _sigmoid, "silu": lambda v: v * _sigmoid(v)}


def _norm_matmul_kernel(x_ref, nw_ref, w_ref, o_ref, *f32_refs, n_chunk, f32_cols, acts):
    xb = _rms(x_ref[...], nw_ref[...]).astype(BF16)
    for n0 in range(0, o_ref.shape[1], n_chunk):
        r = jnp.dot(xb, w_ref[:, n0:n0 + n_chunk], preferred_element_type=F32)
        if f32_cols is not None and n0 <= f32_cols[0] and f32_cols[1] <= n0 + n_chunk:
            f32_refs[0][...] = r[:, f32_cols[0] - n0:f32_cols[1] - n0]
        cuts = sorted({n0, n0 + n_chunk} | {c for lo, hi, _ in acts for c in (lo, hi) if n0 < c < n0 + n_chunk})
        for a, b in zip(cuts[:-1], cuts[1:]):
            piece = r[:, a - n0:b - n0]
            for lo, hi, kind in acts:
                if lo <= a and b <= hi:
                    piece = _ACTIVATIONS[kind](piece)
            o_ref[:, a:b] = piece.astype(o_ref.dtype)


def norm_matmul(x, nw, w, out_dtype, f32_cols=None, acts=()):
    t, d = x.shape
    n = w.shape[1]
    n_chunk = min(n, 1024)
    out_specs = [pl.BlockSpec((ROW_TILE, n), lambda i: (i, 0))]
    out_shape = [jax.ShapeDtypeStruct((t, n), out_dtype)]
    if f32_cols is not None:
        lo, hi = f32_cols
        assert lo // n_chunk == (hi - 1) // n_chunk
        out_specs.append(pl.BlockSpec((ROW_TILE, hi - lo), lambda i: (i, 0)))
        out_shape.append(jax.ShapeDtypeStruct((t, hi - lo), F32))
    out = pl.pallas_call(
        functools.partial(_norm_matmul_kernel, n_chunk=n_chunk, f32_cols=f32_cols, acts=tuple(acts)),
        grid=(t // ROW_TILE,),
        in_specs=[pl.BlockSpec((ROW_TILE, d), lambda i: (i, 0)), _resident((1, d)), _resident((d, n))],
        out_specs=out_specs, out_shape=out_shape,
        compiler_params=_params(), name="norm_matmul",
    )(x, nw.reshape(1, d), w)
    return out if f32_cols is not None else out[0]


def s5_tables(lam_re, lam_im, log_dt, b_re, b_im, c_re, c_im, d_skip):
    g, n, p = b_re.shape
    s = S5_CHUNK
    gl = S5_LANES // p
    j = g // gl
    hi = lax.Precision.HIGHEST
    dt = jnp.exp(log_dt)[:, None]
    a_re = jnp.exp(lam_re * dt) * jnp.cos(lam_im * dt)
    a_im = jnp.exp(lam_re * dt) * jnp.sin(lam_im * dt)
    den = lam_re * lam_re + lam_im * lam_im
    nr, ni = a_re - 1.0, a_im
    co_re = (nr * lam_re + ni * lam_im) / den
    co_im = (ni * lam_re - nr * lam_im) / den
    bb_re = co_re[..., None] * b_re - co_im[..., None] * b_im
    bb_im = co_re[..., None] * b_im + co_im[..., None] * b_re
    tau = jnp.arange(s + 1, dtype=F32)[:, None, None]
    pw_mag = jnp.exp(lam_re * dt * tau)
    pw_re = pw_mag * jnp.cos(lam_im * dt * tau)
    pw_im = pw_mag * jnp.sin(lam_im * dt * tau)
    ca_re = c_re[None] * pw_re[:, :, None, :] - c_im[None] * pw_im[:, :, None, :]
    ca_im = c_re[None] * pw_im[:, :, None, :] + c_im[None] * pw_re[:, :, None, :]
    kq = (jnp.einsum('tgpn,gnq->gtqp', ca_re[:s], bb_re, precision=hi)
          - jnp.einsum('tgpn,gnq->gtqp', ca_im[:s], bb_im, precision=hi))
    ts = np.arange(s)
    lag_onehot = (ts[None, None, :] - ts[None, :, None] == ts[:, None, None]).astype(np.float32)
    rev = s - 1 - ts
    w_re = pw_re[rev][:, :, :, None] * bb_re[None] - pw_im[rev][:, :, :, None] * bb_im[None]
    w_im = pw_re[rev][:, :, :, None] * bb_im[None] + pw_im[rev][:, :, :, None] * bb_re[None]
    m_c = (jnp.einsum('gxqp,xst->gsqtp', kq, lag_onehot, precision=hi)
           .reshape(j, gl, s, p, s * p).transpose(0, 2, 1, 3, 4).reshape(j, s * gl * p, s * p))
    w_c = (jnp.stack([w_re, w_im]).reshape(2, s, j, gl, n, p).transpose(2, 1, 3, 5, 0, 4)
           .reshape(j, s * gl * p, 2 * n))
    v_c = (jnp.stack([ca_re[1:], -ca_im[1:]]).reshape(2, s, j, gl, p, n).transpose(2, 0, 3, 5, 1, 4)
           .reshape(j, 2 * gl * n, s * p))
    fl = s * gl * p
    c_io = np.arange(fl)
    c_st = np.arange(2 * gl * n)
    k_io = np.arange(s * p)
    k_st = np.arange(2 * n)
    spread_io = ((k_io[:, None] // p == c_io[None, :] // (gl * p)) & (k_io[:, None] % p == c_io[None, :] % p))
    spread_st = ((k_st[:, None] // n == c_st[None, :] // (gl * n)) & (k_st[:, None] % n == c_st[None, :] % n))
    grp_io = (c_io // p) % gl
    grp_st = (c_st // n) % gl

    def expand(compact, spread, row_grp, col_grp):
        full = jnp.einsum('jrk,kc->jrc', compact.astype(BF16), jnp.asarray(spread, BF16),
                          preferred_element_type=F32)
        return jnp.where(jnp.asarray(row_grp[:, None] == col_grp[None, :]), full, 0.0).astype(BF16)

    m = expand(m_c, spread_io, grp_io, grp_io)
    w = expand(w_c, spread_st, grp_io, grp_st)
    v = expand(v_c, spread_io, grp_st, grp_io)
    a_s_re = pw_re[s].reshape(1, g * n)
    a_s_im = pw_im[s].reshape(1, g * n)
    dtab = jnp.broadcast_to(d_skip.reshape(j, 1, 1, gl * p), (j, 1, s, gl * p)).reshape(j, 1, s * gl * p)
    return m, w, v, a_s_re, a_s_im, dtab


def _s5_flat(u_ref):
    return jnp.concatenate([u_ref[:, t, :] for t in range(u_ref.shape[1])], axis=1)


def _s5a_kernel(u_ref, w_ref, ire_ref, iim_ref):
    r = jnp.dot(_s5_flat(u_ref).astype(BF16), w_ref[0], preferred_element_type=F32)
    half = r.shape[1] // 2
    ire_ref[...] = r[:, :half]
    iim_ref[...] = r[:, half:]


def _s5scan_kernel(ire_ref, iim_ref, ar_ref, ai_ref, h0r_ref, h0i_ref,
                   hpr_ref, hpi_ref, hfr_ref, hfi_ref):
    nb, nc, _ = ire_ref.shape
    ar, ai = ar_ref[...], ai_ref[...]

    def body(c, carry):
        out = []
        for b in range(nb):
            hr, hi = carry[2 * b], carry[2 * b + 1]
            hpr_ref[b, pl.ds(c, 1), :] = hr
            hpi_ref[b, pl.ds(c, 1), :] = hi
            out.append(ar * hr - ai * hi + ire_ref[b, pl.ds(c, 1), :])
            out.append(ar * hi + ai * hr + iim_ref[b, pl.ds(c, 1), :])
        return tuple(out)

    init = []
    for b in range(nb):
        init += [h0r_ref[b], h0i_ref[b]]
    fin = lax.fori_loop(0, nc, body, tuple(init))
    for b in range(nb):
        hfr_ref[b] = fin[2 * b]
        hfi_ref[b] = fin[2 * b + 1]


def _s5b_kernel(u_ref, hpr_ref, hpi_ref, m_ref, v_ref, d_ref, y_ref):
    uf = _s5_flat(u_ref)
    half = hpr_ref.shape[1]
    y = d_ref[0] * uf + jnp.dot(uf.astype(BF16), m_ref[0], preferred_element_type=F32)
    y += jnp.dot(hpr_ref[...].astype(BF16), v_ref[0, :half, :], preferred_element_type=F32)
    y += jnp.dot(hpi_ref[...].astype(BF16), v_ref[0, half:, :], preferred_element_type=F32)
    lanes = y_ref.shape[2]
    for t in range(y_ref.shape[1]):
        y_ref[:, t, :] = y[:, t * lanes:(t + 1) * lanes]


def s5_apply(u, bsz, h0_re, h0_im, tables):
    m, w, v, a_re, a_im, dtab = tables
    tokens, width = u.shape
    nj = m.shape[0]
    s = S5_CHUNK
    rows = tokens // s
    nc = rows // bsz
    lanes = a_re.shape[1]
    half = w.shape[2] // 2
    fl = m.shape[1]
    rt = min(rows, 512)
    u3 = u.reshape(rows, s, width)
    u_spec = pl.BlockSpec((rt, s, S5_LANES), lambda j, r: (r, 0, j))
    st_spec = pl.BlockSpec((rt, half), lambda j, r: (r, j))
    tab_spec = pl.BlockSpec((1, fl, fl), lambda j, r: (j, 0, 0))
    inj_re, inj_im = pl.pallas_call(
        _s5a_kernel, grid=(nj, rows // rt),
        in_specs=[u_spec, tab_spec],
        out_specs=[st_spec, st_spec],
        out_shape=[jax.ShapeDtypeStruct((rows, lanes), F32)] * 2,
        compiler_params=_params(2), name="s5_chunk_in",
    )(u3, w)

    sb, lw = 4, 512
    seq_spec = pl.BlockSpec((sb, nc, lw), lambda b, l: (b, 0, l))
    vec_spec = pl.BlockSpec((sb, 1, lw), lambda b, l: (b, 0, l))
    atab_spec = pl.BlockSpec((1, lw), lambda b, l: (0, l))
    hp_re, hp_im, hf_re, hf_im = pl.pallas_call(
        _s5scan_kernel, grid=(bsz // sb, lanes // lw),
        in_specs=[seq_spec, seq_spec, atab_spec, atab_spec, vec_spec, vec_spec],
        out_specs=[seq_spec, seq_spec, vec_spec, vec_spec],
        out_shape=[jax.ShapeDtypeStruct((bsz, nc, lanes), F32)] * 2
        + [jax.ShapeDtypeStruct((bsz, 1, lanes), F32)] * 2,
        compiler_params=_params(2), name="s5_scan",
    )(inj_re.reshape(bsz, nc, lanes), inj_im.reshape(bsz, nc, lanes), a_re, a_im,
      h0_re.reshape(bsz, 1, lanes), h0_im.reshape(bsz, 1, lanes))

    y3 = pl.pallas_call(
        _s5b_kernel, grid=(nj, rows // rt),
        in_specs=[u_spec, st_spec, st_spec, tab_spec, tab_spec,
                  pl.BlockSpec((1, 1, fl), lambda j, r: (j, 0, 0))],
        out_specs=u_spec,
        out_shape=jax.ShapeDtypeStruct((rows, s, width), F32),
        compiler_params=_params(2), name="s5_chunk_out",
    )(u3, hp_re.reshape(rows, lanes), hp_im.reshape(rows, lanes), m, v, dtab)
    g = lanes // SSM_STATE
    return y3.reshape(tokens, width), hf_re.reshape(bsz, g, SSM_STATE), hf_im.reshape(bsz, g, SSM_STATE)


def _retention_gammas():
    return 1.0 - np.exp2(-5.0 - np.arange(RET_HEADS, dtype=np.float64))


def retention_tables(tile, chunk):
    gam = _retention_gammas()[:, None, None]
    i = np.arange(tile)[:, None]
    j = np.arange(tile)[None, :]
    same = (i // chunk) == (j // chunk)
    earlier = (j // chunk) < (i // chunk)
    dist = np.where(same, np.abs(i - j), np.where(earlier, i - j, 0))
    dmask = np.where(same | earlier, gam ** dist[None], 0.0)
    qw = np.broadcast_to((gam[:, :, 0] ** (np.arange(tile) + 1.0))[:, :, None], (RET_HEADS, tile, HEAD_DIM))
    kw = np.broadcast_to((gam[:, :, 0] ** (tile - 1.0 - np.arange(tile)))[:, :, None], (RET_HEADS, tile, HEAD_DIM))
    return (jnp.asarray(dmask, F32), jnp.asarray(qw, F32), jnp.asarray(kw, F32),
            tuple(float(x) for x in _retention_gammas() ** tile))


def rope_tables(pos):
    half = HEAD_DIM // 2
    inv = jnp.exp(-math.log(ROPE_BASE) * 2.0 * jnp.arange(half, dtype=F32) / HEAD_DIM)
    ang = pos.astype(F32)[:, None] * inv[None, :]
    cos, sin = jnp.cos(ang), jnp.sin(ang)
    cosf = jnp.concatenate([cos, cos], axis=1)
    sinf = jnp.concatenate([-sin, sin], axis=1)
    return cosf, sinf


def _mixer_kernel(x_ref, zq_ref, xq_ref, gl_ref, y_ref, cq_ref, sq_ref, ck_ref, sk_ref,
                  dm_ref, qw_ref, kw_ref, mk_ref, mv_ref, s0_ref, gn_ref,
                  wro_ref, wglu_ref, wso_ref, wxo_ref, wout_ref,
                  h_ref, sout_ref, s_scr, o_scr, xo_scr, glu_scr, mg_scr, *, nb, tl, tile_decay):
    lt = pl.program_id(1)
    hd = HEAD_DIM
    qk = RET_HEADS * hd

    @pl.when(lt == 0)
    def _():
        s_scr[...] = s0_ref[...]

    cq, sq, ck, sk = cq_ref[...], sq_ref[...], ck_ref[...], sk_ref[...]
    for n in range(nb):
        rows = slice(n * tl, (n + 1) * tl)
        for h in range(RET_HEADS):
            c0 = h * hd
            q = zq_ref[rows, c0:c0 + hd].astype(F32)
            k = zq_ref[rows, qk + c0:qk + c0 + hd].astype(F32)
            v = zq_ref[rows, 2 * qk + c0:2 * qk + c0 + hd]
            g = zq_ref[rows, 3 * qk + c0:3 * qk + c0 + hd].astype(F32)
            qr = q * cq + pltpu.roll(q, hd // 2, 1) * sq
            kr = k * ck + pltpu.roll(k, hd // 2, 1) * sk
            sc = lax.dot_general(qr.astype(BF16), kr.astype(BF16), NT_DIMS,
                                 preferred_element_type=F32) * dm_ref[h]
            o = jnp.dot(sc.astype(BF16), v, preferred_element_type=F32)
            s_old = s_scr[n, h]
            o += jnp.dot((qr * qw_ref[h]).astype(BF16), s_old.astype(BF16), preferred_element_type=F32)
            kv = lax.dot_general((kr * kw_ref[h]).astype(BF16), v, TN_DIMS, preferred_element_type=F32)
            s_scr[n, h] = tile_decay[h] * s_old + kv
            d = o - jnp.mean(o, axis=-1, keepdims=True)
            on = d * lax.rsqrt(jnp.mean(d * d, axis=-1, keepdims=True) + EPS) * gn_ref[:, c0:c0 + hd]
            o_scr[rows, c0:c0 + hd] = (on * g).astype(BF16)
            mem_rows = pl.ds(h, mk_ref.shape[1] // X_HEADS, stride=X_HEADS)
            mkh = mk_ref[n, mem_rows, :].astype(BF16)
            mvh = mv_ref[n, mem_rows, :].astype(BF16)
            s = lax.dot_general(xq_ref[rows, c0:c0 + hd], mkh, NT_DIMS,
                                preferred_element_type=F32) * (hd ** -0.5)
            e = jnp.exp(s - jnp.max(s, axis=-1, keepdims=True))
            p = e / jnp.sum(e, axis=-1, keepdims=True)
            xo_scr[rows, c0:c0 + hd] = jnp.dot(p.astype(BF16), mvh, preferred_element_type=F32).astype(BF16)

    cw = MIX_COLS
    yb = jax.nn.gelu(y_ref[...]).astype(BF16)
    half = wglu_ref.shape[1] // 2
    for c0 in range(0, half, cw):
        ga = jnp.dot(yb, wglu_ref[:, c0:c0 + cw], preferred_element_type=F32)
        gb = jnp.dot(yb, wglu_ref[:, half + c0:half + c0 + cw], preferred_element_type=F32)
        glu_scr[:, c0:c0 + cw] = (ga * _sigmoid(gb)).astype(BF16)
    dm = h_ref.shape[1]
    for c0 in range(0, dm, cw):
        cols = slice(c0, c0 + cw)
        ret = jnp.dot(o_scr[...], wro_ref[:, cols], preferred_element_type=F32)
        ssm = jnp.dot(glu_scr[...], wso_ref[:, cols], preferred_element_type=F32)
        xb = jnp.dot(xo_scr[...], wxo_ref[:, cols], preferred_element_type=F32)
        merged = (gl_ref[:, c0:c0 + cw].astype(F32) * ret
                  + gl_ref[:, dm + c0:dm + c0 + cw].astype(F32) * ssm
                  + gl_ref[:, 2 * dm + c0:2 * dm + c0 + cw].astype(F32) * xb)
        mg_scr[:, cols] = merged.astype(BF16)
    for c0 in range(0, dm, cw):
        cols = slice(c0, c0 + cw)
        h_ref[:, cols] = x_ref[:, cols] + jnp.dot(mg_scr[...], wout_ref[:, cols], preferred_element_type=F32)

    @pl.when(lt == pl.num_programs(1) - 1)
    def _():
        sout_ref[...] = s_scr[...]


def mixer(x, z, y_ssm, pos, mem_k, mem_v, s0, ret_gn, w_ret_o, w_ssm_glu, w_ssm_o, w_x_o, w_out, *, nb, tl):
    bsz, length, dm = x.shape
    chunk = min(CHUNK, length)
    nl = length // tl
    rows = nb * tl
    qk = RET_HEADS * HEAD_DIM
    sw = y_ssm.shape[1]
    xw = X_HEADS * HEAD_DIM
    gate_col = (4 * qk + sw + xw)
    assert gate_col % (3 * dm) == 0 and (4 * qk + sw) % xw == 0
    dmask, qw, kw, tile_decay = retention_tables(tl, chunk)
    cosf, sinf = rope_tables(pos)
    scale = HEAD_DIM ** -0.5
    row_map = lambda b, l: (b * nl + l, 0)
    tab_map = lambda b, l: (l, 0)
    st_spec = pl.BlockSpec((nb, RET_HEADS, HEAD_DIM, HEAD_DIM), lambda b, l: (b, 0, 0, 0))
    mem_spec = pl.BlockSpec((nb,) + mem_k.shape[1:], lambda b, l: (b,) + (0,) * (mem_k.ndim - 1))
    h, s_out = pl.pallas_call(
        functools.partial(_mixer_kernel, nb=nb, tl=tl, tile_decay=tile_decay),
        grid=(bsz // nb, nl),
        in_specs=[pl.BlockSpec((rows, dm), row_map),
                  pl.BlockSpec((rows, 4 * qk), row_map),
                  pl.BlockSpec((rows, xw), lambda b, l: (b * nl + l, (4 * qk + sw) // xw)),
                  pl.BlockSpec((rows, 3 * dm), lambda b, l: (b * nl + l, gate_col // (3 * dm))),
                  pl.BlockSpec((rows, sw), row_map),
                  pl.BlockSpec((tl, HEAD_DIM), tab_map), pl.BlockSpec((tl, HEAD_DIM), tab_map),
                  pl.BlockSpec((tl, HEAD_DIM), tab_map), pl.BlockSpec((tl, HEAD_DIM), tab_map),
                  _resident(dmask.shape), _resident(qw.shape), _resident(kw.shape),
                  mem_spec, mem_spec, st_spec, _resident((1, qk)),
                  _resident(w_ret_o.shape), _resident(w_ssm_glu.shape), _resident(w_ssm_o.shape),
                  _resident(w_x_o.shape), _resident(w_out.shape)],
        out_specs=[pl.BlockSpec((rows, dm), row_map), st_spec],
        out_shape=[jax.ShapeDtypeStruct((bsz * length, dm), F32),
                   jax.ShapeDtypeStruct(s0.shape, F32)],
        scratch_shapes=[pltpu.VMEM((nb, RET_HEADS, HEAD_DIM, HEAD_DIM), F32),
                        pltpu.VMEM((rows, qk), BF16), pltpu.VMEM((rows, xw), BF16),
                        pltpu.VMEM((rows, w_ssm_o.shape[0]), BF16), pltpu.VMEM((rows, dm), BF16)],
        compiler_params=_params(2), name="mixer",
    )(x.reshape(bsz * length, dm), z, z, z, y_ssm,
      cosf * scale, sinf * scale, cosf, sinf, dmask, qw, kw,
      mem_k, mem_v, s0, ret_gn.reshape(1, qk), w_ret_o, w_ssm_glu, w_ssm_o, w_x_o, w_out)
    return h, s_out


def _router_kernel(hp_ref, hs_ref, nw_ref, wrt_ref, br_ref, tri_ref, low_ref,
                   xn_ref, gw_ref, crow_ref, cnt_ref, *, prompt_tiles):
    h = jnp.where(pl.program_id(0) < prompt_tiles, hp_ref[...], hs_ref[...])
    xn = _rms(h, nw_ref[...])
    xn_ref[...] = xn.astype(BF16)
    logits = lax.dot_general(wrt_ref[...], xn, NT_DIMS, precision=lax.Precision.HIGHEST,
                             preferred_element_type=F32) + br_ref[...]
    ne = logits.shape[0]
    iota = lax.broadcasted_iota(jnp.int32, logits.shape, 0)
    rest = logits
    sel = jnp.zeros(logits.shape, jnp.bool_)
    vals, idxs = [], []
    for _ in range(TOP_K):
        m = jnp.max(rest, axis=0, keepdims=True)
        ix = jnp.min(jnp.where(rest == m, iota, ne), axis=0, keepdims=True)
        hit = iota == ix
        vals.append(m)
        idxs.append(ix)
        sel = jnp.logical_or(sel, hit)
        rest = jnp.where(hit, -jnp.inf, rest)
    es = [jnp.exp(v - vals[0]) for v in vals]
    tot = es[0] + es[1] + es[2] + es[3]
    before = jnp.dot(sel.astype(BF16), tri_ref[...], preferred_element_type=F32)
    cnt = jnp.sum(sel.astype(F32), axis=1, keepdims=True)
    seg = jnp.floor((cnt + (ROW_ALIGN - 1.0)) * (1.0 / ROW_ALIGN)) * ROW_ALIGN
    start = jnp.dot(low_ref[...], jnp.broadcast_to(seg, before.shape), precision=lax.Precision.HIGHEST,
                    preferred_element_type=F32)
    place = start + before
    for k in range(TOP_K):
        gw_ref[k:k + 1, :] = es[k] / tot
        crow_ref[k:k + 1, :] = jnp.sum(jnp.where(iota == idxs[k], place, 0.0), axis=0,
                                       keepdims=True).astype(jnp.int32)
    cnt_ref[...] = jnp.broadcast_to(cnt, cnt_ref.shape)


def router(h_p, h_s, norm_ffn, w_router, b_router):
    dm = h_p.shape[1]
    ne = w_router.shape[1]
    tt = ROW_TILE
    npt, nst = h_p.shape[0] // tt, h_s.shape[0] // tt
    t = (npt + nst) * tt
    tri = jnp.asarray(np.triu(np.ones((tt, tt), np.float32), k=1), BF16)
    low = jnp.asarray(np.tril(np.ones((ne, ne), np.float32), k=-1))
    tok_spec = pl.BlockSpec((TOP_K, tt), lambda i: (0, i))
    return pl.pallas_call(
        functools.partial(_router_kernel, prompt_tiles=npt), grid=(npt + nst,),
        in_specs=[pl.BlockSpec((tt, dm), lambda i: (jnp.minimum(i, npt - 1), 0)),
                  pl.BlockSpec((tt, dm), lambda i: (jnp.maximum(i - npt, 0), 0)),
                  _resident((1, dm)), _resident((ne, dm)), _resident((ne, 1)), _resident((tt, tt)),
                  _resident((ne, ne))],
        out_specs=[pl.BlockSpec((tt, dm), lambda i: (i, 0)), tok_spec, tok_spec,
                   pl.BlockSpec((ne, 128), lambda i: (i, 0))],
        out_shape=[jax.ShapeDtypeStruct((t, dm), BF16),
                   jax.ShapeDtypeStruct((TOP_K, t), F32), jax.ShapeDtypeStruct((TOP_K, t), jnp.int32),
                   jax.ShapeDtypeStruct(((npt + nst) * ne, 128), F32)],
        compiler_params=_params(), name="router",
    )(h_p, h_s, norm_ffn.reshape(1, dm), w_router.T, b_router.reshape(ne, 1), tri, low)


def _pack_bf16_pairs(x):
    n = x.shape[1] // 2
    lo = lax.bitcast_convert_type(x[:, :n].astype(BF16).astype(F32), jnp.int32)
    hi = lax.bitcast_convert_type(x[:, n:].astype(BF16).astype(F32), jnp.int32)
    return lax.shift_right_logical(lo, 16) | (hi & -65536)


def _unpack_bf16_pairs(u):
    lo = lax.bitcast_convert_type(lax.shift_left(u, 16), F32).astype(BF16)
    hi = lax.bitcast_convert_type(u & -65536, F32).astype(BF16)
    return lo, hi


def _split_count(n, fn):
    def quad(jq, carry):
        fn(4 * jq, 4)
        return carry

    lax.fori_loop(0, n // 4, quad, 0)

    @pl.when(n % 4 >= 2)
    def _():
        fn(n // 4 * 4, 2)

    @pl.when(n % 2 == 1)
    def _():
        fn(n // 2 * 2, 1)


def _chunk_loop(n, fn):
    def quad(jq, carry):
        for u in range(4):
            fn(4 * jq + u)
        return carry

    def single(j, carry):
        fn(j)
        return carry

    lax.fori_loop(0, n // 4, quad, 0)
    lax.fori_loop(n // 4 * 4, n, single, 0)


def _dispatch_kernel(dst_ref, tch_ref, zs_ref, zn_ref, used_ref,
                     xn_ref, crow_ref, xs_ref, cbuf, zbuf, sems, zsem, *, ne, bm):
    i = pl.program_id(0)
    nt = pl.num_programs(0)
    slot = lax.rem(i, 2)
    tt = xn_ref.shape[0]
    cr = cbuf.shape[1]
    ra = ROW_ALIGN
    n_blocks = xs_ref.shape[0] // bm

    def chunk_copy(sl, src_row, dst_row, rows):
        return pltpu.make_async_copy(cbuf.at[sl, pl.ds(src_row, rows)], xs_ref.at[pl.ds(dst_row, rows)],
                                     sems.at[sl])

    def wait_chunks(sl, n):
        _split_count(n, lambda j, m: chunk_copy(sl, 0, 0, ra * m).wait())

    def tail_copy(e, j):
        return pltpu.make_async_copy(zbuf.at[pl.ds(0, ra)],
                                     xs_ref.at[pl.ds(pl.multiple_of(zs_ref[e] + ra * j, ra), ra)], zsem)

    def block_copy(b):
        return pltpu.make_async_copy(zbuf, xs_ref.at[pl.ds(pl.multiple_of(b * bm, bm), bm)], zsem)

    def zero_fill(start):
        def per_expert(e, carry):
            def per_chunk(j, c2):
                (tail_copy(e, j).start() if start else tail_copy(e, j).wait())
                return c2
            lax.fori_loop(0, zn_ref[e], per_chunk, 0)
            return carry
        lax.fori_loop(0, ne, per_expert, 0)

        def per_block(b, carry):
            (block_copy(b).start() if start else block_copy(b).wait())
            return carry
        lax.fori_loop(used_ref[0], n_blocks, per_block, 0)

    @pl.when(i == 0)
    def _():
        zbuf[...] = jnp.zeros_like(zbuf)
        zero_fill(True)
        zero_fill(False)

    @pl.when(i >= 2)
    def _():
        wait_chunks(slot, tch_ref[jnp.maximum(i - 2, 0)])

    crow = crow_ref[...]
    rows = lax.broadcasted_iota(jnp.int32, (cr, tt), 0)
    hit = rows == crow[0:1, :]
    for k in range(1, TOP_K):
        hit = jnp.logical_or(hit, rows == crow[k:k + 1, :])
    packed = _pack_bf16_pairs(jnp.dot(jnp.where(hit, 1.0, 0.0).astype(BF16), xn_ref[...],
                                      preferred_element_type=F32))
    cbuf[slot] = packed

    chunks_per_tile = cr // ra
    _chunk_loop(tch_ref[i], lambda c: chunk_copy(
        slot, pl.multiple_of(c * ra, ra), pl.multiple_of(dst_ref[i * chunks_per_tile + c], ra), ra).start())

    @pl.when(i == nt - 1)
    def _():
        wait_chunks(slot, tch_ref[i])
        wait_chunks(1 - slot, jnp.where(nt >= 2, tch_ref[jnp.maximum(i - 1, 0)], 0))


def dispatch(xn, crow, tables, n_rows, ne):
    t, dm = xn.shape
    tt = ROW_TILE
    bm = FFN_BLOCK
    cr = _compact_rows(ne)
    grid_spec = pltpu.PrefetchScalarGridSpec(
        num_scalar_prefetch=5, grid=(t // tt,),
        in_specs=[pl.BlockSpec((tt, dm), lambda i, *_: (i, 0)),
                  pl.BlockSpec((TOP_K, tt), lambda i, *_: (0, i))],
        out_specs=pl.BlockSpec(memory_space=pl.ANY),
        scratch_shapes=[pltpu.VMEM((2, cr, dm // 2), jnp.int32), pltpu.VMEM((bm, dm // 2), jnp.int32),
                        pltpu.SemaphoreType.DMA((2,)), pltpu.SemaphoreType.DMA(())])
    return pl.pallas_call(
        functools.partial(_dispatch_kernel, ne=ne, bm=bm), grid_spec=grid_spec,
        out_shape=jax.ShapeDtypeStruct((n_rows, dm // 2), jnp.int32),
        compiler_params=_params(), name="dispatch",
    )(*tables, xn, crow)


def _ffn_kernel(first_ref, nblk_ref, used_ref, xs_hbm, wgu_ref, bgu_ref, wd_ref, bd_ref, y_hbm,
                wgu_bf, wd_bf, xbuf, ybuf, xsem, ysem, zsem, *, bm):
    e = pl.program_id(0)
    n = nblk_ref[e]
    b0 = first_ref[e]
    n_blocks = y_hbm.shape[0] // bm

    def block_rows(b):
        return pl.ds(pl.multiple_of(b * bm, bm), bm)

    def x_copy(j, sl):
        return pltpu.make_async_copy(xs_hbm.at[block_rows(b0 + j)], xbuf.at[sl], xsem.at[sl])

    def y_copy(j, sl):
        return pltpu.make_async_copy(ybuf.at[sl], y_hbm.at[block_rows(b0 + j)], ysem.at[sl])

    @pl.when(n > 0)
    def _():
        x_copy(0, 0).start()
        wgu_bf[...] = wgu_ref[0].astype(BF16)
        wd_bf[...] = wd_ref[0].astype(BF16)

        def block(j, carry):
            sl = lax.rem(j, 2)
            x_copy(j, sl).wait()

            @pl.when(j + 1 < n)
            def _():
                x_copy(j + 1, 1 - sl).start()

            @pl.when(j >= 2)
            def _():
                y_copy(j - 2, sl).wait()

            x_lo, x_hi = _unpack_bf16_pairs(xbuf[sl])
            kh = x_lo.shape[1]
            hgu = (jnp.dot(x_lo, wgu_bf[:kh, :], preferred_element_type=F32)
                   + jnp.dot(x_hi, wgu_bf[kh:, :], preferred_element_type=F32) + bgu_ref[0])
            ff = hgu.shape[1] // 2
            gate = jnp.minimum(hgu[:, :ff], SWIGLU_LIMIT)
            up = jnp.clip(hgu[:, ff:], -SWIGLU_LIMIT, SWIGLU_LIMIT)
            act = (up + 1.0) * gate * _sigmoid(SWIGLU_ALPHA * gate)
            ybuf[sl] = _pack_bf16_pairs(jnp.dot(act.astype(BF16), wd_bf[...], preferred_element_type=F32)
                                        + bd_ref[0])
            y_copy(j, sl).start()
            return carry

        lax.fori_loop(0, n, block, 0)

        @pl.when(n >= 2)
        def _():
            y_copy(n - 2, lax.rem(n, 2)).wait()
        y_copy(n - 1, lax.rem(n - 1, 2)).wait()

    @pl.when(e == pl.num_programs(0) - 1)
    def _():
        ybuf[0] = jnp.zeros(ybuf.shape[1:], ybuf.dtype)

        def zero_copy(b):
            return pltpu.make_async_copy(ybuf.at[0], y_hbm.at[block_rows(b)], zsem)

        def start(b, carry):
            zero_copy(b).start()
            return carry

        def wait(b, carry):
            zero_copy(b).wait()
            return carry

        lax.fori_loop(used_ref[0], n_blocks, start, 0)
        lax.fori_loop(used_ref[0], n_blocks, wait, 0)


def expert_ffn(xs, first_block, n_block, n_used, w_gate_up, b_gate_up, w_down, b_down):
    n_rows = xs.shape[0]
    ne, dm, ff2 = w_gate_up.shape
    bm = FFN_BLOCK
    grid_spec = pltpu.PrefetchScalarGridSpec(
        num_scalar_prefetch=3, grid=(ne,),
        in_specs=[pl.BlockSpec(memory_space=pl.ANY),
                  pl.BlockSpec((1, dm, ff2), lambda e, *_: (e, 0, 0)),
                  pl.BlockSpec((1, 1, ff2), lambda e, *_: (e, 0, 0)),
                  pl.BlockSpec((1, ff2 // 2, dm), lambda e, *_: (e, 0, 0)),
                  pl.BlockSpec((1, 1, dm), lambda e, *_: (e, 0, 0))],
        out_specs=pl.BlockSpec(memory_space=pl.ANY),
        scratch_shapes=[pltpu.VMEM((dm, ff2), BF16), pltpu.VMEM((ff2 // 2, dm), BF16),
                        pltpu.VMEM((2, bm, dm // 2), jnp.int32), pltpu.VMEM((2, bm, dm // 2), jnp.int32),
                        pltpu.SemaphoreType.DMA((2,)), pltpu.SemaphoreType.DMA((2,)),
                        pltpu.SemaphoreType.DMA(())])
    return pl.pallas_call(
        functools.partial(_ffn_kernel, bm=bm), grid_spec=grid_spec,
        out_shape=jax.ShapeDtypeStruct((n_rows, dm // 2), jnp.int32),
        compiler_params=_params(), name="expert_ffn",
    )(first_block, n_block, n_used, xs, w_gate_up, b_gate_up.reshape(ne, 1, ff2), w_down,
      b_down.reshape(ne, 1, dm))


def _combine_kernel(dst_ref, tch_ref,
                    crow_ref, gw_ref, hp_ref, hs_ref, fn_ref, yr_ref, yp_ref, ys_ref, ybuf, sems,
                    *, prompt_tiles):
    i = pl.program_id(0)
    nt = pl.num_programs(0)
    slot = lax.rem(i, 2)
    tt = hp_ref.shape[0]
    cr = ybuf.shape[1]
    ra = ROW_ALIGN

    def chunk_copy(sl, src_row, dst_row, rows):
        return pltpu.make_async_copy(yr_ref.at[pl.ds(src_row, rows)], ybuf.at[sl, pl.ds(dst_row, rows)],
                                     sems.at[sl])

    def fetch(tile, sl):
        chunks_per_tile = cr // ra
        _chunk_loop(tch_ref[tile], lambda c: chunk_copy(
            sl, pl.multiple_of(dst_ref[tile * chunks_per_tile + c], ra), pl.multiple_of(c * ra, ra), ra).start())

    @pl.when(i == 0)
    def _():
        ybuf[...] = jnp.zeros_like(ybuf)
        fetch(0, 0)

    @pl.when(i + 1 < nt)
    def _():
        fetch(i + 1, 1 - slot)

    _split_count(tch_ref[i], lambda j, m: chunk_copy(slot, 0, 0, ra * m).wait())

    y_lo, y_hi = _unpack_bf16_pairs(ybuf[slot])
    cols = lax.broadcasted_iota(jnp.int32, (tt, cr), 1)
    q = jnp.zeros((tt, cr), F32)
    for k in range(TOP_K):
        q += jnp.where(cols == crow_ref[:, k:k + 1], gw_ref[:, k:k + 1], 0.0)
    qb = q.astype(BF16)
    moe = jnp.concatenate([jnp.dot(qb, y_lo, preferred_element_type=F32),
                           jnp.dot(qb, y_hi, preferred_element_type=F32)], axis=1)
    h = jnp.where(i < prompt_tiles, hp_ref[...], hs_ref[...])
    out = _rms(h + moe, fn_ref[...])

    @pl.when(i < prompt_tiles)
    def _():
        yp_ref[...] = out

    @pl.when(i >= prompt_tiles)
    def _():
        ys_ref[...] = out


def combine(crow_t, gw_t, h_p, h_s, final_norm, y_rows, tables, ne):
    dm = h_p.shape[1]
    tt = ROW_TILE
    npt, nst = h_p.shape[0] // tt, h_s.shape[0] // tt
    cr = _compact_rows(ne)
    p_map = lambda i, *_: (jnp.minimum(i, npt - 1), 0)
    s_map = lambda i, *_: (jnp.maximum(i - npt, 0), 0)
    grid_spec = pltpu.PrefetchScalarGridSpec(
        num_scalar_prefetch=2, grid=(npt + nst,),
        in_specs=[pl.BlockSpec((tt, TOP_K), lambda i, *_: (i, 0)),
                  pl.BlockSpec((tt, TOP_K), lambda i, *_: (i, 0)),
                  pl.BlockSpec((tt, dm), p_map), pl.BlockSpec((tt, dm), s_map),
                  pl.BlockSpec((1, dm), lambda i, *_: (0, 0)),
                  pl.BlockSpec(memory_space=pl.ANY)],
        out_specs=[pl.BlockSpec((tt, dm), p_map), pl.BlockSpec((tt, dm), s_map)],
        scratch_shapes=[pltpu.VMEM((2, cr, dm // 2), jnp.int32), pltpu.SemaphoreType.DMA((2,))])
    return pl.pallas_call(
        functools.partial(_combine_kernel, prompt_tiles=npt), grid_spec=grid_spec,
        out_shape=[jax.ShapeDtypeStruct(h_p.shape, F32), jax.ShapeDtypeStruct(h_s.shape, F32)],
        compiler_params=_params(), name="combine",
    )(*tables, crow_t, gw_t, h_p, h_s, final_norm.reshape(1, dm), y_rows)


def _compact_rows(ne):
    return -(-(TOP_K * ROW_TILE + ne * (ROW_ALIGN - 1)) // 128) * 128


def moe_and_final_norm(h_p, h_s, norm_ffn, w_router, b_router, w_gate_up, b_gate_up, w_down, b_down, final_norm):
    ne = w_router.shape[1]
    bm = FFN_BLOCK
    ra = ROW_ALIGN
    xn, gw, crow, cnt = router(h_p, h_s, norm_ffn, w_router, b_router)
    t = xn.shape[0]
    nt = t // ROW_TILE
    seg = -(-cnt[:, 0].astype(jnp.int32).reshape(nt, ne) // ra) * ra
    seg_before = jnp.cumsum(seg, axis=0) - seg
    rows_e = jnp.sum(seg, axis=0)
    padded = -(-rows_e // bm) * bm
    pad_ends = jnp.cumsum(padded)
    pad_starts = pad_ends - padded
    n_blocks = -(-(t * TOP_K + nt * ne * (ra - 1) + ne * (bm - 1)) // bm)
    n_used = pad_ends[-1] // bm
    experts = jnp.arange(ne, dtype=jnp.int32)
    chunks = seg // ra
    chunk_end = jnp.cumsum(chunks, axis=1)
    c_ids = jnp.arange(_compact_rows(ne) // ra, dtype=jnp.int32)
    owner = jnp.sum((chunk_end[:, None, :] <= c_ids[None, :, None]).astype(jnp.int32), axis=2)
    seg_shift = pad_starts[None, :] + seg_before - ra * (chunk_end - chunks)
    dst = ra * c_ids[None, :] + jnp.sum(
        jnp.where(owner[:, :, None] == experts[None, None, :], seg_shift[:, None, :], 0), axis=2)
    seg_tables = (dst.reshape(-1), chunk_end[:, -1])
    fill_tables = (pad_starts + rows_e, (padded - rows_e) // ra, n_used.reshape(1))
    to_i32 = lambda xs: tuple(x.astype(jnp.int32) for x in xs)
    xs = dispatch(xn, crow, to_i32(seg_tables + fill_tables), n_blocks * bm, ne)
    y_rows = expert_ffn(xs, *to_i32((pad_starts // bm, padded // bm, n_used.reshape(1))),
                        w_gate_up, b_gate_up, w_down, b_down)
    return combine(crow.T, gw.T, h_p, h_s, final_norm, y_rows, to_i32(seg_tables), ne)


def kernel(x_prompt, x_sample, cache_mem_k, cache_mem_v, state_ret, state_ssm_re, state_ssm_im, mem_prompt, norm_mix, w_in, ret_gn, w_ret_o, ssm_lam_re, ssm_lam_im, ssm_log_dt, ssm_b_re, ssm_b_im, ssm_c_re, ssm_c_im, ssm_d, w_ssm_glu, w_ssm_o, mem_norm, w_mem_kv, w_x_o, w_out, norm_ffn, w_router, b_router, w_gate_up, b_gate_up, w_down, b_down, final_norm):
    assert norm_mix.shape[0] == 1, "single-layer step"
    bp, lp, dm = x_prompt.shape
    bs, ls, _ = x_sample.shape
    n_mem = mem_prompt.shape[1]
    xw = X_HEADS * HEAD_DIM
    qk = RET_HEADS * HEAD_DIM
    sw = ssm_d.shape[1]
    g = ssm_lam_re.shape[1]

    w_in_b = w_in[0].astype(BF16)
    tables = s5_tables(ssm_lam_re[0], ssm_lam_im[0], ssm_log_dt[0], ssm_b_re[0], ssm_b_im[0],
                       ssm_c_re[0], ssm_c_im[0], ssm_d[0])
    mix_w = (ret_gn[0], w_ret_o[0].astype(BF16), w_ssm_glu[0].astype(BF16), w_ssm_o[0].astype(BF16),
             w_x_o[0].astype(BF16), w_out[0].astype(BF16))

    kv = norm_matmul(mem_prompt.reshape(bp * n_mem, dm), mem_norm[0], w_mem_kv[0].astype(BF16), F32)
    mk_p = kv[:, :xw].reshape(bp, n_mem * X_HEADS, HEAD_DIM)
    mv_p = kv[:, xw:].reshape(bp, n_mem * X_HEADS, HEAD_DIM)

    def group(x, pos, mem_k, mem_v, s_ret, h_re, h_im, nb, tl):
        bsz, length, _ = x.shape
        z, u = norm_matmul(x.reshape(bsz * length, dm), norm_mix[0], w_in_b, BF16,
                           f32_cols=(4 * qk, 4 * qk + sw),
                           acts=((3 * qk, 4 * qk, "silu"), (4 * qk + sw + xw, w_in_b.shape[1], "sigmoid")))
        y, hf_re, hf_im = s5_apply(u, bsz, h_re, h_im, tables)
        h, s_new = mixer(x, z, y, pos, mem_k, mem_v, s_ret, *mix_w, nb=nb, tl=tl)
        return h, s_new, hf_re, hf_im

    zero_ret = jnp.zeros((bp, RET_HEADS, HEAD_DIM, HEAD_DIM), F32)
    zero_ssm = jnp.zeros((bp, g, SSM_STATE), F32)
    h_p, ret_p, sre_p, sim_p = group(x_prompt, jnp.arange(lp, dtype=jnp.int32), mk_p, mv_p,
                                     zero_ret, zero_ssm, zero_ssm, 1, ROW_TILE)
    h_s, ret_s, sre_s, sim_s = group(x_sample, PAST_LEN + jnp.arange(ls, dtype=jnp.int32),
                                     cache_mem_k[0].reshape(bs, n_mem * X_HEADS, HEAD_DIM),
                                     cache_mem_v[0].reshape(bs, n_mem * X_HEADS, HEAD_DIM),
                                     state_ret[0], state_ssm_re[0], state_ssm_im[0], ROW_TILE // ls, ls)

    y_p, y_s = moe_and_final_norm(h_p, h_s, norm_ffn[0], w_router[0], b_router[0],
                                  w_gate_up[0], b_gate_up[0], w_down[0], b_down[0], final_norm)
    return (y_p.reshape(bp, lp, dm), y_s.reshape(bs, ls, dm), ret_p[None], sre_p[None], sim_p[None],
            mk_p.reshape(1, bp, n_mem, X_HEADS, HEAD_DIM), mv_p.reshape(1, bp, n_mem, X_HEADS, HEAD_DIM),
            ret_s[None], sre_s[None], sim_s[None])
```

```python
import functools
import math

import jax
import jax.numpy as jnp
import numpy as np
from jax import lax
from jax.experimental import pallas as pl
from jax.experimental.pallas import tpu as pltpu

F32 = jnp.float32
BF16 = jnp.bfloat16

EPS = 1e-6
CHUNK = 64
PAST_LEN = 2048
ROPE_BASE = 10000.0
RET_HEADS = 4
X_HEADS = 4
HEAD_DIM = 128
SSM_GROUP = 16
SSM_STATE = 64
TOP_K = 4
SWIGLU_ALPHA = 1.702
SWIGLU_LIMIT = 7.0

VMEM_LIMIT = 52 * 1024 * 1024
S5_CHUNK = 8
S5_LANES = 128
ROW_TILE = 256
FFN_BLOCK = 256
ROW_ALIGN = 8
MIX_COLS = 256
NT_DIMS = (((1,), (1,)), ((), ()))
TN_DIMS = (((0,), (0,)), ((), ()))


def _params(n_axes=1):
    return pltpu.CompilerParams(dimension_semantics=("arbitrary",) * n_axes,
                                vmem_limit_bytes=VMEM_LIMIT)


def _resident(shape):
    nd = len(shape)
    return pl.BlockSpec(shape, lambda *_: (0,) * nd, pipeline_mode=pl.Buffered(1))


def _rms(x, w):
    return x * lax.rsqrt(jnp.mean(x * x, axis=-1, keepdims=True) + EPS) * w


def _sigmoid(x):
    return 0.5 * jnp.tanh(0.5 * x) + 0.5


_ACTIVATIONS = {"sigmoid": _sigmoid, "silu": lambda v: v * _sigmoid(v)}


def _norm_matmul_kernel(x_ref, nw_ref, w_ref, o_ref, *f32_refs, n_chunk, f32_cols, acts):
    xb = _rms(x_ref[...], nw_ref[...]).astype(BF16)
    for n0 in range(0, o_ref.shape[1], n_chunk):
        r = jnp.dot(xb, w_ref[:, n0:n0 + n_chunk], preferred_element_type=F32)
        if f32_cols is not None and n0 <= f32_cols[0] and f32_cols[1] <= n0 + n_chunk:
            f32_refs[0][...] = r[:, f32_cols[0] - n0:f32_cols[1] - n0]
        cuts = sorted({n0, n0 + n_chunk} | {c for lo, hi, _ in acts for c in (lo, hi) if n0 < c < n0 + n_chunk})
        for a, b in zip(cuts[:-1], cuts[1:]):
            piece = r[:, a - n0:b - n0]
            for lo, hi, kind in acts:
                if lo <= a and b <= hi:
                    piece = _ACTIVATIONS[kind](piece)
            o_ref[:, a:b] = piece.astype(o_ref.dtype)


def norm_matmul(x, nw, w, out_dtype, f32_cols=None, acts=()):
    t, d = x.shape
    n = w.shape[1]
    n_chunk = min(n, 1024)
    out_specs = [pl.BlockSpec((ROW_TILE, n), lambda i: (i, 0))]
    out_shape = [jax.ShapeDtypeStruct((t, n), out_dtype)]
    if f32_cols is not None:
        lo, hi = f32_cols
        assert lo // n_chunk == (hi - 1) // n_chunk
        out_specs.append(pl.BlockSpec((ROW_TILE, hi - lo), lambda i: (i, 0)))
        out_shape.append(jax.ShapeDtypeStruct((t, hi - lo), F32))
    out = pl.pallas_call(
        functools.partial(_norm_matmul_kernel, n_chunk=n_chunk, f32_cols=f32_cols, acts=tuple(acts)),
        grid=(t // ROW_TILE,),
        in_specs=[pl.BlockSpec((ROW_TILE, d), lambda i: (i, 0)), _resident((1, d)), _resident((d, n))],
        out_specs=out_specs, out_shape=out_shape,
        compiler_params=_params(), name="norm_matmul",
    )(x, nw.reshape(1, d), w)
    return out if f32_cols is not None else out[0]


def s5_tables(lam_re, lam_im, log_dt, b_re, b_im, c_re, c_im, d_skip):
    g, n, p = b_re.shape
    s = S5_CHUNK
    gl = S5_LANES // p
    j = g // gl
    hi = lax.Precision.HIGHEST
    dt = jnp.exp(log_dt)[:, None]
    a_re = jnp.exp(lam_re * dt) * jnp.cos(lam_im * dt)
    a_im = jnp.exp(lam_re * dt) * jnp.sin(lam_im * dt)
    den = lam_re * lam_re + lam_im * lam_im
    nr, ni = a_re - 1.0, a_im
    co_re = (nr * lam_re + ni * lam_im) / den
    co_im = (ni * lam_re - nr * lam_im) / den
    bb_re = co_re[..., None] * b_re - co_im[..., None] * b_im
    bb_im = co_re[..., None] * b_im + co_im[..., None] * b_re
    tau = jnp.arange(s + 1, dtype=F32)[:, None, None]
    pw_mag = jnp.exp(lam_re * dt * tau)
    pw_re = pw_mag * jnp.cos(lam_im * dt * tau)
    pw_im = pw_mag * jnp.sin(lam_im * dt * tau)
    ca_re = c_re[None] * pw_re[:, :, None, :] - c_im[None] * pw_im[:, :, None, :]
    ca_im = c_re[None] * pw_im[:, :, None, :] + c_im[None] * pw_re[:, :, None, :]
    kq = (jnp.einsum('tgpn,gnq->gtqp', ca_re[:s], bb_re, precision=hi)
          - jnp.einsum('tgpn,gnq->gtqp', ca_im[:s], bb_im, precision=hi))
    ts = np.arange(s)
    lag_onehot = (ts[None, None, :] - ts[None, :, None] == ts[:, None, None]).astype(np.float32)
    rev = s - 1 - ts
    w_re = pw_re[rev][:, :, :, None] * bb_re[None] - pw_im[rev][:, :, :, None] * bb_im[None]
    w_im = pw_re[rev][:, :, :, None] * bb_im[None] + pw_im[rev][:, :, :, None] * bb_re[None]
    m_c = (jnp.einsum('gxqp,xst->gsqtp', kq, lag_onehot, precision=hi)
           .reshape(j, gl, s, p, s * p).transpose(0, 2, 1, 3, 4).reshape(j, s * gl * p, s * p))
    w_c = (jnp.stack([w_re, w_im]).reshape(2, s, j, gl, n, p).transpose(2, 1, 3, 5, 0, 4)
           .reshape(j, s * gl * p, 2 * n))
    v_c = (jnp.stack([ca_re[1:], -ca_im[1:]]).reshape(2, s, j, gl, p, n).transpose(2, 0, 3, 5, 1, 4)
           .reshape(j, 2 * gl * n, s * p))
    fl = s * gl * p
    c_io = np.arange(fl)
    c_st = np.arange(2 * gl * n)
    k_io = np.arange(s * p)
    k_st = np.arange(2 * n)
    spread_io = ((k_io[:, None] // p == c_io[None, :] // (gl * p)) & (k_io[:, None] % p == c_io[None, :] % p))
    spread_st = ((k_st[:, None] // n == c_st[None, :] // (gl * n)) & (k_st[:, None] % n == c_st[None, :] % n))
    grp_io = (c_io // p) % gl
    grp_st = (c_st // n) % gl

    def expand(compact, spread, row_grp, col_grp):
        full = jnp.einsum('jrk,kc->jrc', compact.astype(BF16), jnp.asarray(spread, BF16),
                          preferred_element_type=F32)
        return jnp.where(jnp.asarray(row_grp[:, None] == col_grp[None, :]), full, 0.0).astype(BF16)

    m = expand(m_c, spread_io, grp_io, grp_io)
    w = expand(w_c, spread_st, grp_io, grp_st)
    v = expand(v_c, spread_io, grp_st, grp_io)
    a_s_re = pw_re[s].reshape(1, g * n)
    a_s_im = pw_im[s].reshape(1, g * n)
    dtab = jnp.broadcast_to(d_skip.reshape(j, 1, 1, gl * p), (j, 1, s, gl * p)).reshape(j, 1, s * gl * p)
    return m, w, v, a_s_re, a_s_im, dtab


def _s5_flat(u_ref):
    return jnp.concatenate([u_ref[:, t, :] for t in range(u_ref.shape[1])], axis=1)


def _s5a_kernel(u_ref, w_ref, ire_ref, iim_ref):
    r = jnp.dot(_s5_flat(u_ref).astype(BF16), w_ref[0], preferred_element_type=F32)
    half = r.shape[1] // 2
    ire_ref[...] = r[:, :half]
    iim_ref[...] = r[:, half:]


def _s5scan_kernel(ire_ref, iim_ref, ar_ref, ai_ref, h0r_ref, h0i_ref,
                   hpr_ref, hpi_ref, hfr_ref, hfi_ref):
    nb, nc, _ = ire_ref.shape
    ar, ai = ar_ref[...], ai_ref[...]

    def body(c, carry):
        out = []
        for b in range(nb):
            hr, hi = carry[2 * b], carry[2 * b + 1]
            hpr_ref[b, pl.ds(c, 1), :] = hr
            hpi_ref[b, pl.ds(c, 1), :] = hi
            out.append(ar * hr - ai * hi + ire_ref[b, pl.ds(c, 1), :])
            out.append(ar * hi + ai * hr + iim_ref[b, pl.ds(c, 1), :])
        return tuple(out)

    init = []
    for b in range(nb):
        init += [h0r_ref[b], h0i_ref[b]]
    fin = lax.fori_loop(0, nc, body, tuple(init))
    for b in range(nb):
        hfr_ref[b] = fin[2 * b]
        hfi_ref[b] = fin[2 * b + 1]


def _s5b_kernel(u_ref, hpr_ref, hpi_ref, m_ref, v_ref, d_ref, y_ref):
    uf = _s5_flat(u_ref)
    half = hpr_ref.shape[1]
    y = d_ref[0] * uf + jnp.dot(uf.astype(BF16), m_ref[0], preferred_element_type=F32)
    y += jnp.dot(hpr_ref[...].astype(BF16), v_ref[0, :half, :], preferred_element_type=F32)
    y += jnp.dot(hpi_ref[...].astype(BF16), v_ref[0, half:, :], preferred_element_type=F32)
    lanes = y_ref.shape[2]
    for t in range(y_ref.shape[1]):
        y_ref[:, t, :] = y[:, t * lanes:(t + 1) * lanes]


def s5_apply(u, bsz, h0_re, h0_im, tables):
    m, w, v, a_re, a_im, dtab = tables
    tokens, width = u.shape
    nj = m.shape[0]
    s = S5_CHUNK
    rows = tokens // s
    nc = rows // bsz
    lanes = a_re.shape[1]
    half = w.shape[2] // 2
    fl = m.shape[1]
    rt = min(rows, 512)
    u3 = u.reshape(rows, s, width)
    u_spec = pl.BlockSpec((rt, s, S5_LANES), lambda j, r: (r, 0, j))
    st_spec = pl.BlockSpec((rt, half), lambda j, r: (r, j))
    tab_spec = pl.BlockSpec((1, fl, fl), lambda j, r: (j, 0, 0))
    inj_re, inj_im = pl.pallas_call(
        _s5a_kernel, grid=(nj, rows // rt),
        in_specs=[u_spec, tab_spec],
        out_specs=[st_spec, st_spec],
        out_shape=[jax.ShapeDtypeStruct((rows, lanes), F32)] * 2,
        compiler_params=_params(2), name="s5_chunk_in",
    )(u3, w)

    sb, lw = 4, 512
    seq_spec = pl.BlockSpec((sb, nc, lw), lambda b, l: (b, 0, l))
    vec_spec = pl.BlockSpec((sb, 1, lw), lambda b, l: (b, 0, l))
    atab_spec = pl.BlockSpec((1, lw), lambda b, l: (0, l))
    hp_re, hp_im, hf_re, hf_im = pl.pallas_call(
        _s5scan_kernel, grid=(bsz // sb, lanes // lw),
        in_specs=[seq_spec, seq_spec, atab_spec, atab_spec, vec_spec, vec_spec],
        out_specs=[seq_spec, seq_spec, vec_spec, vec_spec],
        out_shape=[jax.ShapeDtypeStruct((bsz, nc, lanes), F32)] * 2
        + [jax.ShapeDtypeStruct((bsz, 1, lanes), F32)] * 2,
        compiler_params=_params(2), name="s5_scan",
    )(inj_re.reshape(bsz, nc, lanes), inj_im.reshape(bsz, nc, lanes), a_re, a_im,
      h0_re.reshape(bsz, 1, lanes), h0_im.reshape(bsz, 1, lanes))

    y3 = pl.pallas_call(
        _s5b_kernel, grid=(nj, rows // rt),
        in_specs=[u_spec, st_spec, st_spec, tab_spec, tab_spec,
                  pl.BlockSpec((1, 1, fl), lambda j, r: (j, 0, 0))],
        out_specs=u_spec,
        out_shape=jax.ShapeDtypeStruct((rows, s, width), F32),
        compiler_params=_params(2), name="s5_chunk_out",
    )(u3, hp_re.reshape(rows, lanes), hp_im.reshape(rows, lanes), m, v, dtab)
    g = lanes // SSM_STATE
    return y3.reshape(tokens, width), hf_re.reshape(bsz, g, SSM_STATE), hf_im.reshape(bsz, g, SSM_STATE)


def _retention_gammas():
    return 1.0 - np.exp2(-5.0 - np.arange(RET_HEADS, dtype=np.float64))


def retention_tables(tile, chunk):
    gam = _retention_gammas()[:, None, None]
    i = np.arange(tile)[:, None]
    j = np.arange(tile)[None, :]
    same = (i // chunk) == (j // chunk)
    earlier = (j // chunk) < (i // chunk)
    dist = np.where(same, np.abs(i - j), np.where(earlier, i - j, 0))
    dmask = np.where(same | earlier, gam ** dist[None], 0.0)
    qw = np.broadcast_to((gam[:, :, 0] ** (np.arange(tile) + 1.0))[:, :, None], (RET_HEADS, tile, HEAD_DIM))
    kw = np.broadcast_to((gam[:, :, 0] ** (tile - 1.0 - np.arange(tile)))[:, :, None], (RET_HEADS, tile, HEAD_DIM))
    return (jnp.asarray(dmask, F32), jnp.asarray(qw, F32), jnp.asarray(kw, F32),
            tuple(float(x) for x in _retention_gammas() ** tile))


def rope_tables(pos):
    half = HEAD_DIM // 2
    inv = jnp.exp(-math.log(ROPE_BASE) * 2.0 * jnp.arange(half, dtype=F32) / HEAD_DIM)
    ang = pos.astype(F32)[:, None] * inv[None, :]
    cos, sin = jnp.cos(ang), jnp.sin(ang)
    cosf = jnp.concatenate([cos, cos], axis=1)
    sinf = jnp.concatenate([-sin, sin], axis=1)
    return cosf, sinf


def _mixer_kernel(x_ref, zq_ref, xq_ref, gl_ref, y_ref, cq_ref, sq_ref, ck_ref, sk_ref,
                  dm_ref, qw_ref, kw_ref, mk_ref, mv_ref, s0_ref, gn_ref,
                  wro_ref, wglu_ref, wso_ref, wxo_ref, wout_ref,
                  h_ref, sout_ref, s_scr, o_scr, xo_scr, glu_scr, mg_scr, *, nb, tl, tile_decay):
    lt = pl.program_id(1)
    hd = HEAD_DIM
    qk = RET_HEADS * hd

    @pl.when(lt == 0)
    def _():
        s_scr[...] = s0_ref[...]

    cq, sq, ck, sk = cq_ref[...], sq_ref[...], ck_ref[...], sk_ref[...]
    for n in range(nb):
        rows = slice(n * tl, (n + 1) * tl)
        for h in range(RET_HEADS):
            c0 = h * hd
            q = zq_ref[rows, c0:c0 + hd].astype(F32)
            k = zq_ref[rows, qk + c0:qk + c0 + hd].astype(F32)
            v = zq_ref[rows, 2 * qk + c0:2 * qk + c0 + hd]
            g = zq_ref[rows, 3 * qk + c0:3 * qk + c0 + hd].astype(F32)
            qr = q * cq + pltpu.roll(q, hd // 2, 1) * sq
            kr = k * ck + pltpu.roll(k, hd // 2, 1) * sk
            sc = lax.dot_general(qr.astype(BF16), kr.astype(BF16), NT_DIMS,
                                 preferred_element_type=F32) * dm_ref[h]
            o = jnp.dot(sc.astype(BF16), v, preferred_element_type=F32)
            s_old = s_scr[n, h]
            o += jnp.dot((qr * qw_ref[h]).astype(BF16), s_old.astype(BF16), preferred_element_type=F32)
            kv = lax.dot_general((kr * kw_ref[h]).astype(BF16), v, TN_DIMS, preferred_element_type=F32)
            s_scr[n, h] = tile_decay[h] * s_old + kv
            d = o - jnp.mean(o, axis=-1, keepdims=True)
            on = d * lax.rsqrt(jnp.mean(d * d, axis=-1, keepdims=True) + EPS) * gn_ref[:, c0:c0 + hd]
            o_scr[rows, c0:c0 + hd] = (on * g).astype(BF16)
            mem_rows = pl.ds(h, mk_ref.shape[1] // X_HEADS, stride=X_HEADS)
            mkh = mk_ref[n, mem_rows, :].astype(BF16)
            mvh = mv_ref[n, mem_rows, :].astype(BF16)
            s = lax.dot_general(xq_ref[rows, c0:c0 + hd], mkh, NT_DIMS,
                                preferred_element_type=F32) * (hd ** -0.5)
            e = jnp.exp(s - jnp.max(s, axis=-1, keepdims=True))
            p = e / jnp.sum(e, axis=-1, keepdims=True)
            xo_scr[rows, c0:c0 + hd] = jnp.dot(p.astype(BF16), mvh, preferred_element_type=F32).astype(BF16)

    cw = MIX_COLS
    yb = jax.nn.gelu(y_ref[...]).astype(BF16)
    half = wglu_ref.shape[1] // 2
    for c0 in range(0, half, cw):
        ga = jnp.dot(yb, wglu_ref[:, c0:c0 + cw], preferred_element_type=F32)
        gb = jnp.dot(yb, wglu_ref[:, half + c0:half + c0 + cw], preferred_element_type=F32)
        glu_scr[:, c0:c0 + cw] = (ga * _sigmoid(gb)).astype(BF16)
    dm = h_ref.shape[1]
    for c0 in range(0, dm, cw):
        cols = slice(c0, c0 + cw)
        ret = jnp.dot(o_scr[...], wro_ref[:, cols], preferred_element_type=F32)
        ssm = jnp.dot(glu_scr[...], wso_ref[:, cols], preferred_element_type=F32)
        xb = jnp.dot(xo_scr[...], wxo_ref[:, cols], preferred_element_type=F32)
        merged = (gl_ref[:, c0:c0 + cw].astype(F32) * ret
                  + gl_ref[:, dm + c0:dm + c0 + cw].astype(F32) * ssm
                  + gl_ref[:, 2 * dm + c0:2 * dm + c0 + cw].astype(F32) * xb)
        mg_scr[:, cols] = merged.astype(BF16)
    for c0 in range(0, dm, cw):
        cols = slice(c0, c0 + cw)
        h_ref[:, cols] = x_ref[:, cols] + jnp.dot(mg_scr[...], wout_ref[:, cols], preferred_element_type=F32)

    @pl.when(lt == pl.num_programs(1) - 1)
    def _():
        sout_ref[...] = s_scr[...]


def mixer(x, z, y_ssm, pos, mem_k, mem_v, s0, ret_gn, w_ret_o, w_ssm_glu, w_ssm_o, w_x_o, w_out, *, nb, tl):
    bsz, length, dm = x.shape
    chunk = min(CHUNK, length)
    nl = length // tl
    rows = nb * tl
    qk = RET_HEADS * HEAD_DIM
    sw = y_ssm.shape[1]
    xw = X_HEADS * HEAD_DIM
    gate_col = (4 * qk + sw + xw)
    assert gate_col % (3 * dm) == 0 and (4 * qk + sw) % xw == 0
    dmask, qw, kw, tile_decay = retention_tables(tl, chunk)
    cosf, sinf = rope_tables(pos)
    scale = HEAD_DIM ** -0.5
    row_map = lambda b, l: (b * nl + l, 0)
    tab_map = lambda b, l: (l, 0)
    st_spec = pl.BlockSpec((nb, RET_HEADS, HEAD_DIM, HEAD_DIM), lambda b, l: (b, 0, 0, 0))
    mem_spec = pl.BlockSpec((nb,) + mem_k.shape[1:], lambda b, l: (b,) + (0,) * (mem_k.ndim - 1))
    h, s_out = pl.pallas_call(
        functools.partial(_mixer_kernel, nb=nb, tl=tl, tile_decay=tile_decay),
        grid=(bsz // nb, nl),
        in_specs=[pl.BlockSpec((rows, dm), row_map),
                  pl.BlockSpec((rows, 4 * qk), row_map),
                  pl.BlockSpec((rows, xw), lambda b, l: (b * nl + l, (4 * qk + sw) // xw)),
                  pl.BlockSpec((rows, 3 * dm), lambda b, l: (b * nl + l, gate_col // (3 * dm))),
                  pl.BlockSpec((rows, sw), row_map),
                  pl.BlockSpec((tl, HEAD_DIM), tab_map), pl.BlockSpec((tl, HEAD_DIM), tab_map),
                  pl.BlockSpec((tl, HEAD_DIM), tab_map), pl.BlockSpec((tl, HEAD_DIM), tab_map),
                  _resident(dmask.shape), _resident(qw.shape), _resident(kw.shape),
                  mem_spec, mem_spec, st_spec, _resident((1, qk)),
                  _resident(w_ret_o.shape), _resident(w_ssm_glu.shape), _resident(w_ssm_o.shape),
                  _resident(w_x_o.shape), _resident(w_out.shape)],
        out_specs=[pl.BlockSpec((rows, dm), row_map), st_spec],
        out_shape=[jax.ShapeDtypeStruct((bsz * length, dm), F32),
                   jax.ShapeDtypeStruct(s0.shape, F32)],
        scratch_shapes=[pltpu.VMEM((nb, RET_HEADS, HEAD_DIM, HEAD_DIM), F32),
                        pltpu.VMEM((rows, qk), BF16), pltpu.VMEM((rows, xw), BF16),
                        pltpu.VMEM((rows, w_ssm_o.shape[0]), BF16), pltpu.VMEM((rows, dm), BF16)],
        compiler_params=_params(2), name="mixer",
    )(x.reshape(bsz * length, dm), z, z, z, y_ssm,
      cosf * scale, sinf * scale, cosf, sinf, dmask, qw, kw,
      mem_k, mem_v, s0, ret_gn.reshape(1, qk), w_ret_o, w_ssm_glu, w_ssm_o, w_x_o, w_out)
    return h, s_out


def _router_kernel(hp_ref, hs_ref, nw_ref, wrt_ref, br_ref, tri_ref, low_ref,
                   xn_ref, gw_ref, crow_ref, cnt_ref, *, prompt_tiles):
    h = jnp.where(pl.program_id(0) < prompt_tiles, hp_ref[...], hs_ref[...])
    xn = _rms(h, nw_ref[...])
    xn_ref[...] = xn.astype(BF16)
    logits = lax.dot_general(wrt_ref[...], xn, NT_DIMS, precision=lax.Precision.HIGHEST,
                             preferred_element_type=F32) + br_ref[...]
    ne = logits.shape[0]
    iota = lax.broadcasted_iota(jnp.int32, logits.shape, 0)
    rest = logits
    sel = jnp.zeros(logits.shape, jnp.bool_)
    vals, idxs = [], []
    for _ in range(TOP_K):
        m = jnp.max(rest, axis=0, keepdims=True)
        ix = jnp.min(jnp.where(rest == m, iota, ne), axis=0, keepdims=True)
        hit = iota == ix
        vals.append(m)
        idxs.append(ix)
        sel = jnp.logical_or(sel, hit)
        rest = jnp.where(hit, -jnp.inf, rest)
    es = [jnp.exp(v - vals[0]) for v in vals]
    tot = es[0] + es[1] + es[2] + es[3]
    before = jnp.dot(sel.astype(BF16), tri_ref[...], preferred_element_type=F32)
    cnt = jnp.sum(sel.astype(F32), axis=1, keepdims=True)
    seg = jnp.floor((cnt + (ROW_ALIGN - 1.0)) * (1.0 / ROW_ALIGN)) * ROW_ALIGN
    start = jnp.dot(low_ref[...], jnp.broadcast_to(seg, before.shape), precision=lax.Precision.HIGHEST,
                    preferred_element_type=F32)
    place = start + before
    for k in range(TOP_K):
        gw_ref[k:k + 1, :] = es[k] / tot
        crow_ref[k:k + 1, :] = jnp.sum(jnp.where(iota == idxs[k], place, 0.0), axis=0,
                                       keepdims=True).astype(jnp.int32)
    cnt_ref[...] = jnp.broadcast_to(cnt, cnt_ref.shape)


def router(h_p, h_s, norm_ffn, w_router, b_router):
    dm = h_p.shape[1]
    ne = w_router.shape[1]
    tt = ROW_TILE
    npt, nst = h_p.shape[0] // tt, h_s.shape[0] // tt
    t = (npt + nst) * tt
    tri = jnp.asarray(np.triu(np.ones((tt, tt), np.float32), k=1), BF16)
    low = jnp.asarray(np.tril(np.ones((ne, ne), np.float32), k=-1))
    tok_spec = pl.BlockSpec((TOP_K, tt), lambda i: (0, i))
    return pl.pallas_call(
        functools.partial(_router_kernel, prompt_tiles=npt), grid=(npt + nst,),
        in_specs=[pl.BlockSpec((tt, dm), lambda i: (jnp.minimum(i, npt - 1), 0)),
                  pl.BlockSpec((tt, dm), lambda i: (jnp.maximum(i - npt, 0), 0)),
                  _resident((1, dm)), _resident((ne, dm)), _resident((ne, 1)), _resident((tt, tt)),
                  _resident((ne, ne))],
        out_specs=[pl.BlockSpec((tt, dm), lambda i: (i, 0)), tok_spec, tok_spec,
                   pl.BlockSpec((ne, 128), lambda i: (i, 0))],
        out_shape=[jax.ShapeDtypeStruct((t, dm), BF16),
                   jax.ShapeDtypeStruct((TOP_K, t), F32), jax.ShapeDtypeStruct((TOP_K, t), jnp.int32),
                   jax.ShapeDtypeStruct(((npt + nst) * ne, 128), F32)],
        compiler_params=_params(), name="router",
    )(h_p, h_s, norm_ffn.reshape(1, dm), w_router.T, b_router.reshape(ne, 1), tri, low)


def _pack_bf16_pairs(x):
    n = x.shape[1] // 2
    lo = lax.bitcast_convert_type(x[:, :n].astype(BF16).astype(F32), jnp.int32)
    hi = lax.bitcast_convert_type(x[:, n:].astype(BF16).astype(F32), jnp.int32)
    return lax.shift_right_logical(lo, 16) | (hi & -65536)


def _unpack_bf16_pairs(u):
    lo = lax.bitcast_convert_type(lax.shift_left(u, 16), F32).astype(BF16)
    hi = lax.bitcast_convert_type(u & -65536, F32).astype(BF16)
    return lo, hi


def _split_count(n, fn):
    def quad(jq, carry):
        fn(4 * jq, 4)
        return carry

    lax.fori_loop(0, n // 4, quad, 0)

    @pl.when(n % 4 >= 2)
    def _():
        fn(n // 4 * 4, 2)

    @pl.when(n % 2 == 1)
    def _():
        fn(n // 2 * 2, 1)


def _chunk_loop(n, fn):
    def quad(jq, carry):
        for u in range(4):
            fn(4 * jq + u)
        return carry

    def single(j, carry):
        fn(j)
        return carry

    lax.fori_loop(0, n // 4, quad, 0)
    lax.fori_loop(n // 4 * 4, n, single, 0)


def _dispatch_kernel(dst_ref, tch_ref, zs_ref, zn_ref, used_ref,
                     xn_ref, crow_ref, xs_ref, cbuf, zbuf, sems, zsem, *, ne, bm):
    i = pl.program_id(0)
    nt = pl.num_programs(0)
    slot = lax.rem(i, 2)
    tt = xn_ref.shape[0]
    cr = cbuf.shape[1]
    ra = ROW_ALIGN
    n_blocks = xs_ref.shape[0] // bm

    def chunk_copy(sl, src_row, dst_row, rows):
        return pltpu.make_async_copy(cbuf.at[sl, pl.ds(src_row, rows)], xs_ref.at[pl.ds(dst_row, rows)],
                                     sems.at[sl])

    def wait_chunks(sl, n):
        _split_count(n, lambda j, m: chunk_copy(sl, 0, 0, ra * m).wait())

    def tail_copy(e, j):
        return pltpu.make_async_copy(zbuf.at[pl.ds(0, ra)],
                                     xs_ref.at[pl.ds(pl.multiple_of(zs_ref[e] + ra * j, ra), ra)], zsem)

    def block_copy(b):
        return pltpu.make_async_copy(zbuf, xs_ref.at[pl.ds(pl.multiple_of(b * bm, bm), bm)], zsem)

    def zero_fill(start):
        def per_expert(e, carry):
            def per_chunk(j, c2):
                (tail_copy(e, j).start() if start else tail_copy(e, j).wait())
                return c2
            lax.fori_loop(0, zn_ref[e], per_chunk, 0)
            return carry
        lax.fori_loop(0, ne, per_expert, 0)

        def per_block(b, carry):
            (block_copy(b).start() if start else block_copy(b).wait())
            return carry
        lax.fori_loop(used_ref[0], n_blocks, per_block, 0)

    @pl.when(i == 0)
    def _():
        zbuf[...] = jnp.zeros_like(zbuf)
        zero_fill(True)
        zero_fill(False)

    @pl.when(i >= 2)
    def _():
        wait_chunks(slot, tch_ref[jnp.maximum(i - 2, 0)])

    crow = crow_ref[...]
    rows = lax.broadcasted_iota(jnp.int32, (cr, tt), 0)
    hit = rows == crow[0:1, :]
    for k in range(1, TOP_K):
        hit = jnp.logical_or(hit, rows == crow[k:k + 1, :])
    packed = _pack_bf16_pairs(jnp.dot(jnp.where(hit, 1.0, 0.0).astype(BF16), xn_ref[...],
                                      preferred_element_type=F32))
    cbuf[slot] = packed

    chunks_per_tile = cr // ra
    _chunk_loop(tch_ref[i], lambda c: chunk_copy(
        slot, pl.multiple_of(c * ra, ra), pl.multiple_of(dst_ref[i * chunks_per_tile + c], ra), ra).start())

    @pl.when(i == nt - 1)
    def _():
        wait_chunks(slot, tch_ref[i])
        wait_chunks(1 - slot, jnp.where(nt >= 2, tch_ref[jnp.maximum(i - 1, 0)], 0))


def dispatch(xn, crow, tables, n_rows, ne):
    t, dm = xn.shape
    tt = ROW_TILE
    bm = FFN_BLOCK
    cr = _compact_rows(ne)
    grid_spec = pltpu.PrefetchScalarGridSpec(
        num_scalar_prefetch=5, grid=(t // tt,),
        in_specs=[pl.BlockSpec((tt, dm), lambda i, *_: (i, 0)),
                  pl.BlockSpec((TOP_K, tt), lambda i, *_: (0, i))],
        out_specs=pl.BlockSpec(memory_space=pl.ANY),
        scratch_shapes=[pltpu.VMEM((2, cr, dm // 2), jnp.int32), pltpu.VMEM((bm, dm // 2), jnp.int32),
                        pltpu.SemaphoreType.DMA((2,)), pltpu.SemaphoreType.DMA(())])
    return pl.pallas_call(
        functools.partial(_dispatch_kernel, ne=ne, bm=bm), grid_spec=grid_spec,
        out_shape=jax.ShapeDtypeStruct((n_rows, dm // 2), jnp.int32),
        compiler_params=_params(), name="dispatch",
    )(*tables, xn, crow)


FFN_W_PIECES = (4, 2)


def _ffn_kernel(first_ref, nblk_ref, used_ref, next_ref, wslot_ref, lead_ref,
                xs_hbm, wgu_hbm, bgu_ref, wd_hbm, bd_ref, y_hbm,
                wgu_f32, wd_f32, wgu_bf, wd_bf, xbuf, ybuf, xsem, ysem, wsem, zsem, *, bm):
    e = pl.program_id(0)
    n = nblk_ref[e]
    b0 = first_ref[e]
    n_blocks = y_hbm.shape[0] // bm
    n_pieces = sum(FFN_W_PIECES)

    def block_rows(b):
        return pl.ds(pl.multiple_of(b * bm, bm), bm)

    def x_copy(j, sl):
        return pltpu.make_async_copy(xs_hbm.at[block_rows(b0 + j)], xbuf.at[sl], xsem.at[sl])

    def y_copy(j, sl):
        return pltpu.make_async_copy(ybuf.at[sl], y_hbm.at[block_rows(b0 + j)], ysem.at[sl])

    def w_piece(ex, sl, p):
        src, dst, q, parts = ((wgu_hbm, wgu_f32, p, FFN_W_PIECES[0]) if p < FFN_W_PIECES[0]
                              else (wd_hbm, wd_f32, p - FFN_W_PIECES[0], FFN_W_PIECES[1]))
        rows = src.shape[1] // parts
        return pltpu.make_async_copy(src.at[ex, pl.ds(q * rows, rows)], dst.at[sl, pl.ds(q * rows, rows)],
                                     wsem.at[sl, p])

    @pl.when(n > 0)
    def _():
        ws = wslot_ref[e]
        nxt = next_ref[e]
        x_copy(0, 0).start()

        @pl.when(lead_ref[0] == e)
        def _():
            for p in range(n_pieces):
                w_piece(e, ws, p).start()

        for p in range(n_pieces):
            w_piece(e, ws, p).wait()
        wgu_bf[...] = wgu_f32[ws].astype(BF16)
        wd_bf[...] = wd_f32[ws].astype(BF16)

        def block(j, carry):
            sl = lax.rem(j, 2)
            x_copy(j, sl).wait()

            @pl.when(j + 1 < n)
            def _():
                x_copy(j + 1, 1 - sl).start()

            for p in range(n_pieces):
                @pl.when(jnp.logical_and(j == p, nxt >= 0))
                def _():
                    w_piece(nxt, 1 - ws, p).start()

            @pl.when(j >= 2)
            def _():
                y_copy(j - 2, sl).wait()

            x_lo, x_hi = _unpack_bf16_pairs(xbuf[sl])
            kh = x_lo.shape[1]
            hgu = (jnp.dot(x_lo, wgu_bf[:kh, :], preferred_element_type=F32)
                   + jnp.dot(x_hi, wgu_bf[kh:, :], preferred_element_type=F32) + bgu_ref[0])
            ff = hgu.shape[1] // 2
            gate = jnp.minimum(hgu[:, :ff], SWIGLU_LIMIT)
            up = jnp.clip(hgu[:, ff:], -SWIGLU_LIMIT, SWIGLU_LIMIT)
            act = (up + 1.0) * gate * _sigmoid(SWIGLU_ALPHA * gate)
            ybuf[sl] = _pack_bf16_pairs(jnp.dot(act.astype(BF16), wd_bf[...], preferred_element_type=F32)
                                        + bd_ref[0])
            y_copy(j, sl).start()
            return carry

        lax.fori_loop(0, n, block, 0)

        for p in range(n_pieces):
            @pl.when(jnp.logical_and(p >= n, nxt >= 0))
            def _():
                w_piece(nxt, 1 - ws, p).start()

        @pl.when(n >= 2)
        def _():
            y_copy(n - 2, lax.rem(n, 2)).wait()
        y_copy(n - 1, lax.rem(n - 1, 2)).wait()

    @pl.when(e == pl.num_programs(0) - 1)
    def _():
        ybuf[0] = jnp.zeros(ybuf.shape[1:], ybuf.dtype)

        def zero_copy(b):
            return pltpu.make_async_copy(ybuf.at[0], y_hbm.at[block_rows(b)], zsem)

        def start(b, carry):
            zero_copy(b).start()
            return carry

        def wait(b, carry):
            zero_copy(b).wait()
            return carry

        lax.fori_loop(used_ref[0], n_blocks, start, 0)
        lax.fori_loop(used_ref[0], n_blocks, wait, 0)


def expert_ffn(xs, expert_tables, w_gate_up, b_gate_up, w_down, b_down):
    n_rows = xs.shape[0]
    ne, dm, ff2 = w_gate_up.shape
    bm = FFN_BLOCK
    grid_spec = pltpu.PrefetchScalarGridSpec(
        num_scalar_prefetch=6, grid=(ne,),
        in_specs=[pl.BlockSpec(memory_space=pl.ANY),
                  pl.BlockSpec(memory_space=pl.ANY),
                  pl.BlockSpec((1, 1, ff2), lambda e, *_: (e, 0, 0)),
                  pl.BlockSpec(memory_space=pl.ANY),
                  pl.BlockSpec((1, 1, dm), lambda e, *_: (e, 0, 0))],
        out_specs=pl.BlockSpec(memory_space=pl.ANY),
        scratch_shapes=[pltpu.VMEM((2, dm, ff2), F32), pltpu.VMEM((2, ff2 // 2, dm), F32),
                        pltpu.VMEM((dm, ff2), BF16), pltpu.VMEM((ff2 // 2, dm), BF16),
                        pltpu.VMEM((2, bm, dm // 2), jnp.int32), pltpu.VMEM((2, bm, dm // 2), jnp.int32),
                        pltpu.SemaphoreType.DMA((2,)), pltpu.SemaphoreType.DMA((2,)),
                        pltpu.SemaphoreType.DMA((2, sum(FFN_W_PIECES))), pltpu.SemaphoreType.DMA(())])
    return pl.pallas_call(
        functools.partial(_ffn_kernel, bm=bm), grid_spec=grid_spec,
        out_shape=jax.ShapeDtypeStruct((n_rows, dm // 2), jnp.int32),
        compiler_params=_params(), name="expert_ffn",
    )(*expert_tables, xs, w_gate_up, b_gate_up.reshape(ne, 1, ff2), w_down, b_down.reshape(ne, 1, dm))


def _combine_kernel(dst_ref, tch_ref,
                    crow_ref, gw_ref, hp_ref, hs_ref, fn_ref, yr_ref, yp_ref, ys_ref, ybuf, sems,
                    *, prompt_tiles):
    i = pl.program_id(0)
    nt = pl.num_programs(0)
    slot = lax.rem(i, 2)
    tt = hp_ref.shape[0]
    cr = ybuf.shape[1]
    ra = ROW_ALIGN

    def chunk_copy(sl, src_row, dst_row, rows):
        return pltpu.make_async_copy(yr_ref.at[pl.ds(src_row, rows)], ybuf.at[sl, pl.ds(dst_row, rows)],
                                     sems.at[sl])

    def fetch(tile, sl):
        chunks_per_tile = cr // ra
        _chunk_loop(tch_ref[tile], lambda c: chunk_copy(
            sl, pl.multiple_of(dst_ref[tile * chunks_per_tile + c], ra), pl.multiple_of(c * ra, ra), ra).start())

    @pl.when(i == 0)
    def _():
        ybuf[...] = jnp.zeros_like(ybuf)
        fetch(0, 0)

    @pl.when(i + 1 < nt)
    def _():
        fetch(i + 1, 1 - slot)

    _split_count(tch_ref[i], lambda j, m: chunk_copy(slot, 0, 0, ra * m).wait())

    y_lo, y_hi = _unpack_bf16_pairs(ybuf[slot])
    cols = lax.broadcasted_iota(jnp.int32, (tt, cr), 1)
    q = jnp.zeros((tt, cr), F32)
    for k in range(TOP_K):
        q += jnp.where(cols == crow_ref[:, k:k + 1], gw_ref[:, k:k + 1], 0.0)
    qb = q.astype(BF16)
    moe = jnp.concatenate([jnp.dot(qb, y_lo, preferred_element_type=F32),
                           jnp.dot(qb, y_hi, preferred_element_type=F32)], axis=1)
    h = jnp.where(i < prompt_tiles, hp_ref[...], hs_ref[...])
    out = _rms(h + moe, fn_ref[...])

    @pl.when(i < prompt_tiles)
    def _():
        yp_ref[...] = out

    @pl.when(i >= prompt_tiles)
    def _():
        ys_ref[...] = out


def combine(crow_t, gw_t, h_p, h_s, final_norm, y_rows, tables, ne):
    dm = h_p.shape[1]
    tt = ROW_TILE
    npt, nst = h_p.shape[0] // tt, h_s.shape[0] // tt
    cr = _compact_rows(ne)
    p_map = lambda i, *_: (jnp.minimum(i, npt - 1), 0)
    s_map = lambda i, *_: (jnp.maximum(i - npt, 0), 0)
    grid_spec = pltpu.PrefetchScalarGridSpec(
        num_scalar_prefetch=2, grid=(npt + nst,),
        in_specs=[pl.BlockSpec((tt, TOP_K), lambda i, *_: (i, 0)),
                  pl.BlockSpec((tt, TOP_K), lambda i, *_: (i, 0)),
                  pl.BlockSpec((tt, dm), p_map), pl.BlockSpec((tt, dm), s_map),
                  pl.BlockSpec((1, dm), lambda i, *_: (0, 0)),
                  pl.BlockSpec(memory_space=pl.ANY)],
        out_specs=[pl.BlockSpec((tt, dm), p_map), pl.BlockSpec((tt, dm), s_map)],
        scratch_shapes=[pltpu.VMEM((2, cr, dm // 2), jnp.int32), pltpu.SemaphoreType.DMA((2,))])
    return pl.pallas_call(
        functools.partial(_combine_kernel, prompt_tiles=npt), grid_spec=grid_spec,
        out_shape=[jax.ShapeDtypeStruct(h_p.shape, F32), jax.ShapeDtypeStruct(h_s.shape, F32)],
        compiler_params=_params(), name="combine",
    )(*tables, crow_t, gw_t, h_p, h_s, final_norm.reshape(1, dm), y_rows)


def _compact_rows(ne):
    return -(-(TOP_K * ROW_TILE + ne * (ROW_ALIGN - 1)) // 128) * 128


def moe_and_final_norm(h_p, h_s, norm_ffn, w_router, b_router, w_gate_up, b_gate_up, w_down, b_down, final_norm):
    ne = w_router.shape[1]
    bm = FFN_BLOCK
    ra = ROW_ALIGN
    xn, gw, crow, cnt = router(h_p, h_s, norm_ffn, w_router, b_router)
    t = xn.shape[0]
    nt = t // ROW_TILE
    seg = -(-cnt[:, 0].astype(jnp.int32).reshape(nt, ne) // ra) * ra
    seg_before = jnp.cumsum(seg, axis=0) - seg
    rows_e = jnp.sum(seg, axis=0)
    padded = -(-rows_e // bm) * bm
    pad_ends = jnp.cumsum(padded)
    pad_starts = pad_ends - padded
    n_blocks = -(-(t * TOP_K + nt * ne * (ra - 1) + ne * (bm - 1)) // bm)
    n_used = pad_ends[-1] // bm
    experts = jnp.arange(ne, dtype=jnp.int32)
    chunks = seg // ra
    chunk_end = jnp.cumsum(chunks, axis=1)
    c_ids = jnp.arange(_compact_rows(ne) // ra, dtype=jnp.int32)
    owner = jnp.sum((chunk_end[:, None, :] <= c_ids[None, :, None]).astype(jnp.int32), axis=2)
    seg_shift = pad_starts[None, :] + seg_before - ra * (chunk_end - chunks)
    dst = ra * c_ids[None, :] + jnp.sum(
        jnp.where(owner[:, :, None] == experts[None, None, :], seg_shift[:, None, :], 0), axis=2)
    seg_tables = (dst.reshape(-1), chunk_end[:, -1])
    fill_tables = (pad_starts + rows_e, (padded - rows_e) // ra, n_used.reshape(1))
    to_i32 = lambda xs: tuple(x.astype(jnp.int32) for x in xs)
    xs = dispatch(xn, crow, to_i32(seg_tables + fill_tables), n_blocks * bm, ne)
    active = padded > 0
    later = jnp.where(active, experts, ne)
    next_active = jnp.concatenate([lax.cummin(later, reverse=True)[1:], jnp.full((1,), ne, jnp.int32)])
    expert_tables = (pad_starts // bm, padded // bm, n_used.reshape(1),
                     jnp.where(next_active < ne, next_active, -1),
                     (jnp.cumsum(active.astype(jnp.int32)) - active.astype(jnp.int32)) % 2,
                     jnp.min(later).reshape(1))
    y_rows = expert_ffn(xs, to_i32(expert_tables), w_gate_up, b_gate_up, w_down, b_down)
    return combine(crow.T, gw.T, h_p, h_s, final_norm, y_rows, to_i32(seg_tables), ne)


def kernel(x_prompt, x_sample, cache_mem_k, cache_mem_v, state_ret, state_ssm_re, state_ssm_im, mem_prompt, norm_mix, w_in, ret_gn, w_ret_o, ssm_lam_re, ssm_lam_im, ssm_log_dt, ssm_b_re, ssm_b_im, ssm_c_re, ssm_c_im, ssm_d, w_ssm_glu, w_ssm_o, mem_norm, w_mem_kv, w_x_o, w_out, norm_ffn, w_router, b_router, w_gate_up, b_gate_up, w_down, b_down, final_norm):
    assert norm_mix.shape[0] == 1, "single-layer step"
    bp, lp, dm = x_prompt.shape
    bs, ls, _ = x_sample.shape
    n_mem = mem_prompt.shape[1]
    xw = X_HEADS * HEAD_DIM
    qk = RET_HEADS * HEAD_DIM
    sw = ssm_d.shape[1]
    g = ssm_lam_re.shape[1]

    w_in_b = w_in[0].astype(BF16)
    tables = s5_tables(ssm_lam_re[0], ssm_lam_im[0], ssm_log_dt[0], ssm_b_re[0], ssm_b_im[0],
                       ssm_c_re[0], ssm_c_im[0], ssm_d[0])
    mix_w = (ret_gn[0], w_ret_o[0].astype(BF16), w_ssm_glu[0].astype(BF16), w_ssm_o[0].astype(BF16),
             w_x_o[0].astype(BF16), w_out[0].astype(BF16))

    kv = norm_matmul(mem_prompt.reshape(bp * n_mem, dm), mem_norm[0], w_mem_kv[0].astype(BF16), F32)
    mk_p = kv[:, :xw].reshape(bp, n_mem * X_HEADS, HEAD_DIM)
    mv_p = kv[:, xw:].reshape(bp, n_mem * X_HEADS, HEAD_DIM)

    def group(x, pos, mem_k, mem_v, s_ret, h_re, h_im, nb, tl):
        bsz, length, _ = x.shape
        z, u = norm_matmul(x.reshape(bsz * length, dm), norm_mix[0], w_in_b, BF16,
                           f32_cols=(4 * qk, 4 * qk + sw),
                           acts=((3 * qk, 4 * qk, "silu"), (4 * qk + sw + xw, w_in_b.shape[1], "sigmoid")))
        y, hf_re, hf_im = s5_apply(u, bsz, h_re, h_im, tables)
        h, s_new = mixer(x, z, y, pos, mem_k, mem_v, s_ret, *mix_w, nb=nb, tl=tl)
        return h, s_new, hf_re, hf_im

    zero_ret = jnp.zeros((bp, RET_HEADS, HEAD_DIM, HEAD_DIM), F32)
    zero_ssm = jnp.zeros((bp, g, SSM_STATE), F32)
    h_p, ret_p, sre_p, sim_p = group(x_prompt, jnp.arange(lp, dtype=jnp.int32), mk_p, mv_p,
                                     zero_ret, zero_ssm, zero_ssm, 1, ROW_TILE)
    h_s, ret_s, sre_s, sim_s = group(x_sample, PAST_LEN + jnp.arange(ls, dtype=jnp.int32),
                                     cache_mem_k[0].reshape(bs, n_mem * X_HEADS, HEAD_DIM),
                                     cache_mem_v[0].reshape(bs, n_mem * X_HEADS, HEAD_DIM),
                                     state_ret[0], state_ssm_re[0], state_ssm_im[0], ROW_TILE // ls, ls)

    y_p, y_s = moe_and_final_norm(h_p, h_s, norm_ffn[0], w_router[0], b_router[0],
                                  w_gate_up[0], b_gate_up[0], w_down[0], b_down[0], final_norm)
    return (y_p.reshape(bp, lp, dm), y_s.reshape(bs, ls, dm), ret_p[None], sre_p[None], sim_p[None],
            mk_p.reshape(1, bp, n_mem, X_HEADS, HEAD_DIM), mv_p.reshape(1, bp, n_mem, X_HEADS, HEAD_DIM),
            ret_s[None], sre_s[None], sim_s[None])
```

```python
import functools
import math

import jax
import jax.numpy as jnp
import numpy as np
from jax import lax
from jax.experimental import pallas as pl
from jax.experimental.pallas import tpu as pltpu

F32 = jnp.float32
BF16 = jnp.bfloat16

EPS = 1e-6
CHUNK = 64
PAST_LEN = 2048
ROPE_BASE = 10000.0
RET_HEADS = 4
X_HEADS = 4
HEAD_DIM = 128
SSM_GROUP = 16
SSM_STATE = 64
TOP_K = 4
SWIGLU_ALPHA = 1.702
SWIGLU_LIMIT = 7.0

VMEM_LIMIT = 52 * 1024 * 1024
S5_CHUNK = 8
S5_LANES = 128
ROW_TILE = 256
FFN_BLOCK = 256
ROW_ALIGN = 8
MIX_COLS = 512
NT_DIMS = (((1,), (1,)), ((), ()))
TN_DIMS = (((0,), (0,)), ((), ()))


def _params(n_axes=1):
    return pltpu.CompilerParams(dimension_semantics=("arbitrary",) * n_axes,
                                vmem_limit_bytes=VMEM_LIMIT)


def _resident(shape):
    nd = len(shape)
    return pl.BlockSpec(shape, lambda *_: (0,) * nd, pipeline_mode=pl.Buffered(1))


def _rms(x, w):
    return x * lax.rsqrt(jnp.mean(x * x, axis=-1, keepdims=True) + EPS) * w


def _sigmoid(x):
    return 0.5 * jnp.tanh(0.5 * x) + 0.5


_ACTIVATIONS = {"sigmoid": _sigmoid, "silu": lambda v: v * _sigmoid(v)}


def _norm_matmul_kernel(x_ref, nw_ref, w_ref, o_ref, *f32_refs, n_chunk, f32_cols, acts):
    xb = _rms(x_ref[...], nw_ref[...]).astype(BF16)
    for n0 in range(0, o_ref.shape[1], n_chunk):
        r = jnp.dot(xb, w_ref[:, n0:n0 + n_chunk], preferred_element_type=F32)
        if f32_cols is not None and n0 <= f32_cols[0] and f32_cols[1] <= n0 + n_chunk:
            f32_refs[0][...] = r[:, f32_cols[0] - n0:f32_cols[1] - n0]
        cuts = sorted({n0, n0 + n_chunk} | {c for lo, hi, _ in acts for c in (lo, hi) if n0 < c < n0 + n_chunk})
        for a, b in zip(cuts[:-1], cuts[1:]):
            piece = r[:, a - n0:b - n0]
            for lo, hi, kind in acts:
                if lo <= a and b <= hi:
                    piece = _ACTIVATIONS[kind](piece)
            o_ref[:, a:b] = piece.astype(o_ref.dtype)


def norm_matmul(x, nw, w, out_dtype, f32_cols=None, acts=()):
    t, d = x.shape
    n = w.shape[1]
    n_chunk = min(n, 1024)
    out_specs = [pl.BlockSpec((ROW_TILE, n), lambda i: (i, 0))]
    out_shape = [jax.ShapeDtypeStruct((t, n), out_dtype)]
    if f32_cols is not None:
        lo, hi = f32_cols
        assert lo // n_chunk == (hi - 1) // n_chunk
        out_specs.append(pl.BlockSpec((ROW_TILE, hi - lo), lambda i: (i, 0)))
        out_shape.append(jax.ShapeDtypeStruct((t, hi - lo), F32))
    out = pl.pallas_call(
        functools.partial(_norm_matmul_kernel, n_chunk=n_chunk, f32_cols=f32_cols, acts=tuple(acts)),
        grid=(t // ROW_TILE,),
        in_specs=[pl.BlockSpec((ROW_TILE, d), lambda i: (i, 0)), _resident((1, d)), _resident((d, n))],
        out_specs=out_specs, out_shape=out_shape,
        compiler_params=_params(), name="norm_matmul",
    )(x, nw.reshape(1, d), w)
    return out if f32_cols is not None else out[0]


def s5_tables(lam_re, lam_im, log_dt, b_re, b_im, c_re, c_im, d_skip):
    g, n, p = b_re.shape
    s = S5_CHUNK
    gl = S5_LANES // p
    j = g // gl
    hi = lax.Precision.HIGHEST
    dt = jnp.exp(log_dt)[:, None]
    a_re = jnp.exp(lam_re * dt) * jnp.cos(lam_im * dt)
    a_im = jnp.exp(lam_re * dt) * jnp.sin(lam_im * dt)
    den = lam_re * lam_re + lam_im * lam_im
    nr, ni = a_re - 1.0, a_im
    co_re = (nr * lam_re + ni * lam_im) / den
    co_im = (ni * lam_re - nr * lam_im) / den
    bb_re = co_re[..., None] * b_re - co_im[..., None] * b_im
    bb_im = co_re[..., None] * b_im + co_im[..., None] * b_re
    tau = jnp.arange(s + 1, dtype=F32)[:, None, None]
    pw_mag = jnp.exp(lam_re * dt * tau)
    pw_re = pw_mag * jnp.cos(lam_im * dt * tau)
    pw_im = pw_mag * jnp.sin(lam_im * dt * tau)
    ca_re = c_re[None] * pw_re[:, :, None, :] - c_im[None] * pw_im[:, :, None, :]
    ca_im = c_re[None] * pw_im[:, :, None, :] + c_im[None] * pw_re[:, :, None, :]
    kq = (jnp.einsum('tgpn,gnq->gtqp', ca_re[:s], bb_re, precision=hi)
          - jnp.einsum('tgpn,gnq->gtqp', ca_im[:s], bb_im, precision=hi))
    ts = np.arange(s)
    lag_onehot = (ts[None, None, :] - ts[None, :, None] == ts[:, None, None]).astype(np.float32)
    rev = s - 1 - ts
    w_re = pw_re[rev][:, :, :, None] * bb_re[None] - pw_im[rev][:, :, :, None] * bb_im[None]
    w_im = pw_re[rev][:, :, :, None] * bb_im[None] + pw_im[rev][:, :, :, None] * bb_re[None]
    m_c = (jnp.einsum('gxqp,xst->gsqtp', kq, lag_onehot, precision=hi)
           .reshape(j, gl, s, p, s * p).transpose(0, 2, 1, 3, 4).reshape(j, s * gl * p, s * p))
    w_c = (jnp.stack([w_re, w_im]).reshape(2, s, j, gl, n, p).transpose(2, 1, 3, 5, 0, 4)
           .reshape(j, s * gl * p, 2 * n))
    v_c = (jnp.stack([ca_re[1:], -ca_im[1:]]).reshape(2, s, j, gl, p, n).transpose(2, 0, 3, 5, 1, 4)
           .reshape(j, 2 * gl * n, s * p))
    fl = s * gl * p
    c_io = np.arange(fl)
    c_st = np.arange(2 * gl * n)
    k_io = np.arange(s * p)
    k_st = np.arange(2 * n)
    spread_io = ((k_io[:, None] // p == c_io[None, :] // (gl * p)) & (k_io[:, None] % p == c_io[None, :] % p))
    spread_st = ((k_st[:, None] // n == c_st[None, :] // (gl * n)) & (k_st[:, None] % n == c_st[None, :] % n))
    grp_io = (c_io // p) % gl
    grp_st = (c_st // n) % gl

    def expand(compact, spread, row_grp, col_grp):
        full = jnp.einsum('jrk,kc->jrc', compact.astype(BF16), jnp.asarray(spread, BF16),
                          preferred_element_type=F32)
        return jnp.where(jnp.asarray(row_grp[:, None] == col_grp[None, :]), full, 0.0).astype(BF16)

    m = expand(m_c, spread_io, grp_io, grp_io)
    w = expand(w_c, spread_st, grp_io, grp_st)
    v = expand(v_c, spread_io, grp_st, grp_io)
    a_s_re = pw_re[s].reshape(1, g * n)
    a_s_im = pw_im[s].reshape(1, g * n)
    dtab = jnp.broadcast_to(d_skip.reshape(j, 1, 1, gl * p), (j, 1, s, gl * p)).reshape(j, 1, s * gl * p)
    return m, w, v, a_s_re, a_s_im, dtab


def _s5_flat(u_ref):
    return jnp.concatenate([u_ref[:, t, :] for t in range(u_ref.shape[1])], axis=1)


def _s5a_kernel(u_ref, w_ref, ire_ref, iim_ref):
    r = jnp.dot(_s5_flat(u_ref).astype(BF16), w_ref[0], preferred_element_type=F32)
    half = r.shape[1] // 2
    ire_ref[...] = r[:, :half]
    iim_ref[...] = r[:, half:]


def _s5scan_kernel(ire_ref, iim_ref, ar_ref, ai_ref, h0r_ref, h0i_ref,
                   hpr_ref, hpi_ref, hfr_ref, hfi_ref):
    nb, nc, _ = ire_ref.shape
    ar, ai = ar_ref[...], ai_ref[...]

    def body(c, carry):
        out = []
        for b in range(nb):
            hr, hi = carry[2 * b], carry[2 * b + 1]
            hpr_ref[b, pl.ds(c, 1), :] = hr
            hpi_ref[b, pl.ds(c, 1), :] = hi
            out.append(ar * hr - ai * hi + ire_ref[b, pl.ds(c, 1), :])
            out.append(ar * hi + ai * hr + iim_ref[b, pl.ds(c, 1), :])
        return tuple(out)

    init = []
    for b in range(nb):
        init += [h0r_ref[b], h0i_ref[b]]
    fin = lax.fori_loop(0, nc, body, tuple(init))
    for b in range(nb):
        hfr_ref[b] = fin[2 * b]
        hfi_ref[b] = fin[2 * b + 1]


def _s5b_kernel(u_ref, hpr_ref, hpi_ref, m_ref, v_ref, d_ref, y_ref):
    uf = _s5_flat(u_ref)
    half = hpr_ref.shape[1]
    y = d_ref[0] * uf + jnp.dot(uf.astype(BF16), m_ref[0], preferred_element_type=F32)
    y += jnp.dot(hpr_ref[...].astype(BF16), v_ref[0, :half, :], preferred_element_type=F32)
    y += jnp.dot(hpi_ref[...].astype(BF16), v_ref[0, half:, :], preferred_element_type=F32)
    lanes = y_ref.shape[2]
    for t in range(y_ref.shape[1]):
        y_ref[:, t, :] = y[:, t * lanes:(t + 1) * lanes]


def s5_apply(u, bsz, h0_re, h0_im, tables):
    m, w, v, a_re, a_im, dtab = tables
    tokens, width = u.shape
    nj = m.shape[0]
    s = S5_CHUNK
    rows = tokens // s
    nc = rows // bsz
    lanes = a_re.shape[1]
    half = w.shape[2] // 2
    fl = m.shape[1]
    rt = min(rows, 512)
    u3 = u.reshape(rows, s, width)
    u_spec = pl.BlockSpec((rt, s, S5_LANES), lambda j, r: (r, 0, j))
    st_spec = pl.BlockSpec((rt, half), lambda j, r: (r, j))
    tab_spec = pl.BlockSpec((1, fl, fl), lambda j, r: (j, 0, 0))
    inj_re, inj_im = pl.pallas_call(
        _s5a_kernel, grid=(nj, rows // rt),
        in_specs=[u_spec, tab_spec],
        out_specs=[st_spec, st_spec],
        out_shape=[jax.ShapeDtypeStruct((rows, lanes), F32)] * 2,
        compiler_params=_params(2), name="s5_chunk_in",
    )(u3, w)

    sb, lw = 4, 512
    seq_spec = pl.BlockSpec((sb, nc, lw), lambda b, l: (b, 0, l))
    vec_spec = pl.BlockSpec((sb, 1, lw), lambda b, l: (b, 0, l))
    atab_spec = pl.BlockSpec((1, lw), lambda b, l: (0, l))
    hp_re, hp_im, hf_re, hf_im = pl.pallas_call(
        _s5scan_kernel, grid=(bsz // sb, lanes // lw),
        in_specs=[seq_spec, seq_spec, atab_spec, atab_spec, vec_spec, vec_spec],
        out_specs=[seq_spec, seq_spec, vec_spec, vec_spec],
        out_shape=[jax.ShapeDtypeStruct((bsz, nc, lanes), F32)] * 2
        + [jax.ShapeDtypeStruct((bsz, 1, lanes), F32)] * 2,
        compiler_params=_params(2), name="s5_scan",
    )(inj_re.reshape(bsz, nc, lanes), inj_im.reshape(bsz, nc, lanes), a_re, a_im,
      h0_re.reshape(bsz, 1, lanes), h0_im.reshape(bsz, 1, lanes))

    y3 = pl.pallas_call(
        _s5b_kernel, grid=(nj, rows // rt),
        in_specs=[u_spec, st_spec, st_spec, tab_spec, tab_spec,
                  pl.BlockSpec((1, 1, fl), lambda j, r: (j, 0, 0))],
        out_specs=u_spec,
        out_shape=jax.ShapeDtypeStruct((rows, s, width), F32),
        compiler_params=_params(2), name="s5_chunk_out",
    )(u3, hp_re.reshape(rows, lanes), hp_im.reshape(rows, lanes), m, v, dtab)
    g = lanes // SSM_STATE
    return y3.reshape(tokens, width), hf_re.reshape(bsz, g, SSM_STATE), hf_im.reshape(bsz, g, SSM_STATE)


def _retention_gammas():
    return 1.0 - np.exp2(-5.0 - np.arange(RET_HEADS, dtype=np.float64))


def retention_tables(tile, chunk):
    gam = _retention_gammas()[:, None, None]
    i = np.arange(tile)[:, None]
    j = np.arange(tile)[None, :]
    same = (i // chunk) == (j // chunk)
    earlier = (j // chunk) < (i // chunk)
    dist = np.where(same, np.abs(i - j), np.where(earlier, i - j, 0))
    dmask = np.where(same | earlier, gam ** dist[None], 0.0)
    qw = np.broadcast_to((gam[:, :, 0] ** (np.arange(tile) + 1.0))[:, :, None], (RET_HEADS, tile, HEAD_DIM))
    kw = np.broadcast_to((gam[:, :, 0] ** (tile - 1.0 - np.arange(tile)))[:, :, None], (RET_HEADS, tile, HEAD_DIM))
    return (jnp.asarray(dmask, F32), jnp.asarray(qw, F32), jnp.asarray(kw, F32),
            tuple(float(x) for x in _retention_gammas() ** tile))


def rope_tables(pos):
    half = HEAD_DIM // 2
    inv = jnp.exp(-math.log(ROPE_BASE) * 2.0 * jnp.arange(half, dtype=F32) / HEAD_DIM)
    ang = pos.astype(F32)[:, None] * inv[None, :]
    cos, sin = jnp.cos(ang), jnp.sin(ang)
    cosf = jnp.concatenate([cos, cos], axis=1)
    sinf = jnp.concatenate([-sin, sin], axis=1)
    return cosf, sinf


def _mixer_kernel(x_ref, zq_ref, xq_ref, gl_ref, y_ref, cq_ref, sq_ref, ck_ref, sk_ref,
                  dm_ref, qw_ref, kw_ref, mk_ref, mv_ref, s0_ref, gn_ref,
                  wro_ref, wglu_ref, wso_ref, wxo_ref, wout_ref,
                  h_ref, sout_ref, s_scr, o_scr, xo_scr, glu_scr, mg_scr, *, nb, tl, tile_decay):
    hd = HEAD_DIM
    qk = RET_HEADS * hd

    @pl.when(pl.program_id(1) == 0)
    def _():
        s_scr[...] = s0_ref[...]

    cq, sq, ck, sk = cq_ref[...], sq_ref[...], ck_ref[...], sk_ref[...]
    for n in range(nb):
        rows = slice(n * tl, (n + 1) * tl)
        for h in range(RET_HEADS):
            c0 = h * hd
            q = zq_ref[rows, c0:c0 + hd].astype(F32)
            k = zq_ref[rows, qk + c0:qk + c0 + hd].astype(F32)
            v = zq_ref[rows, 2 * qk + c0:2 * qk + c0 + hd]
            g = zq_ref[rows, 3 * qk + c0:3 * qk + c0 + hd].astype(F32)
            qr = q * cq + pltpu.roll(q, hd // 2, 1) * sq
            kr = k * ck + pltpu.roll(k, hd // 2, 1) * sk
            sc = lax.dot_general(qr.astype(BF16), kr.astype(BF16), NT_DIMS,
                                 preferred_element_type=F32) * dm_ref[h]
            o = jnp.dot(sc.astype(BF16), v, preferred_element_type=F32)
            s_old = s_scr[n, h]
            o += jnp.dot((qr * qw_ref[h]).astype(BF16), s_old.astype(BF16), preferred_element_type=F32)
            kv = lax.dot_general((kr * kw_ref[h]).astype(BF16), v, TN_DIMS, preferred_element_type=F32)
            s_scr[n, h] = tile_decay[h] * s_old + kv
            d = o - jnp.mean(o, axis=-1, keepdims=True)
            on = d * lax.rsqrt(jnp.mean(d * d, axis=-1, keepdims=True) + EPS) * gn_ref[:, c0:c0 + hd]
            o_scr[rows, c0:c0 + hd] = (on * g).astype(BF16)
            mem_rows = pl.ds(h, mk_ref.shape[1] // X_HEADS, stride=X_HEADS)
            mkh = mk_ref[n, mem_rows, :].astype(BF16)
            mvh = mv_ref[n, mem_rows, :].astype(BF16)
            s = lax.dot_general(xq_ref[rows, c0:c0 + hd], mkh, NT_DIMS,
                                preferred_element_type=F32) * (hd ** -0.5)
            e = jnp.exp(s - jnp.max(s, axis=-1, keepdims=True))
            p = e / jnp.sum(e, axis=-1, keepdims=True)
            xo_scr[rows, c0:c0 + hd] = jnp.dot(p.astype(BF16), mvh, preferred_element_type=F32).astype(BF16)

    cw = MIX_COLS
    dm = h_ref.shape[1]
    half = wglu_ref.shape[1] // 2
    yb = jax.nn.gelu(y_ref[...]).astype(BF16)
    for c0 in range(0, half, cw):
        ga = jnp.dot(yb, wglu_ref[:, c0:c0 + cw], preferred_element_type=F32)
        gb = jnp.dot(yb, wglu_ref[:, half + c0:half + c0 + cw], preferred_element_type=F32)
        glu_scr[:, c0:c0 + cw] = (ga * _sigmoid(gb)).astype(BF16)
    for c0 in range(0, dm, cw):
        cols = slice(c0, c0 + cw)
        ret = jnp.dot(o_scr[...], wro_ref[:, cols], preferred_element_type=F32)
        ssm = jnp.dot(glu_scr[...], wso_ref[:, cols], preferred_element_type=F32)
        xb = jnp.dot(xo_scr[...], wxo_ref[:, cols], preferred_element_type=F32)
        merged = (gl_ref[:, c0:c0 + cw].astype(F32) * ret
                  + gl_ref[:, dm + c0:dm + c0 + cw].astype(F32) * ssm
                  + gl_ref[:, 2 * dm + c0:2 * dm + c0 + cw].astype(F32) * xb)
        mg_scr[:, cols] = merged.astype(BF16)
    for c0 in range(0, dm, cw):
        cols = slice(c0, c0 + cw)
        h_ref[:, cols] = x_ref[:, cols] + jnp.dot(mg_scr[...], wout_ref[:, cols], preferred_element_type=F32)
    sout_ref[...] = s_scr[...]


def mixer(x, z, y_ssm, pos, mem_k, mem_v, s0, ret_gn, w_ret_o, w_ssm_glu, w_ssm_o, w_x_o, w_out, *, nb, tl):
    bsz, length, dm = x.shape
    chunk = min(CHUNK, length)
    nl = length // tl
    rows = nb * tl
    qk = RET_HEADS * HEAD_DIM
    sw = y_ssm.shape[1]
    xw = X_HEADS * HEAD_DIM
    gate_col = (4 * qk + sw + xw)
    assert gate_col % (3 * dm) == 0 and (4 * qk + sw) % xw == 0
    dmask, qw, kw, tile_decay = retention_tables(tl, chunk)
    cosf, sinf = rope_tables(pos)
    scale = HEAD_DIM ** -0.5
    row_map = lambda b, l: (b * nl + l, 0)
    tab_map = lambda b, l: (l, 0)
    st_spec = pl.BlockSpec((nb, RET_HEADS, HEAD_DIM, HEAD_DIM), lambda b, l: (b, 0, 0, 0))
    mem_spec = pl.BlockSpec((nb,) + mem_k.shape[1:], lambda b, l: (b,) + (0,) * (mem_k.ndim - 1))
    h, s_out = pl.pallas_call(
        functools.partial(_mixer_kernel, nb=nb, tl=tl, tile_decay=tile_decay),
        grid=(bsz // nb, nl),
        in_specs=[pl.BlockSpec((rows, dm), row_map),
                  pl.BlockSpec((rows, 4 * qk), row_map),
                  pl.BlockSpec((rows, xw), lambda b, l: (b * nl + l, (4 * qk + sw) // xw)),
                  pl.BlockSpec((rows, 3 * dm), lambda b, l: (b * nl + l, gate_col // (3 * dm))),
                  pl.BlockSpec((rows, sw), row_map),
                  pl.BlockSpec((tl, HEAD_DIM), tab_map), pl.BlockSpec((tl, HEAD_DIM), tab_map),
                  pl.BlockSpec((tl, HEAD_DIM), tab_map), pl.BlockSpec((tl, HEAD_DIM), tab_map),
                  _resident(dmask.shape), _resident(qw.shape), _resident(kw.shape),
                  mem_spec, mem_spec, st_spec, _resident((1, qk)),
                  _resident(w_ret_o.shape), _resident(w_ssm_glu.shape), _resident(w_ssm_o.shape),
                  _resident(w_x_o.shape), _resident(w_out.shape)],
        out_specs=[pl.BlockSpec((rows, dm), row_map), st_spec],
        out_shape=[jax.ShapeDtypeStruct((bsz * length, dm), F32),
                   jax.ShapeDtypeStruct(s0.shape, F32)],
        scratch_shapes=[pltpu.VMEM((nb, RET_HEADS, HEAD_DIM, HEAD_DIM), F32),
                        pltpu.VMEM((rows, qk), BF16), pltpu.VMEM((rows, xw), BF16),
                        pltpu.VMEM((rows, w_ssm_o.shape[0]), BF16), pltpu.VMEM((rows, dm), BF16)],
        compiler_params=_params(2), name="mixer",
    )(x.reshape(bsz * length, dm), z, z, z, y_ssm,
      cosf * scale, sinf * scale, cosf, sinf, dmask, qw, kw,
      mem_k, mem_v, s0, ret_gn.reshape(1, qk), w_ret_o, w_ssm_glu, w_ssm_o, w_x_o, w_out)
    return h, s_out


def _router_kernel(hp_ref, hs_ref, nw_ref, wrt_ref, br_ref, tri_ref, low_ref,
                   xn_ref, gw_ref, crow_ref, cnt_ref, *, prompt_tiles):
    h = jnp.where(pl.program_id(0) < prompt_tiles, hp_ref[...], hs_ref[...])
    xn = _rms(h, nw_ref[...]).astype(BF16)
    xn_ref[...] = xn
    logits = lax.dot_general(wrt_ref[...], xn, NT_DIMS, preferred_element_type=F32) + br_ref[...]
    ne = logits.shape[0]
    iota = lax.broadcasted_iota(jnp.int32, logits.shape, 0)
    rest = logits
    sel = jnp.zeros(logits.shape, jnp.bool_)
    vals, idxs = [], []
    for _ in range(TOP_K):
        m = jnp.max(rest, axis=0, keepdims=True)
        ix = jnp.min(jnp.where(rest == m, iota, ne), axis=0, keepdims=True)
        hit = iota == ix
        vals.append(m)
        idxs.append(ix)
        sel = jnp.logical_or(sel, hit)
        rest = jnp.where(hit, -jnp.inf, rest)
    es = [jnp.exp(v - vals[0]) for v in vals]
    tot = es[0] + es[1] + es[2] + es[3]
    before = jnp.dot(sel.astype(BF16), tri_ref[...], preferred_element_type=F32)
    cnt = jnp.sum(sel.astype(F32), axis=1, keepdims=True)
    seg = jnp.floor((cnt + (ROW_ALIGN - 1.0)) * (1.0 / ROW_ALIGN)) * ROW_ALIGN
    start = jnp.dot(low_ref[...], jnp.broadcast_to(seg, before.shape), precision=lax.Precision.HIGHEST,
                    preferred_element_type=F32)
    place = start + before
    for k in range(TOP_K):
        gw_ref[k:k + 1, :] = es[k] / tot
        crow_ref[k:k + 1, :] = jnp.sum(jnp.where(iota == idxs[k], place, 0.0), axis=0,
                                       keepdims=True).astype(jnp.int32)
    cnt_ref[...] = jnp.broadcast_to(cnt, cnt_ref.shape)


def router(h_p, h_s, norm_ffn, w_router, b_router):
    dm = h_p.shape[1]
    ne = w_router.shape[1]
    tt = ROW_TILE
    npt, nst = h_p.shape[0] // tt, h_s.shape[0] // tt
    t = (npt + nst) * tt
    tri = jnp.asarray(np.triu(np.ones((tt, tt), np.float32), k=1), BF16)
    low = jnp.asarray(np.tril(np.ones((ne, ne), np.float32), k=-1))
    tok_spec = pl.BlockSpec((TOP_K, tt), lambda i: (0, i))
    return pl.pallas_call(
        functools.partial(_router_kernel, prompt_tiles=npt), grid=(npt + nst,),
        in_specs=[pl.BlockSpec((tt, dm), lambda i: (jnp.minimum(i, npt - 1), 0)),
                  pl.BlockSpec((tt, dm), lambda i: (jnp.maximum(i - npt, 0), 0)),
                  _resident((1, dm)), _resident((ne, dm)), _resident((ne, 1)), _resident((tt, tt)),
                  _resident((ne, ne))],
        out_specs=[pl.BlockSpec((tt, dm), lambda i: (i, 0)), tok_spec, tok_spec,
                   pl.BlockSpec((ne, 128), lambda i: (i, 0))],
        out_shape=[jax.ShapeDtypeStruct((t, dm), BF16),
                   jax.ShapeDtypeStruct((TOP_K, t), F32), jax.ShapeDtypeStruct((TOP_K, t), jnp.int32),
                   jax.ShapeDtypeStruct(((npt + nst) * ne, 128), F32)],
        compiler_params=_params(), name="router",
    )(h_p, h_s, norm_ffn.reshape(1, dm), w_router.T.astype(BF16), b_router.reshape(ne, 1), tri, low)


def _pack_bf16_pairs(x, exact=False):
    n = x.shape[1] // 2
    lo, hi = x[:, :n], x[:, n:]
    if not exact:
        lo, hi = lo.astype(BF16).astype(F32), hi.astype(BF16).astype(F32)
    lo = lax.bitcast_convert_type(lo, jnp.int32)
    hi = lax.bitcast_convert_type(hi, jnp.int32)
    return lax.shift_right_logical(lo, 16) | (hi & -65536)


def _unpack_bf16_pairs(u):
    lo = lax.bitcast_convert_type(lax.shift_left(u, 16), F32).astype(BF16)
    hi = lax.bitcast_convert_type(u & -65536, F32).astype(BF16)
    return lo, hi


def _split_count(n, fn):
    def quad(jq, carry):
        fn(4 * jq, 4)
        return carry

    lax.fori_loop(0, n // 4, quad, 0)

    @pl.when(n % 4 >= 2)
    def _():
        fn(n // 4 * 4, 2)

    @pl.when(n % 2 == 1)
    def _():
        fn(n // 2 * 2, 1)


def _chunk_loop(n, fn):
    def quad(jq, carry):
        for u in range(4):
            fn(4 * jq + u)
        return carry

    def single(j, carry):
        fn(j)
        return carry

    lax.fori_loop(0, n // 4, quad, 0)
    lax.fori_loop(n // 4 * 4, n, single, 0)


def _dispatch_kernel(dst_ref, tch_ref, zs_ref, zn_ref, used_ref,
                     xn_ref, crow_ref, xs_ref, cbuf, zbuf, sems, zsem, *, ne, bm):
    i = pl.program_id(0)
    nt = pl.num_programs(0)
    slot = lax.rem(i, 2)
    tt = xn_ref.shape[0]
    cr = cbuf.shape[1]
    ra = ROW_ALIGN
    n_blocks = xs_ref.shape[0] // bm

    def chunk_copy(sl, src_row, dst_row, rows):
        return pltpu.make_async_copy(cbuf.at[sl, pl.ds(src_row, rows)], xs_ref.at[pl.ds(dst_row, rows)],
                                     sems.at[sl])

    def wait_chunks(sl, n):
        _split_count(n, lambda j, m: chunk_copy(sl, 0, 0, ra * m).wait())

    def tail_copy(e, j):
        return pltpu.make_async_copy(zbuf.at[pl.ds(0, ra)],
                                     xs_ref.at[pl.ds(pl.multiple_of(zs_ref[e] + ra * j, ra), ra)], zsem)

    def block_copy(b):
        return pltpu.make_async_copy(zbuf, xs_ref.at[pl.ds(pl.multiple_of(b * bm, bm), bm)], zsem)

    def zero_fill(start):
        def per_expert(e, carry):
            def per_chunk(j, c2):
                (tail_copy(e, j).start() if start else tail_copy(e, j).wait())
                return c2
            lax.fori_loop(0, zn_ref[e], per_chunk, 0)
            return carry
        lax.fori_loop(0, ne, per_expert, 0)

        def per_block(b, carry):
            (block_copy(b).start() if start else block_copy(b).wait())
            return carry
        lax.fori_loop(used_ref[0], n_blocks, per_block, 0)

    @pl.when(i == 0)
    def _():
        zbuf[...] = jnp.zeros_like(zbuf)
        zero_fill(True)
        zero_fill(False)

    @pl.when(i >= 2)
    def _():
        wait_chunks(slot, tch_ref[jnp.maximum(i - 2, 0)])

    crow = crow_ref[...]
    rows = lax.broadcasted_iota(jnp.int32, (cr, tt), 0)
    hit = rows == crow[0:1, :]
    for k in range(1, TOP_K):
        hit = jnp.logical_or(hit, rows == crow[k:k + 1, :])
    packed = _pack_bf16_pairs(jnp.dot(jnp.where(hit, 1.0, 0.0).astype(BF16), xn_ref[...],
                                      preferred_element_type=F32), exact=True)
    cbuf[slot] = packed

    chunks_per_tile = cr // ra
    _chunk_loop(tch_ref[i], lambda c: chunk_copy(
        slot, pl.multiple_of(c * ra, ra), pl.multiple_of(dst_ref[i * chunks_per_tile + c], ra), ra).start())

    @pl.when(i == nt - 1)
    def _():
        wait_chunks(slot, tch_ref[i])
        wait_chunks(1 - slot, jnp.where(nt >= 2, tch_ref[jnp.maximum(i - 1, 0)], 0))


def dispatch(xn, crow, tables, n_rows, ne):
    t, dm = xn.shape
    tt = ROW_TILE
    bm = FFN_BLOCK
    cr = _compact_rows(ne)
    grid_spec = pltpu.PrefetchScalarGridSpec(
        num_scalar_prefetch=5, grid=(t // tt,),
        in_specs=[pl.BlockSpec((tt, dm), lambda i, *_: (i, 0)),
                  pl.BlockSpec((TOP_K, tt), lambda i, *_: (0, i))],
        out_specs=pl.BlockSpec(memory_space=pl.ANY),
        scratch_shapes=[pltpu.VMEM((2, cr, dm // 2), jnp.int32), pltpu.VMEM((bm, dm // 2), jnp.int32),
                        pltpu.SemaphoreType.DMA((2,)), pltpu.SemaphoreType.DMA(())])
    return pl.pallas_call(
        functools.partial(_dispatch_kernel, ne=ne, bm=bm), grid_spec=grid_spec,
        out_shape=jax.ShapeDtypeStruct((n_rows, dm // 2), jnp.int32),
        compiler_params=_params(), name="dispatch",
    )(*tables, xn, crow)


FFN_W_PIECES = (4, 2)


def _ffn_kernel(first_ref, nblk_ref, used_ref, next_ref, wslot_ref, lead_ref,
                xs_hbm, wgu_hbm, bgu_ref, wd_hbm, bd_ref, y_hbm,
                wgu_f32, wd_f32, wgu_bf, wd_bf, xbuf, ybuf, xsem, ysem, wsem, zsem, *, bm):
    e = pl.program_id(0)
    n = nblk_ref[e]
    b0 = first_ref[e]
    n_blocks = y_hbm.shape[0] // bm
    n_pieces = sum(FFN_W_PIECES)

    def block_rows(b):
        return pl.ds(pl.multiple_of(b * bm, bm), bm)

    def x_copy(j, sl):
        return pltpu.make_async_copy(xs_hbm.at[block_rows(b0 + j)], xbuf.at[sl], xsem.at[sl])

    def y_copy(j, sl):
        return pltpu.make_async_copy(ybuf.at[sl], y_hbm.at[block_rows(b0 + j)], ysem.at[sl])

    def w_piece(ex, sl, p):
        src, dst, q, parts = ((wgu_hbm, wgu_f32, p, FFN_W_PIECES[0]) if p < FFN_W_PIECES[0]
                              else (wd_hbm, wd_f32, p - FFN_W_PIECES[0], FFN_W_PIECES[1]))
        rows = src.shape[1] // parts
        return pltpu.make_async_copy(src.at[ex, pl.ds(q * rows, rows)], dst.at[sl, pl.ds(q * rows, rows)],
                                     wsem.at[sl, p])

    @pl.when(n > 0)
    def _():
        ws = wslot_ref[e]
        nxt = next_ref[e]
        x_copy(0, 0).start()

        @pl.when(lead_ref[0] == e)
        def _():
            for p in range(n_pieces):
                w_piece(e, ws, p).start()

        for p in range(n_pieces):
            w_piece(e, ws, p).wait()
        wgu_bf[...] = wgu_f32[ws].astype(BF16)
        wd_bf[...] = wd_f32[ws].astype(BF16)

        def block(j, carry):
            sl = lax.rem(j, 2)
            x_copy(j, sl).wait()

            @pl.when(j + 1 < n)
            def _():
                x_copy(j + 1, 1 - sl).start()

            for p in range(n_pieces):
                @pl.when(jnp.logical_and(j == p, nxt >= 0))
                def _():
                    w_piece(nxt, 1 - ws, p).start()

            @pl.when(j >= 2)
            def _():
                y_copy(j - 2, sl).wait()

            x_lo, x_hi = _unpack_bf16_pairs(xbuf[sl])
            kh = x_lo.shape[1]
            hgu = (jnp.dot(x_lo, wgu_bf[:kh, :], preferred_element_type=F32)
                   + jnp.dot(x_hi, wgu_bf[kh:, :], preferred_element_type=F32) + bgu_ref[0])
            ff = hgu.shape[1] // 2
            gate = jnp.minimum(hgu[:, :ff], SWIGLU_LIMIT)
            up = jnp.clip(hgu[:, ff:], -SWIGLU_LIMIT, SWIGLU_LIMIT)
            act = (up + 1.0) * gate * _sigmoid(SWIGLU_ALPHA * gate)
            ybuf[sl] = _pack_bf16_pairs(jnp.dot(act.astype(BF16), wd_bf[...], preferred_element_type=F32)
                                        + bd_ref[0])
            y_copy(j, sl).start()
            return carry

        lax.fori_loop(0, n, block, 0)

        for p in range(n_pieces):
            @pl.when(jnp.logical_and(p >= n, nxt >= 0))
            def _():
                w_piece(nxt, 1 - ws, p).start()

        @pl.when(n >= 2)
        def _():
            y_copy(n - 2, lax.rem(n, 2)).wait()
        y_copy(n - 1, lax.rem(n - 1, 2)).wait()

    @pl.when(e == pl.num_programs(0) - 1)
    def _():
        ybuf[0] = jnp.zeros(ybuf.shape[1:], ybuf.dtype)

        def zero_copy(b):
            return pltpu.make_async_copy(ybuf.at[0], y_hbm.at[block_rows(b)], zsem)

        def start(b, carry):
            zero_copy(b).start()
            return carry

        def wait(b, carry):
            zero_copy(b).wait()
            return carry

        lax.fori_loop(used_ref[0], n_blocks, start, 0)
        lax.fori_loop(used_ref[0], n_blocks, wait, 0)


def expert_ffn(xs, expert_tables, w_gate_up, b_gate_up, w_down, b_down):
    n_rows = xs.shape[0]
    ne, dm, ff2 = w_gate_up.shape
    bm = FFN_BLOCK
    grid_spec = pltpu.PrefetchScalarGridSpec(
        num_scalar_prefetch=6, grid=(ne,),
        in_specs=[pl.BlockSpec(memory_space=pl.ANY),
                  pl.BlockSpec(memory_space=pl.ANY),
                  pl.BlockSpec((1, 1, ff2), lambda e, *_: (e, 0, 0)),
                  pl.BlockSpec(memory_space=pl.ANY),
                  pl.BlockSpec((1, 1, dm), lambda e, *_: (e, 0, 0))],
        out_specs=pl.BlockSpec(memory_space=pl.ANY),
        scratch_shapes=[pltpu.VMEM((2, dm, ff2), F32), pltpu.VMEM((2, ff2 // 2, dm), F32),
                        pltpu.VMEM((dm, ff2), BF16), pltpu.VMEM((ff2 // 2, dm), BF16),
                        pltpu.VMEM((2, bm, dm // 2), jnp.int32), pltpu.VMEM((2, bm, dm // 2), jnp.int32),
                        pltpu.SemaphoreType.DMA((2,)), pltpu.SemaphoreType.DMA((2,)),
                        pltpu.SemaphoreType.DMA((2, sum(FFN_W_PIECES))), pltpu.SemaphoreType.DMA(())])
    return pl.pallas_call(
        functools.partial(_ffn_kernel, bm=bm), grid_spec=grid_spec,
        out_shape=jax.ShapeDtypeStruct((n_rows, dm // 2), jnp.int32),
        compiler_params=_params(), name="expert_ffn",
    )(*expert_tables, xs, w_gate_up, b_gate_up.reshape(ne, 1, ff2), w_down, b_down.reshape(ne, 1, dm))


def _combine_kernel(dst_ref, tch_ref,
                    crow_ref, gw_ref, hp_ref, hs_ref, fn_ref, yr_ref, yp_ref, ys_ref, ybuf, sems,
                    *, prompt_tiles):
    i = pl.program_id(0)
    nt = pl.num_programs(0)
    slot = lax.rem(i, 2)
    tt = hp_ref.shape[0]
    cr = ybuf.shape[1]
    ra = ROW_ALIGN

    def chunk_copy(sl, src_row, dst_row, rows):
        return pltpu.make_async_copy(yr_ref.at[pl.ds(src_row, rows)], ybuf.at[sl, pl.ds(dst_row, rows)],
                                     sems.at[sl])

    def fetch(tile, sl):
        chunks_per_tile = cr // ra
        _chunk_loop(tch_ref[tile], lambda c: chunk_copy(
            sl, pl.multiple_of(dst_ref[tile * chunks_per_tile + c], ra), pl.multiple_of(c * ra, ra), ra).start())

    @pl.when(i == 0)
    def _():
        ybuf[...] = jnp.zeros_like(ybuf)
        fetch(0, 0)

    @pl.when(i + 1 < nt)
    def _():
        fetch(i + 1, 1 - slot)

    _split_count(tch_ref[i], lambda j, m: chunk_copy(slot, 0, 0, ra * m).wait())

    y_lo, y_hi = _unpack_bf16_pairs(ybuf[slot])
    cols = lax.broadcasted_iota(jnp.int32, (tt, cr), 1)
    q = jnp.zeros((tt, cr), F32)
    for k in range(TOP_K):
        q += jnp.where(cols == crow_ref[:, k:k + 1], gw_ref[:, k:k + 1], 0.0)
    qb = q.astype(BF16)
    moe = jnp.concatenate([jnp.dot(qb, y_lo, preferred_element_type=F32),
                           jnp.dot(qb, y_hi, preferred_element_type=F32)], axis=1)
    h = jnp.where(i < prompt_tiles, hp_ref[...], hs_ref[...])
    out = _rms(h + moe, fn_ref[...])

    @pl.when(i < prompt_tiles)
    def _():
        yp_ref[...] = out

    @pl.when(i >= prompt_tiles)
    def _():
        ys_ref[...] = out


def combine(crow_t, gw_t, h_p, h_s, final_norm, y_rows, tables, ne):
    dm = h_p.shape[1]
    tt = ROW_TILE
    npt, nst = h_p.shape[0] // tt, h_s.shape[0] // tt
    cr = _compact_rows(ne)
    p_map = lambda i, *_: (jnp.minimum(i, npt - 1), 0)
    s_map = lambda i, *_: (jnp.maximum(i - npt, 0), 0)
    grid_spec = pltpu.PrefetchScalarGridSpec(
        num_scalar_prefetch=2, grid=(npt + nst,),
        in_specs=[pl.BlockSpec((tt, TOP_K), lambda i, *_: (i, 0)),
                  pl.BlockSpec((tt, TOP_K), lambda i, *_: (i, 0)),
                  pl.BlockSpec((tt, dm), p_map), pl.BlockSpec((tt, dm), s_map),
                  pl.BlockSpec((1, dm), lambda i, *_: (0, 0)),
                  pl.BlockSpec(memory_space=pl.ANY)],
        out_specs=[pl.BlockSpec((tt, dm), p_map), pl.BlockSpec((tt, dm), s_map)],
        scratch_shapes=[pltpu.VMEM((2, cr, dm // 2), jnp.int32), pltpu.SemaphoreType.DMA((2,))])
    return pl.pallas_call(
        functools.partial(_combine_kernel, prompt_tiles=npt), grid_spec=grid_spec,
        out_shape=[jax.ShapeDtypeStruct(h_p.shape, F32), jax.ShapeDtypeStruct(h_s.shape, F32)],
        compiler_params=_params(), name="combine",
    )(*tables, crow_t, gw_t, h_p, h_s, final_norm.reshape(1, dm), y_rows)


def _compact_rows(ne):
    return -(-(TOP_K * ROW_TILE + ne * (ROW_ALIGN - 1)) // 128) * 128


def moe_and_final_norm(h_p, h_s, norm_ffn, w_router, b_router, w_gate_up, b_gate_up, w_down, b_down, final_norm):
    ne = w_router.shape[1]
    bm = FFN_BLOCK
    ra = ROW_ALIGN
    xn, gw, crow, cnt = router(h_p, h_s, norm_ffn, w_router, b_router)
    t = xn.shape[0]
    nt = t // ROW_TILE
    seg = -(-cnt[:, 0].astype(jnp.int32).reshape(nt, ne) // ra) * ra
    seg_before = jnp.cumsum(seg, axis=0) - seg
    rows_e = jnp.sum(seg, axis=0)
    padded = -(-rows_e // bm) * bm
    pad_ends = jnp.cumsum(padded)
    pad_starts = pad_ends - padded
    n_blocks = -(-(t * TOP_K + nt * ne * (ra - 1) + ne * (bm - 1)) // bm)
    n_used = pad_ends[-1] // bm
    experts = jnp.arange(ne, dtype=jnp.int32)
    chunks = seg // ra
    chunk_end = jnp.cumsum(chunks, axis=1)
    c_ids = jnp.arange(_compact_rows(ne) // ra, dtype=jnp.int32)
    owner = jnp.sum((chunk_end[:, None, :] <= c_ids[None, :, None]).astype(jnp.int32), axis=2)
    seg_shift = pad_starts[None, :] + seg_before - ra * (chunk_end - chunks)
    dst = ra * c_ids[None, :] + jnp.sum(
        jnp.where(owner[:, :, None] == experts[None, None, :], seg_shift[:, None, :], 0), axis=2)
    seg_tables = (dst.reshape(-1), chunk_end[:, -1])
    fill_tables = (pad_starts + rows_e, (padded - rows_e) // ra, n_used.reshape(1))
    to_i32 = lambda xs: tuple(x.astype(jnp.int32) for x in xs)
    xs = dispatch(xn, crow, to_i32(seg_tables + fill_tables), n_blocks * bm, ne)
    active = padded > 0
    later = jnp.where(active, experts, ne)
    next_active = jnp.concatenate([lax.cummin(later, reverse=True)[1:], jnp.full((1,), ne, jnp.int32)])
    expert_tables = (pad_starts // bm, padded // bm, n_used.reshape(1),
                     jnp.where(next_active < ne, next_active, -1),
                     (jnp.cumsum(active.astype(jnp.int32)) - active.astype(jnp.int32)) % 2,
                     jnp.min(later).reshape(1))
    y_rows = expert_ffn(xs, to_i32(expert_tables), w_gate_up, b_gate_up, w_down, b_down)
    return combine(crow.T, gw.T, h_p, h_s, final_norm, y_rows, to_i32(seg_tables), ne)


def kernel(x_prompt, x_sample, cache_mem_k, cache_mem_v, state_ret, state_ssm_re, state_ssm_im, mem_prompt, norm_mix, w_in, ret_gn, w_ret_o, ssm_lam_re, ssm_lam_im, ssm_log_dt, ssm_b_re, ssm_b_im, ssm_c_re, ssm_c_im, ssm_d, w_ssm_glu, w_ssm_o, mem_norm, w_mem_kv, w_x_o, w_out, norm_ffn, w_router, b_router, w_gate_up, b_gate_up, w_down, b_down, final_norm):
    assert norm_mix.shape[0] == 1, "single-layer step"
    bp, lp, dm = x_prompt.shape
    bs, ls, _ = x_sample.shape
    n_mem = mem_prompt.shape[1]
    xw = X_HEADS * HEAD_DIM
    qk = RET_HEADS * HEAD_DIM
    sw = ssm_d.shape[1]
    g = ssm_lam_re.shape[1]

    w_in_b = w_in[0].astype(BF16)
    tables = s5_tables(ssm_lam_re[0], ssm_lam_im[0], ssm_log_dt[0], ssm_b_re[0], ssm_b_im[0],
                       ssm_c_re[0], ssm_c_im[0], ssm_d[0])
    mix_w = (ret_gn[0], w_ret_o[0].astype(BF16), w_ssm_glu[0].astype(BF16), w_ssm_o[0].astype(BF16),
             w_x_o[0].astype(BF16), w_out[0].astype(BF16))

    kv = norm_matmul(mem_prompt.reshape(bp * n_mem, dm), mem_norm[0], w_mem_kv[0].astype(BF16), F32)
    mk_p = kv[:, :xw].reshape(bp, n_mem * X_HEADS, HEAD_DIM)
    mv_p = kv[:, xw:].reshape(bp, n_mem * X_HEADS, HEAD_DIM)

    def group(x, pos, mem_k, mem_v, s_ret, h_re, h_im, nb, tl):
        bsz, length, _ = x.shape
        z, u = norm_matmul(x.reshape(bsz * length, dm), norm_mix[0], w_in_b, BF16,
                           f32_cols=(4 * qk, 4 * qk + sw),
                           acts=((3 * qk, 4 * qk, "silu"), (4 * qk + sw + xw, w_in_b.shape[1], "sigmoid")))
        y, hf_re, hf_im = s5_apply(u, bsz, h_re, h_im, tables)
        h, s_new = mixer(x, z, y, pos, mem_k, mem_v, s_ret, *mix_w, nb=nb, tl=tl)
        return h, s_new, hf_re, hf_im

    zero_ret = jnp.zeros((bp, RET_HEADS, HEAD_DIM, HEAD_DIM), F32)
    zero_ssm = jnp.zeros((bp, g, SSM_STATE), F32)
    h_p, ret_p, sre_p, sim_p = group(x_prompt, jnp.arange(lp, dtype=jnp.int32), mk_p, mv_p,
                                     zero_ret, zero_ssm, zero_ssm, 1, ROW_TILE)
    h_s, ret_s, sre_s, sim_s = group(x_sample, PAST_LEN + jnp.arange(ls, dtype=jnp.int32),
                                     cache_mem_k[0].reshape(bs, n_mem * X_HEADS, HEAD_DIM),
                                     cache_mem_v[0].reshape(bs, n_mem * X_HEADS, HEAD_DIM),
                                     state_ret[0], state_ssm_re[0], state_ssm_im[0], ROW_TILE // ls, ls)

    y_p, y_s = moe_and_final_norm(h_p, h_s, norm_ffn[0], w_router[0], b_router[0],
                                  w_gate_up[0], b_gate_up[0], w_down[0], b_down[0], final_norm)
    return (y_p.reshape(bp, lp, dm), y_s.reshape(bs, ls, dm), ret_p[None], sre_p[None], sim_p[None],
            mk_p.reshape(1, bp, n_mem, X_HEADS, HEAD_DIM), mv_p.reshape(1, bp, n_mem, X_HEADS, HEAD_DIM),
            ret_s[None], sre_s[None], sim_s[None])
```

```python
import functools
import math

import jax
import jax.numpy as jnp
import numpy as np
from jax import lax
from jax.experimental import pallas as pl
from jax.experimental.pallas import tpu as pltpu

F32 = jnp.float32
BF16 = jnp.bfloat16

EPS = 1e-6
CHUNK = 64
PAST_LEN = 2048
ROPE_BASE = 10000.0
RET_HEADS = 4
X_HEADS = 4
HEAD_DIM = 128
SSM_GROUP = 16
SSM_STATE = 64
TOP_K = 4
SWIGLU_ALPHA = 1.702
SWIGLU_LIMIT = 7.0

VMEM_LIMIT = 52 * 1024 * 1024
S5_CHUNK = 8
S5_LANES = 128
ROW_TILE = 256
FFN_BLOCK = 256
ROW_ALIGN = 8
MIX_COLS = 512
NT_DIMS = (((1,), (1,)), ((), ()))
TN_DIMS = (((0,), (0,)), ((), ()))


def _params(n_axes=1):
    return pltpu.CompilerParams(dimension_semantics=("arbitrary",) * n_axes,
                                vmem_limit_bytes=VMEM_LIMIT)


def _resident(shape):
    nd = len(shape)
    return pl.BlockSpec(shape, lambda *_: (0,) * nd, pipeline_mode=pl.Buffered(1))


def _rms(x, w):
    return x * lax.rsqrt(jnp.mean(x * x, axis=-1, keepdims=True) + EPS) * w


def _sigmoid(x):
    return 0.5 * jnp.tanh(0.5 * x) + 0.5


_ACTIVATIONS = {"sigmoid": _sigmoid, "silu": lambda v: v * _sigmoid(v)}


def _norm_matmul_kernel(x_ref, nw_ref, w_ref, o_ref, *f32_refs, n_chunk, f32_cols, acts):
    xb = _rms(x_ref[...], nw_ref[...]).astype(BF16)
    for n0 in range(0, o_ref.shape[1], n_chunk):
        r = jnp.dot(xb, w_ref[:, n0:n0 + n_chunk], preferred_element_type=F32)
        if f32_cols is not None and n0 <= f32_cols[0] and f32_cols[1] <= n0 + n_chunk:
            f32_refs[0][...] = r[:, f32_cols[0] - n0:f32_cols[1] - n0]
        cuts = sorted({n0, n0 + n_chunk} | {c for lo, hi, _ in acts for c in (lo, hi) if n0 < c < n0 + n_chunk})
        for a, b in zip(cuts[:-1], cuts[1:]):
            piece = r[:, a - n0:b - n0]
            for lo, hi, kind in acts:
                if lo <= a and b <= hi:
                    piece = _ACTIVATIONS[kind](piece)
            o_ref[:, a:b] = piece.astype(o_ref.dtype)


def norm_matmul(x, nw, w, out_dtype, f32_cols=None, acts=()):
    t, d = x.shape
    n = w.shape[1]
    n_chunk = min(n, 1024)
    out_specs = [pl.BlockSpec((ROW_TILE, n), lambda i: (i, 0))]
    out_shape = [jax.ShapeDtypeStruct((t, n), out_dtype)]
    if f32_cols is not None:
        lo, hi = f32_cols
        assert lo // n_chunk == (hi - 1) // n_chunk
        out_specs.append(pl.BlockSpec((ROW_TILE, hi - lo), lambda i: (i, 0)))
        out_shape.append(jax.ShapeDtypeStruct((t, hi - lo), F32))
    out = pl.pallas_call(
        functools.partial(_norm_matmul_kernel, n_chunk=n_chunk, f32_cols=f32_cols, acts=tuple(acts)),
        grid=(t // ROW_TILE,),
        in_specs=[pl.BlockSpec((ROW_TILE, d), lambda i: (i, 0)), _resident((1, d)), _resident((d, n))],
        out_specs=out_specs, out_shape=out_shape,
        compiler_params=_params(), name="norm_matmul",
    )(x, nw.reshape(1, d), w)
    return out if f32_cols is not None else out[0]


def s5_tables(lam_re, lam_im, log_dt, b_re, b_im, c_re, c_im, d_skip):
    g, n, p = b_re.shape
    s = S5_CHUNK
    gl = S5_LANES // p
    j = g // gl
    hi = lax.Precision.HIGHEST
    dt = jnp.exp(log_dt)[:, None]
    a_re = jnp.exp(lam_re * dt) * jnp.cos(lam_im * dt)
    a_im = jnp.exp(lam_re * dt) * jnp.sin(lam_im * dt)
    den = lam_re * lam_re + lam_im * lam_im
    nr, ni = a_re - 1.0, a_im
    co_re = (nr * lam_re + ni * lam_im) / den
    co_im = (ni * lam_re - nr * lam_im) / den
    bb_re = co_re[..., None] * b_re - co_im[..., None] * b_im
    bb_im = co_re[..., None] * b_im + co_im[..., None] * b_re
    tau = jnp.arange(s + 1, dtype=F32)[:, None, None]
    pw_mag = jnp.exp(lam_re * dt * tau)
    pw_re = pw_mag * jnp.cos(lam_im * dt * tau)
    pw_im = pw_mag * jnp.sin(lam_im * dt * tau)
    ca_re = c_re[None] * pw_re[:, :, None, :] - c_im[None] * pw_im[:, :, None, :]
    ca_im = c_re[None] * pw_im[:, :, None, :] + c_im[None] * pw_re[:, :, None, :]
    kq = (jnp.einsum('tgpn,gnq->gtqp', ca_re[:s], bb_re, precision=hi)
          - jnp.einsum('tgpn,gnq->gtqp', ca_im[:s], bb_im, precision=hi))
    ts = np.arange(s)
    lag_onehot = (ts[None, None, :] - ts[None, :, None] == ts[:, None, None]).astype(np.float32)
    rev = s - 1 - ts
    w_re = pw_re[rev][:, :, :, None] * bb_re[None] - pw_im[rev][:, :, :, None] * bb_im[None]
    w_im = pw_re[rev][:, :, :, None] * bb_im[None] + pw_im[rev][:, :, :, None] * bb_re[None]
    m_c = (jnp.einsum('gxqp,xst->gsqtp', kq, lag_onehot, precision=hi)
           .reshape(j, gl, s, p, s * p).transpose(0, 2, 1, 3, 4).reshape(j, s * gl * p, s * p))
    w_c = (jnp.stack([w_re, w_im]).reshape(2, s, j, gl, n, p).transpose(2, 1, 3, 5, 0, 4)
           .reshape(j, s * gl * p, 2 * n))
    v_c = (jnp.stack([ca_re[1:], -ca_im[1:]]).reshape(2, s, j, gl, p, n).transpose(2, 0, 3, 5, 1, 4)
           .reshape(j, 2 * gl * n, s * p))
    fl = s * gl * p
    c_io = np.arange(fl)
    c_st = np.arange(2 * gl * n)
    k_io = np.arange(s * p)
    k_st = np.arange(2 * n)
    spread_io = ((k_io[:, None] // p == c_io[None, :] // (gl * p)) & (k_io[:, None] % p == c_io[None, :] % p))
    spread_st = ((k_st[:, None] // n == c_st[None, :] // (gl * n)) & (k_st[:, None] % n == c_st[None, :] % n))
    grp_io = (c_io // p) % gl
    grp_st = (c_st // n) % gl

    def expand(compact, spread, row_grp, col_grp):
        full = jnp.einsum('jrk,kc->jrc', compact.astype(BF16), jnp.asarray(spread, BF16),
                          preferred_element_type=F32)
        return jnp.where(jnp.asarray(row_grp[:, None] == col_grp[None, :]), full, 0.0).astype(BF16)

    m = expand(m_c, spread_io, grp_io, grp_io)
    w = expand(w_c, spread_st, grp_io, grp_st)
    v = expand(v_c, spread_io, grp_st, grp_io)
    a_s_re = pw_re[s].reshape(1, g * n)
    a_s_im = pw_im[s].reshape(1, g * n)
    dtab = jnp.broadcast_to(d_skip.reshape(j, 1, 1, gl * p), (j, 1, s, gl * p)).reshape(j, 1, s * gl * p)
    return m, w, v, a_s_re, a_s_im, dtab


def _s5_flat(u_ref):
    return jnp.concatenate([u_ref[:, t, :] for t in range(u_ref.shape[1])], axis=1)


def _s5a_kernel(u_ref, w_ref, ire_ref, iim_ref):
    r = jnp.dot(_s5_flat(u_ref).astype(BF16), w_ref[0], preferred_element_type=F32)
    half = r.shape[1] // 2
    ire_ref[...] = r[:, :half]
    iim_ref[...] = r[:, half:]


def _s5scan_kernel(ire_ref, iim_ref, ar_ref, ai_ref, h0r_ref, h0i_ref,
                   hpr_ref, hpi_ref, hfr_ref, hfi_ref):
    nb, nc, _ = ire_ref.shape
    ar, ai = ar_ref[...], ai_ref[...]

    def body(c, carry):
        out = []
        for b in range(nb):
            hr, hi = carry[2 * b], carry[2 * b + 1]
            hpr_ref[b, pl.ds(c, 1), :] = hr
            hpi_ref[b, pl.ds(c, 1), :] = hi
            out.append(ar * hr - ai * hi + ire_ref[b, pl.ds(c, 1), :])
            out.append(ar * hi + ai * hr + iim_ref[b, pl.ds(c, 1), :])
        return tuple(out)

    init = []
    for b in range(nb):
        init += [h0r_ref[b], h0i_ref[b]]
    fin = lax.fori_loop(0, nc, body, tuple(init))
    for b in range(nb):
        hfr_ref[b] = fin[2 * b]
        hfi_ref[b] = fin[2 * b + 1]


def _s5b_kernel(u_ref, hpr_ref, hpi_ref, m_ref, v_ref, d_ref, y_ref):
    uf = _s5_flat(u_ref)
    half = hpr_ref.shape[1]
    y = d_ref[0] * uf + jnp.dot(uf.astype(BF16), m_ref[0], preferred_element_type=F32)
    y += jnp.dot(hpr_ref[...].astype(BF16), v_ref[0, :half, :], preferred_element_type=F32)
    y += jnp.dot(hpi_ref[...].astype(BF16), v_ref[0, half:, :], preferred_element_type=F32)
    lanes = y_ref.shape[2]
    for t in range(y_ref.shape[1]):
        y_ref[:, t, :] = y[:, t * lanes:(t + 1) * lanes]


def s5_apply(u, bsz, h0_re, h0_im, tables):
    m, w, v, a_re, a_im, dtab = tables
    tokens, width = u.shape
    nj = m.shape[0]
    s = S5_CHUNK
    rows = tokens // s
    nc = rows // bsz
    lanes = a_re.shape[1]
    half = w.shape[2] // 2
    fl = m.shape[1]
    rt = min(rows, 512)
    u3 = u.reshape(rows, s, width)
    u_spec = pl.BlockSpec((rt, s, S5_LANES), lambda j, r: (r, 0, j))
    st_spec = pl.BlockSpec((rt, half), lambda j, r: (r, j))
    tab_spec = pl.BlockSpec((1, fl, fl), lambda j, r: (j, 0, 0))
    inj_re, inj_im = pl.pallas_call(
        _s5a_kernel, grid=(nj, rows // rt),
        in_specs=[u_spec, tab_spec],
        out_specs=[st_spec, st_spec],
        out_shape=[jax.ShapeDtypeStruct((rows, lanes), F32)] * 2,
        compiler_params=_params(2), name="s5_chunk_in",
    )(u3, w)

    sb, lw = 4, 512
    seq_spec = pl.BlockSpec((sb, nc, lw), lambda b, l: (b, 0, l))
    vec_spec = pl.BlockSpec((sb, 1, lw), lambda b, l: (b, 0, l))
    atab_spec = pl.BlockSpec((1, lw), lambda b, l: (0, l))
    hp_re, hp_im, hf_re, hf_im = pl.pallas_call(
        _s5scan_kernel, grid=(bsz // sb, lanes // lw),
        in_specs=[seq_spec, seq_spec, atab_spec, atab_spec, vec_spec, vec_spec],
        out_specs=[seq_spec, seq_spec, vec_spec, vec_spec],
        out_shape=[jax.ShapeDtypeStruct((bsz, nc, lanes), F32)] * 2
        + [jax.ShapeDtypeStruct((bsz, 1, lanes), F32)] * 2,
        compiler_params=_params(2), name="s5_scan",
    )(inj_re.reshape(bsz, nc, lanes), inj_im.reshape(bsz, nc, lanes), a_re, a_im,
      h0_re.reshape(bsz, 1, lanes), h0_im.reshape(bsz, 1, lanes))

    y3 = pl.pallas_call(
        _s5b_kernel, grid=(nj, rows // rt),
        in_specs=[u_spec, st_spec, st_spec, tab_spec, tab_spec,
                  pl.BlockSpec((1, 1, fl), lambda j, r: (j, 0, 0))],
        out_specs=u_spec,
        out_shape=jax.ShapeDtypeStruct((rows, s, width), F32),
        compiler_params=_params(2), name="s5_chunk_out",
    )(u3, hp_re.reshape(rows, lanes), hp_im.reshape(rows, lanes), m, v, dtab)
    g = lanes // SSM_STATE
    return y3.reshape(tokens, width), hf_re.reshape(bsz, g, SSM_STATE), hf_im.reshape(bsz, g, SSM_STATE)


def _retention_gammas():
    return 1.0 - np.exp2(-5.0 - np.arange(RET_HEADS, dtype=np.float64))


def retention_tables(tile, chunk):
    gam = _retention_gammas()[:, None, None]
    i = np.arange(tile)[:, None]
    j = np.arange(tile)[None, :]
    same = (i // chunk) == (j // chunk)
    earlier = (j // chunk) < (i // chunk)
    dist = np.where(same, np.abs(i - j), np.where(earlier, i - j, 0))
    dmask = np.where(same | earlier, gam ** dist[None], 0.0)
    qw = np.broadcast_to((gam[:, :, 0] ** (np.arange(tile) + 1.0))[:, :, None], (RET_HEADS, tile, HEAD_DIM))
    kw = np.broadcast_to((gam[:, :, 0] ** (tile - 1.0 - np.arange(tile)))[:, :, None], (RET_HEADS, tile, HEAD_DIM))
    return (jnp.asarray(dmask, F32), jnp.asarray(qw, F32), jnp.asarray(kw, F32),
            tuple(float(x) for x in _retention_gammas() ** tile))


def rope_tables(pos):
    half = HEAD_DIM // 2
    inv = jnp.exp(-math.log(ROPE_BASE) * 2.0 * jnp.arange(half, dtype=F32) / HEAD_DIM)
    ang = pos.astype(F32)[:, None] * inv[None, :]
    cos, sin = jnp.cos(ang), jnp.sin(ang)
    cosf = jnp.concatenate([cos, cos], axis=1)
    sinf = jnp.concatenate([-sin, sin], axis=1)
    return cosf, sinf


def _mixer_kernel(x_ref, zq_ref, xq_ref, gl_ref, y_ref, cq_ref, sq_ref, ck_ref, sk_ref,
                  dm_ref, qw_ref, kw_ref, mk_ref, mv_ref, s0_ref, gn_ref,
                  wro_ref, wglu_ref, wso_ref, wxo_ref, wout_ref,
                  h_ref, sout_ref, s_scr, o_scr, xo_scr, glu_scr, mg_scr, *, nb, tl, tile_decay):
    hd = HEAD_DIM
    qk = RET_HEADS * hd

    @pl.when(pl.program_id(1) == 0)
    def _():
        s_scr[...] = s0_ref[...]

    cq, sq, ck, sk = cq_ref[...], sq_ref[...], ck_ref[...], sk_ref[...]
    for n in range(nb):
        rows = slice(n * tl, (n + 1) * tl)
        for h in range(RET_HEADS):
            c0 = h * hd
            q = zq_ref[rows, c0:c0 + hd].astype(F32)
            k = zq_ref[rows, qk + c0:qk + c0 + hd].astype(F32)
            v = zq_ref[rows, 2 * qk + c0:2 * qk + c0 + hd]
            g = zq_ref[rows, 3 * qk + c0:3 * qk + c0 + hd].astype(F32)
            qr = q * cq + pltpu.roll(q, hd // 2, 1) * sq
            kr = k * ck + pltpu.roll(k, hd // 2, 1) * sk
            sc = lax.dot_general(qr.astype(BF16), kr.astype(BF16), NT_DIMS,
                                 preferred_element_type=F32) * dm_ref[h]
            o = jnp.dot(sc.astype(BF16), v, preferred_element_type=F32)
            s_old = s_scr[n, h]
            o += jnp.dot((qr * qw_ref[h]).astype(BF16), s_old.astype(BF16), preferred_element_type=F32)
            kv = lax.dot_general((kr * kw_ref[h]).astype(BF16), v, TN_DIMS, preferred_element_type=F32)
            s_scr[n, h] = tile_decay[h] * s_old + kv
            d = o - jnp.mean(o, axis=-1, keepdims=True)
            on = d * lax.rsqrt(jnp.mean(d * d, axis=-1, keepdims=True) + EPS) * gn_ref[:, c0:c0 + hd]
            o_scr[rows, c0:c0 + hd] = (on * g).astype(BF16)
            mem_rows = pl.ds(h, mk_ref.shape[1] // X_HEADS, stride=X_HEADS)
            mkh = mk_ref[n, mem_rows, :].astype(BF16)
            mvh = mv_ref[n, mem_rows, :].astype(BF16)
            s = lax.dot_general(xq_ref[rows, c0:c0 + hd], mkh, NT_DIMS,
                                preferred_element_type=F32) * (hd ** -0.5)
            e = jnp.exp(s - jnp.max(s, axis=-1, keepdims=True))
            p = e / jnp.sum(e, axis=-1, keepdims=True)
            xo_scr[rows, c0:c0 + hd] = jnp.dot(p.astype(BF16), mvh, preferred_element_type=F32).astype(BF16)

    cw = MIX_COLS
    dm = h_ref.shape[1]
    half = wglu_ref.shape[1] // 2
    yb = jax.nn.gelu(y_ref[...]).astype(BF16)
    for c0 in range(0, half, cw):
        ga = jnp.dot(yb, wglu_ref[:, c0:c0 + cw], preferred_element_type=F32)
        gb = jnp.dot(yb, wglu_ref[:, half + c0:half + c0 + cw], preferred_element_type=F32)
        glu_scr[:, c0:c0 + cw] = (ga * _sigmoid(gb)).astype(BF16)
    for c0 in range(0, dm, cw):
        cols = slice(c0, c0 + cw)
        ret = jnp.dot(o_scr[...], wro_ref[:, cols], preferred_element_type=F32)
        ssm = jnp.dot(glu_scr[...], wso_ref[:, cols], preferred_element_type=F32)
        xb = jnp.dot(xo_scr[...], wxo_ref[:, cols], preferred_element_type=F32)
        merged = (gl_ref[:, c0:c0 + cw].astype(F32) * ret
                  + gl_ref[:, dm + c0:dm + c0 + cw].astype(F32) * ssm
                  + gl_ref[:, 2 * dm + c0:2 * dm + c0 + cw].astype(F32) * xb)
        mg_scr[:, cols] = merged.astype(BF16)
    for c0 in range(0, dm, cw):
        cols = slice(c0, c0 + cw)
        h_ref[:, cols] = x_ref[:, cols] + jnp.dot(mg_scr[...], wout_ref[:, cols], preferred_element_type=F32)
    sout_ref[...] = s_scr[...]


def mixer(x, z, y_ssm, pos, mem_k, mem_v, s0, ret_gn, w_ret_o, w_ssm_glu, w_ssm_o, w_x_o, w_out, *, nb, tl):
    bsz, length, dm = x.shape
    chunk = min(CHUNK, length)
    nl = length // tl
    rows = nb * tl
    qk = RET_HEADS * HEAD_DIM
    sw = y_ssm.shape[1]
    xw = X_HEADS * HEAD_DIM
    gate_col = (4 * qk + sw + xw)
    assert gate_col % (3 * dm) == 0 and (4 * qk + sw) % xw == 0
    dmask, qw, kw, tile_decay = retention_tables(tl, chunk)
    cosf, sinf = rope_tables(pos)
    scale = HEAD_DIM ** -0.5
    row_map = lambda b, l: (b * nl + l, 0)
    tab_map = lambda b, l: (l, 0)
    st_spec = pl.BlockSpec((nb, RET_HEADS, HEAD_DIM, HEAD_DIM), lambda b, l: (b, 0, 0, 0))
    mem_spec = pl.BlockSpec((nb,) + mem_k.shape[1:], lambda b, l: (b,) + (0,) * (mem_k.ndim - 1))
    h, s_out = pl.pallas_call(
        functools.partial(_mixer_kernel, nb=nb, tl=tl, tile_decay=tile_decay),
        grid=(bsz // nb, nl),
        in_specs=[pl.BlockSpec((rows, dm), row_map),
                  pl.BlockSpec((rows, 4 * qk), row_map),
                  pl.BlockSpec((rows, xw), lambda b, l: (b * nl + l, (4 * qk + sw) // xw)),
                  pl.BlockSpec((rows, 3 * dm), lambda b, l: (b * nl + l, gate_col // (3 * dm))),
                  pl.BlockSpec((rows, sw), row_map),
                  pl.BlockSpec((tl, HEAD_DIM), tab_map), pl.BlockSpec((tl, HEAD_DIM), tab_map),
                  pl.BlockSpec((tl, HEAD_DIM), tab_map), pl.BlockSpec((tl, HEAD_DIM), tab_map),
                  _resident(dmask.shape), _resident(qw.shape), _resident(kw.shape),
                  mem_spec, mem_spec, st_spec, _resident((1, qk)),
                  _resident(w_ret_o.shape), _resident(w_ssm_glu.shape), _resident(w_ssm_o.shape),
                  _resident(w_x_o.shape), _resident(w_out.shape)],
        out_specs=[pl.BlockSpec((rows, dm), row_map), st_spec],
        out_shape=[jax.ShapeDtypeStruct((bsz * length, dm), F32),
                   jax.ShapeDtypeStruct(s0.shape, F32)],
        scratch_shapes=[pltpu.VMEM((nb, RET_HEADS, HEAD_DIM, HEAD_DIM), F32),
                        pltpu.VMEM((rows, qk), BF16), pltpu.VMEM((rows, xw), BF16),
                        pltpu.VMEM((rows, w_ssm_o.shape[0]), BF16), pltpu.VMEM((rows, dm), BF16)],
        compiler_params=_params(2), name="mixer",
    )(x.reshape(bsz * length, dm), z, z, z, y_ssm,
      cosf * scale, sinf * scale, cosf, sinf, dmask, qw, kw,
      mem_k, mem_v, s0, ret_gn.reshape(1, qk), w_ret_o, w_ssm_glu, w_ssm_o, w_x_o, w_out)
    return h, s_out


def _router_kernel(hp_ref, hs_ref, nw_ref, wrt_ref, br_ref, tri_ref, low_ref,
                   xn_ref, gw_ref, crow_ref, cnt_ref, *, prompt_tiles):
    h = jnp.where(pl.program_id(0) < prompt_tiles, hp_ref[...], hs_ref[...])
    xn = _rms(h, nw_ref[...]).astype(BF16)
    xn_ref[...] = xn
    logits = lax.dot_general(wrt_ref[...], xn, NT_DIMS, preferred_element_type=F32) + br_ref[...]
    ne = logits.shape[0]
    iota = lax.broadcasted_iota(jnp.int32, logits.shape, 0)
    rest = logits
    sel = jnp.zeros(logits.shape, jnp.bool_)
    vals, idxs = [], []
    for _ in range(TOP_K):
        m = jnp.max(rest, axis=0, keepdims=True)
        ix = jnp.min(jnp.where(rest == m, iota, ne), axis=0, keepdims=True)
        hit = iota == ix
        vals.append(m)
        idxs.append(ix)
        sel = jnp.logical_or(sel, hit)
        rest = jnp.where(hit, -jnp.inf, rest)
    es = [jnp.exp(v - vals[0]) for v in vals]
    tot = es[0] + es[1] + es[2] + es[3]
    before = jnp.dot(sel.astype(BF16), tri_ref[...], preferred_element_type=F32)
    cnt = jnp.sum(sel.astype(F32), axis=1, keepdims=True)
    seg = jnp.floor((cnt + (ROW_ALIGN - 1.0)) * (1.0 / ROW_ALIGN)) * ROW_ALIGN
    start = jnp.dot(low_ref[...], jnp.broadcast_to(seg, before.shape), precision=lax.Precision.HIGHEST,
                    preferred_element_type=F32)
    place = start + before
    for k in range(TOP_K):
        gw_ref[k:k + 1, :] = es[k] / tot
        crow_ref[k:k + 1, :] = jnp.sum(jnp.where(iota == idxs[k], place, 0.0), axis=0,
                                       keepdims=True).astype(jnp.int32)
    cnt_ref[...] = jnp.broadcast_to(cnt, cnt_ref.shape)


def router(h_p, h_s, norm_ffn, w_router, b_router):
    dm = h_p.shape[1]
    ne = w_router.shape[1]
    tt = ROW_TILE
    npt, nst = h_p.shape[0] // tt, h_s.shape[0] // tt
    t = (npt + nst) * tt
    tri = jnp.asarray(np.triu(np.ones((tt, tt), np.float32), k=1), BF16)
    low = jnp.asarray(np.tril(np.ones((ne, ne), np.float32), k=-1))
    tok_spec = pl.BlockSpec((TOP_K, tt), lambda i: (0, i))
    return pl.pallas_call(
        functools.partial(_router_kernel, prompt_tiles=npt), grid=(npt + nst,),
        in_specs=[pl.BlockSpec((tt, dm), lambda i: (jnp.minimum(i, npt - 1), 0)),
                  pl.BlockSpec((tt, dm), lambda i: (jnp.maximum(i - npt, 0), 0)),
                  _resident((1, dm)), _resident((ne, dm)), _resident((ne, 1)), _resident((tt, tt)),
                  _resident((ne, ne))],
        out_specs=[pl.BlockSpec((tt, dm), lambda i: (i, 0)), tok_spec, tok_spec,
                   pl.BlockSpec((ne, 128), lambda i: (i, 0))],
        out_shape=[jax.ShapeDtypeStruct((t, dm), BF16),
                   jax.ShapeDtypeStruct((TOP_K, t), F32), jax.ShapeDtypeStruct((TOP_K, t), jnp.int32),
                   jax.ShapeDtypeStruct(((npt + nst) * ne, 128), F32)],
        compiler_params=_params(), name="router",
    )(h_p, h_s, norm_ffn.reshape(1, dm), w_router.T.astype(BF16), b_router.reshape(ne, 1), tri, low)


def _pack_bf16_pairs(x, exact=False):
    n = x.shape[1] // 2
    lo, hi = x[:, :n], x[:, n:]
    if not exact:
        lo, hi = lo.astype(BF16).astype(F32), hi.astype(BF16).astype(F32)
    lo = lax.bitcast_convert_type(lo, jnp.int32)
    hi = lax.bitcast_convert_type(hi, jnp.int32)
    return lax.shift_right_logical(lo, 16) | (hi & -65536)


def _unpack_bf16_pairs(u):
    lo = lax.bitcast_convert_type(lax.shift_left(u, 16), F32).astype(BF16)
    hi = lax.bitcast_convert_type(u & -65536, F32).astype(BF16)
    return lo, hi


def _split_count(n, fn):
    def quad(jq, carry):
        fn(4 * jq, 4)
        return carry

    lax.fori_loop(0, n // 4, quad, 0)

    @pl.when(n % 4 >= 2)
    def _():
        fn(n // 4 * 4, 2)

    @pl.when(n % 2 == 1)
    def _():
        fn(n // 2 * 2, 1)


def _chunk_loop(n, fn):
    def quad(jq, carry):
        for u in range(4):
            fn(4 * jq + u)
        return carry

    def single(j, carry):
        fn(j)
        return carry

    lax.fori_loop(0, n // 4, quad, 0)
    lax.fori_loop(n // 4 * 4, n, single, 0)


def _dispatch_kernel(dst_ref, tch_ref, zs_ref, zn_ref, used_ref,
                     xn_ref, crow_ref, xs_ref, cbuf, zbuf, sems, zsem, *, ne, bm):
    i = pl.program_id(0)
    nt = pl.num_programs(0)
    slot = lax.rem(i, 2)
    tt = xn_ref.shape[0]
    cr = cbuf.shape[1]
    ra = ROW_ALIGN
    n_blocks = xs_ref.shape[0] // bm

    def chunk_copy(sl, src_row, dst_row, rows):
        return pltpu.make_async_copy(cbuf.at[sl, pl.ds(src_row, rows)], xs_ref.at[pl.ds(dst_row, rows)],
                                     sems.at[sl])

    def wait_chunks(sl, n):
        _split_count(n, lambda j, m: chunk_copy(sl, 0, 0, ra * m).wait())

    def tail_copy(e, j):
        return pltpu.make_async_copy(zbuf.at[pl.ds(0, ra)],
                                     xs_ref.at[pl.ds(pl.multiple_of(zs_ref[e] + ra * j, ra), ra)], zsem)

    def block_copy(b):
        return pltpu.make_async_copy(zbuf, xs_ref.at[pl.ds(pl.multiple_of(b * bm, bm), bm)], zsem)

    def zero_fill(start):
        def per_expert(e, carry):
            def per_chunk(j, c2):
                (tail_copy(e, j).start() if start else tail_copy(e, j).wait())
                return c2
            lax.fori_loop(0, zn_ref[e], per_chunk, 0)
            return carry
        lax.fori_loop(0, ne, per_expert, 0)

        def per_block(b, carry):
            (block_copy(b).start() if start else block_copy(b).wait())
            return carry
        lax.fori_loop(used_ref[0], n_blocks, per_block, 0)

    @pl.when(i == 0)
    def _():
        zbuf[...] = jnp.zeros_like(zbuf)
        zero_fill(True)
        zero_fill(False)

    @pl.when(i >= 2)
    def _():
        wait_chunks(slot, tch_ref[jnp.maximum(i - 2, 0)])

    crow = crow_ref[...]
    rows = lax.broadcasted_iota(jnp.int32, (cr, tt), 0)
    hit = rows == crow[0:1, :]
    for k in range(1, TOP_K):
        hit = jnp.logical_or(hit, rows == crow[k:k + 1, :])
    packed = _pack_bf16_pairs(jnp.dot(jnp.where(hit, 1.0, 0.0).astype(BF16), xn_ref[...],
                                      preferred_element_type=F32), exact=True)
    cbuf[slot] = packed

    chunks_per_tile = cr // ra
    _chunk_loop(tch_ref[i], lambda c: chunk_copy(
        slot, pl.multiple_of(c * ra, ra), pl.multiple_of(dst_ref[i * chunks_per_tile + c], ra), ra).start())

    @pl.when(i == nt - 1)
    def _():
        wait_chunks(slot, tch_ref[i])
        wait_chunks(1 - slot, jnp.where(nt >= 2, tch_ref[jnp.maximum(i - 1, 0)], 0))


def dispatch(xn, crow, tables, n_rows, ne):
    t, dm = xn.shape
    tt = ROW_TILE
    bm = FFN_BLOCK
    cr = _compact_rows(ne)
    grid_spec = pltpu.PrefetchScalarGridSpec(
        num_scalar_prefetch=5, grid=(t // tt,),
        in_specs=[pl.BlockSpec((tt, dm), lambda i, *_: (i, 0)),
                  pl.BlockSpec((TOP_K, tt), lambda i, *_: (0, i))],
        out_specs=pl.BlockSpec(memory_space=pl.ANY),
        scratch_shapes=[pltpu.VMEM((2, cr, dm // 2), jnp.int32), pltpu.VMEM((bm, dm // 2), jnp.int32),
                        pltpu.SemaphoreType.DMA((2,)), pltpu.SemaphoreType.DMA(())])
    return pl.pallas_call(
        functools.partial(_dispatch_kernel, ne=ne, bm=bm), grid_spec=grid_spec,
        out_shape=jax.ShapeDtypeStruct((n_rows, dm // 2), jnp.int32),
        compiler_params=_params(), name="dispatch",
    )(*tables, xn, crow)


FFN_W_PIECES = (4, 2)


def _ffn_kernel(first_ref, nblk_ref, used_ref, next_ref, wslot_ref, lead_ref,
                xs_hbm, wgu_hbm, bgu_ref, wd_hbm, bd_ref, y_hbm,
                wgu_f32, wd_f32, wgu_bf, wd_bf, xbuf, ybuf, xsem, ysem, wsem, zsem, *, bm):
    e = pl.program_id(0)
    n = nblk_ref[e]
    b0 = first_ref[e]
    n_blocks = y_hbm.shape[0] // bm
    n_pieces = sum(FFN_W_PIECES)

    def block_rows(b):
        return pl.ds(pl.multiple_of(b * bm, bm), bm)

    used = used_ref[0]
    x_slots = xbuf.shape[0]

    def x_copy(g):
        sl = lax.rem(g, x_slots)
        return pltpu.make_async_copy(xs_hbm.at[block_rows(g)], xbuf.at[sl], xsem.at[sl])

    def y_copy(g):
        sl = lax.rem(g, 2)
        return pltpu.make_async_copy(ybuf.at[sl], y_hbm.at[block_rows(g)], ysem.at[sl])

    def w_piece(ex, sl, p):
        src, dst, q, parts = ((wgu_hbm, wgu_f32, p, FFN_W_PIECES[0]) if p < FFN_W_PIECES[0]
                              else (wd_hbm, wd_f32, p - FFN_W_PIECES[0], FFN_W_PIECES[1]))
        rows = src.shape[1] // parts
        return pltpu.make_async_copy(src.at[ex, pl.ds(q * rows, rows)], dst.at[sl, pl.ds(q * rows, rows)],
                                     wsem.at[sl, p])

    @pl.when(n > 0)
    def _():
        ws = wslot_ref[e]
        nxt = next_ref[e]

        @pl.when(lead_ref[0] == e)
        def _():
            for g in range(x_slots - 1):
                @pl.when(g < used)
                def _():
                    x_copy(g).start()
            for p in range(n_pieces):
                w_piece(e, ws, p).start()

        for p in range(n_pieces):
            w_piece(e, ws, p).wait()
        wgu_bf[...] = wgu_f32[ws].astype(BF16)
        wd_bf[...] = wd_f32[ws].astype(BF16)

        def block(j, carry):
            g = b0 + j
            x_copy(g).wait()

            @pl.when(g + x_slots - 1 < used)
            def _():
                x_copy(g + x_slots - 1).start()

            for p in range(n_pieces):
                @pl.when(jnp.logical_and(j == p, nxt >= 0))
                def _():
                    w_piece(nxt, 1 - ws, p).start()

            @pl.when(g >= 2)
            def _():
                y_copy(g - 2).wait()

            x = jnp.concatenate(_unpack_bf16_pairs(xbuf[lax.rem(g, x_slots)]), axis=1)
            hgu = jnp.dot(x, wgu_bf[...], preferred_element_type=F32) + bgu_ref[0]
            ff = hgu.shape[1] // 2
            gate = jnp.minimum(hgu[:, :ff], SWIGLU_LIMIT)
            up = jnp.clip(hgu[:, ff:], -SWIGLU_LIMIT, SWIGLU_LIMIT)
            act = (up + 1.0) * gate * _sigmoid(SWIGLU_ALPHA * gate)
            ybuf[lax.rem(g, 2)] = _pack_bf16_pairs(
                jnp.dot(act.astype(BF16), wd_bf[...], preferred_element_type=F32) + bd_ref[0])
            y_copy(g).start()
            return carry

        lax.fori_loop(0, n, block, 0)

        for p in range(n_pieces):
            @pl.when(jnp.logical_and(p >= n, nxt >= 0))
            def _():
                w_piece(nxt, 1 - ws, p).start()

    @pl.when(e == pl.num_programs(0) - 1)
    def _():
        for back in (2, 1):
            @pl.when(used >= back)
            def _():
                y_copy(used - back).wait()
        ybuf[0] = jnp.zeros(ybuf.shape[1:], ybuf.dtype)

        def zero_copy(b):
            return pltpu.make_async_copy(ybuf.at[0], y_hbm.at[block_rows(b)], zsem)

        def start(b, carry):
            zero_copy(b).start()
            return carry

        def wait(b, carry):
            zero_copy(b).wait()
            return carry

        lax.fori_loop(used_ref[0], n_blocks, start, 0)
        lax.fori_loop(used_ref[0], n_blocks, wait, 0)


def expert_ffn(xs, expert_tables, w_gate_up, b_gate_up, w_down, b_down):
    n_rows = xs.shape[0]
    ne, dm, ff2 = w_gate_up.shape
    bm = FFN_BLOCK
    grid_spec = pltpu.PrefetchScalarGridSpec(
        num_scalar_prefetch=6, grid=(ne,),
        in_specs=[pl.BlockSpec(memory_space=pl.ANY),
                  pl.BlockSpec(memory_space=pl.ANY),
                  pl.BlockSpec((1, 1, ff2), lambda e, *_: (e, 0, 0)),
                  pl.BlockSpec(memory_space=pl.ANY),
                  pl.BlockSpec((1, 1, dm), lambda e, *_: (e, 0, 0))],
        out_specs=pl.BlockSpec(memory_space=pl.ANY),
        scratch_shapes=[pltpu.VMEM((2, dm, ff2), F32), pltpu.VMEM((2, ff2 // 2, dm), F32),
                        pltpu.VMEM((dm, ff2), BF16), pltpu.VMEM((ff2 // 2, dm), BF16),
                        pltpu.VMEM((3, bm, dm // 2), jnp.int32), pltpu.VMEM((2, bm, dm // 2), jnp.int32),
                        pltpu.SemaphoreType.DMA((3,)), pltpu.SemaphoreType.DMA((2,)),
                        pltpu.SemaphoreType.DMA((2, sum(FFN_W_PIECES))), pltpu.SemaphoreType.DMA(())])
    return pl.pallas_call(
        functools.partial(_ffn_kernel, bm=bm), grid_spec=grid_spec,
        out_shape=jax.ShapeDtypeStruct((n_rows, dm // 2), jnp.int32),
        compiler_params=_params(), name="expert_ffn",
    )(*expert_tables, xs, w_gate_up, b_gate_up.reshape(ne, 1, ff2), w_down, b_down.reshape(ne, 1, dm))


def _combine_kernel(dst_ref, tch_ref,
                    crow_ref, gw_ref, hp_ref, hs_ref, fn_ref, yr_ref, yp_ref, ys_ref, ybuf, sems,
                    *, prompt_tiles):
    i = pl.program_id(0)
    nt = pl.num_programs(0)
    slot = lax.rem(i, 2)
    tt = hp_ref.shape[0]
    cr = ybuf.shape[1]
    ra = ROW_ALIGN

    def chunk_copy(sl, src_row, dst_row, rows):
        return pltpu.make_async_copy(yr_ref.at[pl.ds(src_row, rows)], ybuf.at[sl, pl.ds(dst_row, rows)],
                                     sems.at[sl])

    def fetch(tile, sl):
        chunks_per_tile = cr // ra
        _chunk_loop(tch_ref[tile], lambda c: chunk_copy(
            sl, pl.multiple_of(dst_ref[tile * chunks_per_tile + c], ra), pl.multiple_of(c * ra, ra), ra).start())

    @pl.when(i == 0)
    def _():
        ybuf[...] = jnp.zeros_like(ybuf)
        fetch(0, 0)

    @pl.when(i + 1 < nt)
    def _():
        fetch(i + 1, 1 - slot)

    _split_count(tch_ref[i], lambda j, m: chunk_copy(slot, 0, 0, ra * m).wait())

    y_lo, y_hi = _unpack_bf16_pairs(ybuf[slot])
    cols = lax.broadcasted_iota(jnp.int32, (tt, cr), 1)
    q = jnp.zeros((tt, cr), F32)
    for k in range(TOP_K):
        q += jnp.where(cols == crow_ref[:, k:k + 1], gw_ref[:, k:k + 1], 0.0)
    qb = q.astype(BF16)
    moe = jnp.concatenate([jnp.dot(qb, y_lo, preferred_element_type=F32),
                           jnp.dot(qb, y_hi, preferred_element_type=F32)], axis=1)
    h = jnp.where(i < prompt_tiles, hp_ref[...], hs_ref[...])
    out = _rms(h + moe, fn_ref[...])

    @pl.when(i < prompt_tiles)
    def _():
        yp_ref[...] = out

    @pl.when(i >= prompt_tiles)
    def _():
        ys_ref[...] = out


def combine(crow_t, gw_t, h_p, h_s, final_norm, y_rows, tables, ne):
    dm = h_p.shape[1]
    tt = ROW_TILE
    npt, nst = h_p.shape[0] // tt, h_s.shape[0] // tt
    cr = _compact_rows(ne)
    p_map = lambda i, *_: (jnp.minimum(i, npt - 1), 0)
    s_map = lambda i, *_: (jnp.maximum(i - npt, 0), 0)
    grid_spec = pltpu.PrefetchScalarGridSpec(
        num_scalar_prefetch=2, grid=(npt + nst,),
        in_specs=[pl.BlockSpec((tt, TOP_K), lambda i, *_: (i, 0)),
                  pl.BlockSpec((tt, TOP_K), lambda i, *_: (i, 0)),
                  pl.BlockSpec((tt, dm), p_map), pl.BlockSpec((tt, dm), s_map),
                  pl.BlockSpec((1, dm), lambda i, *_: (0, 0)),
                  pl.BlockSpec(memory_space=pl.ANY)],
        out_specs=[pl.BlockSpec((tt, dm), p_map), pl.BlockSpec((tt, dm), s_map)],
        scratch_shapes=[pltpu.VMEM((2, cr, dm // 2), jnp.int32), pltpu.SemaphoreType.DMA((2,))])
    return pl.pallas_call(
        functools.partial(_combine_kernel, prompt_tiles=npt), grid_spec=grid_spec,
        out_shape=[jax.ShapeDtypeStruct(h_p.shape, F32), jax.ShapeDtypeStruct(h_s.shape, F32)],
        compiler_params=_params(), name="combine",
    )(*tables, crow_t, gw_t, h_p, h_s, final_norm.reshape(1, dm), y_rows)


def _compact_rows(ne):
    return -(-(TOP_K * ROW_TILE + ne * (ROW_ALIGN - 1)) // 128) * 128


def moe_and_final_norm(h_p, h_s, norm_ffn, w_router, b_router, w_gate_up, b_gate_up, w_down, b_down, final_norm):
    ne = w_router.shape[1]
    bm = FFN_BLOCK
    ra = ROW_ALIGN
    xn, gw, crow, cnt = router(h_p, h_s, norm_ffn, w_router, b_router)
    t = xn.shape[0]
    nt = t // ROW_TILE
    seg = -(-cnt[:, 0].astype(jnp.int32).reshape(nt, ne) // ra) * ra
    seg_before = jnp.cumsum(seg, axis=0) - seg
    rows_e = jnp.sum(seg, axis=0)
    padded = -(-rows_e // bm) * bm
    pad_ends = jnp.cumsum(padded)
    pad_starts = pad_ends - padded
    n_blocks = -(-(t * TOP_K + nt * ne * (ra - 1) + ne * (bm - 1)) // bm)
    n_used = pad_ends[-1] // bm
    experts = jnp.arange(ne, dtype=jnp.int32)
    chunks = seg // ra
    chunk_end = jnp.cumsum(chunks, axis=1)
    c_ids = jnp.arange(_compact_rows(ne) // ra, dtype=jnp.int32)
    owner = jnp.sum((chunk_end[:, None, :] <= c_ids[None, :, None]).astype(jnp.int32), axis=2)
    seg_shift = pad_starts[None, :] + seg_before - ra * (chunk_end - chunks)
    dst = ra * c_ids[None, :] + jnp.sum(
        jnp.where(owner[:, :, None] == experts[None, None, :], seg_shift[:, None, :], 0), axis=2)
    seg_tables = (dst.reshape(-1), chunk_end[:, -1])
    fill_tables = (pad_starts + rows_e, (padded - rows_e) // ra, n_used.reshape(1))
    to_i32 = lambda xs: tuple(x.astype(jnp.int32) for x in xs)
    xs = dispatch(xn, crow, to_i32(seg_tables + fill_tables), n_blocks * bm, ne)
    active = padded > 0
    later = jnp.where(active, experts, ne)
    next_active = jnp.concatenate([lax.cummin(later, reverse=True)[1:], jnp.full((1,), ne, jnp.int32)])
    expert_tables = (pad_starts // bm, padded // bm, n_used.reshape(1),
                     jnp.where(next_active < ne, next_active, -1),
                     (jnp.cumsum(active.astype(jnp.int32)) - active.astype(jnp.int32)) % 2,
                     jnp.min(later).reshape(1))
    y_rows = expert_ffn(xs, to_i32(expert_tables), w_gate_up, b_gate_up, w_down, b_down)
    return combine(crow.T, gw.T, h_p, h_s, final_norm, y_rows, to_i32(seg_tables), ne)


def kernel(x_prompt, x_sample, cache_mem_k, cache_mem_v, state_ret, state_ssm_re, state_ssm_im, mem_prompt, norm_mix, w_in, ret_gn, w_ret_o, ssm_lam_re, ssm_lam_im, ssm_log_dt, ssm_b_re, ssm_b_im, ssm_c_re, ssm_c_im, ssm_d, w_ssm_glu, w_ssm_o, mem_norm, w_mem_kv, w_x_o, w_out, norm_ffn, w_router, b_router, w_gate_up, b_gate_up, w_down, b_down, final_norm):
    assert norm_mix.shape[0] == 1, "single-layer step"
    bp, lp, dm = x_prompt.shape
    bs, ls, _ = x_sample.shape
    n_mem = mem_prompt.shape[1]
    xw = X_HEADS * HEAD_DIM
    qk = RET_HEADS * HEAD_DIM
    sw = ssm_d.shape[1]
    g = ssm_lam_re.shape[1]

    w_in_b = w_in[0].astype(BF16)
    tables = s5_tables(ssm_lam_re[0], ssm_lam_im[0], ssm_log_dt[0], ssm_b_re[0], ssm_b_im[0],
                       ssm_c_re[0], ssm_c_im[0], ssm_d[0])
    mix_w = (ret_gn[0], w_ret_o[0].astype(BF16), w_ssm_glu[0].astype(BF16), w_ssm_o[0].astype(BF16),
             w_x_o[0].astype(BF16), w_out[0].astype(BF16))

    kv = norm_matmul(mem_prompt.reshape(bp * n_mem, dm), mem_norm[0], w_mem_kv[0].astype(BF16), F32)
    mk_p = kv[:, :xw].reshape(bp, n_mem * X_HEADS, HEAD_DIM)
    mv_p = kv[:, xw:].reshape(bp, n_mem * X_HEADS, HEAD_DIM)

    def group(x, pos, mem_k, mem_v, s_ret, h_re, h_im, nb, tl):
        bsz, length, _ = x.shape
        z, u = norm_matmul(x.reshape(bsz * length, dm), norm_mix[0], w_in_b, BF16,
                           f32_cols=(4 * qk, 4 * qk + sw),
                           acts=((3 * qk, 4 * qk, "silu"), (4 * qk + sw + xw, w_in_b.shape[1], "sigmoid")))
        y, hf_re, hf_im = s5_apply(u, bsz, h_re, h_im, tables)
        h, s_new = mixer(x, z, y, pos, mem_k, mem_v, s_ret, *mix_w, nb=nb, tl=tl)
        return h, s_new, hf_re, hf_im

    zero_ret = jnp.zeros((bp, RET_HEADS, HEAD_DIM, HEAD_DIM), F32)
    zero_ssm = jnp.zeros((bp, g, SSM_STATE), F32)
    h_p, ret_p, sre_p, sim_p = group(x_prompt, jnp.arange(lp, dtype=jnp.int32), mk_p, mv_p,
                                     zero_ret, zero_ssm, zero_ssm, 1, ROW_TILE)
    h_s, ret_s, sre_s, sim_s = group(x_sample, PAST_LEN + jnp.arange(ls, dtype=jnp.int32),
                                     cache_mem_k[0].reshape(bs, n_mem * X_HEADS, HEAD_DIM),
                                     cache_mem_v[0].reshape(bs, n_mem * X_HEADS, HEAD_DIM),
                                     state_ret[0], state_ssm_re[0], state_ssm_im[0], ROW_TILE // ls, ls)

    y_p, y_s = moe_and_final_norm(h_p, h_s, norm_ffn[0], w_router[0], b_router[0],
                                  w_gate_up[0], b_gate_up[0], w_down[0], b_down[0], final_norm)
    return (y_p.reshape(bp, lp, dm), y_s.reshape(bs, ls, dm), ret_p[None], sre_p[None], sim_p[None],
            mk_p.reshape(1, bp, n_mem, X_HEADS, HEAD_DIM), mv_p.reshape(1, bp, n_mem, X_HEADS, HEAD_DIM),
            ret_s[None], sre_s[None], sim_s[None])
```

```python
import functools
import math

import jax
import jax.numpy as jnp
import numpy as np
from jax import lax
from jax.experimental import pallas as pl
from jax.experimental.pallas import tpu as pltpu

F32 = jnp.float32
BF16 = jnp.bfloat16

EPS = 1e-6
CHUNK = 64
PAST_LEN = 2048
ROPE_BASE = 10000.0
RET_HEADS = 4
X_HEADS = 4
HEAD_DIM = 128
SSM_GROUP = 16
SSM_STATE = 64
TOP_K = 4
SWIGLU_ALPHA = 1.702
SWIGLU_LIMIT = 7.0

VMEM_LIMIT = 52 * 1024 * 1024
S5_CHUNK = 8
S5_LANES = 128
ROW_TILE = 256
FFN_BLOCK = 256
ROW_ALIGN = 8
MIX_COLS = 512
NT_DIMS = (((1,), (1,)), ((), ()))
TN_DIMS = (((0,), (0,)), ((), ()))


def _params(n_axes=1):
    return pltpu.CompilerParams(dimension_semantics=("arbitrary",) * n_axes,
                                vmem_limit_bytes=VMEM_LIMIT)


def _resident(shape):
    nd = len(shape)
    return pl.BlockSpec(shape, lambda *_: (0,) * nd, pipeline_mode=pl.Buffered(1))


def _rms(x, w):
    return x * lax.rsqrt(jnp.mean(x * x, axis=-1, keepdims=True) + EPS) * w


def _sigmoid(x):
    return 0.5 * jnp.tanh(0.5 * x) + 0.5


_ACTIVATIONS = {"sigmoid": _sigmoid, "silu": lambda v: v * _sigmoid(v)}


def _norm_matmul_kernel(x_ref, nw_ref, w_ref, o_ref, *f32_refs, n_chunk, f32_cols, acts):
    xb = _rms(x_ref[...], nw_ref[...]).astype(BF16)
    for n0 in range(0, o_ref.shape[1], n_chunk):
        r = jnp.dot(xb, w_ref[:, n0:n0 + n_chunk], preferred_element_type=F32)
        if f32_cols is not None and n0 <= f32_cols[0] and f32_cols[1] <= n0 + n_chunk:
            f32_refs[0][...] = r[:, f32_cols[0] - n0:f32_cols[1] - n0]
        cuts = sorted({n0, n0 + n_chunk} | {c for lo, hi, _ in acts for c in (lo, hi) if n0 < c < n0 + n_chunk})
        for a, b in zip(cuts[:-1], cuts[1:]):
            piece = r[:, a - n0:b - n0]
            for lo, hi, kind in acts:
                if lo <= a and b <= hi:
                    piece = _ACTIVATIONS[kind](piece)
            o_ref[:, a:b] = piece.astype(o_ref.dtype)


def norm_matmul(x, nw, w, out_dtype, f32_cols=None, acts=()):
    t, d = x.shape
    n = w.shape[1]
    n_chunk = min(n, 1024)
    out_specs = [pl.BlockSpec((ROW_TILE, n), lambda i: (i, 0))]
    out_shape = [jax.ShapeDtypeStruct((t, n), out_dtype)]
    if f32_cols is not None:
        lo, hi = f32_cols
        assert lo // n_chunk == (hi - 1) // n_chunk
        out_specs.append(pl.BlockSpec((ROW_TILE, hi - lo), lambda i: (i, 0)))
        out_shape.append(jax.ShapeDtypeStruct((t, hi - lo), F32))
    out = pl.pallas_call(
        functools.partial(_norm_matmul_kernel, n_chunk=n_chunk, f32_cols=f32_cols, acts=tuple(acts)),
        grid=(t // ROW_TILE,),
        in_specs=[pl.BlockSpec((ROW_TILE, d), lambda i: (i, 0)), _resident((1, d)), _resident((d, n))],
        out_specs=out_specs, out_shape=out_shape,
        compiler_params=_params(), name="norm_matmul",
    )(x, nw.reshape(1, d), w)
    return out if f32_cols is not None else out[0]


def s5_tables(lam_re, lam_im, log_dt, b_re, b_im, c_re, c_im, d_skip):
    g, n, p = b_re.shape
    s = S5_CHUNK
    gl = S5_LANES // p
    j = g // gl
    hi = lax.Precision.HIGHEST
    dt = jnp.exp(log_dt)[:, None]
    a_re = jnp.exp(lam_re * dt) * jnp.cos(lam_im * dt)
    a_im = jnp.exp(lam_re * dt) * jnp.sin(lam_im * dt)
    den = lam_re * lam_re + lam_im * lam_im
    nr, ni = a_re - 1.0, a_im
    co_re = (nr * lam_re + ni * lam_im) / den
    co_im = (ni * lam_re - nr * lam_im) / den
    bb_re = co_re[..., None] * b_re - co_im[..., None] * b_im
    bb_im = co_re[..., None] * b_im + co_im[..., None] * b_re
    tau = jnp.arange(s + 1, dtype=F32)[:, None, None]
    pw_mag = jnp.exp(lam_re * dt * tau)
    pw_re = pw_mag * jnp.cos(lam_im * dt * tau)
    pw_im = pw_mag * jnp.sin(lam_im * dt * tau)
    ca_re = c_re[None] * pw_re[:, :, None, :] - c_im[None] * pw_im[:, :, None, :]
    ca_im = c_re[None] * pw_im[:, :, None, :] + c_im[None] * pw_re[:, :, None, :]
    bq_re, bq_im = bb_re.transpose(0, 2, 1), bb_im.transpose(0, 2, 1)
    kq = jnp.sum(ca_re[:s, :, :, None, :] * bq_re[None, :, None, :, :]
                 - ca_im[:s, :, :, None, :] * bq_im[None, :, None, :, :], axis=-1).transpose(1, 0, 3, 2)
    ts = np.arange(s)
    lag_onehot = (ts[None, None, :] - ts[None, :, None] == ts[:, None, None]).astype(np.float32)
    rev = s - 1 - ts
    w_re = pw_re[rev][:, :, :, None] * bb_re[None] - pw_im[rev][:, :, :, None] * bb_im[None]
    w_im = pw_re[rev][:, :, :, None] * bb_im[None] + pw_im[rev][:, :, :, None] * bb_re[None]
    m_c = (jnp.einsum('gxqp,xst->gsqtp', kq, lag_onehot, precision=hi)
           .reshape(j, gl, s, p, s * p).transpose(0, 2, 1, 3, 4).reshape(j, s * gl * p, s * p))
    w_c = (jnp.stack([w_re, w_im]).reshape(2, s, j, gl, n, p).transpose(2, 1, 3, 5, 0, 4)
           .reshape(j, s * gl * p, 2 * n))
    v_c = (jnp.stack([ca_re[1:], -ca_im[1:]]).reshape(2, s, j, gl, p, n).transpose(2, 0, 3, 5, 1, 4)
           .reshape(j, 2 * gl * n, s * p))
    fl = s * gl * p
    c_io = np.arange(fl)
    c_st = np.arange(2 * gl * n)
    k_io = np.arange(s * p)
    k_st = np.arange(2 * n)
    spread_io = ((k_io[:, None] // p == c_io[None, :] // (gl * p)) & (k_io[:, None] % p == c_io[None, :] % p))
    spread_st = ((k_st[:, None] // n == c_st[None, :] // (gl * n)) & (k_st[:, None] % n == c_st[None, :] % n))
    grp_io = (c_io // p) % gl
    grp_st = (c_st // n) % gl

    def expand(compact, spread, row_grp, col_grp):
        full = jnp.einsum('jrk,kc->jrc', compact.astype(BF16), jnp.asarray(spread, BF16),
                          preferred_element_type=F32)
        return jnp.where(jnp.asarray(row_grp[:, None] == col_grp[None, :]), full, 0.0).astype(BF16)

    m = expand(m_c, spread_io, grp_io, grp_io)
    w = expand(w_c, spread_st, grp_io, grp_st)
    v = expand(v_c, spread_io, grp_st, grp_io)
    a_s_re = pw_re[s].reshape(1, g * n)
    a_s_im = pw_im[s].reshape(1, g * n)
    dtab = jnp.broadcast_to(d_skip.reshape(j, 1, 1, gl * p), (j, 1, s, gl * p)).reshape(j, 1, s * gl * p)
    return jnp.concatenate([m, v], axis=1), w, a_s_re, a_s_im, dtab


def _s5_flat(u_ref):
    return jnp.concatenate([u_ref[:, t, :] for t in range(u_ref.shape[1])], axis=1)


def _s5a_kernel(u_ref, w_ref, ire_ref, iim_ref):
    r = jnp.dot(_s5_flat(u_ref).astype(BF16), w_ref[0], preferred_element_type=F32)
    half = r.shape[1] // 2
    ire_ref[...] = r[:, :half]
    iim_ref[...] = r[:, half:]


def _s5scan_kernel(ire_ref, iim_ref, ar_ref, ai_ref, h0r_ref, h0i_ref,
                   hpr_ref, hpi_ref, hfr_ref, hfi_ref):
    nb, nc, _ = ire_ref.shape
    ar, ai = ar_ref[...], ai_ref[...]

    def body(c, carry):
        out = []
        for b in range(nb):
            hr, hi = carry[2 * b], carry[2 * b + 1]
            hpr_ref[b, pl.ds(c, 1), :] = hr
            hpi_ref[b, pl.ds(c, 1), :] = hi
            out.append(ar * hr - ai * hi + ire_ref[b, pl.ds(c, 1), :])
            out.append(ar * hi + ai * hr + iim_ref[b, pl.ds(c, 1), :])
        return tuple(out)

    init = []
    for b in range(nb):
        init += [h0r_ref[b], h0i_ref[b]]
    fin = lax.fori_loop(0, nc, body, tuple(init))
    for b in range(nb):
        hfr_ref[b] = fin[2 * b]
        hfi_ref[b] = fin[2 * b + 1]


def _s5b_kernel(u_ref, hpr_ref, hpi_ref, mv_ref, d_ref, y_ref):
    uf = _s5_flat(u_ref)
    lhs = jnp.concatenate([uf.astype(BF16), hpr_ref[...].astype(BF16), hpi_ref[...].astype(BF16)], axis=1)
    y = d_ref[0] * uf + jnp.dot(lhs, mv_ref[0], preferred_element_type=F32)
    lanes = y_ref.shape[2]
    for t in range(y_ref.shape[1]):
        y_ref[:, t, :] = y[:, t * lanes:(t + 1) * lanes]


def s5_apply(u, bsz, h0_re, h0_im, tables):
    mv, w, a_re, a_im, dtab = tables
    tokens, width = u.shape
    nj = w.shape[0]
    s = S5_CHUNK
    rows = tokens // s
    nc = rows // bsz
    lanes = a_re.shape[1]
    half = w.shape[2] // 2
    fl = w.shape[1]
    rt = min(rows, 512)
    u3 = u.reshape(rows, s, width)
    u_spec = pl.BlockSpec((rt, s, S5_LANES), lambda j, r: (r, 0, j))
    st_spec = pl.BlockSpec((rt, half), lambda j, r: (r, j))
    tab_spec = pl.BlockSpec((1, fl, fl), lambda j, r: (j, 0, 0))
    inj_re, inj_im = pl.pallas_call(
        _s5a_kernel, grid=(nj, rows // rt),
        in_specs=[u_spec, tab_spec],
        out_specs=[st_spec, st_spec],
        out_shape=[jax.ShapeDtypeStruct((rows, lanes), F32)] * 2,
        compiler_params=_params(2), name="s5_chunk_in",
    )(u3, w)

    sb, lw = 4, 512
    seq_spec = pl.BlockSpec((sb, nc, lw), lambda b, l: (b, 0, l))
    vec_spec = pl.BlockSpec((sb, 1, lw), lambda b, l: (b, 0, l))
    atab_spec = pl.BlockSpec((1, lw), lambda b, l: (0, l))
    hp_re, hp_im, hf_re, hf_im = pl.pallas_call(
        _s5scan_kernel, grid=(bsz // sb, lanes // lw),
        in_specs=[seq_spec, seq_spec, atab_spec, atab_spec, vec_spec, vec_spec],
        out_specs=[seq_spec, seq_spec, vec_spec, vec_spec],
        out_shape=[jax.ShapeDtypeStruct((bsz, nc, lanes), F32)] * 2
        + [jax.ShapeDtypeStruct((bsz, 1, lanes), F32)] * 2,
        compiler_params=_params(2), name="s5_scan",
    )(inj_re.reshape(bsz, nc, lanes), inj_im.reshape(bsz, nc, lanes), a_re, a_im,
      h0_re.reshape(bsz, 1, lanes), h0_im.reshape(bsz, 1, lanes))

    y3 = pl.pallas_call(
        _s5b_kernel, grid=(nj, rows // rt),
        in_specs=[u_spec, st_spec, st_spec, pl.BlockSpec((1, 2 * fl, fl), lambda j, r: (j, 0, 0)),
                  pl.BlockSpec((1, 1, fl), lambda j, r: (j, 0, 0))],
        out_specs=u_spec,
        out_shape=jax.ShapeDtypeStruct((rows, s, width), F32),
        compiler_params=_params(2), name="s5_chunk_out",
    )(u3, hp_re.reshape(rows, lanes), hp_im.reshape(rows, lanes), mv, dtab)
    g = lanes // SSM_STATE
    return y3.reshape(tokens, width), hf_re.reshape(bsz, g, SSM_STATE), hf_im.reshape(bsz, g, SSM_STATE)


def _retention_gammas():
    return 1.0 - np.exp2(-5.0 - np.arange(RET_HEADS, dtype=np.float64))


def retention_tables(tile, chunk):
    gam = _retention_gammas()[:, None, None]
    i = np.arange(tile)[:, None]
    j = np.arange(tile)[None, :]
    same = (i // chunk) == (j // chunk)
    earlier = (j // chunk) < (i // chunk)
    dist = np.where(same, np.abs(i - j), np.where(earlier, i - j, 0))
    dmask = np.where(same | earlier, gam ** dist[None], 0.0)
    qw = np.broadcast_to((gam[:, :, 0] ** (np.arange(tile) + 1.0))[:, :, None], (RET_HEADS, tile, HEAD_DIM))
    kw = np.broadcast_to((gam[:, :, 0] ** (tile - 1.0 - np.arange(tile)))[:, :, None], (RET_HEADS, tile, HEAD_DIM))
    return (jnp.asarray(dmask, F32), jnp.asarray(qw, F32), jnp.asarray(kw, F32),
            tuple(float(x) for x in _retention_gammas() ** tile))


def rope_tables(pos):
    half = HEAD_DIM // 2
    inv = jnp.exp(-math.log(ROPE_BASE) * 2.0 * jnp.arange(half, dtype=F32) / HEAD_DIM)
    ang = pos.astype(F32)[:, None] * inv[None, :]
    cos, sin = jnp.cos(ang), jnp.sin(ang)
    cosf = jnp.concatenate([cos, cos], axis=1)
    sinf = jnp.concatenate([-sin, sin], axis=1)
    return cosf, sinf


def _mixer_kernel(x_ref, zq_ref, xq_ref, gl_ref, y_ref, cq_ref, sq_ref, ck_ref, sk_ref,
                  dm_ref, qw_ref, kw_ref, mk_ref, mv_ref, s0_ref, gn_ref,
                  wro_ref, wglu_ref, wso_ref, wxo_ref, wout_ref,
                  h_ref, sout_ref, s_scr, o_scr, xo_scr, glu_scr, mg_scr, *, nb, tl, tile_decay):
    hd = HEAD_DIM
    qk = RET_HEADS * hd

    @pl.when(pl.program_id(1) == 0)
    def _():
        s_scr[...] = s0_ref[...]

    cq, sq, ck, sk = cq_ref[...], sq_ref[...], ck_ref[...], sk_ref[...]
    for n in range(nb):
        rows = slice(n * tl, (n + 1) * tl)
        for h in range(RET_HEADS):
            c0 = h * hd
            q = zq_ref[rows, c0:c0 + hd].astype(F32)
            k = zq_ref[rows, qk + c0:qk + c0 + hd].astype(F32)
            v = zq_ref[rows, 2 * qk + c0:2 * qk + c0 + hd]
            g = zq_ref[rows, 3 * qk + c0:3 * qk + c0 + hd].astype(F32)
            qr = q * cq + pltpu.roll(q, hd // 2, 1) * sq
            kr = k * ck + pltpu.roll(k, hd // 2, 1) * sk
            sc = lax.dot_general(qr.astype(BF16), kr.astype(BF16), NT_DIMS,
                                 preferred_element_type=F32) * dm_ref[h]
            o = jnp.dot(sc.astype(BF16), v, preferred_element_type=F32)
            s_old = s_scr[n, h]
            o += jnp.dot((qr * qw_ref[h]).astype(BF16), s_old.astype(BF16), preferred_element_type=F32)
            kv = lax.dot_general((kr * kw_ref[h]).astype(BF16), v, TN_DIMS, preferred_element_type=F32)
            s_scr[n, h] = tile_decay[h] * s_old + kv
            d = o - jnp.mean(o, axis=-1, keepdims=True)
            on = d * lax.rsqrt(jnp.mean(d * d, axis=-1, keepdims=True) + EPS) * gn_ref[:, c0:c0 + hd]
            o_scr[rows, c0:c0 + hd] = (on * g).astype(BF16)
            mem_rows = pl.ds(h, mk_ref.shape[1] // X_HEADS, stride=X_HEADS)
            mkh = mk_ref[n, mem_rows, :].astype(BF16)
            mvh = mv_ref[n, mem_rows, :].astype(BF16)
            s = lax.dot_general(xq_ref[rows, c0:c0 + hd], mkh, NT_DIMS,
                                preferred_element_type=F32) * (hd ** -0.5)
            e = jnp.exp(s - jnp.max(s, axis=-1, keepdims=True))
            p = e / jnp.sum(e, axis=-1, keepdims=True)
            xo_scr[rows, c0:c0 + hd] = jnp.dot(p.astype(BF16), mvh, preferred_element_type=F32).astype(BF16)

    cw = MIX_COLS
    dm = h_ref.shape[1]
    half = wglu_ref.shape[1] // 2
    yb = jax.nn.gelu(y_ref[...]).astype(BF16)
    for c0 in range(0, half, cw):
        ga = jnp.dot(yb, wglu_ref[:, c0:c0 + cw], preferred_element_type=F32)
        gb = jnp.dot(yb, wglu_ref[:, half + c0:half + c0 + cw], preferred_element_type=F32)
        glu_scr[:, c0:c0 + cw] = (ga * _sigmoid(gb)).astype(BF16)
    for c0 in range(0, dm, cw):
        cols = slice(c0, c0 + cw)
        ret = jnp.dot(o_scr[...], wro_ref[:, cols], preferred_element_type=F32)
        ssm = jnp.dot(glu_scr[...], wso_ref[:, cols], preferred_element_type=F32)
        xb = jnp.dot(xo_scr[...], wxo_ref[:, cols], preferred_element_type=F32)
        merged = (gl_ref[:, c0:c0 + cw].astype(F32) * ret
                  + gl_ref[:, dm + c0:dm + c0 + cw].astype(F32) * ssm
                  + gl_ref[:, 2 * dm + c0:2 * dm + c0 + cw].astype(F32) * xb)
        mg_scr[:, cols] = merged.astype(BF16)
    for c0 in range(0, dm, cw):
        cols = slice(c0, c0 + cw)
        h_ref[:, cols] = x_ref[:, cols] + jnp.dot(mg_scr[...], wout_ref[:, cols], preferred_element_type=F32)
    sout_ref[...] = s_scr[...]


def mixer(x, z, y_ssm, pos, mem_k, mem_v, s0, ret_gn, w_ret_o, w_ssm_glu, w_ssm_o, w_x_o, w_out, *, nb, tl):
    bsz, length, dm = x.shape
    chunk = min(CHUNK, length)
    nl = length // tl
    rows = nb * tl
    qk = RET_HEADS * HEAD_DIM
    sw = y_ssm.shape[1]
    xw = X_HEADS * HEAD_DIM
    gate_col = (4 * qk + sw + xw)
    assert gate_col % (3 * dm) == 0 and (4 * qk + sw) % xw == 0
    dmask, qw, kw, tile_decay = retention_tables(tl, chunk)
    cosf, sinf = rope_tables(pos)
    scale = HEAD_DIM ** -0.5
    row_map = lambda b, l: (b * nl + l, 0)
    tab_map = lambda b, l: (l, 0)
    st_spec = pl.BlockSpec((nb, RET_HEADS, HEAD_DIM, HEAD_DIM), lambda b, l: (b, 0, 0, 0))
    mem_spec = pl.BlockSpec((nb,) + mem_k.shape[1:], lambda b, l: (b,) + (0,) * (mem_k.ndim - 1))
    h, s_out = pl.pallas_call(
        functools.partial(_mixer_kernel, nb=nb, tl=tl, tile_decay=tile_decay),
        grid=(bsz // nb, nl),
        in_specs=[pl.BlockSpec((rows, dm), row_map),
                  pl.BlockSpec((rows, 4 * qk), row_map),
                  pl.BlockSpec((rows, xw), lambda b, l: (b * nl + l, (4 * qk + sw) // xw)),
                  pl.BlockSpec((rows, 3 * dm), lambda b, l: (b * nl + l, gate_col // (3 * dm))),
                  pl.BlockSpec((rows, sw), row_map),
                  pl.BlockSpec((tl, HEAD_DIM), tab_map), pl.BlockSpec((tl, HEAD_DIM), tab_map),
                  pl.BlockSpec((tl, HEAD_DIM), tab_map), pl.BlockSpec((tl, HEAD_DIM), tab_map),
                  _resident(dmask.shape), _resident(qw.shape), _resident(kw.shape),
                  mem_spec, mem_spec, st_spec, _resident((1, qk)),
                  _resident(w_ret_o.shape), _resident(w_ssm_glu.shape), _resident(w_ssm_o.shape),
                  _resident(w_x_o.shape), _resident(w_out.shape)],
        out_specs=[pl.BlockSpec((rows, dm), row_map), st_spec],
        out_shape=[jax.ShapeDtypeStruct((bsz * length, dm), F32),
                   jax.ShapeDtypeStruct(s0.shape, F32)],
        scratch_shapes=[pltpu.VMEM((nb, RET_HEADS, HEAD_DIM, HEAD_DIM), F32),
                        pltpu.VMEM((rows, qk), BF16), pltpu.VMEM((rows, xw), BF16),
                        pltpu.VMEM((rows, w_ssm_o.shape[0]), BF16), pltpu.VMEM((rows, dm), BF16)],
        compiler_params=_params(2), name="mixer",
    )(x.reshape(bsz * length, dm), z, z, z, y_ssm,
      cosf * scale, sinf * scale, cosf, sinf, dmask, qw, kw,
      mem_k, mem_v, s0, ret_gn.reshape(1, qk), w_ret_o, w_ssm_glu, w_ssm_o, w_x_o, w_out)
    return h, s_out


def _router_kernel(hp_ref, hs_ref, nw_ref, wrt_ref, br_ref, tri_ref, low_ref,
                   xn_ref, gw_ref, crow_ref, cnt_ref, *, prompt_tiles):
    h = jnp.where(pl.program_id(0) < prompt_tiles, hp_ref[...], hs_ref[...])
    xn = _rms(h, nw_ref[...]).astype(BF16)
    xn_ref[...] = xn
    logits = lax.dot_general(wrt_ref[...], xn, NT_DIMS, preferred_element_type=F32) + br_ref[...]
    ne = logits.shape[0]
    iota = lax.broadcasted_iota(jnp.int32, logits.shape, 0)
    rest = logits
    sel = jnp.zeros(logits.shape, jnp.bool_)
    vals, idxs = [], []
    for _ in range(TOP_K):
        m = jnp.max(rest, axis=0, keepdims=True)
        ix = jnp.min(jnp.where(rest == m, iota, ne), axis=0, keepdims=True)
        hit = iota == ix
        vals.append(m)
        idxs.append(ix)
        sel = jnp.logical_or(sel, hit)
        rest = jnp.where(hit, -jnp.inf, rest)
    es = [jnp.exp(v - vals[0]) for v in vals]
    tot = es[0] + es[1] + es[2] + es[3]
    before = jnp.dot(sel.astype(BF16), tri_ref[...], preferred_element_type=F32)
    cnt = jnp.sum(sel.astype(F32), axis=1, keepdims=True)
    seg = jnp.floor((cnt + (ROW_ALIGN - 1.0)) * (1.0 / ROW_ALIGN)) * ROW_ALIGN
    start = jnp.dot(low_ref[...], jnp.broadcast_to(seg, before.shape), precision=lax.Precision.HIGHEST,
                    preferred_element_type=F32)
    place = start + before
    for k in range(TOP_K):
        gw_ref[k:k + 1, :] = es[k] / tot
        crow_ref[k:k + 1, :] = jnp.sum(jnp.where(iota == idxs[k], place, 0.0), axis=0,
                                       keepdims=True).astype(jnp.int32)
    cnt_ref[...] = jnp.broadcast_to(cnt, cnt_ref.shape)


def router(h_p, h_s, norm_ffn, w_router, b_router):
    dm = h_p.shape[1]
    ne = w_router.shape[1]
    tt = ROW_TILE
    npt, nst = h_p.shape[0] // tt, h_s.shape[0] // tt
    t = (npt + nst) * tt
    tri = jnp.asarray(np.triu(np.ones((tt, tt), np.float32), k=1), BF16)
    low = jnp.asarray(np.tril(np.ones((ne, ne), np.float32), k=-1))
    tok_spec = pl.BlockSpec((TOP_K, tt), lambda i: (0, i))
    return pl.pallas_call(
        functools.partial(_router_kernel, prompt_tiles=npt), grid=(npt + nst,),
        in_specs=[pl.BlockSpec((tt, dm), lambda i: (jnp.minimum(i, npt - 1), 0)),
                  pl.BlockSpec((tt, dm), lambda i: (jnp.maximum(i - npt, 0), 0)),
                  _resident((1, dm)), _resident((ne, dm)), _resident((ne, 1)), _resident((tt, tt)),
                  _resident((ne, ne))],
        out_specs=[pl.BlockSpec((tt, dm), lambda i: (i, 0)), tok_spec, tok_spec,
                   pl.BlockSpec((ne, 128), lambda i: (i, 0))],
        out_shape=[jax.ShapeDtypeStruct((t, dm), BF16),
                   jax.ShapeDtypeStruct((TOP_K, t), F32), jax.ShapeDtypeStruct((TOP_K, t), jnp.int32),
                   jax.ShapeDtypeStruct(((npt + nst) * ne, 128), F32)],
        compiler_params=_params(), name="router",
    )(h_p, h_s, norm_ffn.reshape(1, dm), w_router.T.astype(BF16), b_router.reshape(ne, 1), tri, low)


def _pack_bf16_pairs(x, exact=False):
    n = x.shape[1] // 2
    lo, hi = x[:, :n], x[:, n:]
    if not exact:
        lo, hi = lo.astype(BF16).astype(F32), hi.astype(BF16).astype(F32)
    lo = lax.bitcast_convert_type(lo, jnp.int32)
    hi = lax.bitcast_convert_type(hi, jnp.int32)
    return lax.shift_right_logical(lo, 16) | (hi & -65536)


def _unpack_bf16_pairs(u):
    lo = lax.bitcast_convert_type(lax.shift_left(u, 16), F32).astype(BF16)
    hi = lax.bitcast_convert_type(u & -65536, F32).astype(BF16)
    return lo, hi


def _split_count(n, fn):
    def quad(jq, carry):
        fn(4 * jq, 4)
        return carry

    lax.fori_loop(0, n // 4, quad, 0)

    @pl.when(n % 4 >= 2)
    def _():
        fn(n // 4 * 4, 2)

    @pl.when(n % 2 == 1)
    def _():
        fn(n // 2 * 2, 1)


def _chunk_loop(n, fn):
    def quad(jq, carry):
        for u in range(4):
            fn(4 * jq + u)
        return carry

    def single(j, carry):
        fn(j)
        return carry

    lax.fori_loop(0, n // 4, quad, 0)
    lax.fori_loop(n // 4 * 4, n, single, 0)


def _dispatch_kernel(dst_ref, tch_ref, zs_ref, zn_ref, used_ref,
                     xn_ref, crow_ref, xs_ref, cbuf, zbuf, sems, zsem, *, ne, bm):
    i = pl.program_id(0)
    nt = pl.num_programs(0)
    slot = lax.rem(i, 2)
    tt = xn_ref.shape[0]
    cr = cbuf.shape[1]
    ra = ROW_ALIGN
    n_blocks = xs_ref.shape[0] // bm

    def chunk_copy(sl, src_row, dst_row, rows):
        return pltpu.make_async_copy(cbuf.at[sl, pl.ds(src_row, rows)], xs_ref.at[pl.ds(dst_row, rows)],
                                     sems.at[sl])

    def wait_chunks(sl, n):
        _split_count(n, lambda j, m: chunk_copy(sl, 0, 0, ra * m).wait())

    def tail_copy(e, j):
        return pltpu.make_async_copy(zbuf.at[pl.ds(0, ra)],
                                     xs_ref.at[pl.ds(pl.multiple_of(zs_ref[e] + ra * j, ra), ra)], zsem)

    def block_copy(b):
        return pltpu.make_async_copy(zbuf, xs_ref.at[pl.ds(pl.multiple_of(b * bm, bm), bm)], zsem)

    def zero_fill(start):
        def per_expert(e, carry):
            def per_chunk(j, c2):
                (tail_copy(e, j).start() if start else tail_copy(e, j).wait())
                return c2
            lax.fori_loop(0, zn_ref[e], per_chunk, 0)
            return carry
        lax.fori_loop(0, ne, per_expert, 0)

        def per_block(b, carry):
            (block_copy(b).start() if start else block_copy(b).wait())
            return carry
        lax.fori_loop(used_ref[0], n_blocks, per_block, 0)

    @pl.when(i == 0)
    def _():
        zbuf[...] = jnp.zeros_like(zbuf)
        zero_fill(True)
        zero_fill(False)

    @pl.when(i >= 2)
    def _():
        wait_chunks(slot, tch_ref[jnp.maximum(i - 2, 0)])

    crow = crow_ref[...]
    rows = lax.broadcasted_iota(jnp.int32, (cr, tt), 0)
    hit = rows == crow[0:1, :]
    for k in range(1, TOP_K):
        hit = jnp.logical_or(hit, rows == crow[k:k + 1, :])
    packed = _pack_bf16_pairs(jnp.dot(jnp.where(hit, 1.0, 0.0).astype(BF16), xn_ref[...],
                                      preferred_element_type=F32), exact=True)
    cbuf[slot] = packed

    chunks_per_tile = cr // ra
    _chunk_loop(tch_ref[i], lambda c: chunk_copy(
        slot, pl.multiple_of(c * ra, ra), pl.multiple_of(dst_ref[i * chunks_per_tile + c], ra), ra).start())

    @pl.when(i == nt - 1)
    def _():
        wait_chunks(slot, tch_ref[i])
        wait_chunks(1 - slot, jnp.where(nt >= 2, tch_ref[jnp.maximum(i - 1, 0)], 0))


def dispatch(xn, crow, tables, n_rows, ne):
    t, dm = xn.shape
    tt = ROW_TILE
    bm = FFN_BLOCK
    cr = _compact_rows(ne)
    grid_spec = pltpu.PrefetchScalarGridSpec(
        num_scalar_prefetch=5, grid=(t // tt,),
        in_specs=[pl.BlockSpec((tt, dm), lambda i, *_: (i, 0)),
                  pl.BlockSpec((TOP_K, tt), lambda i, *_: (0, i))],
        out_specs=pl.BlockSpec(memory_space=pl.ANY),
        scratch_shapes=[pltpu.VMEM((2, cr, dm // 2), jnp.int32), pltpu.VMEM((bm, dm // 2), jnp.int32),
                        pltpu.SemaphoreType.DMA((2,)), pltpu.SemaphoreType.DMA(())])
    return pl.pallas_call(
        functools.partial(_dispatch_kernel, ne=ne, bm=bm), grid_spec=grid_spec,
        out_shape=jax.ShapeDtypeStruct((n_rows, dm // 2), jnp.int32),
        compiler_params=_params(), name="dispatch",
    )(*tables, xn, crow)


FFN_W_PIECES = (4, 2)


def _ffn_kernel(first_ref, nblk_ref, used_ref, next_ref, wslot_ref, lead_ref,
                xs_hbm, wgu_hbm, bgu_ref, wd_hbm, bd_ref, y_hbm,
                wgu_f32, wd_f32, wgu_bf, wd_bf, xbuf, ybuf, xsem, ysem, wsem, zsem, *, bm):
    e = pl.program_id(0)
    n = nblk_ref[e]
    b0 = first_ref[e]
    n_blocks = y_hbm.shape[0] // bm
    n_pieces = sum(FFN_W_PIECES)

    def block_rows(b):
        return pl.ds(pl.multiple_of(b * bm, bm), bm)

    used = used_ref[0]
    x_slots = xbuf.shape[0]

    def x_copy(g):
        sl = lax.rem(g, x_slots)
        return pltpu.make_async_copy(xs_hbm.at[block_rows(g)], xbuf.at[sl], xsem.at[sl])

    def y_copy(g):
        sl = lax.rem(g, 2)
        return pltpu.make_async_copy(ybuf.at[sl], y_hbm.at[block_rows(g)], ysem.at[sl])

    def w_piece(ex, sl, p):
        src, dst, q, parts = ((wgu_hbm, wgu_f32, p, FFN_W_PIECES[0]) if p < FFN_W_PIECES[0]
                              else (wd_hbm, wd_f32, p - FFN_W_PIECES[0], FFN_W_PIECES[1]))
        rows = src.shape[1] // parts
        return pltpu.make_async_copy(src.at[ex, pl.ds(q * rows, rows)], dst.at[sl, pl.ds(q * rows, rows)],
                                     wsem.at[sl, p])

    @pl.when(n > 0)
    def _():
        ws = wslot_ref[e]
        nxt = next_ref[e]

        @pl.when(lead_ref[0] == e)
        def _():
            for g in range(x_slots - 1):
                @pl.when(g < used)
                def _():
                    x_copy(g).start()
            for p in range(n_pieces):
                w_piece(e, ws, p).start()

        for p in range(n_pieces):
            w_piece(e, ws, p).wait()
        wgu_bf[...] = wgu_f32[ws].astype(BF16)
        wd_bf[...] = wd_f32[ws].astype(BF16)

        def block(j, carry):
            g = b0 + j
            x_copy(g).wait()

            @pl.when(g + x_slots - 1 < used)
            def _():
                x_copy(g + x_slots - 1).start()

            for p in range(n_pieces):
                @pl.when(jnp.logical_and(j == p, nxt >= 0))
                def _():
                    w_piece(nxt, 1 - ws, p).start()

            @pl.when(g >= 2)
            def _():
                y_copy(g - 2).wait()

            x = jnp.concatenate(_unpack_bf16_pairs(xbuf[lax.rem(g, x_slots)]), axis=1)
            hgu = jnp.dot(x, wgu_bf[...], preferred_element_type=F32) + bgu_ref[0]
            ff = hgu.shape[1] // 2
            gate = jnp.minimum(hgu[:, :ff], SWIGLU_LIMIT)
            up = jnp.clip(hgu[:, ff:], -SWIGLU_LIMIT, SWIGLU_LIMIT)
            act = (up + 1.0) * gate * _sigmoid(SWIGLU_ALPHA * gate)
            ybuf[lax.rem(g, 2)] = _pack_bf16_pairs(
                jnp.dot(act.astype(BF16), wd_bf[...], preferred_element_type=F32) + bd_ref[0])
            y_copy(g).start()
            return carry

        lax.fori_loop(0, n, block, 0)

        for p in range(n_pieces):
            @pl.when(jnp.logical_and(p >= n, nxt >= 0))
            def _():
                w_piece(nxt, 1 - ws, p).start()

    @pl.when(e == pl.num_programs(0) - 1)
    def _():
        for back in (2, 1):
            @pl.when(used >= back)
            def _():
                y_copy(used - back).wait()
        ybuf[0] = jnp.zeros(ybuf.shape[1:], ybuf.dtype)

        def zero_copy(b):
            return pltpu.make_async_copy(ybuf.at[0], y_hbm.at[block_rows(b)], zsem)

        def start(b, carry):
            zero_copy(b).start()
            return carry

        def wait(b, carry):
            zero_copy(b).wait()
            return carry

        lax.fori_loop(used_ref[0], n_blocks, start, 0)
        lax.fori_loop(used_ref[0], n_blocks, wait, 0)


def expert_ffn(xs, expert_tables, w_gate_up, b_gate_up, w_down, b_down):
    n_rows = xs.shape[0]
    ne, dm, ff2 = w_gate_up.shape
    bm = FFN_BLOCK
    grid_spec = pltpu.PrefetchScalarGridSpec(
        num_scalar_prefetch=6, grid=(ne,),
        in_specs=[pl.BlockSpec(memory_space=pl.ANY),
                  pl.BlockSpec(memory_space=pl.ANY),
                  pl.BlockSpec((1, 1, ff2), lambda e, *_: (e, 0, 0)),
                  pl.BlockSpec(memory_space=pl.ANY),
                  pl.BlockSpec((1, 1, dm), lambda e, *_: (e, 0, 0))],
        out_specs=pl.BlockSpec(memory_space=pl.ANY),
        scratch_shapes=[pltpu.VMEM((2, dm, ff2), F32), pltpu.VMEM((2, ff2 // 2, dm), F32),
                        pltpu.VMEM((dm, ff2), BF16), pltpu.VMEM((ff2 // 2, dm), BF16),
                        pltpu.VMEM((3, bm, dm // 2), jnp.int32), pltpu.VMEM((2, bm, dm // 2), jnp.int32),
                        pltpu.SemaphoreType.DMA((3,)), pltpu.SemaphoreType.DMA((2,)),
                        pltpu.SemaphoreType.DMA((2, sum(FFN_W_PIECES))), pltpu.SemaphoreType.DMA(())])
    return pl.pallas_call(
        functools.partial(_ffn_kernel, bm=bm), grid_spec=grid_spec,
        out_shape=jax.ShapeDtypeStruct((n_rows, dm // 2), jnp.int32),
        compiler_params=_params(), name="expert_ffn",
    )(*expert_tables, xs, w_gate_up, b_gate_up.reshape(ne, 1, ff2), w_down, b_down.reshape(ne, 1, dm))


def _combine_kernel(dst_ref, tch_ref,
                    crow_ref, gw_ref, hp_ref, hs_ref, fn_ref, yr_ref, yp_ref, ys_ref, ybuf, sems,
                    *, prompt_tiles):
    i = pl.program_id(0)
    nt = pl.num_programs(0)
    slot = lax.rem(i, 2)
    tt = hp_ref.shape[0]
    cr = ybuf.shape[1]
    ra = ROW_ALIGN

    def chunk_copy(sl, src_row, dst_row, rows):
        return pltpu.make_async_copy(yr_ref.at[pl.ds(src_row, rows)], ybuf.at[sl, pl.ds(dst_row, rows)],
                                     sems.at[sl])

    def fetch(tile, sl):
        chunks_per_tile = cr // ra
        _chunk_loop(tch_ref[tile], lambda c: chunk_copy(
            sl, pl.multiple_of(dst_ref[tile * chunks_per_tile + c], ra), pl.multiple_of(c * ra, ra), ra).start())

    @pl.when(i == 0)
    def _():
        ybuf[...] = jnp.zeros_like(ybuf)
        fetch(0, 0)

    @pl.when(i + 1 < nt)
    def _():
        fetch(i + 1, 1 - slot)

    _split_count(tch_ref[i], lambda j, m: chunk_copy(slot, 0, 0, ra * m).wait())

    y_lo, y_hi = _unpack_bf16_pairs(ybuf[slot])
    cols = lax.broadcasted_iota(jnp.int32, (tt, cr), 1)
    q = jnp.zeros((tt, cr), F32)
    for k in range(TOP_K):
        q = jnp.where(cols == crow_ref[:, k:k + 1], gw_ref[:, k:k + 1], q)
    qb = q.astype(BF16)
    moe = jnp.concatenate([jnp.dot(qb, y_lo, preferred_element_type=F32),
                           jnp.dot(qb, y_hi, preferred_element_type=F32)], axis=1)
    h = jnp.where(i < prompt_tiles, hp_ref[...], hs_ref[...])
    out = _rms(h + moe, fn_ref[...])

    @pl.when(i < prompt_tiles)
    def _():
        yp_ref[...] = out

    @pl.when(i >= prompt_tiles)
    def _():
        ys_ref[...] = out


def combine(crow_t, gw_t, h_p, h_s, final_norm, y_rows, tables, ne):
    dm = h_p.shape[1]
    tt = ROW_TILE
    npt, nst = h_p.shape[0] // tt, h_s.shape[0] // tt
    cr = _compact_rows(ne)
    p_map = lambda i, *_: (jnp.minimum(i, npt - 1), 0)
    s_map = lambda i, *_: (jnp.maximum(i - npt, 0), 0)
    grid_spec = pltpu.PrefetchScalarGridSpec(
        num_scalar_prefetch=2, grid=(npt + nst,),
        in_specs=[pl.BlockSpec((tt, TOP_K), lambda i, *_: (i, 0)),
                  pl.BlockSpec((tt, TOP_K), lambda i, *_: (i, 0)),
                  pl.BlockSpec((tt, dm), p_map), pl.BlockSpec((tt, dm), s_map),
                  pl.BlockSpec((1, dm), lambda i, *_: (0, 0)),
                  pl.BlockSpec(memory_space=pl.ANY)],
        out_specs=[pl.BlockSpec((tt, dm), p_map), pl.BlockSpec((tt, dm), s_map)],
        scratch_shapes=[pltpu.VMEM((2, cr, dm // 2), jnp.int32), pltpu.SemaphoreType.DMA((2,))])
    return pl.pallas_call(
        functools.partial(_combine_kernel, prompt_tiles=npt), grid_spec=grid_spec,
        out_shape=[jax.ShapeDtypeStruct(h_p.shape, F32), jax.ShapeDtypeStruct(h_s.shape, F32)],
        compiler_params=_params(), name="combine",
    )(*tables, crow_t, gw_t, h_p, h_s, final_norm.reshape(1, dm), y_rows)


def _compact_rows(ne):
    return -(-(TOP_K * ROW_TILE + ne * (ROW_ALIGN - 1)) // 128) * 128


def moe_and_final_norm(h_p, h_s, norm_ffn, w_router, b_router, w_gate_up, b_gate_up, w_down, b_down, final_norm):
    ne = w_router.shape[1]
    bm = FFN_BLOCK
    ra = ROW_ALIGN
    xn, gw, crow, cnt = router(h_p, h_s, norm_ffn, w_router, b_router)
    t = xn.shape[0]
    nt = t // ROW_TILE
    seg = -(-cnt[:, 0].astype(jnp.int32).reshape(nt, ne) // ra) * ra
    seg_before = jnp.cumsum(seg, axis=0) - seg
    rows_e = jnp.sum(seg, axis=0)
    padded = -(-rows_e // bm) * bm
    pad_ends = jnp.cumsum(padded)
    pad_starts = pad_ends - padded
    n_blocks = -(-(t * TOP_K + nt * ne * (ra - 1) + ne * (bm - 1)) // bm)
    n_used = pad_ends[-1] // bm
    experts = jnp.arange(ne, dtype=jnp.int32)
    chunks = seg // ra
    chunk_end = jnp.cumsum(chunks, axis=1)
    c_ids = jnp.arange(_compact_rows(ne) // ra, dtype=jnp.int32)
    owner = jnp.sum((chunk_end[:, None, :] <= c_ids[None, :, None]).astype(jnp.int32), axis=2)
    seg_shift = pad_starts[None, :] + seg_before - ra * (chunk_end - chunks)
    dst = ra * c_ids[None, :] + jnp.sum(
        jnp.where(owner[:, :, None] == experts[None, None, :], seg_shift[:, None, :], 0), axis=2)
    seg_tables = (dst.reshape(-1), chunk_end[:, -1])
    fill_tables = (pad_starts + rows_e, (padded - rows_e) // ra, n_used.reshape(1))
    to_i32 = lambda xs: tuple(x.astype(jnp.int32) for x in xs)
    xs = dispatch(xn, crow, to_i32(seg_tables + fill_tables), n_blocks * bm, ne)
    active = padded > 0
    later = jnp.where(active, experts, ne)
    next_active = jnp.concatenate([lax.cummin(later, reverse=True)[1:], jnp.full((1,), ne, jnp.int32)])
    expert_tables = (pad_starts // bm, padded // bm, n_used.reshape(1),
                     jnp.where(next_active < ne, next_active, -1),
                     (jnp.cumsum(active.astype(jnp.int32)) - active.astype(jnp.int32)) % 2,
                     jnp.min(later).reshape(1))
    y_rows = expert_ffn(xs, to_i32(expert_tables), w_gate_up, b_gate_up, w_down, b_down)
    return combine(crow.T, gw.T, h_p, h_s, final_norm, y_rows, to_i32(seg_tables), ne)


def kernel(x_prompt, x_sample, cache_mem_k, cache_mem_v, state_ret, state_ssm_re, state_ssm_im, mem_prompt, norm_mix, w_in, ret_gn, w_ret_o, ssm_lam_re, ssm_lam_im, ssm_log_dt, ssm_b_re, ssm_b_im, ssm_c_re, ssm_c_im, ssm_d, w_ssm_glu, w_ssm_o, mem_norm, w_mem_kv, w_x_o, w_out, norm_ffn, w_router, b_router, w_gate_up, b_gate_up, w_down, b_down, final_norm):
    assert norm_mix.shape[0] == 1, "single-layer step"
    bp, lp, dm = x_prompt.shape
    bs, ls, _ = x_sample.shape
    n_mem = mem_prompt.shape[1]
    xw = X_HEADS * HEAD_DIM
    qk = RET_HEADS * HEAD_DIM
    sw = ssm_d.shape[1]
    g = ssm_lam_re.shape[1]

    w_in_b = w_in[0].astype(BF16)
    tables = s5_tables(ssm_lam_re[0], ssm_lam_im[0], ssm_log_dt[0], ssm_b_re[0], ssm_b_im[0],
                       ssm_c_re[0], ssm_c_im[0], ssm_d[0])
    mix_w = (ret_gn[0], w_ret_o[0].astype(BF16), w_ssm_glu[0].astype(BF16), w_ssm_o[0].astype(BF16),
             w_x_o[0].astype(BF16), w_out[0].astype(BF16))

    kv = norm_matmul(mem_prompt.reshape(bp * n_mem, dm), mem_norm[0], w_mem_kv[0].astype(BF16), F32)
    mk_p = kv[:, :xw].reshape(bp, n_mem * X_HEADS, HEAD_DIM)
    mv_p = kv[:, xw:].reshape(bp, n_mem * X_HEADS, HEAD_DIM)

    def group(x, pos, mem_k, mem_v, s_ret, h_re, h_im, nb, tl):
        bsz, length, _ = x.shape
        z, u = norm_matmul(x.reshape(bsz * length, dm), norm_mix[0], w_in_b, BF16,
                           f32_cols=(4 * qk, 4 * qk + sw),
                           acts=((3 * qk, 4 * qk, "silu"), (4 * qk + sw + xw, w_in_b.shape[1], "sigmoid")))
        y, hf_re, hf_im = s5_apply(u, bsz, h_re, h_im, tables)
        h, s_new = mixer(x, z, y, pos, mem_k, mem_v, s_ret, *mix_w, nb=nb, tl=tl)
        return h, s_new, hf_re, hf_im

    zero_ret = jnp.zeros((bp, RET_HEADS, HEAD_DIM, HEAD_DIM), F32)
    zero_ssm = jnp.zeros((bp, g, SSM_STATE), F32)
    h_p, ret_p, sre_p, sim_p = group(x_prompt, jnp.arange(lp, dtype=jnp.int32), mk_p, mv_p,
                                     zero_ret, zero_ssm, zero_ssm, 1, ROW_TILE)
    h_s, ret_s, sre_s, sim_s = group(x_sample, PAST_LEN + jnp.arange(ls, dtype=jnp.int32),
                                     cache_mem_k[0].reshape(bs, n_mem * X_HEADS, HEAD_DIM),
                                     cache_mem_v[0].reshape(bs, n_mem * X_HEADS, HEAD_DIM),
                                     state_ret[0], state_ssm_re[0], state_ssm_im[0], ROW_TILE // ls, ls)

    y_p, y_s = moe_and_final_norm(h_p, h_s, norm_ffn[0], w_router[0], b_router[0],
                                  w_gate_up[0], b_gate_up[0], w_down[0], b_down[0], final_norm)
    return (y_p.reshape(bp, lp, dm), y_s.reshape(bs, ls, dm), ret_p[None], sre_p[None], sim_p[None],
            mk_p.reshape(1, bp, n_mem, X_HEADS, HEAD_DIM), mv_p.reshape(1, bp, n_mem, X_HEADS, HEAD_DIM),
            ret_s[None], sre_s[None], sim_s[None])
```

```python
import functools
import math

import jax
import jax.numpy as jnp
import numpy as np
from jax import lax
from jax.experimental import pallas as pl
from jax.experimental.pallas import tpu as pltpu

F32 = jnp.float32
BF16 = jnp.bfloat16

EPS = 1e-6
CHUNK = 64
PAST_LEN = 2048
ROPE_BASE = 10000.0
RET_HEADS = 4
X_HEADS = 4
HEAD_DIM = 128
SSM_GROUP = 16
SSM_STATE = 64
TOP_K = 4
SWIGLU_ALPHA = 1.702
SWIGLU_LIMIT = 7.0

VMEM_LIMIT = 52 * 1024 * 1024
S5_CHUNK = 8
S5_LANES = 128
ROW_TILE = 256
FFN_BLOCK = 256
ROW_ALIGN = 8
MIX_COLS = 512
NT_DIMS = (((1,), (1,)), ((), ()))
TN_DIMS = (((0,), (0,)), ((), ()))


def _params(n_axes=1):
    return pltpu.CompilerParams(dimension_semantics=("arbitrary",) * n_axes,
                                vmem_limit_bytes=VMEM_LIMIT)


def _resident(shape):
    nd = len(shape)
    return pl.BlockSpec(shape, lambda *_: (0,) * nd, pipeline_mode=pl.Buffered(1))


def _rms(x, w):
    return x * lax.rsqrt(jnp.mean(x * x, axis=-1, keepdims=True) + EPS) * w


def _sigmoid(x):
    return 0.5 * jnp.tanh(0.5 * x) + 0.5


_ACTIVATIONS = {"sigmoid": _sigmoid, "silu": lambda v: v * _sigmoid(v)}


def _norm_matmul_kernel(x_ref, nw_ref, w_ref, o_ref, *f32_refs, n_chunk, f32_cols, acts):
    xb = _rms(x_ref[...], nw_ref[...]).astype(BF16)
    for n0 in range(0, o_ref.shape[1], n_chunk):
        r = jnp.dot(xb, w_ref[:, n0:n0 + n_chunk], preferred_element_type=F32)
        if f32_cols is not None and n0 <= f32_cols[0] and f32_cols[1] <= n0 + n_chunk:
            f32_refs[0][...] = r[:, f32_cols[0] - n0:f32_cols[1] - n0]
        cuts = sorted({n0, n0 + n_chunk} | {c for lo, hi, _ in acts for c in (lo, hi) if n0 < c < n0 + n_chunk})
        for a, b in zip(cuts[:-1], cuts[1:]):
            piece = r[:, a - n0:b - n0]
            for lo, hi, kind in acts:
                if lo <= a and b <= hi:
                    piece = _ACTIVATIONS[kind](piece)
            o_ref[:, a:b] = piece.astype(o_ref.dtype)


def norm_matmul(x, nw, w, out_dtype, f32_cols=None, acts=()):
    t, d = x.shape
    n = w.shape[1]
    n_chunk = min(n, 1024)
    out_specs = [pl.BlockSpec((ROW_TILE, n), lambda i: (i, 0))]
    out_shape = [jax.ShapeDtypeStruct((t, n), out_dtype)]
    if f32_cols is not None:
        lo, hi = f32_cols
        assert lo // n_chunk == (hi - 1) // n_chunk
        out_specs.append(pl.BlockSpec((ROW_TILE, hi - lo), lambda i: (i, 0)))
        out_shape.append(jax.ShapeDtypeStruct((t, hi - lo), F32))
    out = pl.pallas_call(
        functools.partial(_norm_matmul_kernel, n_chunk=n_chunk, f32_cols=f32_cols, acts=tuple(acts)),
        grid=(t // ROW_TILE,),
        in_specs=[pl.BlockSpec((ROW_TILE, d), lambda i: (i, 0)), _resident((1, d)), _resident((d, n))],
        out_specs=out_specs, out_shape=out_shape,
        compiler_params=_params(), name="norm_matmul",
    )(x, nw.reshape(1, d), w)
    return out if f32_cols is not None else out[0]


def s5_tables(lam_re, lam_im, log_dt, b_re, b_im, c_re, c_im, d_skip):
    g, n, p = b_re.shape
    s = S5_CHUNK
    gl = S5_LANES // p
    j = g // gl
    hi = lax.Precision.HIGHEST
    dt = jnp.exp(log_dt)[:, None]
    a_re = jnp.exp(lam_re * dt) * jnp.cos(lam_im * dt)
    a_im = jnp.exp(lam_re * dt) * jnp.sin(lam_im * dt)
    den = lam_re * lam_re + lam_im * lam_im
    nr, ni = a_re - 1.0, a_im
    co_re = (nr * lam_re + ni * lam_im) / den
    co_im = (ni * lam_re - nr * lam_im) / den
    bb_re = co_re[..., None] * b_re - co_im[..., None] * b_im
    bb_im = co_re[..., None] * b_im + co_im[..., None] * b_re
    tau = jnp.arange(s + 1, dtype=F32)[:, None, None]
    pw_mag = jnp.exp(lam_re * dt * tau)
    pw_re = pw_mag * jnp.cos(lam_im * dt * tau)
    pw_im = pw_mag * jnp.sin(lam_im * dt * tau)
    ca_re = c_re[None] * pw_re[:, :, None, :] - c_im[None] * pw_im[:, :, None, :]
    ca_im = c_re[None] * pw_im[:, :, None, :] + c_im[None] * pw_re[:, :, None, :]
    bq_re, bq_im = bb_re.transpose(0, 2, 1), bb_im.transpose(0, 2, 1)
    kq = jnp.sum(ca_re[:s, :, :, None, :] * bq_re[None, :, None, :, :]
                 - ca_im[:s, :, :, None, :] * bq_im[None, :, None, :, :], axis=-1).transpose(1, 0, 3, 2)
    ts = np.arange(s)
    lag_onehot = (ts[None, None, :] - ts[None, :, None] == ts[:, None, None]).astype(np.float32)
    rev = s - 1 - ts
    w_re = pw_re[rev][:, :, :, None] * bb_re[None] - pw_im[rev][:, :, :, None] * bb_im[None]
    w_im = pw_re[rev][:, :, :, None] * bb_im[None] + pw_im[rev][:, :, :, None] * bb_re[None]
    m_c = (jnp.einsum('gxqp,xst->gsqtp', kq, lag_onehot, precision=hi)
           .reshape(j, gl, s, p, s * p).transpose(0, 2, 1, 3, 4).reshape(j, s * gl * p, s * p))
    w_c = (jnp.stack([w_re, w_im]).reshape(2, s, j, gl, n, p).transpose(2, 1, 3, 5, 0, 4)
           .reshape(j, s * gl * p, 2 * n))
    v_c = (jnp.stack([ca_re[1:], -ca_im[1:]]).reshape(2, s, j, gl, p, n).transpose(2, 0, 3, 5, 1, 4)
           .reshape(j, 2 * gl * n, s * p))
    fl = s * gl * p
    c_io = np.arange(fl)
    c_st = np.arange(2 * gl * n)
    k_io = np.arange(s * p)
    k_st = np.arange(2 * n)
    spread_io = ((k_io[:, None] // p == c_io[None, :] // (gl * p)) & (k_io[:, None] % p == c_io[None, :] % p))
    spread_st = ((k_st[:, None] // n == c_st[None, :] // (gl * n)) & (k_st[:, None] % n == c_st[None, :] % n))
    grp_io = (c_io // p) % gl
    grp_st = (c_st // n) % gl

    def expand(compact, spread, row_grp, col_grp):
        full = jnp.einsum('jrk,kc->jrc', compact.astype(BF16), jnp.asarray(spread, BF16),
                          preferred_element_type=F32)
        return jnp.where(jnp.asarray(row_grp[:, None] == col_grp[None, :]), full, 0.0).astype(BF16)

    m = expand(m_c, spread_io, grp_io, grp_io)
    w = expand(w_c, spread_st, grp_io, grp_st)
    v = expand(v_c, spread_io, grp_st, grp_io)
    a_s_re = pw_re[s].reshape(1, g * n)
    a_s_im = pw_im[s].reshape(1, g * n)
    dtab = jnp.broadcast_to(d_skip.reshape(j, 1, 1, gl * p), (j, 1, s, gl * p)).reshape(j, 1, s * gl * p)
    return jnp.concatenate([m, v], axis=1), w, a_s_re, a_s_im, dtab


def _s5_flat(u_ref):
    return jnp.concatenate([u_ref[:, t, :] for t in range(u_ref.shape[1])], axis=1)


def _s5a_kernel(u_ref, w_ref, ire_ref, iim_ref):
    r = jnp.dot(_s5_flat(u_ref).astype(BF16), w_ref[0], preferred_element_type=F32)
    half = r.shape[1] // 2
    ire_ref[...] = r[:, :half]
    iim_ref[...] = r[:, half:]


def _s5scan_kernel(ire_ref, iim_ref, ar_ref, ai_ref, h0r_ref, h0i_ref,
                   hpr_ref, hpi_ref, hfr_ref, hfi_ref):
    nb, nc, _ = ire_ref.shape
    ar, ai = ar_ref[...], ai_ref[...]

    def body(c, carry):
        out = []
        for b in range(nb):
            hr, hi = carry[2 * b], carry[2 * b + 1]
            hpr_ref[b, pl.ds(c, 1), :] = hr
            hpi_ref[b, pl.ds(c, 1), :] = hi
            out.append(ar * hr - ai * hi + ire_ref[b, pl.ds(c, 1), :])
            out.append(ar * hi + ai * hr + iim_ref[b, pl.ds(c, 1), :])
        return tuple(out)

    init = []
    for b in range(nb):
        init += [h0r_ref[b], h0i_ref[b]]
    fin = lax.fori_loop(0, nc, body, tuple(init))
    for b in range(nb):
        hfr_ref[b] = fin[2 * b]
        hfi_ref[b] = fin[2 * b + 1]


def _s5b_kernel(u_ref, hpr_ref, hpi_ref, mv_ref, d_ref, y_ref):
    uf = _s5_flat(u_ref)
    lhs = jnp.concatenate([uf.astype(BF16), hpr_ref[...].astype(BF16), hpi_ref[...].astype(BF16)], axis=1)
    y = d_ref[0] * uf + jnp.dot(lhs, mv_ref[0], preferred_element_type=F32)
    lanes = y_ref.shape[2]
    for t in range(y_ref.shape[1]):
        y_ref[:, t, :] = y[:, t * lanes:(t + 1) * lanes]


def s5_apply(u, bsz, h0_re, h0_im, tables):
    mv, w, a_re, a_im, dtab = tables
    tokens, width = u.shape
    nj = w.shape[0]
    s = S5_CHUNK
    rows = tokens // s
    nc = rows // bsz
    lanes = a_re.shape[1]
    half = w.shape[2] // 2
    fl = w.shape[1]
    rt = min(rows, 1024)
    u3 = u.reshape(rows, s, width)
    u_spec = pl.BlockSpec((rt, s, S5_LANES), lambda j, r: (r, 0, j))
    st_spec = pl.BlockSpec((rt, half), lambda j, r: (r, j))
    tab_spec = pl.BlockSpec((1, fl, fl), lambda j, r: (j, 0, 0))
    inj_re, inj_im = pl.pallas_call(
        _s5a_kernel, grid=(nj, rows // rt),
        in_specs=[u_spec, tab_spec],
        out_specs=[st_spec, st_spec],
        out_shape=[jax.ShapeDtypeStruct((rows, lanes), F32)] * 2,
        compiler_params=_params(2), name="s5_chunk_in",
    )(u3, w)

    sb, lw = 4, (512 if nc > 16 else lanes)
    seq_spec = pl.BlockSpec((sb, nc, lw), lambda b, l: (b, 0, l))
    vec_spec = pl.BlockSpec((sb, 1, lw), lambda b, l: (b, 0, l))
    atab_spec = pl.BlockSpec((1, lw), lambda b, l: (0, l))
    hp_re, hp_im, hf_re, hf_im = pl.pallas_call(
        _s5scan_kernel, grid=(bsz // sb, lanes // lw),
        in_specs=[seq_spec, seq_spec, atab_spec, atab_spec, vec_spec, vec_spec],
        out_specs=[seq_spec, seq_spec, vec_spec, vec_spec],
        out_shape=[jax.ShapeDtypeStruct((bsz, nc, lanes), F32)] * 2
        + [jax.ShapeDtypeStruct((bsz, 1, lanes), F32)] * 2,
        compiler_params=_params(2), name="s5_scan",
    )(inj_re.reshape(bsz, nc, lanes), inj_im.reshape(bsz, nc, lanes), a_re, a_im,
      h0_re.reshape(bsz, 1, lanes), h0_im.reshape(bsz, 1, lanes))

    y3 = pl.pallas_call(
        _s5b_kernel, grid=(nj, rows // rt),
        in_specs=[u_spec, st_spec, st_spec, pl.BlockSpec((1, 2 * fl, fl), lambda j, r: (j, 0, 0)),
                  pl.BlockSpec((1, 1, fl), lambda j, r: (j, 0, 0))],
        out_specs=u_spec,
        out_shape=jax.ShapeDtypeStruct((rows, s, width), F32),
        compiler_params=_params(2), name="s5_chunk_out",
    )(u3, hp_re.reshape(rows, lanes), hp_im.reshape(rows, lanes), mv, dtab)
    g = lanes // SSM_STATE
    return y3.reshape(tokens, width), hf_re.reshape(bsz, g, SSM_STATE), hf_im.reshape(bsz, g, SSM_STATE)


def _retention_gammas():
    return 1.0 - np.exp2(-5.0 - np.arange(RET_HEADS, dtype=np.float64))


def retention_tables(tile, chunk, nb):
    gam = _retention_gammas()[:, None, None]
    i = np.arange(tile)[:, None]
    j = np.arange(tile)[None, :]
    same = (i // chunk) == (j // chunk)
    earlier = (j // chunk) < (i // chunk)
    dist = np.where(same, np.abs(i - j), np.where(earlier, i - j, 0))
    dmask = np.where(same | earlier, gam ** dist[None], 0.0)
    qw = np.broadcast_to((gam[:, :, 0] ** (np.arange(tile) + 1.0))[:, :, None], (RET_HEADS, tile, HEAD_DIM))
    kw = np.broadcast_to((gam[:, :, 0] ** (tile - 1.0 - np.arange(tile)))[:, :, None], (RET_HEADS, tile, HEAD_DIM))
    dmask = np.stack([np.kron(np.eye(nb), m) for m in dmask])
    return (jnp.asarray(dmask, F32), jnp.asarray(np.tile(qw, (1, nb, 1)), F32),
            jnp.asarray(np.tile(kw, (1, nb, 1)), F32), tuple(float(x) for x in _retention_gammas() ** tile))


def rope_tables(pos):
    half = HEAD_DIM // 2
    inv = jnp.exp(-math.log(ROPE_BASE) * 2.0 * jnp.arange(half, dtype=F32) / HEAD_DIM)
    ang = pos.astype(F32)[:, None] * inv[None, :]
    cos, sin = jnp.cos(ang), jnp.sin(ang)
    cosf = jnp.concatenate([cos, cos], axis=1)
    sinf = jnp.concatenate([-sin, sin], axis=1)
    return cosf, sinf


def _mixer_kernel(x_ref, zq_ref, xq_ref, gl_ref, y_ref, cq_ref, sq_ref, ck_ref, sk_ref,
                  dm_ref, qw_ref, kw_ref, vm_ref, am_ref, mk_ref, mv_ref, s0_ref, gn_ref,
                  wro_ref, wglu_ref, wso_ref, wxo_ref, wout_ref,
                  h_ref, sout_ref, s_scr, o_scr, xo_scr, glu_scr, mg_scr, *, nb, tl, tile_decay):
    hd = HEAD_DIM
    qk = RET_HEADS * hd
    n_mem = mk_ref.shape[1] // X_HEADS

    def own_blocks(a, width):
        if nb == 1:
            return a
        return jnp.concatenate([a[n * tl:(n + 1) * tl, n * width:(n + 1) * width] for n in range(nb)], axis=0)

    def spread_blocks(a, mask_ref):
        if nb == 1:
            return a
        return jnp.concatenate([a] * nb, axis=1) * mask_ref[...]

    @pl.when(pl.program_id(1) == 0)
    def _():
        s_scr[...] = s0_ref[0]

    cq, sq, ck, sk = cq_ref[...], sq_ref[...], ck_ref[...], sk_ref[...]
    for h in range(RET_HEADS):
        c0 = h * hd
        q = zq_ref[:, c0:c0 + hd].astype(F32)
        k = zq_ref[:, qk + c0:qk + c0 + hd].astype(F32)
        v = zq_ref[:, 2 * qk + c0:2 * qk + c0 + hd]
        g = zq_ref[:, 3 * qk + c0:3 * qk + c0 + hd].astype(F32)
        qr = q * cq + pltpu.roll(q, hd // 2, 1) * sq
        kr = k * ck + pltpu.roll(k, hd // 2, 1) * sk
        sc = lax.dot_general(qr.astype(BF16), kr.astype(BF16), NT_DIMS,
                             preferred_element_type=F32) * dm_ref[h]
        o = jnp.dot(sc.astype(BF16), v, preferred_element_type=F32)
        s_old = s_scr[h]
        o += own_blocks(jnp.dot((qr * qw_ref[h]).astype(BF16), s_old.astype(BF16),
                                preferred_element_type=F32), hd)
        kv = lax.dot_general((kr * kw_ref[h]).astype(BF16), spread_blocks(v, vm_ref), TN_DIMS,
                             preferred_element_type=F32)
        s_scr[h] = tile_decay[h] * s_old + kv
        d = o - jnp.mean(o, axis=-1, keepdims=True)
        on = d * lax.rsqrt(jnp.mean(d * d, axis=-1, keepdims=True) + EPS) * gn_ref[:, c0:c0 + hd]
        o_scr[:, c0:c0 + hd] = (on * g).astype(BF16)
        mem_rows = pl.ds(h, n_mem, stride=X_HEADS)
        mkh = mk_ref[:, mem_rows, :].astype(BF16).reshape(nb * n_mem, hd)
        mvh = mv_ref[:, mem_rows, :].astype(BF16).reshape(nb * n_mem, hd)
        s = own_blocks(lax.dot_general(xq_ref[:, c0:c0 + hd], mkh, NT_DIMS,
                                       preferred_element_type=F32), n_mem) * (hd ** -0.5)
        e = jnp.exp(s - jnp.max(s, axis=-1, keepdims=True))
        p = (e / jnp.sum(e, axis=-1, keepdims=True)).astype(BF16)
        xo_scr[:, c0:c0 + hd] = jnp.dot(spread_blocks(p, am_ref), mvh,
                                        preferred_element_type=F32).astype(BF16)

    cw = MIX_COLS
    dm = h_ref.shape[1]
    half = wglu_ref.shape[1] // 2
    yb = jax.nn.gelu(y_ref[...]).astype(BF16)
    for c0 in range(0, half, cw):
        ga = jnp.dot(yb, wglu_ref[:, c0:c0 + cw], preferred_element_type=F32)
        gb = jnp.dot(yb, wglu_ref[:, half + c0:half + c0 + cw], preferred_element_type=F32)
        glu_scr[:, c0:c0 + cw] = (ga * _sigmoid(gb)).astype(BF16)
    for c0 in range(0, dm, cw):
        cols = slice(c0, c0 + cw)
        ret = jnp.dot(o_scr[...], wro_ref[:, cols], preferred_element_type=F32)
        ssm = jnp.dot(glu_scr[...], wso_ref[:, cols], preferred_element_type=F32)
        xb = jnp.dot(xo_scr[...], wxo_ref[:, cols], preferred_element_type=F32)
        merged = (gl_ref[:, c0:c0 + cw].astype(F32) * ret
                  + gl_ref[:, dm + c0:dm + c0 + cw].astype(F32) * ssm
                  + gl_ref[:, 2 * dm + c0:2 * dm + c0 + cw].astype(F32) * xb)
        mg_scr[:, cols] = merged.astype(BF16)
    for c0 in range(0, dm, cw):
        cols = slice(c0, c0 + cw)
        h_ref[:, cols] = x_ref[:, cols] + jnp.dot(mg_scr[...], wout_ref[:, cols], preferred_element_type=F32)
    sout_ref[0] = s_scr[...]


def mixer(x, z, y_ssm, pos, mem_k, mem_v, s0, ret_gn, w_ret_o, w_ssm_glu, w_ssm_o, w_x_o, w_out, *, nb, tl):
    bsz, length, dm = x.shape
    chunk = min(CHUNK, length)
    nl = length // tl
    rows = nb * tl
    qk = RET_HEADS * HEAD_DIM
    sw = y_ssm.shape[1]
    xw = X_HEADS * HEAD_DIM
    gate_col = (4 * qk + sw + xw)
    assert gate_col % (3 * dm) == 0 and (4 * qk + sw) % xw == 0
    dmask, qw, kw, tile_decay = retention_tables(tl, chunk, nb)
    seq_of_row = np.arange(rows)[:, None] // tl
    own_v = jnp.asarray(seq_of_row == np.arange(nb * HEAD_DIM)[None, :] // HEAD_DIM, BF16)
    n_mem = mem_k.shape[1] // X_HEADS
    own_mem = jnp.asarray(seq_of_row == np.arange(nb * n_mem)[None, :] // n_mem, BF16)
    cosf, sinf = (jnp.broadcast_to(t.reshape(nl, 1, tl, HEAD_DIM), (nl, nb, tl, HEAD_DIM)).reshape(nl * rows, HEAD_DIM)
                  for t in rope_tables(pos))
    scale = HEAD_DIM ** -0.5
    s0 = s0.reshape(bsz // nb, nb, RET_HEADS, HEAD_DIM, HEAD_DIM).transpose(0, 2, 3, 1, 4).reshape(
        bsz // nb, RET_HEADS, HEAD_DIM, nb * HEAD_DIM)
    row_map = lambda b, l: (b * nl + l, 0)
    tab_map = lambda b, l: (l, 0)
    st_spec = pl.BlockSpec((1, RET_HEADS, HEAD_DIM, nb * HEAD_DIM), lambda b, l: (b, 0, 0, 0))
    mem_spec = pl.BlockSpec((nb,) + mem_k.shape[1:], lambda b, l: (b,) + (0,) * (mem_k.ndim - 1))
    h, s_out = pl.pallas_call(
        functools.partial(_mixer_kernel, nb=nb, tl=tl, tile_decay=tile_decay),
        grid=(bsz // nb, nl),
        in_specs=[pl.BlockSpec((rows, dm), row_map),
                  pl.BlockSpec((rows, 4 * qk), row_map),
                  pl.BlockSpec((rows, xw), lambda b, l: (b * nl + l, (4 * qk + sw) // xw)),
                  pl.BlockSpec((rows, 3 * dm), lambda b, l: (b * nl + l, gate_col // (3 * dm))),
                  pl.BlockSpec((rows, sw), row_map),
                  pl.BlockSpec((rows, HEAD_DIM), tab_map), pl.BlockSpec((rows, HEAD_DIM), tab_map),
                  pl.BlockSpec((rows, HEAD_DIM), tab_map), pl.BlockSpec((rows, HEAD_DIM), tab_map),
                  _resident(dmask.shape), _resident(qw.shape), _resident(kw.shape),
                  _resident(own_v.shape), _resident(own_mem.shape),
                  mem_spec, mem_spec, st_spec, _resident((1, qk)),
                  _resident(w_ret_o.shape), _resident(w_ssm_glu.shape), _resident(w_ssm_o.shape),
                  _resident(w_x_o.shape), _resident(w_out.shape)],
        out_specs=[pl.BlockSpec((rows, dm), row_map), st_spec],
        out_shape=[jax.ShapeDtypeStruct((bsz * length, dm), F32),
                   jax.ShapeDtypeStruct(s0.shape, F32)],
        scratch_shapes=[pltpu.VMEM((RET_HEADS, HEAD_DIM, nb * HEAD_DIM), F32),
                        pltpu.VMEM((rows, qk), BF16), pltpu.VMEM((rows, xw), BF16),
                        pltpu.VMEM((rows, w_ssm_o.shape[0]), BF16), pltpu.VMEM((rows, dm), BF16)],
        compiler_params=_params(2), name="mixer",
    )(x.reshape(bsz * length, dm), z, z, z, y_ssm,
      cosf * scale, sinf * scale, cosf, sinf, dmask, qw, kw, own_v, own_mem,
      mem_k, mem_v, s0, ret_gn.reshape(1, qk), w_ret_o, w_ssm_glu, w_ssm_o, w_x_o, w_out)
    s_out = s_out.reshape(bsz // nb, RET_HEADS, HEAD_DIM, nb, HEAD_DIM).transpose(0, 3, 1, 2, 4).reshape(
        bsz, RET_HEADS, HEAD_DIM, HEAD_DIM)
    return h, s_out


def _router_kernel(hp_ref, hs_ref, nw_ref, wrt_ref, br_ref, tri_ref, low_ref,
                   xn_ref, gw_ref, crow_ref, cnt_ref, *, prompt_tiles):
    h = jnp.where(pl.program_id(0) < prompt_tiles, hp_ref[...], hs_ref[...])
    xn = _rms(h, nw_ref[...]).astype(BF16)
    xn_ref[...] = xn
    logits = lax.dot_general(wrt_ref[...], xn, NT_DIMS, preferred_element_type=F32) + br_ref[...]
    ne = logits.shape[0]
    iota = lax.broadcasted_iota(jnp.int32, logits.shape, 0)
    rest = logits
    sel = jnp.zeros(logits.shape, jnp.bool_)
    vals, idxs = [], []
    for _ in range(TOP_K):
        m = jnp.max(rest, axis=0, keepdims=True)
        ix = jnp.min(jnp.where(rest == m, iota, ne), axis=0, keepdims=True)
        hit = iota == ix
        vals.append(m)
        idxs.append(ix)
        sel = jnp.logical_or(sel, hit)
        rest = jnp.where(hit, -jnp.inf, rest)
    es = [jnp.exp(v - vals[0]) for v in vals]
    tot = es[0] + es[1] + es[2] + es[3]
    before = jnp.dot(sel.astype(BF16), tri_ref[...], preferred_element_type=F32)
    cnt = jnp.sum(sel.astype(F32), axis=1, keepdims=True)
    seg = jnp.floor((cnt + (ROW_ALIGN - 1.0)) * (1.0 / ROW_ALIGN)) * ROW_ALIGN
    start = jnp.dot(low_ref[...], jnp.broadcast_to(seg, before.shape), precision=lax.Precision.HIGHEST,
                    preferred_element_type=F32)
    place = start + before
    for k in range(TOP_K):
        gw_ref[k:k + 1, :] = es[k] / tot
        crow_ref[k:k + 1, :] = jnp.sum(jnp.where(iota == idxs[k], place, 0.0), axis=0,
                                       keepdims=True).astype(jnp.int32)
    cnt_ref[...] = jnp.broadcast_to(cnt, cnt_ref.shape)


def router(h_p, h_s, norm_ffn, w_router, b_router):
    dm = h_p.shape[1]
    ne = w_router.shape[1]
    tt = ROW_TILE
    npt, nst = h_p.shape[0] // tt, h_s.shape[0] // tt
    t = (npt + nst) * tt
    tri = jnp.asarray(np.triu(np.ones((tt, tt), np.float32), k=1), BF16)
    low = jnp.asarray(np.tril(np.ones((ne, ne), np.float32), k=-1))
    tok_spec = pl.BlockSpec((TOP_K, tt), lambda i: (0, i))
    return pl.pallas_call(
        functools.partial(_router_kernel, prompt_tiles=npt), grid=(npt + nst,),
        in_specs=[pl.BlockSpec((tt, dm), lambda i: (jnp.minimum(i, npt - 1), 0)),
                  pl.BlockSpec((tt, dm), lambda i: (jnp.maximum(i - npt, 0), 0)),
                  _resident((1, dm)), _resident((ne, dm)), _resident((ne, 1)), _resident((tt, tt)),
                  _resident((ne, ne))],
        out_specs=[pl.BlockSpec((tt, dm), lambda i: (i, 0)), tok_spec, tok_spec,
                   pl.BlockSpec((ne, 128), lambda i: (i, 0))],
        out_shape=[jax.ShapeDtypeStruct((t, dm), BF16),
                   jax.ShapeDtypeStruct((TOP_K, t), F32), jax.ShapeDtypeStruct((TOP_K, t), jnp.int32),
                   jax.ShapeDtypeStruct(((npt + nst) * ne, 128), F32)],
        compiler_params=_params(), name="router",
    )(h_p, h_s, norm_ffn.reshape(1, dm), w_router.T.astype(BF16), b_router.reshape(ne, 1), tri, low)


def _pack_bf16_pairs(x, exact=False):
    n = x.shape[1] // 2
    lo, hi = x[:, :n], x[:, n:]
    if not exact:
        lo, hi = lo.astype(BF16).astype(F32), hi.astype(BF16).astype(F32)
    lo = lax.bitcast_convert_type(lo, jnp.int32)
    hi = lax.bitcast_convert_type(hi, jnp.int32)
    return lax.shift_right_logical(lo, 16) | (hi & -65536)


def _unpack_bf16_pairs(u):
    lo = lax.bitcast_convert_type(lax.shift_left(u, 16), F32).astype(BF16)
    hi = lax.bitcast_convert_type(u & -65536, F32).astype(BF16)
    return lo, hi


def _split_count(n, fn):
    def quad(jq, carry):
        fn(4 * jq, 4)
        return carry

    lax.fori_loop(0, n // 4, quad, 0)

    @pl.when(n % 4 >= 2)
    def _():
        fn(n // 4 * 4, 2)

    @pl.when(n % 2 == 1)
    def _():
        fn(n // 2 * 2, 1)


def _chunk_loop(n, fn):
    def quad(jq, carry):
        for u in range(4):
            fn(4 * jq + u)
        return carry

    def single(j, carry):
        fn(j)
        return carry

    lax.fori_loop(0, n // 4, quad, 0)
    lax.fori_loop(n // 4 * 4, n, single, 0)


def _dispatch_kernel(dst_ref, tch_ref, zs_ref, zn_ref, used_ref,
                     xn_ref, crow_ref, xs_ref, cbuf, zbuf, sems, zsem, *, ne, bm):
    i = pl.program_id(0)
    nt = pl.num_programs(0)
    slot = lax.rem(i, 2)
    tt = xn_ref.shape[0]
    cr = cbuf.shape[1]
    ra = ROW_ALIGN
    n_blocks = xs_ref.shape[0] // bm

    def chunk_copy(sl, src_row, dst_row, rows):
        return pltpu.make_async_copy(cbuf.at[sl, pl.ds(src_row, rows)], xs_ref.at[pl.ds(dst_row, rows)],
                                     sems.at[sl])

    def wait_chunks(sl, n):
        _split_count(n, lambda j, m: chunk_copy(sl, 0, 0, ra * m).wait())

    def tail_copy(e, j):
        return pltpu.make_async_copy(zbuf.at[pl.ds(0, ra)],
                                     xs_ref.at[pl.ds(pl.multiple_of(zs_ref[e] + ra * j, ra), ra)], zsem)

    def block_copy(b):
        return pltpu.make_async_copy(zbuf, xs_ref.at[pl.ds(pl.multiple_of(b * bm, bm), bm)], zsem)

    def zero_fill(start):
        def per_expert(e, carry):
            def per_chunk(j, c2):
                (tail_copy(e, j).start() if start else tail_copy(e, j).wait())
                return c2
            lax.fori_loop(0, zn_ref[e], per_chunk, 0)
            return carry
        lax.fori_loop(0, ne, per_expert, 0)

        def per_block(b, carry):
            (block_copy(b).start() if start else block_copy(b).wait())
            return carry
        lax.fori_loop(used_ref[0], n_blocks, per_block, 0)

    @pl.when(i == 0)
    def _():
        zbuf[...] = jnp.zeros_like(zbuf)
        zero_fill(True)
        zero_fill(False)

    @pl.when(i >= 2)
    def _():
        wait_chunks(slot, tch_ref[jnp.maximum(i - 2, 0)])

    crow = crow_ref[...]
    rows = lax.broadcasted_iota(jnp.int32, (cr, tt), 0)
    hit = rows == crow[0:1, :]
    for k in range(1, TOP_K):
        hit = jnp.logical_or(hit, rows == crow[k:k + 1, :])
    packed = _pack_bf16_pairs(jnp.dot(jnp.where(hit, 1.0, 0.0).astype(BF16), xn_ref[...],
                                      preferred_element_type=F32), exact=True)
    cbuf[slot] = packed

    chunks_per_tile = cr // ra
    _chunk_loop(tch_ref[i], lambda c: chunk_copy(
        slot, pl.multiple_of(c * ra, ra), pl.multiple_of(dst_ref[i * chunks_per_tile + c], ra), ra).start())

    @pl.when(i == nt - 1)
    def _():
        wait_chunks(slot, tch_ref[i])
        wait_chunks(1 - slot, jnp.where(nt >= 2, tch_ref[jnp.maximum(i - 1, 0)], 0))


def dispatch(xn, crow, tables, n_rows, ne):
    t, dm = xn.shape
    tt = ROW_TILE
    bm = FFN_BLOCK
    cr = _compact_rows(ne)
    grid_spec = pltpu.PrefetchScalarGridSpec(
        num_scalar_prefetch=5, grid=(t // tt,),
        in_specs=[pl.BlockSpec((tt, dm), lambda i, *_: (i, 0)),
                  pl.BlockSpec((TOP_K, tt), lambda i, *_: (0, i))],
        out_specs=pl.BlockSpec(memory_space=pl.ANY),
        scratch_shapes=[pltpu.VMEM((2, cr, dm // 2), jnp.int32), pltpu.VMEM((bm, dm // 2), jnp.int32),
                        pltpu.SemaphoreType.DMA((2,)), pltpu.SemaphoreType.DMA(())])
    return pl.pallas_call(
        functools.partial(_dispatch_kernel, ne=ne, bm=bm), grid_spec=grid_spec,
        out_shape=jax.ShapeDtypeStruct((n_rows, dm // 2), jnp.int32),
        compiler_params=_params(), name="dispatch",
    )(*tables, xn, crow)


FFN_W_PIECES = (4, 2)


def _ffn_kernel(first_ref, nblk_ref, used_ref, next_ref, wslot_ref, lead_ref,
                xs_hbm, wgu_hbm, bgu_ref, wd_hbm, bd_ref, y_hbm,
                wgu_f32, wd_f32, wgu_bf, wd_bf, xbuf, ybuf, xsem, ysem, wsem, zsem, *, bm):
    e = pl.program_id(0)
    n = nblk_ref[e]
    b0 = first_ref[e]
    n_blocks = y_hbm.shape[0] // bm
    n_pieces = sum(FFN_W_PIECES)

    def block_rows(b):
        return pl.ds(pl.multiple_of(b * bm, bm), bm)

    used = used_ref[0]
    x_slots = xbuf.shape[0]

    def x_copy(g):
        sl = lax.rem(g, x_slots)
        return pltpu.make_async_copy(xs_hbm.at[block_rows(g)], xbuf.at[sl], xsem.at[sl])

    def y_copy(g):
        sl = lax.rem(g, 2)
        return pltpu.make_async_copy(ybuf.at[sl], y_hbm.at[block_rows(g)], ysem.at[sl])

    def w_piece(ex, sl, p):
        src, dst, q, parts = ((wgu_hbm, wgu_f32, p, FFN_W_PIECES[0]) if p < FFN_W_PIECES[0]
                              else (wd_hbm, wd_f32, p - FFN_W_PIECES[0], FFN_W_PIECES[1]))
        rows = src.shape[1] // parts
        return pltpu.make_async_copy(src.at[ex, pl.ds(q * rows, rows)], dst.at[sl, pl.ds(q * rows, rows)],
                                     wsem.at[sl, p])

    @pl.when(n > 0)
    def _():
        ws = wslot_ref[e]
        nxt = next_ref[e]

        @pl.when(lead_ref[0] == e)
        def _():
            for g in range(x_slots - 1):
                @pl.when(g < used)
                def _():
                    x_copy(g).start()
            for p in range(n_pieces):
                w_piece(e, ws, p).start()

        for p in range(n_pieces):
            w_piece(e, ws, p).wait()
        wgu_bf[...] = wgu_f32[ws].astype(BF16)
        wd_bf[...] = wd_f32[ws].astype(BF16)

        def block(j, carry):
            g = b0 + j
            x_copy(g).wait()

            @pl.when(g + x_slots - 1 < used)
            def _():
                x_copy(g + x_slots - 1).start()

            for p in range(n_pieces):
                @pl.when(jnp.logical_and(j == p, nxt >= 0))
                def _():
                    w_piece(nxt, 1 - ws, p).start()

            @pl.when(g >= 2)
            def _():
                y_copy(g - 2).wait()

            x = jnp.concatenate(_unpack_bf16_pairs(xbuf[lax.rem(g, x_slots)]), axis=1)
            hgu = jnp.dot(x, wgu_bf[...], preferred_element_type=F32) + bgu_ref[0]
            ff = hgu.shape[1] // 2
            gate = jnp.minimum(hgu[:, :ff], SWIGLU_LIMIT)
            up = jnp.clip(hgu[:, ff:], -SWIGLU_LIMIT, SWIGLU_LIMIT)
            act = (up + 1.0) * gate * _sigmoid(SWIGLU_ALPHA * gate)
            ybuf[lax.rem(g, 2)] = _pack_bf16_pairs(
                jnp.dot(act.astype(BF16), wd_bf[...], preferred_element_type=F32) + bd_ref[0])
            y_copy(g).start()
            return carry

        lax.fori_loop(0, n, block, 0)

        for p in range(n_pieces):
            @pl.when(jnp.logical_and(p >= n, nxt >= 0))
            def _():
                w_piece(nxt, 1 - ws, p).start()

    @pl.when(e == pl.num_programs(0) - 1)
    def _():
        for back in (2, 1):
            @pl.when(used >= back)
            def _():
                y_copy(used - back).wait()
        ybuf[0] = jnp.zeros(ybuf.shape[1:], ybuf.dtype)

        def zero_copy(b):
            return pltpu.make_async_copy(ybuf.at[0], y_hbm.at[block_rows(b)], zsem)

        def start(b, carry):
            zero_copy(b).start()
            return carry

        def wait(b, carry):
            zero_copy(b).wait()
            return carry

        lax.fori_loop(used_ref[0], n_blocks, start, 0)
        lax.fori_loop(used_ref[0], n_blocks, wait, 0)


def expert_ffn(xs, expert_tables, w_gate_up, b_gate_up, w_down, b_down):
    n_rows = xs.shape[0]
    ne, dm, ff2 = w_gate_up.shape
    bm = FFN_BLOCK
    grid_spec = pltpu.PrefetchScalarGridSpec(
        num_scalar_prefetch=6, grid=(ne,),
        in_specs=[pl.BlockSpec(memory_space=pl.ANY),
                  pl.BlockSpec(memory_space=pl.ANY),
                  pl.BlockSpec((1, 1, ff2), lambda e, *_: (e, 0, 0)),
                  pl.BlockSpec(memory_space=pl.ANY),
                  pl.BlockSpec((1, 1, dm), lambda e, *_: (e, 0, 0))],
        out_specs=pl.BlockSpec(memory_space=pl.ANY),
        scratch_shapes=[pltpu.VMEM((2, dm, ff2), F32), pltpu.VMEM((2, ff2 // 2, dm), F32),
                        pltpu.VMEM((dm, ff2), BF16), pltpu.VMEM((ff2 // 2, dm), BF16),
                        pltpu.VMEM((3, bm, dm // 2), jnp.int32), pltpu.VMEM((2, bm, dm // 2), jnp.int32),
                        pltpu.SemaphoreType.DMA((3,)), pltpu.SemaphoreType.DMA((2,)),
                        pltpu.SemaphoreType.DMA((2, sum(FFN_W_PIECES))), pltpu.SemaphoreType.DMA(())])
    return pl.pallas_call(
        functools.partial(_ffn_kernel, bm=bm), grid_spec=grid_spec,
        out_shape=jax.ShapeDtypeStruct((n_rows, dm // 2), jnp.int32),
        compiler_params=_params(), name="expert_ffn",
    )(*expert_tables, xs, w_gate_up, b_gate_up.reshape(ne, 1, ff2), w_down, b_down.reshape(ne, 1, dm))


def _combine_kernel(dst_ref, tch_ref,
                    crow_ref, gw_ref, hp_ref, hs_ref, fn_ref, yr_ref, yp_ref, ys_ref, ybuf, sems,
                    *, prompt_tiles):
    i = pl.program_id(0)
    nt = pl.num_programs(0)
    slot = lax.rem(i, 2)
    tt = hp_ref.shape[0]
    cr = ybuf.shape[1]
    ra = ROW_ALIGN

    def chunk_copy(sl, src_row, dst_row, rows):
        return pltpu.make_async_copy(yr_ref.at[pl.ds(src_row, rows)], ybuf.at[sl, pl.ds(dst_row, rows)],
                                     sems.at[sl])

    def fetch(tile, sl):
        chunks_per_tile = cr // ra
        _chunk_loop(tch_ref[tile], lambda c: chunk_copy(
            sl, pl.multiple_of(dst_ref[tile * chunks_per_tile + c], ra), pl.multiple_of(c * ra, ra), ra).start())

    @pl.when(i == 0)
    def _():
        ybuf[...] = jnp.zeros_like(ybuf)
        fetch(0, 0)

    @pl.when(i + 1 < nt)
    def _():
        fetch(i + 1, 1 - slot)

    _split_count(tch_ref[i], lambda j, m: chunk_copy(slot, 0, 0, ra * m).wait())

    y_lo, y_hi = _unpack_bf16_pairs(ybuf[slot])
    cols = lax.broadcasted_iota(jnp.int32, (tt, cr), 1)
    q = jnp.zeros((tt, cr), F32)
    for k in range(TOP_K):
        q = jnp.where(cols == crow_ref[:, k:k + 1], gw_ref[:, k:k + 1], q)
    qb = q.astype(BF16)
    moe = jnp.concatenate([jnp.dot(qb, y_lo, preferred_element_type=F32),
                           jnp.dot(qb, y_hi, preferred_element_type=F32)], axis=1)
    h = jnp.where(i < prompt_tiles, hp_ref[...], hs_ref[...])
    out = _rms(h + moe, fn_ref[...])

    @pl.when(i < prompt_tiles)
    def _():
        yp_ref[...] = out

    @pl.when(i >= prompt_tiles)
    def _():
        ys_ref[...] = out


def combine(crow_t, gw_t, h_p, h_s, final_norm, y_rows, tables, ne):
    dm = h_p.shape[1]
    tt = ROW_TILE
    npt, nst = h_p.shape[0] // tt, h_s.shape[0] // tt
    cr = _compact_rows(ne)
    p_map = lambda i, *_: (jnp.minimum(i, npt - 1), 0)
    s_map = lambda i, *_: (jnp.maximum(i - npt, 0), 0)
    grid_spec = pltpu.PrefetchScalarGridSpec(
        num_scalar_prefetch=2, grid=(npt + nst,),
        in_specs=[pl.BlockSpec((tt, TOP_K), lambda i, *_: (i, 0)),
                  pl.BlockSpec((tt, TOP_K), lambda i, *_: (i, 0)),
                  pl.BlockSpec((tt, dm), p_map), pl.BlockSpec((tt, dm), s_map),
                  pl.BlockSpec((1, dm), lambda i, *_: (0, 0)),
                  pl.BlockSpec(memory_space=pl.ANY)],
        out_specs=[pl.BlockSpec((tt, dm), p_map), pl.BlockSpec((tt, dm), s_map)],
        scratch_shapes=[pltpu.VMEM((2, cr, dm // 2), jnp.int32), pltpu.SemaphoreType.DMA((2,))])
    return pl.pallas_call(
        functools.partial(_combine_kernel, prompt_tiles=npt), grid_spec=grid_spec,
        out_shape=[jax.ShapeDtypeStruct(h_p.shape, F32), jax.ShapeDtypeStruct(h_s.shape, F32)],
        compiler_params=_params(), name="combine",
    )(*tables, crow_t, gw_t, h_p, h_s, final_norm.reshape(1, dm), y_rows)


def _compact_rows(ne):
    return -(-(TOP_K * ROW_TILE + ne * (ROW_ALIGN - 1)) // 128) * 128


def moe_and_final_norm(h_p, h_s, norm_ffn, w_router, b_router, w_gate_up, b_gate_up, w_down, b_down, final_norm):
    ne = w_router.shape[1]
    bm = FFN_BLOCK
    ra = ROW_ALIGN
    xn, gw, crow, cnt = router(h_p, h_s, norm_ffn, w_router, b_router)
    t = xn.shape[0]
    nt = t // ROW_TILE
    seg = -(-cnt[:, 0].astype(jnp.int32).reshape(nt, ne) // ra) * ra
    seg_before = jnp.cumsum(seg, axis=0) - seg
    rows_e = jnp.sum(seg, axis=0)
    padded = -(-rows_e // bm) * bm
    pad_ends = jnp.cumsum(padded)
    pad_starts = pad_ends - padded
    n_blocks = -(-(t * TOP_K + nt * ne * (ra - 1) + ne * (bm - 1)) // bm)
    n_used = pad_ends[-1] // bm
    experts = jnp.arange(ne, dtype=jnp.int32)
    chunks = seg // ra
    chunk_end = jnp.cumsum(chunks, axis=1)
    c_ids = jnp.arange(_compact_rows(ne) // ra, dtype=jnp.int32)
    owner = jnp.sum((chunk_end[:, None, :] <= c_ids[None, :, None]).astype(jnp.int32), axis=2)
    seg_shift = pad_starts[None, :] + seg_before - ra * (chunk_end - chunks)
    dst = ra * c_ids[None, :] + jnp.sum(
        jnp.where(owner[:, :, None] == experts[None, None, :], seg_shift[:, None, :], 0), axis=2)
    seg_tables = (dst.reshape(-1), chunk_end[:, -1])
    fill_tables = (pad_starts + rows_e, (padded - rows_e) // ra, n_used.reshape(1))
    to_i32 = lambda xs: tuple(x.astype(jnp.int32) for x in xs)
    xs = dispatch(xn, crow, to_i32(seg_tables + fill_tables), n_blocks * bm, ne)
    active = padded > 0
    later = jnp.where(active, experts, ne)
    next_active = jnp.concatenate([lax.cummin(later, reverse=True)[1:], jnp.full((1,), ne, jnp.int32)])
    expert_tables = (pad_starts // bm, padded // bm, n_used.reshape(1),
                     jnp.where(next_active < ne, next_active, -1),
                     (jnp.cumsum(active.astype(jnp.int32)) - active.astype(jnp.int32)) % 2,
                     jnp.min(later).reshape(1))
    y_rows = expert_ffn(xs, to_i32(expert_tables), w_gate_up, b_gate_up, w_down, b_down)
    return combine(crow.T, gw.T, h_p, h_s, final_norm, y_rows, to_i32(seg_tables), ne)


def kernel(x_prompt, x_sample, cache_mem_k, cache_mem_v, state_ret, state_ssm_re, state_ssm_im, mem_prompt, norm_mix, w_in, ret_gn, w_ret_o, ssm_lam_re, ssm_lam_im, ssm_log_dt, ssm_b_re, ssm_b_im, ssm_c_re, ssm_c_im, ssm_d, w_ssm_glu, w_ssm_o, mem_norm, w_mem_kv, w_x_o, w_out, norm_ffn, w_router, b_router, w_gate_up, b_gate_up, w_down, b_down, final_norm):
    assert norm_mix.shape[0] == 1, "single-layer step"
    bp, lp, dm = x_prompt.shape
    bs, ls, _ = x_sample.shape
    n_mem = mem_prompt.shape[1]
    xw = X_HEADS * HEAD_DIM
    qk = RET_HEADS * HEAD_DIM
    sw = ssm_d.shape[1]
    g = ssm_lam_re.shape[1]

    w_in_b = w_in[0].astype(BF16)
    tables = s5_tables(ssm_lam_re[0], ssm_lam_im[0], ssm_log_dt[0], ssm_b_re[0], ssm_b_im[0],
                       ssm_c_re[0], ssm_c_im[0], ssm_d[0])
    mix_w = (ret_gn[0], w_ret_o[0].astype(BF16), w_ssm_glu[0].astype(BF16), w_ssm_o[0].astype(BF16),
             w_x_o[0].astype(BF16), w_out[0].astype(BF16))

    kv = norm_matmul(mem_prompt.reshape(bp * n_mem, dm), mem_norm[0], w_mem_kv[0].astype(BF16), F32)
    mk_p = kv[:, :xw].reshape(bp, n_mem * X_HEADS, HEAD_DIM)
    mv_p = kv[:, xw:].reshape(bp, n_mem * X_HEADS, HEAD_DIM)

    def group(x, pos, mem_k, mem_v, s_ret, h_re, h_im, nb, tl):
        bsz, length, _ = x.shape
        z, u = norm_matmul(x.reshape(bsz * length, dm), norm_mix[0], w_in_b, BF16,
                           f32_cols=(4 * qk, 4 * qk + sw),
                           acts=((3 * qk, 4 * qk, "silu"), (4 * qk + sw + xw, w_in_b.shape[1], "sigmoid")))
        y, hf_re, hf_im = s5_apply(u, bsz, h_re, h_im, tables)
        h, s_new = mixer(x, z, y, pos, mem_k, mem_v, s_ret, *mix_w, nb=nb, tl=tl)
        return h, s_new, hf_re, hf_im

    zero_ret = jnp.zeros((bp, RET_HEADS, HEAD_DIM, HEAD_DIM), F32)
    zero_ssm = jnp.zeros((bp, g, SSM_STATE), F32)
    h_p, ret_p, sre_p, sim_p = group(x_prompt, jnp.arange(lp, dtype=jnp.int32), mk_p, mv_p,
                                     zero_ret, zero_ssm, zero_ssm, 1, ROW_TILE)
    h_s, ret_s, sre_s, sim_s = group(x_sample, PAST_LEN + jnp.arange(ls, dtype=jnp.int32),
                                     cache_mem_k[0].reshape(bs, n_mem * X_HEADS, HEAD_DIM),
                                     cache_mem_v[0].reshape(bs, n_mem * X_HEADS, HEAD_DIM),
                                     state_ret[0], state_ssm_re[0], state_ssm_im[0], ROW_TILE // ls, ls)

    y_p, y_s = moe_and_final_norm(h_p, h_s, norm_ffn[0], w_router[0], b_router[0],
                                  w_gate_up[0], b_gate_up[0], w_down[0], b_down[0], final_norm)
    return (y_p.reshape(bp, lp, dm), y_s.reshape(bs, ls, dm), ret_p[None], sre_p[None], sim_p[None],
            mk_p.reshape(1, bp, n_mem, X_HEADS, HEAD_DIM), mv_p.reshape(1, bp, n_mem, X_HEADS, HEAD_DIM),
            ret_s[None], sre_s[None], sim_s[None])
```

```python
import functools
import math

import jax
import jax.numpy as jnp
import numpy as np
from jax import lax
from jax.experimental import pallas as pl
from jax.experimental.pallas import tpu as pltpu

F32 = jnp.float32
BF16 = jnp.bfloat16

EPS = 1e-6
CHUNK = 64
PAST_LEN = 2048
ROPE_BASE = 10000.0
RET_HEADS = 4
X_HEADS = 4
HEAD_DIM = 128
SSM_GROUP = 16
SSM_STATE = 64
TOP_K = 4
SWIGLU_ALPHA = 1.702
SWIGLU_LIMIT = 7.0

VMEM_LIMIT = 52 * 1024 * 1024
S5_CHUNK = 8
S5_LANES = 128
ROW_TILE = 256
FFN_BLOCK = 256
ROW_ALIGN = 8
MIX_COLS = 512
NT_DIMS = (((1,), (1,)), ((), ()))
TN_DIMS = (((0,), (0,)), ((), ()))


def _params(n_axes=1):
    return pltpu.CompilerParams(dimension_semantics=("arbitrary",) * n_axes,
                                vmem_limit_bytes=VMEM_LIMIT)


def _resident(shape):
    nd = len(shape)
    return pl.BlockSpec(shape, lambda *_: (0,) * nd, pipeline_mode=pl.Buffered(1))


def _rms(x, w):
    return x * lax.rsqrt(jnp.mean(x * x, axis=-1, keepdims=True) + EPS) * w


def _sigmoid(x):
    return 0.5 * jnp.tanh(0.5 * x) + 0.5


_ACTIVATIONS = {"sigmoid": _sigmoid, "silu": lambda v: v * _sigmoid(v)}


def _norm_matmul_kernel(x_ref, nw_ref, w_ref, o_ref, *f32_refs, n_chunk, f32_cols, acts):
    xb = _rms(x_ref[...], nw_ref[...]).astype(BF16)
    for n0 in range(0, o_ref.shape[1], n_chunk):
        r = jnp.dot(xb, w_ref[:, n0:n0 + n_chunk], preferred_element_type=F32)
        if f32_cols is not None and n0 <= f32_cols[0] and f32_cols[1] <= n0 + n_chunk:
            f32_refs[0][...] = r[:, f32_cols[0] - n0:f32_cols[1] - n0]
        cuts = sorted({n0, n0 + n_chunk} | {c for lo, hi, _ in acts for c in (lo, hi) if n0 < c < n0 + n_chunk})
        for a, b in zip(cuts[:-1], cuts[1:]):
            piece = r[:, a - n0:b - n0]
            for lo, hi, kind in acts:
                if lo <= a and b <= hi:
                    piece = _ACTIVATIONS[kind](piece)
            o_ref[:, a:b] = piece.astype(o_ref.dtype)


def norm_matmul(x, nw, w, out_dtype, f32_cols=None, acts=()):
    t, d = x.shape
    n = w.shape[1]
    n_chunk = min(n, 1024)
    out_specs = [pl.BlockSpec((ROW_TILE, n), lambda i: (i, 0))]
    out_shape = [jax.ShapeDtypeStruct((t, n), out_dtype)]
    if f32_cols is not None:
        lo, hi = f32_cols
        assert lo // n_chunk == (hi - 1) // n_chunk
        out_specs.append(pl.BlockSpec((ROW_TILE, hi - lo), lambda i: (i, 0)))
        out_shape.append(jax.ShapeDtypeStruct((t, hi - lo), F32))
    out = pl.pallas_call(
        functools.partial(_norm_matmul_kernel, n_chunk=n_chunk, f32_cols=f32_cols, acts=tuple(acts)),
        grid=(t // ROW_TILE,),
        in_specs=[pl.BlockSpec((ROW_TILE, d), lambda i: (i, 0)), _resident((1, d)), _resident((d, n))],
        out_specs=out_specs, out_shape=out_shape,
        compiler_params=_params(), name="norm_matmul",
    )(x, nw.reshape(1, d), w)
    return out if f32_cols is not None else out[0]


def s5_tables(lam_re, lam_im, log_dt, b_re, b_im, c_re, c_im, d_skip):
    g, n, p = b_re.shape
    s = S5_CHUNK
    gl = S5_LANES // p
    j = g // gl
    hi = lax.Precision.HIGHEST
    dt = jnp.exp(log_dt)[:, None]
    a_re = jnp.exp(lam_re * dt) * jnp.cos(lam_im * dt)
    a_im = jnp.exp(lam_re * dt) * jnp.sin(lam_im * dt)
    den = lam_re * lam_re + lam_im * lam_im
    nr, ni = a_re - 1.0, a_im
    co_re = (nr * lam_re + ni * lam_im) / den
    co_im = (ni * lam_re - nr * lam_im) / den
    bb_re = co_re[..., None] * b_re - co_im[..., None] * b_im
    bb_im = co_re[..., None] * b_im + co_im[..., None] * b_re
    tau = jnp.arange(s + 1, dtype=F32)[:, None, None]
    pw_mag = jnp.exp(lam_re * dt * tau)
    pw_re = pw_mag * jnp.cos(lam_im * dt * tau)
    pw_im = pw_mag * jnp.sin(lam_im * dt * tau)
    ca_re = c_re[None] * pw_re[:, :, None, :] - c_im[None] * pw_im[:, :, None, :]
    ca_im = c_re[None] * pw_im[:, :, None, :] + c_im[None] * pw_re[:, :, None, :]
    bq_re, bq_im = bb_re.transpose(0, 2, 1), bb_im.transpose(0, 2, 1)
    kq = jnp.sum(ca_re[:s, :, :, None, :] * bq_re[None, :, None, :, :]
                 - ca_im[:s, :, :, None, :] * bq_im[None, :, None, :, :], axis=-1).transpose(1, 0, 3, 2)
    ts = np.arange(s)
    lag_onehot = (ts[None, None, :] - ts[None, :, None] == ts[:, None, None]).astype(np.float32)
    rev = s - 1 - ts
    w_re = pw_re[rev][:, :, :, None] * bb_re[None] - pw_im[rev][:, :, :, None] * bb_im[None]
    w_im = pw_re[rev][:, :, :, None] * bb_im[None] + pw_im[rev][:, :, :, None] * bb_re[None]
    m_c = (jnp.einsum('gxqp,xst->gsqtp', kq, lag_onehot, precision=hi)
           .reshape(j, gl, s, p, s * p).transpose(0, 2, 1, 3, 4).reshape(j, s * gl * p, s * p))
    w_c = (jnp.stack([w_re, w_im]).reshape(2, s, j, gl, n, p).transpose(2, 1, 3, 5, 0, 4)
           .reshape(j, s * gl * p, 2 * n))
    v_c = (jnp.stack([ca_re[1:], -ca_im[1:]]).reshape(2, s, j, gl, p, n).transpose(2, 0, 3, 5, 1, 4)
           .reshape(j, 2 * gl * n, s * p))
    fl = s * gl * p
    c_io = np.arange(fl)
    c_st = np.arange(2 * gl * n)
    k_io = np.arange(s * p)
    k_st = np.arange(2 * n)
    spread_io = ((k_io[:, None] // p == c_io[None, :] // (gl * p)) & (k_io[:, None] % p == c_io[None, :] % p))
    spread_st = ((k_st[:, None] // n == c_st[None, :] // (gl * n)) & (k_st[:, None] % n == c_st[None, :] % n))
    grp_io = (c_io // p) % gl
    grp_st = (c_st // n) % gl

    def expand(compact, spread, row_grp, col_grp):
        full = jnp.einsum('jrk,kc->jrc', compact.astype(BF16), jnp.asarray(spread, BF16),
                          preferred_element_type=F32)
        return jnp.where(jnp.asarray(row_grp[:, None] == col_grp[None, :]), full, 0.0).astype(BF16)

    m = expand(m_c, spread_io, grp_io, grp_io)
    w = expand(w_c, spread_st, grp_io, grp_st)
    v = expand(v_c, spread_io, grp_st, grp_io)
    a_s_re = pw_re[s].reshape(1, g * n)
    a_s_im = pw_im[s].reshape(1, g * n)
    dtab = jnp.broadcast_to(d_skip.reshape(j, 1, 1, gl * p), (j, 1, s, gl * p)).reshape(j, 1, s * gl * p)
    return jnp.concatenate([m, v], axis=1), w, a_s_re, a_s_im, dtab


def _s5_flat(u_ref):
    return jnp.concatenate([u_ref[:, t, :] for t in range(u_ref.shape[1])], axis=1)


def _s5a_kernel(u_ref, w_ref, ire_ref, iim_ref):
    r = jnp.dot(_s5_flat(u_ref).astype(BF16), w_ref[0], preferred_element_type=F32)
    half = r.shape[1] // 2
    ire_ref[...] = r[:, :half]
    iim_ref[...] = r[:, half:]


def _s5scan_kernel(ire_ref, iim_ref, ar_ref, ai_ref, h0r_ref, h0i_ref,
                   hpr_ref, hpi_ref, hfr_ref, hfi_ref):
    nb, nc, _ = ire_ref.shape
    ar, ai = ar_ref[...], ai_ref[...]

    def body(c, carry):
        out = []
        for b in range(nb):
            hr, hi = carry[2 * b], carry[2 * b + 1]
            hpr_ref[b, pl.ds(c, 1), :] = hr
            hpi_ref[b, pl.ds(c, 1), :] = hi
            out.append(ar * hr - ai * hi + ire_ref[b, pl.ds(c, 1), :])
            out.append(ar * hi + ai * hr + iim_ref[b, pl.ds(c, 1), :])
        return tuple(out)

    init = []
    for b in range(nb):
        init += [h0r_ref[b], h0i_ref[b]]
    fin = lax.fori_loop(0, nc, body, tuple(init))
    for b in range(nb):
        hfr_ref[b] = fin[2 * b]
        hfi_ref[b] = fin[2 * b + 1]


def _s5b_kernel(u_ref, hpr_ref, hpi_ref, mv_ref, d_ref, y_ref):
    uf = _s5_flat(u_ref)
    lhs = jnp.concatenate([uf.astype(BF16), hpr_ref[...].astype(BF16), hpi_ref[...].astype(BF16)], axis=1)
    y = d_ref[0] * uf + jnp.dot(lhs, mv_ref[0], preferred_element_type=F32)
    lanes = y_ref.shape[2]
    for t in range(y_ref.shape[1]):
        y_ref[:, t, :] = y[:, t * lanes:(t + 1) * lanes]


def s5_apply(u, bsz, h0_re, h0_im, tables):
    mv, w, a_re, a_im, dtab = tables
    tokens, width = u.shape
    nj = w.shape[0]
    s = S5_CHUNK
    rows = tokens // s
    nc = rows // bsz
    lanes = a_re.shape[1]
    half = w.shape[2] // 2
    fl = w.shape[1]
    rt = min(rows, 1024)
    u3 = u.reshape(rows, s, width)
    u_spec = pl.BlockSpec((rt, s, S5_LANES), lambda j, r: (r, 0, j))
    st_spec = pl.BlockSpec((rt, half), lambda j, r: (r, j))
    tab_spec = pl.BlockSpec((1, fl, fl), lambda j, r: (j, 0, 0))
    inj_re, inj_im = pl.pallas_call(
        _s5a_kernel, grid=(nj, rows // rt),
        in_specs=[u_spec, tab_spec],
        out_specs=[st_spec, st_spec],
        out_shape=[jax.ShapeDtypeStruct((rows, lanes), F32)] * 2,
        compiler_params=_params(2), name="s5_chunk_in",
    )(u3, w)

    sb, lw = 4, (512 if nc > 16 else lanes)
    seq_spec = pl.BlockSpec((sb, nc, lw), lambda b, l: (b, 0, l))
    vec_spec = pl.BlockSpec((sb, 1, lw), lambda b, l: (b, 0, l))
    atab_spec = pl.BlockSpec((1, lw), lambda b, l: (0, l))
    hp_re, hp_im, hf_re, hf_im = pl.pallas_call(
        _s5scan_kernel, grid=(bsz // sb, lanes // lw),
        in_specs=[seq_spec, seq_spec, atab_spec, atab_spec, vec_spec, vec_spec],
        out_specs=[seq_spec, seq_spec, vec_spec, vec_spec],
        out_shape=[jax.ShapeDtypeStruct((bsz, nc, lanes), F32)] * 2
        + [jax.ShapeDtypeStruct((bsz, 1, lanes), F32)] * 2,
        compiler_params=_params(2), name="s5_scan",
    )(inj_re.reshape(bsz, nc, lanes), inj_im.reshape(bsz, nc, lanes), a_re, a_im,
      h0_re.reshape(bsz, 1, lanes), h0_im.reshape(bsz, 1, lanes))

    y3 = pl.pallas_call(
        _s5b_kernel, grid=(nj, rows // rt),
        in_specs=[u_spec, st_spec, st_spec, pl.BlockSpec((1, 2 * fl, fl), lambda j, r: (j, 0, 0)),
                  pl.BlockSpec((1, 1, fl), lambda j, r: (j, 0, 0))],
        out_specs=u_spec,
        out_shape=jax.ShapeDtypeStruct((rows, s, width), F32),
        compiler_params=_params(2), name="s5_chunk_out",
    )(u3, hp_re.reshape(rows, lanes), hp_im.reshape(rows, lanes), mv, dtab)
    g = lanes // SSM_STATE
    return y3.reshape(tokens, width), hf_re.reshape(bsz, g, SSM_STATE), hf_im.reshape(bsz, g, SSM_STATE)


def _retention_gammas():
    return 1.0 - np.exp2(-5.0 - np.arange(RET_HEADS, dtype=np.float64))


def retention_tables(tile, chunk, nb):
    gam = _retention_gammas()[:, None, None]
    i = np.arange(tile)[:, None]
    j = np.arange(tile)[None, :]
    same = (i // chunk) == (j // chunk)
    earlier = (j // chunk) < (i // chunk)
    dist = np.where(same, np.abs(i - j), np.where(earlier, i - j, 0))
    dmask = np.where(same | earlier, gam ** dist[None], 0.0)
    qw = np.broadcast_to((gam[:, :, 0] ** (np.arange(tile) + 1.0))[:, :, None], (RET_HEADS, tile, HEAD_DIM))
    kw = np.broadcast_to((gam[:, :, 0] ** (tile - 1.0 - np.arange(tile)))[:, :, None], (RET_HEADS, tile, HEAD_DIM))
    dmask = np.stack([np.kron(np.eye(nb), m) for m in dmask])
    return (jnp.asarray(dmask, F32), jnp.asarray(np.tile(qw, (1, nb, 1)), F32),
            jnp.asarray(np.tile(kw, (1, nb, 1)), F32), tuple(float(x) for x in _retention_gammas() ** tile))


def rope_tables(pos):
    half = HEAD_DIM // 2
    inv = jnp.exp(-math.log(ROPE_BASE) * 2.0 * jnp.arange(half, dtype=F32) / HEAD_DIM)
    ang = pos.astype(F32)[:, None] * inv[None, :]
    cos, sin = jnp.cos(ang), jnp.sin(ang)
    cosf = jnp.concatenate([cos, cos], axis=1)
    sinf = jnp.concatenate([-sin, sin], axis=1)
    return cosf, sinf


def _mixer_kernel(x_ref, zq_ref, xq_ref, gl_ref, y_ref, cq_ref, sq_ref, ck_ref, sk_ref,
                  dm_ref, qw_ref, kw_ref, vm_ref, am_ref, mk_ref, mv_ref, s0_ref, gn_ref,
                  wro_ref, wglu_ref, wso_ref, wxo_ref, wout_ref,
                  h_ref, sout_ref, s_scr, o_scr, xo_scr, glu_scr, mg_scr, *, nb, tl, tile_decay):
    hd = HEAD_DIM
    qk = RET_HEADS * hd
    n_mem = mk_ref.shape[1] // X_HEADS

    def own_blocks(a, width):
        if nb == 1:
            return a
        return jnp.concatenate([a[n * tl:(n + 1) * tl, n * width:(n + 1) * width] for n in range(nb)], axis=0)

    def spread_blocks(a, mask_ref):
        if nb == 1:
            return a
        return jnp.concatenate([a] * nb, axis=1) * mask_ref[...]

    @pl.when(pl.program_id(1) == 0)
    def _():
        for n in range(nb):
            for h in range(RET_HEADS):
                s_scr[h, :, n * hd:(n + 1) * hd] = s0_ref[n, h]

    cq, sq, ck, sk = cq_ref[...], sq_ref[...], ck_ref[...], sk_ref[...]
    for h in range(RET_HEADS):
        c0 = h * hd
        q = zq_ref[:, c0:c0 + hd].astype(F32)
        k = zq_ref[:, qk + c0:qk + c0 + hd].astype(F32)
        v = zq_ref[:, 2 * qk + c0:2 * qk + c0 + hd]
        g = zq_ref[:, 3 * qk + c0:3 * qk + c0 + hd].astype(F32)
        qr = q * cq + pltpu.roll(q, hd // 2, 1) * sq
        kr = k * ck + pltpu.roll(k, hd // 2, 1) * sk
        sc = lax.dot_general(qr.astype(BF16), kr.astype(BF16), NT_DIMS,
                             preferred_element_type=F32) * dm_ref[h]
        o = jnp.dot(sc.astype(BF16), v, preferred_element_type=F32)
        s_old = s_scr[h]
        o += own_blocks(jnp.dot((qr * qw_ref[h]).astype(BF16), s_old.astype(BF16),
                                preferred_element_type=F32), hd)
        kv = lax.dot_general((kr * kw_ref[h]).astype(BF16), spread_blocks(v, vm_ref), TN_DIMS,
                             preferred_element_type=F32)
        s_scr[h] = tile_decay[h] * s_old + kv
        d = o - jnp.mean(o, axis=-1, keepdims=True)
        on = d * lax.rsqrt(jnp.mean(d * d, axis=-1, keepdims=True) + EPS) * gn_ref[:, c0:c0 + hd]
        o_scr[:, c0:c0 + hd] = (on * g).astype(BF16)
        mem_rows = pl.ds(h, n_mem, stride=X_HEADS)
        mkh = mk_ref[:, mem_rows, :].astype(BF16).reshape(nb * n_mem, hd)
        mvh = mv_ref[:, mem_rows, :].astype(BF16).reshape(nb * n_mem, hd)
        s = own_blocks(lax.dot_general(xq_ref[:, c0:c0 + hd], mkh, NT_DIMS,
                                       preferred_element_type=F32), n_mem) * (hd ** -0.5)
        e = jnp.exp(s - jnp.max(s, axis=-1, keepdims=True))
        p = (e / jnp.sum(e, axis=-1, keepdims=True)).astype(BF16)
        xo_scr[:, c0:c0 + hd] = jnp.dot(spread_blocks(p, am_ref), mvh,
                                        preferred_element_type=F32).astype(BF16)

    cw = MIX_COLS
    dm = h_ref.shape[1]
    half = wglu_ref.shape[1] // 2
    yb = jax.nn.gelu(y_ref[...]).astype(BF16)
    for c0 in range(0, half, cw):
        ga = jnp.dot(yb, wglu_ref[:, c0:c0 + cw], preferred_element_type=F32)
        gb = jnp.dot(yb, wglu_ref[:, half + c0:half + c0 + cw], preferred_element_type=F32)
        glu_scr[:, c0:c0 + cw] = (ga * _sigmoid(gb)).astype(BF16)
    for c0 in range(0, dm, cw):
        cols = slice(c0, c0 + cw)
        ret = jnp.dot(o_scr[...], wro_ref[:, cols], preferred_element_type=F32)
        ssm = jnp.dot(glu_scr[...], wso_ref[:, cols], preferred_element_type=F32)
        xb = jnp.dot(xo_scr[...], wxo_ref[:, cols], preferred_element_type=F32)
        merged = (gl_ref[:, c0:c0 + cw].astype(F32) * ret
                  + gl_ref[:, dm + c0:dm + c0 + cw].astype(F32) * ssm
                  + gl_ref[:, 2 * dm + c0:2 * dm + c0 + cw].astype(F32) * xb)
        mg_scr[:, cols] = merged.astype(BF16)
    for c0 in range(0, dm, cw):
        cols = slice(c0, c0 + cw)
        h_ref[:, cols] = x_ref[:, cols] + jnp.dot(mg_scr[...], wout_ref[:, cols], preferred_element_type=F32)
    for n in range(nb):
        for h in range(RET_HEADS):
            sout_ref[n, h] = s_scr[h, :, n * hd:(n + 1) * hd]


def mixer(x, z, y_ssm, pos, mem_k, mem_v, s0, ret_gn, w_ret_o, w_ssm_glu, w_ssm_o, w_x_o, w_out, *, nb, tl):
    bsz, length, dm = x.shape
    chunk = min(CHUNK, length)
    nl = length // tl
    rows = nb * tl
    qk = RET_HEADS * HEAD_DIM
    sw = y_ssm.shape[1]
    xw = X_HEADS * HEAD_DIM
    gate_col = (4 * qk + sw + xw)
    assert gate_col % (3 * dm) == 0 and (4 * qk + sw) % xw == 0
    dmask, qw, kw, tile_decay = retention_tables(tl, chunk, nb)
    seq_of_row = np.arange(rows)[:, None] // tl
    own_v = jnp.asarray(seq_of_row == np.arange(nb * HEAD_DIM)[None, :] // HEAD_DIM, BF16)
    n_mem = mem_k.shape[1] // X_HEADS
    own_mem = jnp.asarray(seq_of_row == np.arange(nb * n_mem)[None, :] // n_mem, BF16)
    cosf, sinf = (jnp.broadcast_to(t.reshape(nl, 1, tl, HEAD_DIM), (nl, nb, tl, HEAD_DIM)).reshape(nl * rows, HEAD_DIM)
                  for t in rope_tables(pos))
    scale = HEAD_DIM ** -0.5
    row_map = lambda b, l: (b * nl + l, 0)
    tab_map = lambda b, l: (l, 0)
    st_spec = pl.BlockSpec((nb, RET_HEADS, HEAD_DIM, HEAD_DIM), lambda b, l: (b, 0, 0, 0))
    mem_spec = pl.BlockSpec((nb,) + mem_k.shape[1:], lambda b, l: (b,) + (0,) * (mem_k.ndim - 1))
    h, s_out = pl.pallas_call(
        functools.partial(_mixer_kernel, nb=nb, tl=tl, tile_decay=tile_decay),
        grid=(bsz // nb, nl),
        in_specs=[pl.BlockSpec((rows, dm), row_map),
                  pl.BlockSpec((rows, 4 * qk), row_map),
                  pl.BlockSpec((rows, xw), lambda b, l: (b * nl + l, (4 * qk + sw) // xw)),
                  pl.BlockSpec((rows, 3 * dm), lambda b, l: (b * nl + l, gate_col // (3 * dm))),
                  pl.BlockSpec((rows, sw), row_map),
                  pl.BlockSpec((rows, HEAD_DIM), tab_map), pl.BlockSpec((rows, HEAD_DIM), tab_map),
                  pl.BlockSpec((rows, HEAD_DIM), tab_map), pl.BlockSpec((rows, HEAD_DIM), tab_map),
                  _resident(dmask.shape), _resident(qw.shape), _resident(kw.shape),
                  _resident(own_v.shape), _resident(own_mem.shape),
                  mem_spec, mem_spec, st_spec, _resident((1, qk)),
                  _resident(w_ret_o.shape), _resident(w_ssm_glu.shape), _resident(w_ssm_o.shape),
                  _resident(w_x_o.shape), _resident(w_out.shape)],
        out_specs=[pl.BlockSpec((rows, dm), row_map), st_spec],
        out_shape=[jax.ShapeDtypeStruct((bsz * length, dm), F32),
                   jax.ShapeDtypeStruct(s0.shape, F32)],
        scratch_shapes=[pltpu.VMEM((RET_HEADS, HEAD_DIM, nb * HEAD_DIM), F32),
                        pltpu.VMEM((rows, qk), BF16), pltpu.VMEM((rows, xw), BF16),
                        pltpu.VMEM((rows, w_ssm_o.shape[0]), BF16), pltpu.VMEM((rows, dm), BF16)],
        compiler_params=_params(2), name="mixer",
    )(x.reshape(bsz * length, dm), z, z, z, y_ssm,
      cosf * scale, sinf * scale, cosf, sinf, dmask, qw, kw, own_v, own_mem,
      mem_k, mem_v, s0, ret_gn.reshape(1, qk), w_ret_o, w_ssm_glu, w_ssm_o, w_x_o, w_out)
    return h, s_out


def _router_kernel(hp_ref, hs_ref, nw_ref, wrt_ref, br_ref, tri_ref, low_ref,
                   xn_ref, gw_ref, crow_ref, cnt_ref, *, prompt_tiles):
    h = jnp.where(pl.program_id(0) < prompt_tiles, hp_ref[...], hs_ref[...])
    xn = _rms(h, nw_ref[...]).astype(BF16)
    xn_ref[...] = xn
    logits = lax.dot_general(wrt_ref[...], xn, NT_DIMS, preferred_element_type=F32) + br_ref[...]
    ne = logits.shape[0]
    iota = lax.broadcasted_iota(jnp.int32, logits.shape, 0)
    rest = logits
    sel = jnp.zeros(logits.shape, jnp.bool_)
    vals, idxs = [], []
    for _ in range(TOP_K):
        m = jnp.max(rest, axis=0, keepdims=True)
        ix = jnp.min(jnp.where(rest == m, iota, ne), axis=0, keepdims=True)
        hit = iota == ix
        vals.append(m)
        idxs.append(ix)
        sel = jnp.logical_or(sel, hit)
        rest = jnp.where(hit, -jnp.inf, rest)
    es = [jnp.exp(v - vals[0]) for v in vals]
    tot = es[0] + es[1] + es[2] + es[3]
    before = jnp.dot(sel.astype(BF16), tri_ref[...], preferred_element_type=F32)
    cnt = jnp.sum(sel.astype(F32), axis=1, keepdims=True)
    seg = jnp.floor((cnt + (ROW_ALIGN - 1.0)) * (1.0 / ROW_ALIGN)) * ROW_ALIGN
    start = jnp.dot(low_ref[...], jnp.broadcast_to(seg, before.shape), precision=lax.Precision.HIGHEST,
                    preferred_element_type=F32)
    place = start + before
    for k in range(TOP_K):
        gw_ref[k:k + 1, :] = es[k] / tot
        crow_ref[k:k + 1, :] = jnp.sum(jnp.where(iota == idxs[k], place, 0.0), axis=0,
                                       keepdims=True).astype(jnp.int32)
    cnt_ref[...] = jnp.broadcast_to(cnt, cnt_ref.shape)


def router(h_p, h_s, norm_ffn, w_router, b_router):
    dm = h_p.shape[1]
    ne = w_router.shape[1]
    tt = ROW_TILE
    npt, nst = h_p.shape[0] // tt, h_s.shape[0] // tt
    t = (npt + nst) * tt
    tri = jnp.asarray(np.triu(np.ones((tt, tt), np.float32), k=1), BF16)
    low = jnp.asarray(np.tril(np.ones((ne, ne), np.float32), k=-1))
    tok_spec = pl.BlockSpec((TOP_K, tt), lambda i: (0, i))
    return pl.pallas_call(
        functools.partial(_router_kernel, prompt_tiles=npt), grid=(npt + nst,),
        in_specs=[pl.BlockSpec((tt, dm), lambda i: (jnp.minimum(i, npt - 1), 0)),
                  pl.BlockSpec((tt, dm), lambda i: (jnp.maximum(i - npt, 0), 0)),
                  _resident((1, dm)), _resident((ne, dm)), _resident((ne, 1)), _resident((tt, tt)),
                  _resident((ne, ne))],
        out_specs=[pl.BlockSpec((tt, dm), lambda i: (i, 0)), tok_spec, tok_spec,
                   pl.BlockSpec((ne, 128), lambda i: (i, 0))],
        out_shape=[jax.ShapeDtypeStruct((t, dm), BF16),
                   jax.ShapeDtypeStruct((TOP_K, t), F32), jax.ShapeDtypeStruct((TOP_K, t), jnp.int32),
                   jax.ShapeDtypeStruct(((npt + nst) * ne, 128), F32)],
        compiler_params=_params(), name="router",
    )(h_p, h_s, norm_ffn.reshape(1, dm), w_router.T.astype(BF16), b_router.reshape(ne, 1), tri, low)


def _pack_bf16_pairs(x, exact=False):
    n = x.shape[1] // 2
    lo, hi = x[:, :n], x[:, n:]
    if not exact:
        lo, hi = lo.astype(BF16).astype(F32), hi.astype(BF16).astype(F32)
    lo = lax.bitcast_convert_type(lo, jnp.int32)
    hi = lax.bitcast_convert_type(hi, jnp.int32)
    return lax.shift_right_logical(lo, 16) | (hi & -65536)


def _unpack_bf16_pairs(u):
    lo = lax.bitcast_convert_type(lax.shift_left(u, 16), F32).astype(BF16)
    hi = lax.bitcast_convert_type(u & -65536, F32).astype(BF16)
    return lo, hi


def _split_count(n, fn):
    def quad(jq, carry):
        fn(4 * jq, 4)
        return carry

    lax.fori_loop(0, n // 4, quad, 0)

    @pl.when(n % 4 >= 2)
    def _():
        fn(n // 4 * 4, 2)

    @pl.when(n % 2 == 1)
    def _():
        fn(n // 2 * 2, 1)


def _chunk_loop(n, fn):
    def quad(jq, carry):
        for u in range(4):
            fn(4 * jq + u)
        return carry

    def single(j, carry):
        fn(j)
        return carry

    lax.fori_loop(0, n // 4, quad, 0)
    lax.fori_loop(n // 4 * 4, n, single, 0)


def _dispatch_kernel(dst_ref, tch_ref, zs_ref, zn_ref, used_ref,
                     xn_ref, crow_ref, xs_ref, cbuf, zbuf, sems, zsem, *, ne, bm):
    i = pl.program_id(0)
    nt = pl.num_programs(0)
    slot = lax.rem(i, 2)
    tt = xn_ref.shape[0]
    cr = cbuf.shape[1]
    ra = ROW_ALIGN
    n_blocks = xs_ref.shape[0] // bm

    def chunk_copy(sl, src_row, dst_row, rows):
        return pltpu.make_async_copy(cbuf.at[sl, pl.ds(src_row, rows)], xs_ref.at[pl.ds(dst_row, rows)],
                                     sems.at[sl])

    def wait_chunks(sl, n):
        _split_count(n, lambda j, m: chunk_copy(sl, 0, 0, ra * m).wait())

    def tail_copy(e, j):
        return pltpu.make_async_copy(zbuf.at[pl.ds(0, ra)],
                                     xs_ref.at[pl.ds(pl.multiple_of(zs_ref[e] + ra * j, ra), ra)], zsem)

    def block_copy(b):
        return pltpu.make_async_copy(zbuf, xs_ref.at[pl.ds(pl.multiple_of(b * bm, bm), bm)], zsem)

    def zero_fill(start):
        def per_expert(e, carry):
            def per_chunk(j, c2):
                (tail_copy(e, j).start() if start else tail_copy(e, j).wait())
                return c2
            lax.fori_loop(0, zn_ref[e], per_chunk, 0)
            return carry
        lax.fori_loop(0, ne, per_expert, 0)

        def per_block(b, carry):
            (block_copy(b).start() if start else block_copy(b).wait())
            return carry
        lax.fori_loop(used_ref[0], n_blocks, per_block, 0)

    @pl.when(i == 0)
    def _():
        zbuf[...] = jnp.zeros_like(zbuf)
        zero_fill(True)
        zero_fill(False)

    @pl.when(i >= 2)
    def _():
        wait_chunks(slot, tch_ref[jnp.maximum(i - 2, 0)])

    crow = crow_ref[...]
    rows = lax.broadcasted_iota(jnp.int32, (cr, tt), 0)
    hit = rows == crow[0:1, :]
    for k in range(1, TOP_K):
        hit = jnp.logical_or(hit, rows == crow[k:k + 1, :])
    packed = _pack_bf16_pairs(jnp.dot(jnp.where(hit, 1.0, 0.0).astype(BF16), xn_ref[...],
                                      preferred_element_type=F32), exact=True)
    cbuf[slot] = packed

    chunks_per_tile = cr // ra
    _chunk_loop(tch_ref[i], lambda c: chunk_copy(
        slot, pl.multiple_of(c * ra, ra), pl.multiple_of(dst_ref[i * chunks_per_tile + c], ra), ra).start())

    @pl.when(i == nt - 1)
    def _():
        wait_chunks(slot, tch_ref[i])
        wait_chunks(1 - slot, jnp.where(nt >= 2, tch_ref[jnp.maximum(i - 1, 0)], 0))


def dispatch(xn, crow, tables, n_rows, ne):
    t, dm = xn.shape
    tt = ROW_TILE
    bm = FFN_BLOCK
    cr = _compact_rows(ne)
    grid_spec = pltpu.PrefetchScalarGridSpec(
        num_scalar_prefetch=5, grid=(t // tt,),
        in_specs=[pl.BlockSpec((tt, dm), lambda i, *_: (i, 0)),
                  pl.BlockSpec((TOP_K, tt), lambda i, *_: (0, i))],
        out_specs=pl.BlockSpec(memory_space=pl.ANY),
        scratch_shapes=[pltpu.VMEM((2, cr, dm // 2), jnp.int32), pltpu.VMEM((bm, dm // 2), jnp.int32),
                        pltpu.SemaphoreType.DMA((2,)), pltpu.SemaphoreType.DMA(())])
    return pl.pallas_call(
        functools.partial(_dispatch_kernel, ne=ne, bm=bm), grid_spec=grid_spec,
        out_shape=jax.ShapeDtypeStruct((n_rows, dm // 2), jnp.int32),
        compiler_params=_params(), name="dispatch",
    )(*tables, xn, crow)


FFN_W_PIECES = (4, 2)


def _ffn_kernel(first_ref, nblk_ref, used_ref, next_ref, wslot_ref, lead_ref,
                xs_hbm, wgu_hbm, bgu_ref, wd_hbm, bd_ref, y_hbm,
                wgu_f32, wd_f32, wgu_bf, wd_bf, xbuf, ybuf, xsem, ysem, wsem, zsem, *, bm):
    e = pl.program_id(0)
    n = nblk_ref[e]
    b0 = first_ref[e]
    n_blocks = y_hbm.shape[0] // bm
    n_pieces = sum(FFN_W_PIECES)

    def block_rows(b):
        return pl.ds(pl.multiple_of(b * bm, bm), bm)

    used = used_ref[0]
    x_slots = xbuf.shape[0]

    def x_copy(g):
        sl = lax.rem(g, x_slots)
        return pltpu.make_async_copy(xs_hbm.at[block_rows(g)], xbuf.at[sl], xsem.at[sl])

    def y_copy(g):
        sl = lax.rem(g, 2)
        return pltpu.make_async_copy(ybuf.at[sl], y_hbm.at[block_rows(g)], ysem.at[sl])

    def w_piece(ex, sl, p):
        src, dst, q, parts = ((wgu_hbm, wgu_f32, p, FFN_W_PIECES[0]) if p < FFN_W_PIECES[0]
                              else (wd_hbm, wd_f32, p - FFN_W_PIECES[0], FFN_W_PIECES[1]))
        rows = src.shape[1] // parts
        return pltpu.make_async_copy(src.at[ex, pl.ds(q * rows, rows)], dst.at[sl, pl.ds(q * rows, rows)],
                                     wsem.at[sl, p])

    @pl.when(n > 0)
    def _():
        ws = wslot_ref[e]
        nxt = next_ref[e]

        @pl.when(lead_ref[0] == e)
        def _():
            for g in range(x_slots - 1):
                @pl.when(g < used)
                def _():
                    x_copy(g).start()
            for p in range(n_pieces):
                w_piece(e, ws, p).start()

        for p in range(n_pieces):
            w_piece(e, ws, p).wait()
        wgu_bf[...] = wgu_f32[ws].astype(BF16)
        wd_bf[...] = wd_f32[ws].astype(BF16)

        def block(j, carry):
            g = b0 + j
            x_copy(g).wait()

            @pl.when(g + x_slots - 1 < used)
            def _():
                x_copy(g + x_slots - 1).start()

            for p in range(n_pieces):
                @pl.when(jnp.logical_and(j == p, nxt >= 0))
                def _():
                    w_piece(nxt, 1 - ws, p).start()

            @pl.when(g >= 2)
            def _():
                y_copy(g - 2).wait()

            x = jnp.concatenate(_unpack_bf16_pairs(xbuf[lax.rem(g, x_slots)]), axis=1)
            hgu = jnp.dot(x, wgu_bf[...], preferred_element_type=F32) + bgu_ref[0]
            ff = hgu.shape[1] // 2
            gate = jnp.minimum(hgu[:, :ff], SWIGLU_LIMIT)
            up = jnp.clip(hgu[:, ff:], -SWIGLU_LIMIT, SWIGLU_LIMIT)
            act = (up + 1.0) * gate * _sigmoid(SWIGLU_ALPHA * gate)
            ybuf[lax.rem(g, 2)] = _pack_bf16_pairs(
                jnp.dot(act.astype(BF16), wd_bf[...], preferred_element_type=F32) + bd_ref[0])
            y_copy(g).start()
            return carry

        lax.fori_loop(0, n, block, 0)

        for p in range(n_pieces):
            @pl.when(jnp.logical_and(p >= n, nxt >= 0))
            def _():
                w_piece(nxt, 1 - ws, p).start()

    @pl.when(e == pl.num_programs(0) - 1)
    def _():
        for back in (2, 1):
            @pl.when(used >= back)
            def _():
                y_copy(used - back).wait()
        ybuf[0] = jnp.zeros(ybuf.shape[1:], ybuf.dtype)

        def zero_copy(b):
            return pltpu.make_async_copy(ybuf.at[0], y_hbm.at[block_rows(b)], zsem)

        def start(b, carry):
            zero_copy(b).start()
            return carry

        def wait(b, carry):
            zero_copy(b).wait()
            return carry

        lax.fori_loop(used_ref[0], n_blocks, start, 0)
        lax.fori_loop(used_ref[0], n_blocks, wait, 0)


def expert_ffn(xs, expert_tables, w_gate_up, b_gate_up, w_down, b_down):
    n_rows = xs.shape[0]
    ne, dm, ff2 = w_gate_up.shape
    bm = FFN_BLOCK
    grid_spec = pltpu.PrefetchScalarGridSpec(
        num_scalar_prefetch=6, grid=(ne,),
        in_specs=[pl.BlockSpec(memory_space=pl.ANY),
                  pl.BlockSpec(memory_space=pl.ANY),
                  pl.BlockSpec((1, 1, ff2), lambda e, *_: (e, 0, 0)),
                  pl.BlockSpec(memory_space=pl.ANY),
                  pl.BlockSpec((1, 1, dm), lambda e, *_: (e, 0, 0))],
        out_specs=pl.BlockSpec(memory_space=pl.ANY),
        scratch_shapes=[pltpu.VMEM((2, dm, ff2), F32), pltpu.VMEM((2, ff2 // 2, dm), F32),
                        pltpu.VMEM((dm, ff2), BF16), pltpu.VMEM((ff2 // 2, dm), BF16),
                        pltpu.VMEM((3, bm, dm // 2), jnp.int32), pltpu.VMEM((2, bm, dm // 2), jnp.int32),
                        pltpu.SemaphoreType.DMA((3,)), pltpu.SemaphoreType.DMA((2,)),
                        pltpu.SemaphoreType.DMA((2, sum(FFN_W_PIECES))), pltpu.SemaphoreType.DMA(())])
    return pl.pallas_call(
        functools.partial(_ffn_kernel, bm=bm), grid_spec=grid_spec,
        out_shape=jax.ShapeDtypeStruct((n_rows, dm // 2), jnp.int32),
        compiler_params=_params(), name="expert_ffn",
    )(*expert_tables, xs, w_gate_up, b_gate_up.reshape(ne, 1, ff2), w_down, b_down.reshape(ne, 1, dm))


def _combine_kernel(dst_ref, tch_ref,
                    crow_ref, gw_ref, hp_ref, hs_ref, fn_ref, yr_ref, yp_ref, ys_ref, ybuf, sems,
                    *, prompt_tiles):
    i = pl.program_id(0)
    nt = pl.num_programs(0)
    slot = lax.rem(i, 2)
    tt = hp_ref.shape[0]
    cr = ybuf.shape[1]
    ra = ROW_ALIGN

    def chunk_copy(sl, src_row, dst_row, rows):
        return pltpu.make_async_copy(yr_ref.at[pl.ds(src_row, rows)], ybuf.at[sl, pl.ds(dst_row, rows)],
                                     sems.at[sl])

    def fetch(tile, sl):
        chunks_per_tile = cr // ra
        _chunk_loop(tch_ref[tile], lambda c: chunk_copy(
            sl, pl.multiple_of(dst_ref[tile * chunks_per_tile + c], ra), pl.multiple_of(c * ra, ra), ra).start())

    @pl.when(i == 0)
    def _():
        ybuf[...] = jnp.zeros_like(ybuf)
        fetch(0, 0)

    @pl.when(i + 1 < nt)
    def _():
        fetch(i + 1, 1 - slot)

    _split_count(tch_ref[i], lambda j, m: chunk_copy(slot, 0, 0, ra * m).wait())

    y_lo, y_hi = _unpack_bf16_pairs(ybuf[slot])
    cols = lax.broadcasted_iota(jnp.int32, (tt, cr), 1)
    q = jnp.zeros((tt, cr), F32)
    for k in range(TOP_K):
        q = jnp.where(cols == crow_ref[:, k:k + 1], gw_ref[:, k:k + 1], q)
    qb = q.astype(BF16)
    moe = jnp.concatenate([jnp.dot(qb, y_lo, preferred_element_type=F32),
                           jnp.dot(qb, y_hi, preferred_element_type=F32)], axis=1)
    h = jnp.where(i < prompt_tiles, hp_ref[...], hs_ref[...])
    out = _rms(h + moe, fn_ref[...])

    @pl.when(i < prompt_tiles)
    def _():
        yp_ref[...] = out

    @pl.when(i >= prompt_tiles)
    def _():
        ys_ref[...] = out


def combine(crow_t, gw_t, h_p, h_s, final_norm, y_rows, tables, ne):
    dm = h_p.shape[1]
    tt = ROW_TILE
    npt, nst = h_p.shape[0] // tt, h_s.shape[0] // tt
    cr = _compact_rows(ne)
    p_map = lambda i, *_: (jnp.minimum(i, npt - 1), 0)
    s_map = lambda i, *_: (jnp.maximum(i - npt, 0), 0)
    grid_spec = pltpu.PrefetchScalarGridSpec(
        num_scalar_prefetch=2, grid=(npt + nst,),
        in_specs=[pl.BlockSpec((tt, TOP_K), lambda i, *_: (i, 0)),
                  pl.BlockSpec((tt, TOP_K), lambda i, *_: (i, 0)),
                  pl.BlockSpec((tt, dm), p_map), pl.BlockSpec((tt, dm), s_map),
                  pl.BlockSpec((1, dm), lambda i, *_: (0, 0)),
                  pl.BlockSpec(memory_space=pl.ANY)],
        out_specs=[pl.BlockSpec((tt, dm), p_map), pl.BlockSpec((tt, dm), s_map)],
        scratch_shapes=[pltpu.VMEM((2, cr, dm // 2), jnp.int32), pltpu.SemaphoreType.DMA((2,))])
    return pl.pallas_call(
        functools.partial(_combine_kernel, prompt_tiles=npt), grid_spec=grid_spec,
        out_shape=[jax.ShapeDtypeStruct(h_p.shape, F32), jax.ShapeDtypeStruct(h_s.shape, F32)],
        compiler_params=_params(), name="combine",
    )(*tables, crow_t, gw_t, h_p, h_s, final_norm.reshape(1, dm), y_rows)


def _compact_rows(ne):
    return -(-(TOP_K * ROW_TILE + ne * (ROW_ALIGN - 1)) // 128) * 128


def moe_and_final_norm(h_p, h_s, norm_ffn, w_router, b_router, w_gate_up, b_gate_up, w_down, b_down, final_norm):
    ne = w_router.shape[1]
    bm = FFN_BLOCK
    ra = ROW_ALIGN
    xn, gw, crow, cnt = router(h_p, h_s, norm_ffn, w_router, b_router)
    t = xn.shape[0]
    nt = t // ROW_TILE
    seg = -(-cnt[:, 0].astype(jnp.int32).reshape(nt, ne) // ra) * ra
    seg_before = jnp.cumsum(seg, axis=0) - seg
    rows_e = jnp.sum(seg, axis=0)
    padded = -(-rows_e // bm) * bm
    pad_ends = jnp.cumsum(padded)
    pad_starts = pad_ends - padded
    n_blocks = -(-(t * TOP_K + nt * ne * (ra - 1) + ne * (bm - 1)) // bm)
    n_used = pad_ends[-1] // bm
    experts = jnp.arange(ne, dtype=jnp.int32)
    chunks = seg // ra
    chunk_end = jnp.cumsum(chunks, axis=1)
    c_ids = jnp.arange(_compact_rows(ne) // ra, dtype=jnp.int32)
    owner = jnp.sum((chunk_end[:, None, :] <= c_ids[None, :, None]).astype(jnp.int32), axis=2)
    seg_shift = pad_starts[None, :] + seg_before - ra * (chunk_end - chunks)
    dst = ra * c_ids[None, :] + jnp.sum(
        jnp.where(owner[:, :, None] == experts[None, None, :], seg_shift[:, None, :], 0), axis=2)
    seg_tables = (dst.reshape(-1), chunk_end[:, -1])
    fill_tables = (pad_starts + rows_e, (padded - rows_e) // ra, n_used.reshape(1))
    to_i32 = lambda xs: tuple(x.astype(jnp.int32) for x in xs)
    xs = dispatch(xn, crow, to_i32(seg_tables + fill_tables), n_blocks * bm, ne)
    active = padded > 0
    later = jnp.where(active, experts, ne)
    next_active = jnp.concatenate([lax.cummin(later, reverse=True)[1:], jnp.full((1,), ne, jnp.int32)])
    expert_tables = (pad_starts // bm, padded // bm, n_used.reshape(1),
                     jnp.where(next_active < ne, next_active, -1),
                     (jnp.cumsum(active.astype(jnp.int32)) - active.astype(jnp.int32)) % 2,
                     jnp.min(later).reshape(1))
    y_rows = expert_ffn(xs, to_i32(expert_tables), w_gate_up, b_gate_up, w_down, b_down)
    return combine(crow.T, gw.T, h_p, h_s, final_norm, y_rows, to_i32(seg_tables), ne)


def kernel(x_prompt, x_sample, cache_mem_k, cache_mem_v, state_ret, state_ssm_re, state_ssm_im, mem_prompt, norm_mix, w_in, ret_gn, w_ret_o, ssm_lam_re, ssm_lam_im, ssm_log_dt, ssm_b_re, ssm_b_im, ssm_c_re, ssm_c_im, ssm_d, w_ssm_glu, w_ssm_o, mem_norm, w_mem_kv, w_x_o, w_out, norm_ffn, w_router, b_router, w_gate_up, b_gate_up, w_down, b_down, final_norm):
    assert norm_mix.shape[0] == 1, "single-layer step"
    bp, lp, dm = x_prompt.shape
    bs, ls, _ = x_sample.shape
    n_mem = mem_prompt.shape[1]
    xw = X_HEADS * HEAD_DIM
    qk = RET_HEADS * HEAD_DIM
    sw = ssm_d.shape[1]
    g = ssm_lam_re.shape[1]

    w_in_b = w_in[0].astype(BF16)
    tables = s5_tables(ssm_lam_re[0], ssm_lam_im[0], ssm_log_dt[0], ssm_b_re[0], ssm_b_im[0],
                       ssm_c_re[0], ssm_c_im[0], ssm_d[0])
    mix_w = (ret_gn[0], w_ret_o[0].astype(BF16), w_ssm_glu[0].astype(BF16), w_ssm_o[0].astype(BF16),
             w_x_o[0].astype(BF16), w_out[0].astype(BF16))

    kv = norm_matmul(mem_prompt.reshape(bp * n_mem, dm), mem_norm[0], w_mem_kv[0].astype(BF16), F32)
    mk_p = kv[:, :xw].reshape(bp, n_mem * X_HEADS, HEAD_DIM)
    mv_p = kv[:, xw:].reshape(bp, n_mem * X_HEADS, HEAD_DIM)

    def group(x, pos, mem_k, mem_v, s_ret, h_re, h_im, nb, tl):
        bsz, length, _ = x.shape
        z, u = norm_matmul(x.reshape(bsz * length, dm), norm_mix[0], w_in_b, BF16,
                           f32_cols=(4 * qk, 4 * qk + sw),
                           acts=((3 * qk, 4 * qk, "silu"), (4 * qk + sw + xw, w_in_b.shape[1], "sigmoid")))
        y, hf_re, hf_im = s5_apply(u, bsz, h_re, h_im, tables)
        h, s_new = mixer(x, z, y, pos, mem_k, mem_v, s_ret, *mix_w, nb=nb, tl=tl)
        return h, s_new, hf_re, hf_im

    zero_ret = jnp.zeros((bp, RET_HEADS, HEAD_DIM, HEAD_DIM), F32)
    zero_ssm = jnp.zeros((bp, g, SSM_STATE), F32)
    h_p, ret_p, sre_p, sim_p = group(x_prompt, jnp.arange(lp, dtype=jnp.int32), mk_p, mv_p,
                                     zero_ret, zero_ssm, zero_ssm, 1, ROW_TILE)
    h_s, ret_s, sre_s, sim_s = group(x_sample, PAST_LEN + jnp.arange(ls, dtype=jnp.int32),
                                     cache_mem_k[0].reshape(bs, n_mem * X_HEADS, HEAD_DIM),
                                     cache_mem_v[0].reshape(bs, n_mem * X_HEADS, HEAD_DIM),
                                     state_ret[0], state_ssm_re[0], state_ssm_im[0], ROW_TILE // ls, ls)

    y_p, y_s = moe_and_final_norm(h_p, h_s, norm_ffn[0], w_router[0], b_router[0],
                                  w_gate_up[0], b_gate_up[0], w_down[0], b_down[0], final_norm)
    return (y_p.reshape(bp, lp, dm), y_s.reshape(bs, ls, dm), ret_p[None], sre_p[None], sim_p[None],
            mk_p.reshape(1, bp, n_mem, X_HEADS, HEAD_DIM), mv_p.reshape(1, bp, n_mem, X_HEADS, HEAD_DIM),
            ret_s[None], sre_s[None], sim_s[None])
```

```python
import functools
import math

import jax
import jax.numpy as jnp
import numpy as np
from jax import lax
from jax.experimental import pallas as pl
from jax.experimental.pallas import tpu as pltpu

F32 = jnp.float32
BF16 = jnp.bfloat16

EPS = 1e-6
CHUNK = 64
PAST_LEN = 2048
ROPE_BASE = 10000.0
RET_HEADS = 4
X_HEADS = 4
HEAD_DIM = 128
SSM_GROUP = 16
SSM_STATE = 64
TOP_K = 4
SWIGLU_ALPHA = 1.702
SWIGLU_LIMIT = 7.0

VMEM_LIMIT = 52 * 1024 * 1024
S5_CHUNK = 8
S5_LANES = 128
ROW_TILE = 256
FFN_BLOCK = 256
ROW_ALIGN = 8
MIX_COLS = 512
NT_DIMS = (((1,), (1,)), ((), ()))
TN_DIMS = (((0,), (0,)), ((), ()))


def _params(n_axes=1):
    return pltpu.CompilerParams(dimension_semantics=("arbitrary",) * n_axes,
                                vmem_limit_bytes=VMEM_LIMIT)


def _resident(shape):
    nd = len(shape)
    return pl.BlockSpec(shape, lambda *_: (0,) * nd, pipeline_mode=pl.Buffered(1))


def _rms(x, w):
    return x * lax.rsqrt(jnp.mean(x * x, axis=-1, keepdims=True) + EPS) * w


def _sigmoid(x):
    return 0.5 * jnp.tanh(0.5 * x) + 0.5


_ACTIVATIONS = {"sigmoid": _sigmoid, "silu": lambda v: v * _sigmoid(v)}


def _norm_matmul_kernel(x_ref, nw_ref, w_ref, o_ref, *f32_refs, n_chunk, f32_cols, acts):
    xb = _rms(x_ref[...], nw_ref[...]).astype(BF16)
    for n0 in range(0, o_ref.shape[1], n_chunk):
        r = jnp.dot(xb, w_ref[:, n0:n0 + n_chunk], preferred_element_type=F32)
        if f32_cols is not None and n0 <= f32_cols[0] and f32_cols[1] <= n0 + n_chunk:
            f32_refs[0][...] = r[:, f32_cols[0] - n0:f32_cols[1] - n0]
        cuts = sorted({n0, n0 + n_chunk} | {c for lo, hi, _ in acts for c in (lo, hi) if n0 < c < n0 + n_chunk})
        for a, b in zip(cuts[:-1], cuts[1:]):
            piece = r[:, a - n0:b - n0]
            for lo, hi, kind in acts:
                if lo <= a and b <= hi:
                    piece = _ACTIVATIONS[kind](piece)
            o_ref[:, a:b] = piece.astype(o_ref.dtype)


def norm_matmul(x, nw, w, out_dtype, f32_cols=None, acts=()):
    t, d = x.shape
    n = w.shape[1]
    n_chunk = min(n, 1024)
    out_specs = [pl.BlockSpec((ROW_TILE, n), lambda i: (i, 0))]
    out_shape = [jax.ShapeDtypeStruct((t, n), out_dtype)]
    if f32_cols is not None:
        lo, hi = f32_cols
        assert lo // n_chunk == (hi - 1) // n_chunk
        out_specs.append(pl.BlockSpec((ROW_TILE, hi - lo), lambda i: (i, 0)))
        out_shape.append(jax.ShapeDtypeStruct((t, hi - lo), F32))
    out = pl.pallas_call(
        functools.partial(_norm_matmul_kernel, n_chunk=n_chunk, f32_cols=f32_cols, acts=tuple(acts)),
        grid=(t // ROW_TILE,),
        in_specs=[pl.BlockSpec((ROW_TILE, d), lambda i: (i, 0)), _resident((1, d)), _resident((d, n))],
        out_specs=out_specs, out_shape=out_shape,
        compiler_params=_params(), name="norm_matmul",
    )(x, nw.reshape(1, d), w)
    return out if f32_cols is not None else out[0]


def s5_tables(lam_re, lam_im, log_dt, b_re, b_im, c_re, c_im, d_skip):
    g, n, p = b_re.shape
    s = S5_CHUNK
    gl = S5_LANES // p
    j = g // gl
    dt = jnp.exp(log_dt)[:, None]
    a_re = jnp.exp(lam_re * dt) * jnp.cos(lam_im * dt)
    a_im = jnp.exp(lam_re * dt) * jnp.sin(lam_im * dt)
    den = lam_re * lam_re + lam_im * lam_im
    nr, ni = a_re - 1.0, a_im
    co_re = (nr * lam_re + ni * lam_im) / den
    co_im = (ni * lam_re - nr * lam_im) / den
    bb_re = co_re[..., None] * b_re - co_im[..., None] * b_im
    bb_im = co_re[..., None] * b_im + co_im[..., None] * b_re
    tau = jnp.arange(s + 1, dtype=F32)[:, None, None]
    pw_mag = jnp.exp(lam_re * dt * tau)
    pw_re = pw_mag * jnp.cos(lam_im * dt * tau)
    pw_im = pw_mag * jnp.sin(lam_im * dt * tau)
    ca_re = c_re[None] * pw_re[:, :, None, :] - c_im[None] * pw_im[:, :, None, :]
    ca_im = c_re[None] * pw_im[:, :, None, :] + c_im[None] * pw_re[:, :, None, :]
    bq_re, bq_im = bb_re.transpose(0, 2, 1), bb_im.transpose(0, 2, 1)
    kq = jnp.sum(ca_re[:s, :, :, None, :] * bq_re[None, :, None, :, :]
                 - ca_im[:s, :, :, None, :] * bq_im[None, :, None, :, :], axis=-1).transpose(1, 0, 3, 2)
    fl = s * gl * p
    c_io = np.arange(fl)
    c_st = np.arange(2 * gl * n)
    grp_io = (c_io // p) % gl
    grp_st = (c_st // n) % gl
    ts = np.arange(s)
    to_io = ((ts[:, None, None] == c_io[None, None, :] // (gl * p))
             & (np.arange(p)[None, :, None] == c_io[None, None, :] % p))
    lagged = np.stack([(ts[:, None, None] == (c_io[None, None, :] // (gl * p)) - s_in)
                       & (np.arange(p)[None, :, None] == c_io[None, None, :] % p)
                       for s_in in range(s)], axis=2)
    to_st = ((np.arange(2)[:, None, None] == c_st[None, None, :] // (gl * n))
             & (np.arange(n)[None, :, None] == c_st[None, None, :] % n))

    def select(spec, src, sel):
        return jnp.einsum(spec, src.astype(BF16), jnp.asarray(sel, BF16), preferred_element_type=F32)

    def own_group(full, row_grp, col_grp):
        return jnp.where(jnp.asarray(row_grp[:, None] == col_grp[None, :]), full, 0.0).astype(BF16)

    m = select('gxqp,xpsc->sgqc', kq, lagged).reshape(s, j, gl * p, fl).transpose(1, 0, 2, 3)
    m = own_group(m.reshape(j, fl, fl), grp_io, grp_io)
    v = jnp.stack([select('tgpn,tpc->gnc', ca_re[1:], to_io), select('tgpn,tpc->gnc', -ca_im[1:], to_io)])
    v = own_group(v.reshape(2, j, gl * n, fl).transpose(1, 0, 2, 3).reshape(j, 2 * gl * n, fl), grp_st, grp_io)
    rev = s - 1 - ts
    pr, pi = pw_re[rev][:, :, None, :], pw_im[rev][:, :, None, :]
    w_src = jnp.stack([pr * bq_re[None] - pi * bq_im[None], pr * bq_im[None] + pi * bq_re[None]], axis=3)
    w = select('sgqpn,pnc->sgqc', w_src, to_st).reshape(s, j, gl * p, 2 * gl * n).transpose(1, 0, 2, 3)
    w = own_group(w.reshape(j, fl, 2 * gl * n), grp_io, grp_st)
    a_s_re = pw_re[s].reshape(1, g * n)
    a_s_im = pw_im[s].reshape(1, g * n)
    dtab = jnp.broadcast_to(d_skip.reshape(j, 1, 1, gl * p), (j, 1, s, gl * p)).reshape(j, 1, s * gl * p)
    return jnp.concatenate([m, v], axis=1), w, a_s_re, a_s_im, dtab


def _s5_flat(u_ref):
    return jnp.concatenate([u_ref[:, t, :] for t in range(u_ref.shape[1])], axis=1)


def _s5a_kernel(u_ref, w_ref, ire_ref, iim_ref):
    r = jnp.dot(_s5_flat(u_ref).astype(BF16), w_ref[0], preferred_element_type=F32)
    half = r.shape[1] // 2
    ire_ref[...] = r[:, :half]
    iim_ref[...] = r[:, half:]


def _s5scan_kernel(ire_ref, iim_ref, ar_ref, ai_ref, h0r_ref, h0i_ref,
                   hpr_ref, hpi_ref, hfr_ref, hfi_ref):
    nb, nc, _ = ire_ref.shape
    ar, ai = ar_ref[...], ai_ref[...]

    def body(c, carry):
        out = []
        for b in range(nb):
            hr, hi = carry[2 * b], carry[2 * b + 1]
            hpr_ref[b, pl.ds(c, 1), :] = hr
            hpi_ref[b, pl.ds(c, 1), :] = hi
            out.append(ar * hr - ai * hi + ire_ref[b, pl.ds(c, 1), :])
            out.append(ar * hi + ai * hr + iim_ref[b, pl.ds(c, 1), :])
        return tuple(out)

    init = []
    for b in range(nb):
        init += [h0r_ref[b], h0i_ref[b]]
    fin = lax.fori_loop(0, nc, body, tuple(init))
    for b in range(nb):
        hfr_ref[b] = fin[2 * b]
        hfi_ref[b] = fin[2 * b + 1]


def _s5b_kernel(u_ref, hpr_ref, hpi_ref, mv_ref, d_ref, y_ref):
    uf = _s5_flat(u_ref)
    lhs = jnp.concatenate([uf.astype(BF16), hpr_ref[...].astype(BF16), hpi_ref[...].astype(BF16)], axis=1)
    y = d_ref[0] * uf + jnp.dot(lhs, mv_ref[0], preferred_element_type=F32)
    lanes = y_ref.shape[2]
    for t in range(y_ref.shape[1]):
        y_ref[:, t, :] = y[:, t * lanes:(t + 1) * lanes]


def s5_apply(u, bsz, h0_re, h0_im, tables):
    mv, w, a_re, a_im, dtab = tables
    tokens, width = u.shape
    nj = w.shape[0]
    s = S5_CHUNK
    rows = tokens // s
    nc = rows // bsz
    lanes = a_re.shape[1]
    half = w.shape[2] // 2
    fl = w.shape[1]
    rt = min(rows, 1024)
    u3 = u.reshape(rows, s, width)
    u_spec = pl.BlockSpec((rt, s, S5_LANES), lambda j, r: (r, 0, j))
    st_spec = pl.BlockSpec((rt, half), lambda j, r: (r, j))
    tab_spec = pl.BlockSpec((1, fl, fl), lambda j, r: (j, 0, 0))
    inj_re, inj_im = pl.pallas_call(
        _s5a_kernel, grid=(nj, rows // rt),
        in_specs=[u_spec, tab_spec],
        out_specs=[st_spec, st_spec],
        out_shape=[jax.ShapeDtypeStruct((rows, lanes), F32)] * 2,
        compiler_params=_params(2), name="s5_chunk_in",
    )(u3, w)

    sb, lw = 4, (512 if nc > 16 else lanes)
    seq_spec = pl.BlockSpec((sb, nc, lw), lambda b, l: (b, 0, l))
    vec_spec = pl.BlockSpec((sb, 1, lw), lambda b, l: (b, 0, l))
    atab_spec = pl.BlockSpec((1, lw), lambda b, l: (0, l))
    hp_re, hp_im, hf_re, hf_im = pl.pallas_call(
        _s5scan_kernel, grid=(bsz // sb, lanes // lw),
        in_specs=[seq_spec, seq_spec, atab_spec, atab_spec, vec_spec, vec_spec],
        out_specs=[seq_spec, seq_spec, vec_spec, vec_spec],
        out_shape=[jax.ShapeDtypeStruct((bsz, nc, lanes), F32)] * 2
        + [jax.ShapeDtypeStruct((bsz, 1, lanes), F32)] * 2,
        compiler_params=_params(2), name="s5_scan",
    )(inj_re.reshape(bsz, nc, lanes), inj_im.reshape(bsz, nc, lanes), a_re, a_im,
      h0_re.reshape(bsz, 1, lanes), h0_im.reshape(bsz, 1, lanes))

    y3 = pl.pallas_call(
        _s5b_kernel, grid=(nj, rows // rt),
        in_specs=[u_spec, st_spec, st_spec, pl.BlockSpec((1, 2 * fl, fl), lambda j, r: (j, 0, 0)),
                  pl.BlockSpec((1, 1, fl), lambda j, r: (j, 0, 0))],
        out_specs=u_spec,
        out_shape=jax.ShapeDtypeStruct((rows, s, width), F32),
        compiler_params=_params(2), name="s5_chunk_out",
    )(u3, hp_re.reshape(rows, lanes), hp_im.reshape(rows, lanes), mv, dtab)
    g = lanes // SSM_STATE
    return y3.reshape(tokens, width), hf_re.reshape(bsz, g, SSM_STATE), hf_im.reshape(bsz, g, SSM_STATE)


def _retention_gammas():
    return 1.0 - np.exp2(-5.0 - np.arange(RET_HEADS, dtype=np.float64))


def retention_tables(tile, chunk, nb):
    gam = _retention_gammas()[:, None, None]
    i = np.arange(tile)[:, None]
    j = np.arange(tile)[None, :]
    same = (i // chunk) == (j // chunk)
    earlier = (j // chunk) < (i // chunk)
    dist = np.where(same, np.abs(i - j), np.where(earlier, i - j, 0))
    dmask = np.where(same | earlier, gam ** dist[None], 0.0)
    qw = np.broadcast_to((gam[:, :, 0] ** (np.arange(tile) + 1.0))[:, :, None], (RET_HEADS, tile, HEAD_DIM))
    kw = np.broadcast_to((gam[:, :, 0] ** (tile - 1.0 - np.arange(tile)))[:, :, None], (RET_HEADS, tile, HEAD_DIM))
    dmask = np.stack([np.kron(np.eye(nb), m) for m in dmask])
    return (jnp.asarray(dmask, F32), jnp.asarray(np.tile(qw, (1, nb, 1)), F32),
            jnp.asarray(np.tile(kw, (1, nb, 1)), F32), tuple(float(x) for x in _retention_gammas() ** tile))


def rope_tables(pos):
    half = HEAD_DIM // 2
    inv = jnp.exp(-math.log(ROPE_BASE) * 2.0 * jnp.arange(half, dtype=F32) / HEAD_DIM)
    ang = pos.astype(F32)[:, None] * inv[None, :]
    cos, sin = jnp.cos(ang), jnp.sin(ang)
    cosf = jnp.concatenate([cos, cos], axis=1)
    sinf = jnp.concatenate([-sin, sin], axis=1)
    return cosf, sinf


def _mixer_kernel(x_ref, zq_ref, xq_ref, gl_ref, y_ref, cq_ref, sq_ref, ck_ref, sk_ref,
                  dm_ref, qw_ref, kw_ref, vm_ref, am_ref, mk_ref, mv_ref, s0_ref, gn_ref,
                  wro_ref, wglu_ref, wso_ref, wxo_ref, wout_ref,
                  h_ref, sout_ref, s_scr, o_scr, xo_scr, glu_scr, mg_scr, *, nb, tl, tile_decay):
    hd = HEAD_DIM
    qk = RET_HEADS * hd
    n_mem = mk_ref.shape[1] // X_HEADS

    def own_blocks(a, width):
        if nb == 1:
            return a
        return jnp.concatenate([a[n * tl:(n + 1) * tl, n * width:(n + 1) * width] for n in range(nb)], axis=0)

    def spread_blocks(a, mask_ref):
        if nb == 1:
            return a
        return jnp.concatenate([a] * nb, axis=1) * mask_ref[...]

    @pl.when(pl.program_id(1) == 0)
    def _():
        for n in range(nb):
            for h in range(RET_HEADS):
                s_scr[h, :, n * hd:(n + 1) * hd] = s0_ref[n, h]

    cq, sq, ck, sk = cq_ref[...], sq_ref[...], ck_ref[...], sk_ref[...]
    for h in range(RET_HEADS):
        c0 = h * hd
        q = zq_ref[:, c0:c0 + hd].astype(F32)
        k = zq_ref[:, qk + c0:qk + c0 + hd].astype(F32)
        v = zq_ref[:, 2 * qk + c0:2 * qk + c0 + hd]
        g = zq_ref[:, 3 * qk + c0:3 * qk + c0 + hd].astype(F32)
        qr = q * cq + pltpu.roll(q, hd // 2, 1) * sq
        kr = k * ck + pltpu.roll(k, hd // 2, 1) * sk
        sc = lax.dot_general(qr.astype(BF16), kr.astype(BF16), NT_DIMS,
                             preferred_element_type=F32) * dm_ref[h]
        o = jnp.dot(sc.astype(BF16), v, preferred_element_type=F32)
        s_old = s_scr[h]
        o += own_blocks(jnp.dot((qr * qw_ref[h]).astype(BF16), s_old.astype(BF16),
                                preferred_element_type=F32), hd)
        kv = lax.dot_general((kr * kw_ref[h]).astype(BF16), spread_blocks(v, vm_ref), TN_DIMS,
                             preferred_element_type=F32)
        s_scr[h] = tile_decay[h] * s_old + kv
        d = o - jnp.mean(o, axis=-1, keepdims=True)
        on = d * lax.rsqrt(jnp.mean(d * d, axis=-1, keepdims=True) + EPS) * gn_ref[:, c0:c0 + hd]
        o_scr[:, c0:c0 + hd] = (on * g).astype(BF16)
        mem_rows = pl.ds(h, n_mem, stride=X_HEADS)
        mkh = mk_ref[:, mem_rows, :].astype(BF16).reshape(nb * n_mem, hd)
        mvh = mv_ref[:, mem_rows, :].astype(BF16).reshape(nb * n_mem, hd)
        s = own_blocks(lax.dot_general(xq_ref[:, c0:c0 + hd], mkh, NT_DIMS,
                                       preferred_element_type=F32), n_mem) * (hd ** -0.5)
        e = jnp.exp(s - jnp.max(s, axis=-1, keepdims=True))
        p = (e / jnp.sum(e, axis=-1, keepdims=True)).astype(BF16)
        xo_scr[:, c0:c0 + hd] = jnp.dot(spread_blocks(p, am_ref), mvh,
                                        preferred_element_type=F32).astype(BF16)

    cw = MIX_COLS
    dm = h_ref.shape[1]
    half = wglu_ref.shape[1] // 2
    yb = jax.nn.gelu(y_ref[...]).astype(BF16)
    for c0 in range(0, half, cw):
        ga = jnp.dot(yb, wglu_ref[:, c0:c0 + cw], preferred_element_type=F32)
        gb = jnp.dot(yb, wglu_ref[:, half + c0:half + c0 + cw], preferred_element_type=F32)
        glu_scr[:, c0:c0 + cw] = (ga * _sigmoid(gb)).astype(BF16)
    for c0 in range(0, dm, cw):
        cols = slice(c0, c0 + cw)
        ret = jnp.dot(o_scr[...], wro_ref[:, cols], preferred_element_type=F32)
        ssm = jnp.dot(glu_scr[...], wso_ref[:, cols], preferred_element_type=F32)
        xb = jnp.dot(xo_scr[...], wxo_ref[:, cols], preferred_element_type=F32)
        merged = (gl_ref[:, c0:c0 + cw].astype(F32) * ret
                  + gl_ref[:, dm + c0:dm + c0 + cw].astype(F32) * ssm
                  + gl_ref[:, 2 * dm + c0:2 * dm + c0 + cw].astype(F32) * xb)
        mg_scr[:, cols] = merged.astype(BF16)
    for c0 in range(0, dm, cw):
        cols = slice(c0, c0 + cw)
        h_ref[:, cols] = x_ref[:, cols] + jnp.dot(mg_scr[...], wout_ref[:, cols], preferred_element_type=F32)
    for n in range(nb):
        for h in range(RET_HEADS):
            sout_ref[n, h] = s_scr[h, :, n * hd:(n + 1) * hd]


def mixer(x, z, y_ssm, pos, mem_k, mem_v, s0, ret_gn, w_ret_o, w_ssm_glu, w_ssm_o, w_x_o, w_out, *, nb, tl):
    bsz, length, dm = x.shape
    chunk = min(CHUNK, length)
    nl = length // tl
    rows = nb * tl
    qk = RET_HEADS * HEAD_DIM
    sw = y_ssm.shape[1]
    xw = X_HEADS * HEAD_DIM
    gate_col = (4 * qk + sw + xw)
    assert gate_col % (3 * dm) == 0 and (4 * qk + sw) % xw == 0
    dmask, qw, kw, tile_decay = retention_tables(tl, chunk, nb)
    seq_of_row = np.arange(rows)[:, None] // tl
    own_v = jnp.asarray(seq_of_row == np.arange(nb * HEAD_DIM)[None, :] // HEAD_DIM, BF16)
    n_mem = mem_k.shape[1] // X_HEADS
    own_mem = jnp.asarray(seq_of_row == np.arange(nb * n_mem)[None, :] // n_mem, BF16)
    cosf, sinf = (jnp.broadcast_to(t.reshape(nl, 1, tl, HEAD_DIM), (nl, nb, tl, HEAD_DIM)).reshape(nl * rows, HEAD_DIM)
                  for t in rope_tables(pos))
    scale = HEAD_DIM ** -0.5
    row_map = lambda b, l: (b * nl + l, 0)
    tab_map = lambda b, l: (l, 0)
    st_spec = pl.BlockSpec((nb, RET_HEADS, HEAD_DIM, HEAD_DIM), lambda b, l: (b, 0, 0, 0))
    mem_spec = pl.BlockSpec((nb,) + mem_k.shape[1:], lambda b, l: (b,) + (0,) * (mem_k.ndim - 1))
    h, s_out = pl.pallas_call(
        functools.partial(_mixer_kernel, nb=nb, tl=tl, tile_decay=tile_decay),
        grid=(bsz // nb, nl),
        in_specs=[pl.BlockSpec((rows, dm), row_map),
                  pl.BlockSpec((rows, 4 * qk), row_map),
                  pl.BlockSpec((rows, xw), lambda b, l: (b * nl + l, (4 * qk + sw) // xw)),
                  pl.BlockSpec((rows, 3 * dm), lambda b, l: (b * nl + l, gate_col // (3 * dm))),
                  pl.BlockSpec((rows, sw), row_map),
                  pl.BlockSpec((rows, HEAD_DIM), tab_map), pl.BlockSpec((rows, HEAD_DIM), tab_map),
                  pl.BlockSpec((rows, HEAD_DIM), tab_map), pl.BlockSpec((rows, HEAD_DIM), tab_map),
                  _resident(dmask.shape), _resident(qw.shape), _resident(kw.shape),
                  _resident(own_v.shape), _resident(own_mem.shape),
                  mem_spec, mem_spec, st_spec, _resident((1, qk)),
                  _resident(w_ret_o.shape), _resident(w_ssm_glu.shape), _resident(w_ssm_o.shape),
                  _resident(w_x_o.shape), _resident(w_out.shape)],
        out_specs=[pl.BlockSpec((rows, dm), row_map), st_spec],
        out_shape=[jax.ShapeDtypeStruct((bsz * length, dm), F32),
                   jax.ShapeDtypeStruct(s0.shape, F32)],
        scratch_shapes=[pltpu.VMEM((RET_HEADS, HEAD_DIM, nb * HEAD_DIM), F32),
                        pltpu.VMEM((rows, qk), BF16), pltpu.VMEM((rows, xw), BF16),
                        pltpu.VMEM((rows, w_ssm_o.shape[0]), BF16), pltpu.VMEM((rows, dm), BF16)],
        compiler_params=_params(2), name="mixer",
    )(x.reshape(bsz * length, dm), z, z, z, y_ssm,
      cosf * scale, sinf * scale, cosf, sinf, dmask, qw, kw, own_v, own_mem,
      mem_k, mem_v, s0, ret_gn.reshape(1, qk), w_ret_o, w_ssm_glu, w_ssm_o, w_x_o, w_out)
    return h, s_out


def _router_kernel(hp_ref, hs_ref, nw_ref, wrt_ref, br_ref, tri_ref, low_ref,
                   xn_ref, gw_ref, crow_ref, cnt_ref, *, prompt_tiles):
    h = jnp.where(pl.program_id(0) < prompt_tiles, hp_ref[...], hs_ref[...])
    xn = _rms(h, nw_ref[...]).astype(BF16)
    xn_ref[...] = xn
    logits = lax.dot_general(wrt_ref[...], xn, NT_DIMS, preferred_element_type=F32) + br_ref[...]
    ne = logits.shape[0]
    iota = lax.broadcasted_iota(jnp.int32, logits.shape, 0)
    rest = logits
    sel = jnp.zeros(logits.shape, jnp.bool_)
    vals, idxs = [], []
    for _ in range(TOP_K):
        m = jnp.max(rest, axis=0, keepdims=True)
        ix = jnp.min(jnp.where(rest == m, iota, ne), axis=0, keepdims=True)
        hit = iota == ix
        vals.append(m)
        idxs.append(ix)
        sel = jnp.logical_or(sel, hit)
        rest = jnp.where(hit, -jnp.inf, rest)
    es = [jnp.exp(v - vals[0]) for v in vals]
    tot = es[0] + es[1] + es[2] + es[3]
    before = jnp.dot(sel.astype(BF16), tri_ref[...], preferred_element_type=F32)
    cnt = jnp.sum(sel.astype(F32), axis=1, keepdims=True)
    seg = jnp.floor((cnt + (ROW_ALIGN - 1.0)) * (1.0 / ROW_ALIGN)) * ROW_ALIGN
    start = jnp.dot(low_ref[...], jnp.broadcast_to(seg, before.shape), precision=lax.Precision.HIGHEST,
                    preferred_element_type=F32)
    place = start + before
    for k in range(TOP_K):
        gw_ref[k:k + 1, :] = es[k] / tot
        crow_ref[k:k + 1, :] = jnp.sum(jnp.where(iota == idxs[k], place, 0.0), axis=0,
                                       keepdims=True).astype(jnp.int32)
    cnt_ref[...] = jnp.broadcast_to(cnt, cnt_ref.shape)


def router(h_p, h_s, norm_ffn, w_router, b_router):
    dm = h_p.shape[1]
    ne = w_router.shape[1]
    tt = ROW_TILE
    npt, nst = h_p.shape[0] // tt, h_s.shape[0] // tt
    t = (npt + nst) * tt
    tri = jnp.asarray(np.triu(np.ones((tt, tt), np.float32), k=1), BF16)
    low = jnp.asarray(np.tril(np.ones((ne, ne), np.float32), k=-1))
    tok_spec = pl.BlockSpec((TOP_K, tt), lambda i: (0, i))
    return pl.pallas_call(
        functools.partial(_router_kernel, prompt_tiles=npt), grid=(npt + nst,),
        in_specs=[pl.BlockSpec((tt, dm), lambda i: (jnp.minimum(i, npt - 1), 0)),
                  pl.BlockSpec((tt, dm), lambda i: (jnp.maximum(i - npt, 0), 0)),
                  _resident((1, dm)), _resident((ne, dm)), _resident((ne, 1)), _resident((tt, tt)),
                  _resident((ne, ne))],
        out_specs=[pl.BlockSpec((tt, dm), lambda i: (i, 0)), tok_spec, tok_spec,
                   pl.BlockSpec((ne, 128), lambda i: (i, 0))],
        out_shape=[jax.ShapeDtypeStruct((t, dm), BF16),
                   jax.ShapeDtypeStruct((TOP_K, t), F32), jax.ShapeDtypeStruct((TOP_K, t), jnp.int32),
                   jax.ShapeDtypeStruct(((npt + nst) * ne, 128), F32)],
        compiler_params=_params(), name="router",
    )(h_p, h_s, norm_ffn.reshape(1, dm), w_router.T.astype(BF16), b_router.reshape(ne, 1), tri, low)


def _pack_bf16_pairs(x, exact=False):
    n = x.shape[1] // 2
    lo, hi = x[:, :n], x[:, n:]
    if not exact:
        lo, hi = lo.astype(BF16).astype(F32), hi.astype(BF16).astype(F32)
    lo = lax.bitcast_convert_type(lo, jnp.int32)
    hi = lax.bitcast_convert_type(hi, jnp.int32)
    return lax.shift_right_logical(lo, 16) | (hi & -65536)


def _unpack_bf16_pairs(u):
    lo = lax.bitcast_convert_type(lax.shift_left(u, 16), F32).astype(BF16)
    hi = lax.bitcast_convert_type(u & -65536, F32).astype(BF16)
    return lo, hi


def _split_count(n, fn):
    def quad(jq, carry):
        fn(4 * jq, 4)
        return carry

    lax.fori_loop(0, n // 4, quad, 0)

    @pl.when(n % 4 >= 2)
    def _():
        fn(n // 4 * 4, 2)

    @pl.when(n % 2 == 1)
    def _():
        fn(n // 2 * 2, 1)


def _chunk_loop(n, fn):
    def quad(jq, carry):
        for u in range(4):
            fn(4 * jq + u)
        return carry

    def single(j, carry):
        fn(j)
        return carry

    lax.fori_loop(0, n // 4, quad, 0)
    lax.fori_loop(n // 4 * 4, n, single, 0)


def _dispatch_kernel(dst_ref, tch_ref, zs_ref, zn_ref, used_ref,
                     xn_ref, crow_ref, xs_ref, cbuf, zbuf, sems, zsem, *, ne, bm):
    i = pl.program_id(0)
    nt = pl.num_programs(0)
    slot = lax.rem(i, 2)
    tt = xn_ref.shape[0]
    cr = cbuf.shape[1]
    ra = ROW_ALIGN
    n_blocks = xs_ref.shape[0] // bm

    def chunk_copy(sl, src_row, dst_row, rows):
        return pltpu.make_async_copy(cbuf.at[sl, pl.ds(src_row, rows)], xs_ref.at[pl.ds(dst_row, rows)],
                                     sems.at[sl])

    def wait_chunks(sl, n):
        _split_count(n, lambda j, m: chunk_copy(sl, 0, 0, ra * m).wait())

    def tail_copy(e, j):
        return pltpu.make_async_copy(zbuf.at[pl.ds(0, ra)],
                                     xs_ref.at[pl.ds(pl.multiple_of(zs_ref[e] + ra * j, ra), ra)], zsem)

    def block_copy(b):
        return pltpu.make_async_copy(zbuf, xs_ref.at[pl.ds(pl.multiple_of(b * bm, bm), bm)], zsem)

    def zero_fill(start):
        def per_expert(e, carry):
            def per_chunk(j, c2):
                (tail_copy(e, j).start() if start else tail_copy(e, j).wait())
                return c2
            lax.fori_loop(0, zn_ref[e], per_chunk, 0)
            return carry
        lax.fori_loop(0, ne, per_expert, 0)

        def per_block(b, carry):
            (block_copy(b).start() if start else block_copy(b).wait())
            return carry
        lax.fori_loop(used_ref[0], n_blocks, per_block, 0)

    @pl.when(i == 0)
    def _():
        zbuf[...] = jnp.zeros_like(zbuf)
        zero_fill(True)
        zero_fill(False)

    @pl.when(i >= 2)
    def _():
        wait_chunks(slot, tch_ref[jnp.maximum(i - 2, 0)])

    crow = crow_ref[...]
    rows = lax.broadcasted_iota(jnp.int32, (cr, tt), 0)
    hit = rows == crow[0:1, :]
    for k in range(1, TOP_K):
        hit = jnp.logical_or(hit, rows == crow[k:k + 1, :])
    packed = _pack_bf16_pairs(jnp.dot(jnp.where(hit, 1.0, 0.0).astype(BF16), xn_ref[...],
                                      preferred_element_type=F32), exact=True)
    cbuf[slot] = packed

    chunks_per_tile = cr // ra
    _chunk_loop(tch_ref[i], lambda c: chunk_copy(
        slot, pl.multiple_of(c * ra, ra), pl.multiple_of(dst_ref[i * chunks_per_tile + c], ra), ra).start())

    @pl.when(i == nt - 1)
    def _():
        wait_chunks(slot, tch_ref[i])
        wait_chunks(1 - slot, jnp.where(nt >= 2, tch_ref[jnp.maximum(i - 1, 0)], 0))


def dispatch(xn, crow, tables, n_rows, ne):
    t, dm = xn.shape
    tt = ROW_TILE
    bm = FFN_BLOCK
    cr = _compact_rows(ne)
    grid_spec = pltpu.PrefetchScalarGridSpec(
        num_scalar_prefetch=5, grid=(t // tt,),
        in_specs=[pl.BlockSpec((tt, dm), lambda i, *_: (i, 0)),
                  pl.BlockSpec((TOP_K, tt), lambda i, *_: (0, i))],
        out_specs=pl.BlockSpec(memory_space=pl.ANY),
        scratch_shapes=[pltpu.VMEM((2, cr, dm // 2), jnp.int32), pltpu.VMEM((bm, dm // 2), jnp.int32),
                        pltpu.SemaphoreType.DMA((2,)), pltpu.SemaphoreType.DMA(())])
    return pl.pallas_call(
        functools.partial(_dispatch_kernel, ne=ne, bm=bm), grid_spec=grid_spec,
        out_shape=jax.ShapeDtypeStruct((n_rows, dm // 2), jnp.int32),
        compiler_params=_params(), name="dispatch",
    )(*tables, xn, crow)


FFN_W_PIECES = (4, 2)


def _ffn_kernel(first_ref, nblk_ref, used_ref, next_ref, wslot_ref, lead_ref,
                xs_hbm, wgu_hbm, bgu_ref, wd_hbm, bd_ref, y_hbm,
                wgu_f32, wd_f32, wgu_bf, wd_bf, xbuf, ybuf, xsem, ysem, wsem, zsem, *, bm):
    e = pl.program_id(0)
    n = nblk_ref[e]
    b0 = first_ref[e]
    n_blocks = y_hbm.shape[0] // bm
    n_pieces = sum(FFN_W_PIECES)

    def block_rows(b):
        return pl.ds(pl.multiple_of(b * bm, bm), bm)

    used = used_ref[0]
    x_slots = xbuf.shape[0]

    def x_copy(g):
        sl = lax.rem(g, x_slots)
        return pltpu.make_async_copy(xs_hbm.at[block_rows(g)], xbuf.at[sl], xsem.at[sl])

    def y_copy(g):
        sl = lax.rem(g, 2)
        return pltpu.make_async_copy(ybuf.at[sl], y_hbm.at[block_rows(g)], ysem.at[sl])

    def w_piece(ex, sl, p):
        src, dst, q, parts = ((wgu_hbm, wgu_f32, p, FFN_W_PIECES[0]) if p < FFN_W_PIECES[0]
                              else (wd_hbm, wd_f32, p - FFN_W_PIECES[0], FFN_W_PIECES[1]))
        rows = src.shape[1] // parts
        return pltpu.make_async_copy(src.at[ex, pl.ds(q * rows, rows)], dst.at[sl, pl.ds(q * rows, rows)],
                                     wsem.at[sl, p])

    @pl.when(n > 0)
    def _():
        ws = wslot_ref[e]
        nxt = next_ref[e]

        @pl.when(lead_ref[0] == e)
        def _():
            for g in range(x_slots - 1):
                @pl.when(g < used)
                def _():
                    x_copy(g).start()
            for p in range(n_pieces):
                w_piece(e, ws, p).start()

        for p in range(n_pieces):
            w_piece(e, ws, p).wait()
        wgu_bf[...] = wgu_f32[ws].astype(BF16)
        wd_bf[...] = wd_f32[ws].astype(BF16)

        def block(j, carry):
            g = b0 + j
            x_copy(g).wait()

            @pl.when(g + x_slots - 1 < used)
            def _():
                x_copy(g + x_slots - 1).start()

            for p in range(n_pieces):
                @pl.when(jnp.logical_and(j == p, nxt >= 0))
                def _():
                    w_piece(nxt, 1 - ws, p).start()

            @pl.when(g >= 2)
            def _():
                y_copy(g - 2).wait()

            x = jnp.concatenate(_unpack_bf16_pairs(xbuf[lax.rem(g, x_slots)]), axis=1)
            hgu = jnp.dot(x, wgu_bf[...], preferred_element_type=F32) + bgu_ref[0]
            ff = hgu.shape[1] // 2
            gate = jnp.minimum(hgu[:, :ff], SWIGLU_LIMIT)
            up = jnp.clip(hgu[:, ff:], -SWIGLU_LIMIT, SWIGLU_LIMIT)
            act = (up + 1.0) * gate * _sigmoid(SWIGLU_ALPHA * gate)
            ybuf[lax.rem(g, 2)] = _pack_bf16_pairs(
                jnp.dot(act.astype(BF16), wd_bf[...], preferred_element_type=F32) + bd_ref[0])
            y_copy(g).start()
            return carry

        lax.fori_loop(0, n, block, 0)

        for p in range(n_pieces):
            @pl.when(jnp.logical_and(p >= n, nxt >= 0))
            def _():
                w_piece(nxt, 1 - ws, p).start()

    @pl.when(e == pl.num_programs(0) - 1)
    def _():
        for back in (2, 1):
            @pl.when(used >= back)
            def _():
                y_copy(used - back).wait()
        ybuf[0] = jnp.zeros(ybuf.shape[1:], ybuf.dtype)

        def zero_copy(b):
            return pltpu.make_async_copy(ybuf.at[0], y_hbm.at[block_rows(b)], zsem)

        def start(b, carry):
            zero_copy(b).start()
            return carry

        def wait(b, carry):
            zero_copy(b).wait()
            return carry

        lax.fori_loop(used_ref[0], n_blocks, start, 0)
        lax.fori_loop(used_ref[0], n_blocks, wait, 0)


def expert_ffn(xs, expert_tables, w_gate_up, b_gate_up, w_down, b_down):
    n_rows = xs.shape[0]
    ne, dm, ff2 = w_gate_up.shape
    bm = FFN_BLOCK
    grid_spec = pltpu.PrefetchScalarGridSpec(
        num_scalar_prefetch=6, grid=(ne,),
        in_specs=[pl.BlockSpec(memory_space=pl.ANY),
                  pl.BlockSpec(memory_space=pl.ANY),
                  pl.BlockSpec((1, 1, ff2), lambda e, *_: (e, 0, 0)),
                  pl.BlockSpec(memory_space=pl.ANY),
                  pl.BlockSpec((1, 1, dm), lambda e, *_: (e, 0, 0))],
        out_specs=pl.BlockSpec(memory_space=pl.ANY),
        scratch_shapes=[pltpu.VMEM((2, dm, ff2), F32), pltpu.VMEM((2, ff2 // 2, dm), F32),
                        pltpu.VMEM((dm, ff2), BF16), pltpu.VMEM((ff2 // 2, dm), BF16),
                        pltpu.VMEM((3, bm, dm // 2), jnp.int32), pltpu.VMEM((2, bm, dm // 2), jnp.int32),
                        pltpu.SemaphoreType.DMA((3,)), pltpu.SemaphoreType.DMA((2,)),
                        pltpu.SemaphoreType.DMA((2, sum(FFN_W_PIECES))), pltpu.SemaphoreType.DMA(())])
    return pl.pallas_call(
        functools.partial(_ffn_kernel, bm=bm), grid_spec=grid_spec,
        out_shape=jax.ShapeDtypeStruct((n_rows, dm // 2), jnp.int32),
        compiler_params=_params(), name="expert_ffn",
    )(*expert_tables, xs, w_gate_up, b_gate_up.reshape(ne, 1, ff2), w_down, b_down.reshape(ne, 1, dm))


def _combine_kernel(dst_ref, tch_ref,
                    crow_ref, gw_ref, hp_ref, hs_ref, fn_ref, yr_ref, yp_ref, ys_ref, ybuf, sems,
                    *, prompt_tiles):
    i = pl.program_id(0)
    nt = pl.num_programs(0)
    slot = lax.rem(i, 2)
    tt = hp_ref.shape[0]
    cr = ybuf.shape[1]
    ra = ROW_ALIGN

    def chunk_copy(sl, src_row, dst_row, rows):
        return pltpu.make_async_copy(yr_ref.at[pl.ds(src_row, rows)], ybuf.at[sl, pl.ds(dst_row, rows)],
                                     sems.at[sl])

    def fetch(tile, sl):
        chunks_per_tile = cr // ra
        _chunk_loop(tch_ref[tile], lambda c: chunk_copy(
            sl, pl.multiple_of(dst_ref[tile * chunks_per_tile + c], ra), pl.multiple_of(c * ra, ra), ra).start())

    @pl.when(i == 0)
    def _():
        ybuf[...] = jnp.zeros_like(ybuf)
        fetch(0, 0)

    @pl.when(i + 1 < nt)
    def _():
        fetch(i + 1, 1 - slot)

    _split_count(tch_ref[i], lambda j, m: chunk_copy(slot, 0, 0, ra * m).wait())

    y_lo, y_hi = _unpack_bf16_pairs(ybuf[slot])
    cols = lax.broadcasted_iota(jnp.int32, (tt, cr), 1)
    q = jnp.zeros((tt, cr), F32)
    for k in range(TOP_K):
        q = jnp.where(cols == crow_ref[:, k:k + 1], gw_ref[:, k:k + 1], q)
    qb = q.astype(BF16)
    moe = jnp.concatenate([jnp.dot(qb, y_lo, preferred_element_type=F32),
                           jnp.dot(qb, y_hi, preferred_element_type=F32)], axis=1)
    h = jnp.where(i < prompt_tiles, hp_ref[...], hs_ref[...])
    out = _rms(h + moe, fn_ref[...])

    @pl.when(i < prompt_tiles)
    def _():
        yp_ref[...] = out

    @pl.when(i >= prompt_tiles)
    def _():
        ys_ref[...] = out


def combine(crow_t, gw_t, h_p, h_s, final_norm, y_rows, tables, ne):
    dm = h_p.shape[1]
    tt = ROW_TILE
    npt, nst = h_p.shape[0] // tt, h_s.shape[0] // tt
    cr = _compact_rows(ne)
    p_map = lambda i, *_: (jnp.minimum(i, npt - 1), 0)
    s_map = lambda i, *_: (jnp.maximum(i - npt, 0), 0)
    grid_spec = pltpu.PrefetchScalarGridSpec(
        num_scalar_prefetch=2, grid=(npt + nst,),
        in_specs=[pl.BlockSpec((tt, TOP_K), lambda i, *_: (i, 0)),
                  pl.BlockSpec((tt, TOP_K), lambda i, *_: (i, 0)),
                  pl.BlockSpec((tt, dm), p_map), pl.BlockSpec((tt, dm), s_map),
                  pl.BlockSpec((1, dm), lambda i, *_: (0, 0)),
                  pl.BlockSpec(memory_space=pl.ANY)],
        out_specs=[pl.BlockSpec((tt, dm), p_map), pl.BlockSpec((tt, dm), s_map)],
        scratch_shapes=[pltpu.VMEM((2, cr, dm // 2), jnp.int32), pltpu.SemaphoreType.DMA((2,))])
    return pl.pallas_call(
        functools.partial(_combine_kernel, prompt_tiles=npt), grid_spec=grid_spec,
        out_shape=[jax.ShapeDtypeStruct(h_p.shape, F32), jax.ShapeDtypeStruct(h_s.shape, F32)],
        compiler_params=_params(), name="combine",
    )(*tables, crow_t, gw_t, h_p, h_s, final_norm.reshape(1, dm), y_rows)


def _compact_rows(ne):
    return -(-(TOP_K * ROW_TILE + ne * (ROW_ALIGN - 1)) // 128) * 128


def moe_and_final_norm(h_p, h_s, norm_ffn, w_router, b_router, w_gate_up, b_gate_up, w_down, b_down, final_norm):
    ne = w_router.shape[1]
    bm = FFN_BLOCK
    ra = ROW_ALIGN
    xn, gw, crow, cnt = router(h_p, h_s, norm_ffn, w_router, b_router)
    t = xn.shape[0]
    nt = t // ROW_TILE
    seg = -(-cnt[:, 0].astype(jnp.int32).reshape(nt, ne) // ra) * ra
    seg_before = jnp.cumsum(seg, axis=0) - seg
    rows_e = jnp.sum(seg, axis=0)
    padded = -(-rows_e // bm) * bm
    pad_ends = jnp.cumsum(padded)
    pad_starts = pad_ends - padded
    n_blocks = -(-(t * TOP_K + nt * ne * (ra - 1) + ne * (bm - 1)) // bm)
    n_used = pad_ends[-1] // bm
    experts = jnp.arange(ne, dtype=jnp.int32)
    chunks = seg // ra
    chunk_end = jnp.cumsum(chunks, axis=1)
    c_ids = jnp.arange(_compact_rows(ne) // ra, dtype=jnp.int32)
    owner = jnp.sum((chunk_end[:, None, :] <= c_ids[None, :, None]).astype(jnp.int32), axis=2)
    seg_shift = pad_starts[None, :] + seg_before - ra * (chunk_end - chunks)
    dst = ra * c_ids[None, :] + jnp.sum(
        jnp.where(owner[:, :, None] == experts[None, None, :], seg_shift[:, None, :], 0), axis=2)
    seg_tables = (dst.reshape(-1), chunk_end[:, -1])
    fill_tables = (pad_starts + rows_e, (padded - rows_e) // ra, n_used.reshape(1))
    to_i32 = lambda xs: tuple(x.astype(jnp.int32) for x in xs)
    xs = dispatch(xn, crow, to_i32(seg_tables + fill_tables), n_blocks * bm, ne)
    active = padded > 0
    later = jnp.where(active, experts, ne)
    next_active = jnp.concatenate([lax.cummin(later, reverse=True)[1:], jnp.full((1,), ne, jnp.int32)])
    expert_tables = (pad_starts // bm, padded // bm, n_used.reshape(1),
                     jnp.where(next_active < ne, next_active, -1),
                     (jnp.cumsum(active.astype(jnp.int32)) - active.astype(jnp.int32)) % 2,
                     jnp.min(later).reshape(1))
    y_rows = expert_ffn(xs, to_i32(expert_tables), w_gate_up, b_gate_up, w_down, b_down)
    return combine(crow.T, gw.T, h_p, h_s, final_norm, y_rows, to_i32(seg_tables), ne)


def kernel(x_prompt, x_sample, cache_mem_k, cache_mem_v, state_ret, state_ssm_re, state_ssm_im, mem_prompt, norm_mix, w_in, ret_gn, w_ret_o, ssm_lam_re, ssm_lam_im, ssm_log_dt, ssm_b_re, ssm_b_im, ssm_c_re, ssm_c_im, ssm_d, w_ssm_glu, w_ssm_o, mem_norm, w_mem_kv, w_x_o, w_out, norm_ffn, w_router, b_router, w_gate_up, b_gate_up, w_down, b_down, final_norm):
    assert norm_mix.shape[0] == 1, "single-layer step"
    bp, lp, dm = x_prompt.shape
    bs, ls, _ = x_sample.shape
    n_mem = mem_prompt.shape[1]
    xw = X_HEADS * HEAD_DIM
    qk = RET_HEADS * HEAD_DIM
    sw = ssm_d.shape[1]
    g = ssm_lam_re.shape[1]

    w_in_b = w_in[0].astype(BF16)
    tables = s5_tables(ssm_lam_re[0], ssm_lam_im[0], ssm_log_dt[0], ssm_b_re[0], ssm_b_im[0],
                       ssm_c_re[0], ssm_c_im[0], ssm_d[0])
    mix_w = (ret_gn[0], w_ret_o[0].astype(BF16), w_ssm_glu[0].astype(BF16), w_ssm_o[0].astype(BF16),
             w_x_o[0].astype(BF16), w_out[0].astype(BF16))

    kv = norm_matmul(mem_prompt.reshape(bp * n_mem, dm), mem_norm[0], w_mem_kv[0].astype(BF16), F32)
    mk_p = kv[:, :xw].reshape(bp, n_mem * X_HEADS, HEAD_DIM)
    mv_p = kv[:, xw:].reshape(bp, n_mem * X_HEADS, HEAD_DIM)

    def group(x, pos, mem_k, mem_v, s_ret, h_re, h_im, nb, tl):
        bsz, length, _ = x.shape
        z, u = norm_matmul(x.reshape(bsz * length, dm), norm_mix[0], w_in_b, BF16,
                           f32_cols=(4 * qk, 4 * qk + sw),
                           acts=((3 * qk, 4 * qk, "silu"), (4 * qk + sw + xw, w_in_b.shape[1], "sigmoid")))
        y, hf_re, hf_im = s5_apply(u, bsz, h_re, h_im, tables)
        h, s_new = mixer(x, z, y, pos, mem_k, mem_v, s_ret, *mix_w, nb=nb, tl=tl)
        return h, s_new, hf_re, hf_im

    zero_ret = jnp.zeros((bp, RET_HEADS, HEAD_DIM, HEAD_DIM), F32)
    zero_ssm = jnp.zeros((bp, g, SSM_STATE), F32)
    h_p, ret_p, sre_p, sim_p = group(x_prompt, jnp.arange(lp, dtype=jnp.int32), mk_p, mv_p,
                                     zero_ret, zero_ssm, zero_ssm, 1, ROW_TILE)
    h_s, ret_s, sre_s, sim_s = group(x_sample, PAST_LEN + jnp.arange(ls, dtype=jnp.int32),
                                     cache_mem_k[0].reshape(bs, n_mem * X_HEADS, HEAD_DIM),
                                     cache_mem_v[0].reshape(bs, n_mem * X_HEADS, HEAD_DIM),
                                     state_ret[0], state_ssm_re[0], state_ssm_im[0], ROW_TILE // ls, ls)

    y_p, y_s = moe_and_final_norm(h_p, h_s, norm_ffn[0], w_router[0], b_router[0],
                                  w_gate_up[0], b_gate_up[0], w_down[0], b_down[0], final_norm)
    return (y_p.reshape(bp, lp, dm), y_s.reshape(bs, ls, dm), ret_p[None], sre_p[None], sim_p[None],
            mk_p.reshape(1, bp, n_mem, X_HEADS, HEAD_DIM), mv_p.reshape(1, bp, n_mem, X_HEADS, HEAD_DIM),
            ret_s[None], sre_s[None], sim_s[None])
```

```python
import functools
import math

import jax
import jax.numpy as jnp
import numpy as np
from jax import lax
from jax.experimental import pallas as pl
from jax.experimental.pallas import tpu as pltpu

F32 = jnp.float32
BF16 = jnp.bfloat16

EPS = 1e-6
CHUNK = 64
PAST_LEN = 2048
ROPE_BASE = 10000.0
RET_HEADS = 4
X_HEADS = 4
HEAD_DIM = 128
SSM_GROUP = 16
SSM_STATE = 64
TOP_K = 4
SWIGLU_ALPHA = 1.702
SWIGLU_LIMIT = 7.0

VMEM_LIMIT = 52 * 1024 * 1024
S5_CHUNK = 8
S5_LANES = 128
ROW_TILE = 256
PROJ_TILE = 512
FFN_BLOCK = 256
ROW_ALIGN = 8
MIX_COLS = 512
NT_DIMS = (((1,), (1,)), ((), ()))
TN_DIMS = (((0,), (0,)), ((), ()))


def _params(n_axes=1):
    return pltpu.CompilerParams(dimension_semantics=("arbitrary",) * n_axes,
                                vmem_limit_bytes=VMEM_LIMIT)


def _resident(shape):
    nd = len(shape)
    return pl.BlockSpec(shape, lambda *_: (0,) * nd, pipeline_mode=pl.Buffered(1))


def _rms(x, w):
    return x * lax.rsqrt(jnp.mean(x * x, axis=-1, keepdims=True) + EPS) * w


def _sigmoid(x):
    return 0.5 * jnp.tanh(0.5 * x) + 0.5


_ACTIVATIONS = {"sigmoid": _sigmoid, "silu": lambda v: v * _sigmoid(v)}


def _norm_matmul_kernel(x_ref, nw_ref, w_ref, o_ref, *f32_refs, n_chunk, f32_cols, acts):
    xb = _rms(x_ref[...], nw_ref[...]).astype(BF16)
    for n0 in range(0, o_ref.shape[1], n_chunk):
        r = jnp.dot(xb, w_ref[:, n0:n0 + n_chunk], preferred_element_type=F32)
        if f32_cols is not None and n0 <= f32_cols[0] and f32_cols[1] <= n0 + n_chunk:
            f32_refs[0][...] = r[:, f32_cols[0] - n0:f32_cols[1] - n0]
        cuts = sorted({n0, n0 + n_chunk} | {c for lo, hi, _ in acts for c in (lo, hi) if n0 < c < n0 + n_chunk})
        for a, b in zip(cuts[:-1], cuts[1:]):
            piece = r[:, a - n0:b - n0]
            for lo, hi, kind in acts:
                if lo <= a and b <= hi:
                    piece = _ACTIVATIONS[kind](piece)
            o_ref[:, a:b] = piece.astype(o_ref.dtype)


def norm_matmul(x, nw, w, out_dtype, f32_cols=None, acts=()):
    t, d = x.shape
    n = w.shape[1]
    n_chunk = min(n, 1024)
    tm = PROJ_TILE
    out_specs = [pl.BlockSpec((tm, n), lambda i: (i, 0))]
    out_shape = [jax.ShapeDtypeStruct((t, n), out_dtype)]
    if f32_cols is not None:
        lo, hi = f32_cols
        assert lo // n_chunk == (hi - 1) // n_chunk
        out_specs.append(pl.BlockSpec((tm, hi - lo), lambda i: (i, 0)))
        out_shape.append(jax.ShapeDtypeStruct((t, hi - lo), F32))
    out = pl.pallas_call(
        functools.partial(_norm_matmul_kernel, n_chunk=n_chunk, f32_cols=f32_cols, acts=tuple(acts)),
        grid=(t // tm,),
        in_specs=[pl.BlockSpec((tm, d), lambda i: (i, 0)), _resident((1, d)), _resident((d, n))],
        out_specs=out_specs, out_shape=out_shape,
        compiler_params=_params(), name="norm_matmul",
    )(x, nw.reshape(1, d), w)
    return out if f32_cols is not None else out[0]


def s5_tables(lam_re, lam_im, log_dt, b_re, b_im, c_re, c_im, d_skip):
    g, n, p = b_re.shape
    s = S5_CHUNK
    gl = S5_LANES // p
    j = g // gl
    hi = lax.Precision.HIGHEST
    dt = jnp.exp(log_dt)[:, None]
    a_re = jnp.exp(lam_re * dt) * jnp.cos(lam_im * dt)
    a_im = jnp.exp(lam_re * dt) * jnp.sin(lam_im * dt)
    den = lam_re * lam_re + lam_im * lam_im
    nr, ni = a_re - 1.0, a_im
    co_re = (nr * lam_re + ni * lam_im) / den
    co_im = (ni * lam_re - nr * lam_im) / den
    bb_re = co_re[..., None] * b_re - co_im[..., None] * b_im
    bb_im = co_re[..., None] * b_im + co_im[..., None] * b_re
    tau = jnp.arange(s + 1, dtype=F32)[:, None, None]
    pw_mag = jnp.exp(lam_re * dt * tau)
    pw_re = pw_mag * jnp.cos(lam_im * dt * tau)
    pw_im = pw_mag * jnp.sin(lam_im * dt * tau)
    ca_re = c_re[None] * pw_re[:, :, None, :] - c_im[None] * pw_im[:, :, None, :]
    ca_im = c_re[None] * pw_im[:, :, None, :] + c_im[None] * pw_re[:, :, None, :]
    bq_re, bq_im = bb_re.transpose(0, 2, 1), bb_im.transpose(0, 2, 1)
    kq = jnp.sum(ca_re[:s, :, :, None, :] * bq_re[None, :, None, :, :]
                 - ca_im[:s, :, :, None, :] * bq_im[None, :, None, :, :], axis=-1).transpose(1, 0, 3, 2)
    ts = np.arange(s)
    lag_onehot = (ts[None, None, :] - ts[None, :, None] == ts[:, None, None]).astype(np.float32)
    rev = s - 1 - ts
    w_re = pw_re[rev][:, :, :, None] * bb_re[None] - pw_im[rev][:, :, :, None] * bb_im[None]
    w_im = pw_re[rev][:, :, :, None] * bb_im[None] + pw_im[rev][:, :, :, None] * bb_re[None]
    m_c = (jnp.einsum('gxqp,xst->gsqtp', kq, lag_onehot, precision=hi)
           .reshape(j, gl, s, p, s * p).transpose(0, 2, 1, 3, 4).reshape(j, s * gl * p, s * p))
    w_c = (jnp.stack([w_re, w_im]).reshape(2, s, j, gl, n, p).transpose(2, 1, 3, 5, 0, 4)
           .reshape(j, s * gl * p, 2 * n))
    v_c = (jnp.stack([ca_re[1:], -ca_im[1:]]).reshape(2, s, j, gl, p, n).transpose(2, 0, 3, 5, 1, 4)
           .reshape(j, 2 * gl * n, s * p))
    fl = s * gl * p
    c_io = np.arange(fl)
    c_st = np.arange(2 * gl * n)
    k_io = np.arange(s * p)
    k_st = np.arange(2 * n)
    spread_io = ((k_io[:, None] // p == c_io[None, :] // (gl * p)) & (k_io[:, None] % p == c_io[None, :] % p))
    spread_st = ((k_st[:, None] // n == c_st[None, :] // (gl * n)) & (k_st[:, None] % n == c_st[None, :] % n))
    grp_io = (c_io // p) % gl
    grp_st = (c_st // n) % gl

    def expand(compact, spread, row_grp, col_grp):
        full = jnp.einsum('jrk,kc->jrc', compact.astype(BF16), jnp.asarray(spread, BF16),
                          preferred_element_type=F32)
        return jnp.where(jnp.asarray(row_grp[:, None] == col_grp[None, :]), full, 0.0).astype(BF16)

    m = expand(m_c, spread_io, grp_io, grp_io)
    w = expand(w_c, spread_st, grp_io, grp_st)
    v = expand(v_c, spread_io, grp_st, grp_io)
    a_s_re = pw_re[s].reshape(1, g * n)
    a_s_im = pw_im[s].reshape(1, g * n)
    dtab = jnp.broadcast_to(d_skip.reshape(j, 1, 1, gl * p), (j, 1, s, gl * p)).reshape(j, 1, s * gl * p)
    return jnp.concatenate([m, v], axis=1), w, a_s_re, a_s_im, dtab


def _s5_flat(u_ref):
    return jnp.concatenate([u_ref[:, t, :] for t in range(u_ref.shape[1])], axis=1)


def _s5a_kernel(u_ref, w_ref, ire_ref, iim_ref):
    r = jnp.dot(_s5_flat(u_ref).astype(BF16), w_ref[0], preferred_element_type=F32)
    half = r.shape[1] // 2
    ire_ref[...] = r[:, :half]
    iim_ref[...] = r[:, half:]


def _s5scan_kernel(ire_ref, iim_ref, ar_ref, ai_ref, h0r_ref, h0i_ref,
                   hpr_ref, hpi_ref, hfr_ref, hfi_ref):
    nb, nc, _ = ire_ref.shape
    ar, ai = ar_ref[...], ai_ref[...]

    def body(c, carry):
        out = []
        for b in range(nb):
            hr, hi = carry[2 * b], carry[2 * b + 1]
            hpr_ref[b, pl.ds(c, 1), :] = hr
            hpi_ref[b, pl.ds(c, 1), :] = hi
            out.append(ar * hr - ai * hi + ire_ref[b, pl.ds(c, 1), :])
            out.append(ar * hi + ai * hr + iim_ref[b, pl.ds(c, 1), :])
        return tuple(out)

    init = []
    for b in range(nb):
        init += [h0r_ref[b], h0i_ref[b]]
    fin = lax.fori_loop(0, nc, body, tuple(init))
    for b in range(nb):
        hfr_ref[b] = fin[2 * b]
        hfi_ref[b] = fin[2 * b + 1]


def _s5b_kernel(u_ref, hpr_ref, hpi_ref, mv_ref, d_ref, y_ref):
    uf = _s5_flat(u_ref)
    lhs = jnp.concatenate([uf.astype(BF16), hpr_ref[...].astype(BF16), hpi_ref[...].astype(BF16)], axis=1)
    y = d_ref[0] * uf + jnp.dot(lhs, mv_ref[0], preferred_element_type=F32)
    lanes = y_ref.shape[2]
    for t in range(y_ref.shape[1]):
        y_ref[:, t, :] = y[:, t * lanes:(t + 1) * lanes]


def s5_apply(u, bsz, h0_re, h0_im, tables):
    mv, w, a_re, a_im, dtab = tables
    tokens, width = u.shape
    nj = w.shape[0]
    s = S5_CHUNK
    rows = tokens // s
    nc = rows // bsz
    lanes = a_re.shape[1]
    half = w.shape[2] // 2
    fl = w.shape[1]
    rt = min(rows, 1024)
    u3 = u.reshape(rows, s, width)
    u_spec = pl.BlockSpec((rt, s, S5_LANES), lambda j, r: (r, 0, j))
    st_spec = pl.BlockSpec((rt, half), lambda j, r: (r, j))
    tab_spec = pl.BlockSpec((1, fl, fl), lambda j, r: (j, 0, 0))
    inj_re, inj_im = pl.pallas_call(
        _s5a_kernel, grid=(nj, rows // rt),
        in_specs=[u_spec, tab_spec],
        out_specs=[st_spec, st_spec],
        out_shape=[jax.ShapeDtypeStruct((rows, lanes), F32)] * 2,
        compiler_params=_params(2), name="s5_chunk_in",
    )(u3, w)

    sb, lw = 4, (512 if nc > 16 else lanes)
    seq_spec = pl.BlockSpec((sb, nc, lw), lambda b, l: (b, 0, l))
    vec_spec = pl.BlockSpec((sb, 1, lw), lambda b, l: (b, 0, l))
    atab_spec = pl.BlockSpec((1, lw), lambda b, l: (0, l))
    hp_re, hp_im, hf_re, hf_im = pl.pallas_call(
        _s5scan_kernel, grid=(bsz // sb, lanes // lw),
        in_specs=[seq_spec, seq_spec, atab_spec, atab_spec, vec_spec, vec_spec],
        out_specs=[seq_spec, seq_spec, vec_spec, vec_spec],
        out_shape=[jax.ShapeDtypeStruct((bsz, nc, lanes), F32)] * 2
        + [jax.ShapeDtypeStruct((bsz, 1, lanes), F32)] * 2,
        compiler_params=_params(2), name="s5_scan",
    )(inj_re.reshape(bsz, nc, lanes), inj_im.reshape(bsz, nc, lanes), a_re, a_im,
      h0_re.reshape(bsz, 1, lanes), h0_im.reshape(bsz, 1, lanes))

    y3 = pl.pallas_call(
        _s5b_kernel, grid=(nj, rows // rt),
        in_specs=[u_spec, st_spec, st_spec, pl.BlockSpec((1, 2 * fl, fl), lambda j, r: (j, 0, 0)),
                  pl.BlockSpec((1, 1, fl), lambda j, r: (j, 0, 0))],
        out_specs=u_spec,
        out_shape=jax.ShapeDtypeStruct((rows, s, width), F32),
        compiler_params=_params(2), name="s5_chunk_out",
    )(u3, hp_re.reshape(rows, lanes), hp_im.reshape(rows, lanes), mv, dtab)
    g = lanes // SSM_STATE
    return y3.reshape(tokens, width), hf_re.reshape(bsz, g, SSM_STATE), hf_im.reshape(bsz, g, SSM_STATE)


def _retention_gammas():
    return 1.0 - np.exp2(-5.0 - np.arange(RET_HEADS, dtype=np.float64))


def retention_tables(tile, chunk, nb):
    gam = _retention_gammas()[:, None, None]
    i = np.arange(tile)[:, None]
    j = np.arange(tile)[None, :]
    same = (i // chunk) == (j // chunk)
    earlier = (j // chunk) < (i // chunk)
    dist = np.where(same, np.abs(i - j), np.where(earlier, i - j, 0))
    dmask = np.where(same | earlier, gam ** dist[None], 0.0)
    qw = np.broadcast_to((gam[:, :, 0] ** (np.arange(tile) + 1.0))[:, :, None], (RET_HEADS, tile, HEAD_DIM))
    kw = np.broadcast_to((gam[:, :, 0] ** (tile - 1.0 - np.arange(tile)))[:, :, None], (RET_HEADS, tile, HEAD_DIM))
    dmask = np.stack([np.kron(np.eye(nb), m) for m in dmask])
    return (jnp.asarray(dmask, F32), jnp.asarray(np.tile(qw, (1, nb, 1)), F32),
            jnp.asarray(np.tile(kw, (1, nb, 1)), F32), tuple(float(x) for x in _retention_gammas() ** tile))


def rope_tables(pos):
    half = HEAD_DIM // 2
    inv = jnp.exp(-math.log(ROPE_BASE) * 2.0 * jnp.arange(half, dtype=F32) / HEAD_DIM)
    ang = pos.astype(F32)[:, None] * inv[None, :]
    cos, sin = jnp.cos(ang), jnp.sin(ang)
    cosf = jnp.concatenate([cos, cos], axis=1)
    sinf = jnp.concatenate([-sin, sin], axis=1)
    return cosf, sinf


def _mixer_kernel(x_ref, zq_ref, xq_ref, gl_ref, y_ref, cq_ref, sq_ref, ck_ref, sk_ref,
                  dm_ref, qw_ref, kw_ref, vm_ref, am_ref, mk_ref, mv_ref, s0_ref, gn_ref,
                  wro_ref, wglu_ref, wso_ref, wxo_ref, wout_ref,
                  h_ref, sout_ref, s_scr, o_scr, xo_scr, glu_scr, mg_scr, *, nb, tl, tile_decay):
    hd = HEAD_DIM
    qk = RET_HEADS * hd
    n_mem = mk_ref.shape[1] // X_HEADS

    def own_blocks(a, width):
        if nb == 1:
            return a
        return jnp.concatenate([a[n * tl:(n + 1) * tl, n * width:(n + 1) * width] for n in range(nb)], axis=0)

    def spread_blocks(a, mask_ref):
        if nb == 1:
            return a
        return jnp.concatenate([a] * nb, axis=1) * mask_ref[...]

    @pl.when(pl.program_id(1) == 0)
    def _():
        for n in range(nb):
            for h in range(RET_HEADS):
                s_scr[h, :, n * hd:(n + 1) * hd] = s0_ref[n, h]

    cq, sq, ck, sk = cq_ref[...], sq_ref[...], ck_ref[...], sk_ref[...]
    for h in range(RET_HEADS):
        c0 = h * hd
        q = zq_ref[:, c0:c0 + hd].astype(F32)
        k = zq_ref[:, qk + c0:qk + c0 + hd].astype(F32)
        v = zq_ref[:, 2 * qk + c0:2 * qk + c0 + hd]
        g = zq_ref[:, 3 * qk + c0:3 * qk + c0 + hd].astype(F32)
        qr = q * cq + pltpu.roll(q, hd // 2, 1) * sq
        kr = k * ck + pltpu.roll(k, hd // 2, 1) * sk
        sc = lax.dot_general(qr.astype(BF16), kr.astype(BF16), NT_DIMS,
                             preferred_element_type=F32) * dm_ref[h]
        o = jnp.dot(sc.astype(BF16), v, preferred_element_type=F32)
        s_old = s_scr[h]
        o += own_blocks(jnp.dot((qr * qw_ref[h]).astype(BF16), s_old.astype(BF16),
                                preferred_element_type=F32), hd)
        kv = lax.dot_general((kr * kw_ref[h]).astype(BF16), spread_blocks(v, vm_ref), TN_DIMS,
                             preferred_element_type=F32)
        s_scr[h] = tile_decay[h] * s_old + kv
        d = o - jnp.mean(o, axis=-1, keepdims=True)
        on = d * lax.rsqrt(jnp.mean(d * d, axis=-1, keepdims=True) + EPS) * gn_ref[:, c0:c0 + hd]
        o_scr[:, c0:c0 + hd] = (on * g).astype(BF16)
        mem_rows = pl.ds(h, n_mem, stride=X_HEADS)
        mkh = mk_ref[:, mem_rows, :].astype(BF16).reshape(nb * n_mem, hd)
        mvh = mv_ref[:, mem_rows, :].astype(BF16).reshape(nb * n_mem, hd)
        s = own_blocks(lax.dot_general(xq_ref[:, c0:c0 + hd], mkh, NT_DIMS,
                                       preferred_element_type=F32), n_mem) * (hd ** -0.5)
        e = jnp.exp(s - jnp.max(s, axis=-1, keepdims=True))
        p = (e / jnp.sum(e, axis=-1, keepdims=True)).astype(BF16)
        xo_scr[:, c0:c0 + hd] = jnp.dot(spread_blocks(p, am_ref), mvh,
                                        preferred_element_type=F32).astype(BF16)

    cw = MIX_COLS
    dm = h_ref.shape[1]
    half = wglu_ref.shape[1] // 2
    yb = jax.nn.gelu(y_ref[...]).astype(BF16)
    for c0 in range(0, half, cw):
        ga = jnp.dot(yb, wglu_ref[:, c0:c0 + cw], preferred_element_type=F32)
        gb = jnp.dot(yb, wglu_ref[:, half + c0:half + c0 + cw], preferred_element_type=F32)
        glu_scr[:, c0:c0 + cw] = (ga * _sigmoid(gb)).astype(BF16)
    for c0 in range(0, dm, cw):
        cols = slice(c0, c0 + cw)
        ret = jnp.dot(o_scr[...], wro_ref[:, cols], preferred_element_type=F32)
        ssm = jnp.dot(glu_scr[...], wso_ref[:, cols], preferred_element_type=F32)
        xb = jnp.dot(xo_scr[...], wxo_ref[:, cols], preferred_element_type=F32)
        merged = (gl_ref[:, c0:c0 + cw].astype(F32) * ret
                  + gl_ref[:, dm + c0:dm + c0 + cw].astype(F32) * ssm
                  + gl_ref[:, 2 * dm + c0:2 * dm + c0 + cw].astype(F32) * xb)
        mg_scr[:, cols] = merged.astype(BF16)
    for c0 in range(0, dm, cw):
        cols = slice(c0, c0 + cw)
        h_ref[:, cols] = x_ref[:, cols] + jnp.dot(mg_scr[...], wout_ref[:, cols], preferred_element_type=F32)
    for n in range(nb):
        for h in range(RET_HEADS):
            sout_ref[n, h] = s_scr[h, :, n * hd:(n + 1) * hd]


def mixer(x, z, y_ssm, pos, mem_k, mem_v, s0, ret_gn, w_ret_o, w_ssm_glu, w_ssm_o, w_x_o, w_out, *, nb, tl):
    bsz, length, dm = x.shape
    chunk = min(CHUNK, length)
    nl = length // tl
    rows = nb * tl
    qk = RET_HEADS * HEAD_DIM
    sw = y_ssm.shape[1]
    xw = X_HEADS * HEAD_DIM
    gate_col = (4 * qk + sw + xw)
    assert gate_col % (3 * dm) == 0 and (4 * qk + sw) % xw == 0
    dmask, qw, kw, tile_decay = retention_tables(tl, chunk, nb)
    seq_of_row = np.arange(rows)[:, None] // tl
    own_v = jnp.asarray(seq_of_row == np.arange(nb * HEAD_DIM)[None, :] // HEAD_DIM, BF16)
    n_mem = mem_k.shape[1] // X_HEADS
    own_mem = jnp.asarray(seq_of_row == np.arange(nb * n_mem)[None, :] // n_mem, BF16)
    cosf, sinf = (jnp.broadcast_to(t.reshape(nl, 1, tl, HEAD_DIM), (nl, nb, tl, HEAD_DIM)).reshape(nl * rows, HEAD_DIM)
                  for t in rope_tables(pos))
    scale = HEAD_DIM ** -0.5
    row_map = lambda b, l: (b * nl + l, 0)
    tab_map = lambda b, l: (l, 0)
    st_spec = pl.BlockSpec((nb, RET_HEADS, HEAD_DIM, HEAD_DIM), lambda b, l: (b, 0, 0, 0))
    mem_spec = pl.BlockSpec((nb,) + mem_k.shape[1:], lambda b, l: (b,) + (0,) * (mem_k.ndim - 1))
    h, s_out = pl.pallas_call(
        functools.partial(_mixer_kernel, nb=nb, tl=tl, tile_decay=tile_decay),
        grid=(bsz // nb, nl),
        in_specs=[pl.BlockSpec((rows, dm), row_map),
                  pl.BlockSpec((rows, 4 * qk), row_map),
                  pl.BlockSpec((rows, xw), lambda b, l: (b * nl + l, (4 * qk + sw) // xw)),
                  pl.BlockSpec((rows, 3 * dm), lambda b, l: (b * nl + l, gate_col // (3 * dm))),
                  pl.BlockSpec((rows, sw), row_map),
                  pl.BlockSpec((rows, HEAD_DIM), tab_map), pl.BlockSpec((rows, HEAD_DIM), tab_map),
                  pl.BlockSpec((rows, HEAD_DIM), tab_map), pl.BlockSpec((rows, HEAD_DIM), tab_map),
                  _resident(dmask.shape), _resident(qw.shape), _resident(kw.shape),
                  _resident(own_v.shape), _resident(own_mem.shape),
                  mem_spec, mem_spec, st_spec, _resident((1, qk)),
                  _resident(w_ret_o.shape), _resident(w_ssm_glu.shape), _resident(w_ssm_o.shape),
                  _resident(w_x_o.shape), _resident(w_out.shape)],
        out_specs=[pl.BlockSpec((rows, dm), row_map), st_spec],
        out_shape=[jax.ShapeDtypeStruct((bsz * length, dm), F32),
                   jax.ShapeDtypeStruct(s0.shape, F32)],
        scratch_shapes=[pltpu.VMEM((RET_HEADS, HEAD_DIM, nb * HEAD_DIM), F32),
                        pltpu.VMEM((rows, qk), BF16), pltpu.VMEM((rows, xw), BF16),
                        pltpu.VMEM((rows, w_ssm_o.shape[0]), BF16), pltpu.VMEM((rows, dm), BF16)],
        compiler_params=_params(2), name="mixer",
    )(x.reshape(bsz * length, dm), z, z, z, y_ssm,
      cosf * scale, sinf * scale, cosf, sinf, dmask, qw, kw, own_v, own_mem,
      mem_k, mem_v, s0, ret_gn.reshape(1, qk), w_ret_o, w_ssm_glu, w_ssm_o, w_x_o, w_out)
    return h, s_out


def _router_kernel(hp_ref, hs_ref, nw_ref, wrt_ref, br_ref, tri_ref, low_ref,
                   xn_ref, gw_ref, crow_ref, cnt_ref, *, prompt_tiles):
    h = jnp.where(pl.program_id(0) < prompt_tiles, hp_ref[...], hs_ref[...])
    xn = _rms(h, nw_ref[...]).astype(BF16)
    xn_ref[...] = xn
    logits = lax.dot_general(wrt_ref[...], xn, NT_DIMS, preferred_element_type=F32) + br_ref[...]
    ne = logits.shape[0]
    iota = lax.broadcasted_iota(jnp.int32, logits.shape, 0)
    rest = logits
    sel = jnp.zeros(logits.shape, jnp.bool_)
    vals, idxs = [], []
    for _ in range(TOP_K):
        m = jnp.max(rest, axis=0, keepdims=True)
        ix = jnp.min(jnp.where(rest == m, iota, ne), axis=0, keepdims=True)
        hit = iota == ix
        vals.append(m)
        idxs.append(ix)
        sel = jnp.logical_or(sel, hit)
        rest = jnp.where(hit, -jnp.inf, rest)
    es = [jnp.exp(v - vals[0]) for v in vals]
    tot = es[0] + es[1] + es[2] + es[3]
    before = jnp.dot(sel.astype(BF16), tri_ref[...], preferred_element_type=F32)
    cnt = jnp.sum(sel.astype(F32), axis=1, keepdims=True)
    seg = jnp.floor((cnt + (ROW_ALIGN - 1.0)) * (1.0 / ROW_ALIGN)) * ROW_ALIGN
    start = jnp.dot(low_ref[...], jnp.broadcast_to(seg, before.shape), precision=lax.Precision.HIGHEST,
                    preferred_element_type=F32)
    place = start + before
    for k in range(TOP_K):
        gw_ref[k:k + 1, :] = es[k] / tot
        crow_ref[k:k + 1, :] = jnp.sum(jnp.where(iota == idxs[k], place, 0.0), axis=0,
                                       keepdims=True).astype(jnp.int32)
    cnt_ref[...] = jnp.broadcast_to(cnt, cnt_ref.shape)


def router(h_p, h_s, norm_ffn, w_router, b_router):
    dm = h_p.shape[1]
    ne = w_router.shape[1]
    tt = ROW_TILE
    npt, nst = h_p.shape[0] // tt, h_s.shape[0] // tt
    t = (npt + nst) * tt
    tri = jnp.asarray(np.triu(np.ones((tt, tt), np.float32), k=1), BF16)
    low = jnp.asarray(np.tril(np.ones((ne, ne), np.float32), k=-1))
    tok_spec = pl.BlockSpec((TOP_K, tt), lambda i: (0, i))
    return pl.pallas_call(
        functools.partial(_router_kernel, prompt_tiles=npt), grid=(npt + nst,),
        in_specs=[pl.BlockSpec((tt, dm), lambda i: (jnp.minimum(i, npt - 1), 0)),
                  pl.BlockSpec((tt, dm), lambda i: (jnp.maximum(i - npt, 0), 0)),
                  _resident((1, dm)), _resident((ne, dm)), _resident((ne, 1)), _resident((tt, tt)),
                  _resident((ne, ne))],
        out_specs=[pl.BlockSpec((tt, dm), lambda i: (i, 0)), tok_spec, tok_spec,
                   pl.BlockSpec((ne, 128), lambda i: (i, 0))],
        out_shape=[jax.ShapeDtypeStruct((t, dm), BF16),
                   jax.ShapeDtypeStruct((TOP_K, t), F32), jax.ShapeDtypeStruct((TOP_K, t), jnp.int32),
                   jax.ShapeDtypeStruct(((npt + nst) * ne, 128), F32)],
        compiler_params=_params(), name="router",
    )(h_p, h_s, norm_ffn.reshape(1, dm), w_router.T.astype(BF16), b_router.reshape(ne, 1), tri, low)


def _pack_bf16_pairs(x, exact=False):
    n = x.shape[1] // 2
    lo, hi = x[:, :n], x[:, n:]
    if not exact:
        lo, hi = lo.astype(BF16).astype(F32), hi.astype(BF16).astype(F32)
    lo = lax.bitcast_convert_type(lo, jnp.int32)
    hi = lax.bitcast_convert_type(hi, jnp.int32)
    return lax.shift_right_logical(lo, 16) | (hi & -65536)


def _unpack_bf16_pairs(u):
    lo = lax.bitcast_convert_type(lax.shift_left(u, 16), F32).astype(BF16)
    hi = lax.bitcast_convert_type(u & -65536, F32).astype(BF16)
    return lo, hi


def _split_count(n, fn):
    def quad(jq, carry):
        fn(4 * jq, 4)
        return carry

    lax.fori_loop(0, n // 4, quad, 0)

    @pl.when(n % 4 >= 2)
    def _():
        fn(n // 4 * 4, 2)

    @pl.when(n % 2 == 1)
    def _():
        fn(n // 2 * 2, 1)


def _chunk_loop(n, fn):
    def quad(jq, carry):
        for u in range(4):
            fn(4 * jq + u)
        return carry

    def single(j, carry):
        fn(j)
        return carry

    lax.fori_loop(0, n // 4, quad, 0)
    lax.fori_loop(n // 4 * 4, n, single, 0)


def _dispatch_kernel(dst_ref, tch_ref, zs_ref, zn_ref, used_ref,
                     xn_ref, crow_ref, xs_ref, cbuf, zbuf, sems, zsem, *, ne, bm):
    i = pl.program_id(0)
    nt = pl.num_programs(0)
    slot = lax.rem(i, 2)
    tt = xn_ref.shape[0]
    cr = cbuf.shape[1]
    ra = ROW_ALIGN
    n_blocks = xs_ref.shape[0] // bm

    def chunk_copy(sl, src_row, dst_row, rows):
        return pltpu.make_async_copy(cbuf.at[sl, pl.ds(src_row, rows)], xs_ref.at[pl.ds(dst_row, rows)],
                                     sems.at[sl])

    def wait_chunks(sl, n):
        _split_count(n, lambda j, m: chunk_copy(sl, 0, 0, ra * m).wait())

    def tail_copy(e, j):
        return pltpu.make_async_copy(zbuf.at[pl.ds(0, ra)],
                                     xs_ref.at[pl.ds(pl.multiple_of(zs_ref[e] + ra * j, ra), ra)], zsem)

    def block_copy(b):
        return pltpu.make_async_copy(zbuf, xs_ref.at[pl.ds(pl.multiple_of(b * bm, bm), bm)], zsem)

    def zero_fill(start):
        def per_expert(e, carry):
            def per_chunk(j, c2):
                (tail_copy(e, j).start() if start else tail_copy(e, j).wait())
                return c2
            lax.fori_loop(0, zn_ref[e], per_chunk, 0)
            return carry
        lax.fori_loop(0, ne, per_expert, 0)

        def per_block(b, carry):
            (block_copy(b).start() if start else block_copy(b).wait())
            return carry
        lax.fori_loop(used_ref[0], n_blocks, per_block, 0)

    @pl.when(i == 0)
    def _():
        zbuf[...] = jnp.zeros_like(zbuf)
        zero_fill(True)
        zero_fill(False)

    @pl.when(i >= 2)
    def _():
        wait_chunks(slot, tch_ref[jnp.maximum(i - 2, 0)])

    crow = crow_ref[...]
    rows = lax.broadcasted_iota(jnp.int32, (cr, tt), 0)
    hit = rows == crow[0:1, :]
    for k in range(1, TOP_K):
        hit = jnp.logical_or(hit, rows == crow[k:k + 1, :])
    packed = _pack_bf16_pairs(jnp.dot(jnp.where(hit, 1.0, 0.0).astype(BF16), xn_ref[...],
                                      preferred_element_type=F32), exact=True)
    cbuf[slot] = packed

    chunks_per_tile = cr // ra
    _chunk_loop(tch_ref[i], lambda c: chunk_copy(
        slot, pl.multiple_of(c * ra, ra), pl.multiple_of(dst_ref[i * chunks_per_tile + c], ra), ra).start())

    @pl.when(i == nt - 1)
    def _():
        wait_chunks(slot, tch_ref[i])
        wait_chunks(1 - slot, jnp.where(nt >= 2, tch_ref[jnp.maximum(i - 1, 0)], 0))


def dispatch(xn, crow, tables, n_rows, ne):
    t, dm = xn.shape
    tt = ROW_TILE
    bm = FFN_BLOCK
    cr = _compact_rows(ne)
    grid_spec = pltpu.PrefetchScalarGridSpec(
        num_scalar_prefetch=5, grid=(t // tt,),
        in_specs=[pl.BlockSpec((tt, dm), lambda i, *_: (i, 0)),
                  pl.BlockSpec((TOP_K, tt), lambda i, *_: (0, i))],
        out_specs=pl.BlockSpec(memory_space=pl.ANY),
        scratch_shapes=[pltpu.VMEM((2, cr, dm // 2), jnp.int32), pltpu.VMEM((bm, dm // 2), jnp.int32),
                        pltpu.SemaphoreType.DMA((2,)), pltpu.SemaphoreType.DMA(())])
    return pl.pallas_call(
        functools.partial(_dispatch_kernel, ne=ne, bm=bm), grid_spec=grid_spec,
        out_shape=jax.ShapeDtypeStruct((n_rows, dm // 2), jnp.int32),
        compiler_params=_params(), name="dispatch",
    )(*tables, xn, crow)


FFN_W_PIECES = (4, 2)


def _ffn_kernel(first_ref, nblk_ref, used_ref, next_ref, wslot_ref, lead_ref,
                xs_hbm, wgu_hbm, bgu_ref, wd_hbm, bd_ref, y_hbm,
                wgu_f32, wd_f32, wgu_bf, wd_bf, xbuf, ybuf, xsem, ysem, wsem, zsem, *, bm):
    e = pl.program_id(0)
    n = nblk_ref[e]
    b0 = first_ref[e]
    n_blocks = y_hbm.shape[0] // bm
    n_pieces = sum(FFN_W_PIECES)

    def block_rows(b):
        return pl.ds(pl.multiple_of(b * bm, bm), bm)

    used = used_ref[0]
    x_slots = xbuf.shape[0]

    def x_copy(g):
        sl = lax.rem(g, x_slots)
        return pltpu.make_async_copy(xs_hbm.at[block_rows(g)], xbuf.at[sl], xsem.at[sl])

    def y_copy(g):
        sl = lax.rem(g, 2)
        return pltpu.make_async_copy(ybuf.at[sl], y_hbm.at[block_rows(g)], ysem.at[sl])

    def w_piece(ex, sl, p):
        src, dst, q, parts = ((wgu_hbm, wgu_f32, p, FFN_W_PIECES[0]) if p < FFN_W_PIECES[0]
                              else (wd_hbm, wd_f32, p - FFN_W_PIECES[0], FFN_W_PIECES[1]))
        rows = src.shape[1] // parts
        return pltpu.make_async_copy(src.at[ex, pl.ds(q * rows, rows)], dst.at[sl, pl.ds(q * rows, rows)],
                                     wsem.at[sl, p])

    @pl.when(n > 0)
    def _():
        ws = wslot_ref[e]
        nxt = next_ref[e]

        @pl.when(lead_ref[0] == e)
        def _():
            for g in range(x_slots - 1):
                @pl.when(g < used)
                def _():
                    x_copy(g).start()
            for p in range(n_pieces):
                w_piece(e, ws, p).start()

        for p in range(n_pieces):
            w_piece(e, ws, p).wait()
        wgu_bf[...] = wgu_f32[ws].astype(BF16)
        wd_bf[...] = wd_f32[ws].astype(BF16)

        def block(j, carry):
            g = b0 + j
            x_copy(g).wait()

            @pl.when(g + x_slots - 1 < used)
            def _():
                x_copy(g + x_slots - 1).start()

            for p in range(n_pieces):
                @pl.when(jnp.logical_and(j == p, nxt >= 0))
                def _():
                    w_piece(nxt, 1 - ws, p).start()

            @pl.when(g >= 2)
            def _():
                y_copy(g - 2).wait()

            x = jnp.concatenate(_unpack_bf16_pairs(xbuf[lax.rem(g, x_slots)]), axis=1)
            hgu = jnp.dot(x, wgu_bf[...], preferred_element_type=F32) + bgu_ref[0]
            ff = hgu.shape[1] // 2
            gate = jnp.minimum(hgu[:, :ff], SWIGLU_LIMIT)
            up = jnp.clip(hgu[:, ff:], -SWIGLU_LIMIT, SWIGLU_LIMIT)
            act = (up + 1.0) * gate * _sigmoid(SWIGLU_ALPHA * gate)
            ybuf[lax.rem(g, 2)] = _pack_bf16_pairs(
                jnp.dot(act.astype(BF16), wd_bf[...], preferred_element_type=F32) + bd_ref[0])
            y_copy(g).start()
            return carry

        lax.fori_loop(0, n, block, 0)

        for p in range(n_pieces):
            @pl.when(jnp.logical_and(p >= n, nxt >= 0))
            def _():
                w_piece(nxt, 1 - ws, p).start()

    @pl.when(e == pl.num_programs(0) - 1)
    def _():
        for back in (2, 1):
            @pl.when(used >= back)
            def _():
                y_copy(used - back).wait()
        ybuf[0] = jnp.zeros(ybuf.shape[1:], ybuf.dtype)

        def zero_copy(b):
            return pltpu.make_async_copy(ybuf.at[0], y_hbm.at[block_rows(b)], zsem)

        def start(b, carry):
            zero_copy(b).start()
            return carry

        def wait(b, carry):
            zero_copy(b).wait()
            return carry

        lax.fori_loop(used_ref[0], n_blocks, start, 0)
        lax.fori_loop(used_ref[0], n_blocks, wait, 0)


def expert_ffn(xs, expert_tables, w_gate_up, b_gate_up, w_down, b_down):
    n_rows = xs.shape[0]
    ne, dm, ff2 = w_gate_up.shape
    bm = FFN_BLOCK
    grid_spec = pltpu.PrefetchScalarGridSpec(
        num_scalar_prefetch=6, grid=(ne,),
        in_specs=[pl.BlockSpec(memory_space=pl.ANY),
                  pl.BlockSpec(memory_space=pl.ANY),
                  pl.BlockSpec((1, 1, ff2), lambda e, *_: (e, 0, 0)),
                  pl.BlockSpec(memory_space=pl.ANY),
                  pl.BlockSpec((1, 1, dm), lambda e, *_: (e, 0, 0))],
        out_specs=pl.BlockSpec(memory_space=pl.ANY),
        scratch_shapes=[pltpu.VMEM((2, dm, ff2), F32), pltpu.VMEM((2, ff2 // 2, dm), F32),
                        pltpu.VMEM((dm, ff2), BF16), pltpu.VMEM((ff2 // 2, dm), BF16),
                        pltpu.VMEM((3, bm, dm // 2), jnp.int32), pltpu.VMEM((2, bm, dm // 2), jnp.int32),
                        pltpu.SemaphoreType.DMA((3,)), pltpu.SemaphoreType.DMA((2,)),
                        pltpu.SemaphoreType.DMA((2, sum(FFN_W_PIECES))), pltpu.SemaphoreType.DMA(())])
    return pl.pallas_call(
        functools.partial(_ffn_kernel, bm=bm), grid_spec=grid_spec,
        out_shape=jax.ShapeDtypeStruct((n_rows, dm // 2), jnp.int32),
        compiler_params=_params(), name="expert_ffn",
    )(*expert_tables, xs, w_gate_up, b_gate_up.reshape(ne, 1, ff2), w_down, b_down.reshape(ne, 1, dm))


def _combine_kernel(dst_ref, tch_ref,
                    crow_ref, gw_ref, hp_ref, hs_ref, fn_ref, yr_ref, yp_ref, ys_ref, ybuf, sems,
                    *, prompt_tiles):
    i = pl.program_id(0)
    nt = pl.num_programs(0)
    slot = lax.rem(i, 2)
    tt = hp_ref.shape[0]
    cr = ybuf.shape[1]
    ra = ROW_ALIGN

    def chunk_copy(sl, src_row, dst_row, rows):
        return pltpu.make_async_copy(yr_ref.at[pl.ds(src_row, rows)], ybuf.at[sl, pl.ds(dst_row, rows)],
                                     sems.at[sl])

    def fetch(tile, sl):
        chunks_per_tile = cr // ra
        _chunk_loop(tch_ref[tile], lambda c: chunk_copy(
            sl, pl.multiple_of(dst_ref[tile * chunks_per_tile + c], ra), pl.multiple_of(c * ra, ra), ra).start())

    @pl.when(i == 0)
    def _():
        ybuf[...] = jnp.zeros_like(ybuf)
        fetch(0, 0)

    @pl.when(i + 1 < nt)
    def _():
        fetch(i + 1, 1 - slot)

    _split_count(tch_ref[i], lambda j, m: chunk_copy(slot, 0, 0, ra * m).wait())

    y_lo, y_hi = _unpack_bf16_pairs(ybuf[slot])
    cols = lax.broadcasted_iota(jnp.int32, (tt, cr), 1)
    q = jnp.zeros((tt, cr), F32)
    for k in range(TOP_K):
        q = jnp.where(cols == crow_ref[:, k:k + 1], gw_ref[:, k:k + 1], q)
    qb = q.astype(BF16)
    moe = jnp.concatenate([jnp.dot(qb, y_lo, preferred_element_type=F32),
                           jnp.dot(qb, y_hi, preferred_element_type=F32)], axis=1)
    h = jnp.where(i < prompt_tiles, hp_ref[...], hs_ref[...])
    out = _rms(h + moe, fn_ref[...])

    @pl.when(i < prompt_tiles)
    def _():
        yp_ref[...] = out

    @pl.when(i >= prompt_tiles)
    def _():
        ys_ref[...] = out


def combine(crow_t, gw_t, h_p, h_s, final_norm, y_rows, tables, ne):
    dm = h_p.shape[1]
    tt = ROW_TILE
    npt, nst = h_p.shape[0] // tt, h_s.shape[0] // tt
    cr = _compact_rows(ne)
    p_map = lambda i, *_: (jnp.minimum(i, npt - 1), 0)
    s_map = lambda i, *_: (jnp.maximum(i - npt, 0), 0)
    grid_spec = pltpu.PrefetchScalarGridSpec(
        num_scalar_prefetch=2, grid=(npt + nst,),
        in_specs=[pl.BlockSpec((tt, TOP_K), lambda i, *_: (i, 0)),
                  pl.BlockSpec((tt, TOP_K), lambda i, *_: (i, 0)),
                  pl.BlockSpec((tt, dm), p_map), pl.BlockSpec((tt, dm), s_map),
                  pl.BlockSpec((1, dm), lambda i, *_: (0, 0)),
                  pl.BlockSpec(memory_space=pl.ANY)],
        out_specs=[pl.BlockSpec((tt, dm), p_map), pl.BlockSpec((tt, dm), s_map)],
        scratch_shapes=[pltpu.VMEM((2, cr, dm // 2), jnp.int32), pltpu.SemaphoreType.DMA((2,))])
    return pl.pallas_call(
        functools.partial(_combine_kernel, prompt_tiles=npt), grid_spec=grid_spec,
        out_shape=[jax.ShapeDtypeStruct(h_p.shape, F32), jax.ShapeDtypeStruct(h_s.shape, F32)],
        compiler_params=_params(), name="combine",
    )(*tables, crow_t, gw_t, h_p, h_s, final_norm.reshape(1, dm), y_rows)


def _compact_rows(ne):
    return -(-(TOP_K * ROW_TILE + ne * (ROW_ALIGN - 1)) // 128) * 128


def moe_and_final_norm(h_p, h_s, norm_ffn, w_router, b_router, w_gate_up, b_gate_up, w_down, b_down, final_norm):
    ne = w_router.shape[1]
    bm = FFN_BLOCK
    ra = ROW_ALIGN
    xn, gw, crow, cnt = router(h_p, h_s, norm_ffn, w_router, b_router)
    t = xn.shape[0]
    nt = t // ROW_TILE
    seg = -(-cnt[:, 0].astype(jnp.int32).reshape(nt, ne) // ra) * ra
    seg_before = jnp.cumsum(seg, axis=0) - seg
    rows_e = jnp.sum(seg, axis=0)
    padded = -(-rows_e // bm) * bm
    pad_ends = jnp.cumsum(padded)
    pad_starts = pad_ends - padded
    n_blocks = -(-(t * TOP_K + nt * ne * (ra - 1) + ne * (bm - 1)) // bm)
    n_used = pad_ends[-1] // bm
    experts = jnp.arange(ne, dtype=jnp.int32)
    chunks = seg // ra
    chunk_end = jnp.cumsum(chunks, axis=1)
    c_ids = jnp.arange(_compact_rows(ne) // ra, dtype=jnp.int32)
    owner = jnp.sum((chunk_end[:, None, :] <= c_ids[None, :, None]).astype(jnp.int32), axis=2)
    seg_shift = pad_starts[None, :] + seg_before - ra * (chunk_end - chunks)
    dst = ra * c_ids[None, :] + jnp.sum(
        jnp.where(owner[:, :, None] == experts[None, None, :], seg_shift[:, None, :], 0), axis=2)
    seg_tables = (dst.reshape(-1), chunk_end[:, -1])
    fill_tables = (pad_starts + rows_e, (padded - rows_e) // ra, n_used.reshape(1))
    to_i32 = lambda xs: tuple(x.astype(jnp.int32) for x in xs)
    xs = dispatch(xn, crow, to_i32(seg_tables + fill_tables), n_blocks * bm, ne)
    active = padded > 0
    later = jnp.where(active, experts, ne)
    next_active = jnp.concatenate([lax.cummin(later, reverse=True)[1:], jnp.full((1,), ne, jnp.int32)])
    expert_tables = (pad_starts // bm, padded // bm, n_used.reshape(1),
                     jnp.where(next_active < ne, next_active, -1),
                     (jnp.cumsum(active.astype(jnp.int32)) - active.astype(jnp.int32)) % 2,
                     jnp.min(later).reshape(1))
    y_rows = expert_ffn(xs, to_i32(expert_tables), w_gate_up, b_gate_up, w_down, b_down)
    return combine(crow.T, gw.T, h_p, h_s, final_norm, y_rows, to_i32(seg_tables), ne)


def kernel(x_prompt, x_sample, cache_mem_k, cache_mem_v, state_ret, state_ssm_re, state_ssm_im, mem_prompt, norm_mix, w_in, ret_gn, w_ret_o, ssm_lam_re, ssm_lam_im, ssm_log_dt, ssm_b_re, ssm_b_im, ssm_c_re, ssm_c_im, ssm_d, w_ssm_glu, w_ssm_o, mem_norm, w_mem_kv, w_x_o, w_out, norm_ffn, w_router, b_router, w_gate_up, b_gate_up, w_down, b_down, final_norm):
    assert norm_mix.shape[0] == 1, "single-layer step"
    bp, lp, dm = x_prompt.shape
    bs, ls, _ = x_sample.shape
    n_mem = mem_prompt.shape[1]
    xw = X_HEADS * HEAD_DIM
    qk = RET_HEADS * HEAD_DIM
    sw = ssm_d.shape[1]
    g = ssm_lam_re.shape[1]

    w_in_b = w_in[0].astype(BF16)
    tables = s5_tables(ssm_lam_re[0], ssm_lam_im[0], ssm_log_dt[0], ssm_b_re[0], ssm_b_im[0],
                       ssm_c_re[0], ssm_c_im[0], ssm_d[0])
    mix_w = (ret_gn[0], w_ret_o[0].astype(BF16), w_ssm_glu[0].astype(BF16), w_ssm_o[0].astype(BF16),
             w_x_o[0].astype(BF16), w_out[0].astype(BF16))

    kv = norm_matmul(mem_prompt.reshape(bp * n_mem, dm), mem_norm[0], w_mem_kv[0].astype(BF16), F32)
    mk_p = kv[:, :xw].reshape(bp, n_mem * X_HEADS, HEAD_DIM)
    mv_p = kv[:, xw:].reshape(bp, n_mem * X_HEADS, HEAD_DIM)

    def group(x, pos, mem_k, mem_v, s_ret, h_re, h_im, nb, tl):
        bsz, length, _ = x.shape
        z, u = norm_matmul(x.reshape(bsz * length, dm), norm_mix[0], w_in_b, BF16,
                           f32_cols=(4 * qk, 4 * qk + sw),
                           acts=((3 * qk, 4 * qk, "silu"), (4 * qk + sw + xw, w_in_b.shape[1], "sigmoid")))
        y, hf_re, hf_im = s5_apply(u, bsz, h_re, h_im, tables)
        h, s_new = mixer(x, z, y, pos, mem_k, mem_v, s_ret, *mix_w, nb=nb, tl=tl)
        return h, s_new, hf_re, hf_im

    zero_ret = jnp.zeros((bp, RET_HEADS, HEAD_DIM, HEAD_DIM), F32)
    zero_ssm = jnp.zeros((bp, g, SSM_STATE), F32)
    h_p, ret_p, sre_p, sim_p = group(x_prompt, jnp.arange(lp, dtype=jnp.int32), mk_p, mv_p,
                                     zero_ret, zero_ssm, zero_ssm, 1, ROW_TILE)
    h_s, ret_s, sre_s, sim_s = group(x_sample, PAST_LEN + jnp.arange(ls, dtype=jnp.int32),
                                     cache_mem_k[0].reshape(bs, n_mem * X_HEADS, HEAD_DIM),
                                     cache_mem_v[0].reshape(bs, n_mem * X_HEADS, HEAD_DIM),
                                     state_ret[0], state_ssm_re[0], state_ssm_im[0], ROW_TILE // ls, ls)

    y_p, y_s = moe_and_final_norm(h_p, h_s, norm_ffn[0], w_router[0], b_router[0],
                                  w_gate_up[0], b_gate_up[0], w_down[0], b_down[0], final_norm)
    return (y_p.reshape(bp, lp, dm), y_s.reshape(bs, ls, dm), ret_p[None], sre_p[None], sim_p[None],
            mk_p.reshape(1, bp, n_mem, X_HEADS, HEAD_DIM), mv_p.reshape(1, bp, n_mem, X_HEADS, HEAD_DIM),
            ret_s[None], sre_s[None], sim_s[None])
```

```python
import functools
import math

import jax
import jax.numpy as jnp
import numpy as np
from jax import lax
from jax.experimental import pallas as pl
from jax.experimental.pallas import tpu as pltpu

F32 = jnp.float32
BF16 = jnp.bfloat16

EPS = 1e-6
CHUNK = 64
PAST_LEN = 2048
ROPE_BASE = 10000.0
RET_HEADS = 4
X_HEADS = 4
HEAD_DIM = 128
SSM_GROUP = 16
SSM_STATE = 64
TOP_K = 4
SWIGLU_ALPHA = 1.702
SWIGLU_LIMIT = 7.0

VMEM_LIMIT = 52 * 1024 * 1024
S5_CHUNK = 8
S5_LANES = 128
ROW_TILE = 256
PROJ_TILE = 512
FFN_BLOCK = 256
ROW_ALIGN = 8
MIX_COLS = 512
NT_DIMS = (((1,), (1,)), ((), ()))
TN_DIMS = (((0,), (0,)), ((), ()))


def _params(n_axes=1):
    return pltpu.CompilerParams(dimension_semantics=("arbitrary",) * n_axes,
                                vmem_limit_bytes=VMEM_LIMIT)


def _resident(shape):
    nd = len(shape)
    return pl.BlockSpec(shape, lambda *_: (0,) * nd, pipeline_mode=pl.Buffered(1))


def _rms(x, w):
    return x * lax.rsqrt(jnp.mean(x * x, axis=-1, keepdims=True) + EPS) * w


def _sigmoid(x):
    return 0.5 * jnp.tanh(0.5 * x) + 0.5


_ACTIVATIONS = {"sigmoid": _sigmoid, "silu": lambda v: v * _sigmoid(v)}


def _norm_matmul_kernel(x_ref, nw_ref, w_ref, o_ref, *f32_refs, n_chunk, f32_cols, acts):
    xb = _rms(x_ref[...], nw_ref[...]).astype(BF16)
    for n0 in range(0, o_ref.shape[1], n_chunk):
        r = jnp.dot(xb, w_ref[:, n0:n0 + n_chunk], preferred_element_type=F32)
        if f32_cols is not None and n0 <= f32_cols[0] and f32_cols[1] <= n0 + n_chunk:
            f32_refs[0][...] = r[:, f32_cols[0] - n0:f32_cols[1] - n0]
        cuts = sorted({n0, n0 + n_chunk} | {c for lo, hi, _ in acts for c in (lo, hi) if n0 < c < n0 + n_chunk})
        for a, b in zip(cuts[:-1], cuts[1:]):
            piece = r[:, a - n0:b - n0]
            for lo, hi, kind in acts:
                if lo <= a and b <= hi:
                    piece = _ACTIVATIONS[kind](piece)
            o_ref[:, a:b] = piece.astype(o_ref.dtype)


def norm_matmul(x, nw, w, out_dtype, f32_cols=None, acts=()):
    t, d = x.shape
    n = w.shape[1]
    n_chunk = min(n, 1024)
    tm = PROJ_TILE
    assert t % tm == 0 and n % n_chunk == 0, (t, n)
    out_specs = [pl.BlockSpec((tm, n), lambda i: (i, 0))]
    out_shape = [jax.ShapeDtypeStruct((t, n), out_dtype)]
    if f32_cols is not None:
        lo, hi = f32_cols
        assert lo // n_chunk == (hi - 1) // n_chunk
        out_specs.append(pl.BlockSpec((tm, hi - lo), lambda i: (i, 0)))
        out_shape.append(jax.ShapeDtypeStruct((t, hi - lo), F32))
    out = pl.pallas_call(
        functools.partial(_norm_matmul_kernel, n_chunk=n_chunk, f32_cols=f32_cols, acts=tuple(acts)),
        grid=(t // tm,),
        in_specs=[pl.BlockSpec((tm, d), lambda i: (i, 0)), _resident((1, d)), _resident((d, n))],
        out_specs=out_specs, out_shape=out_shape,
        compiler_params=_params(), name="norm_matmul",
    )(x, nw.reshape(1, d), w)
    return out if f32_cols is not None else out[0]


def s5_tables(lam_re, lam_im, log_dt, b_re, b_im, c_re, c_im, d_skip):
    g, n, p = b_re.shape
    s = S5_CHUNK
    gl = S5_LANES // p
    j = g // gl
    hi = lax.Precision.HIGHEST
    dt = jnp.exp(log_dt)[:, None]
    a_re = jnp.exp(lam_re * dt) * jnp.cos(lam_im * dt)
    a_im = jnp.exp(lam_re * dt) * jnp.sin(lam_im * dt)
    den = lam_re * lam_re + lam_im * lam_im
    nr, ni = a_re - 1.0, a_im
    co_re = (nr * lam_re + ni * lam_im) / den
    co_im = (ni * lam_re - nr * lam_im) / den
    bb_re = co_re[..., None] * b_re - co_im[..., None] * b_im
    bb_im = co_re[..., None] * b_im + co_im[..., None] * b_re
    tau = jnp.arange(s + 1, dtype=F32)[:, None, None]
    pw_mag = jnp.exp(lam_re * dt * tau)
    pw_re = pw_mag * jnp.cos(lam_im * dt * tau)
    pw_im = pw_mag * jnp.sin(lam_im * dt * tau)
    ca_re = c_re[None] * pw_re[:, :, None, :] - c_im[None] * pw_im[:, :, None, :]
    ca_im = c_re[None] * pw_im[:, :, None, :] + c_im[None] * pw_re[:, :, None, :]
    bq_re, bq_im = bb_re.transpose(0, 2, 1), bb_im.transpose(0, 2, 1)
    kq = jnp.sum(ca_re[:s, :, :, None, :] * bq_re[None, :, None, :, :]
                 - ca_im[:s, :, :, None, :] * bq_im[None, :, None, :, :], axis=-1).transpose(1, 0, 3, 2)
    ts = np.arange(s)
    lag_onehot = (ts[None, None, :] - ts[None, :, None] == ts[:, None, None]).astype(np.float32)
    rev = s - 1 - ts
    w_re = pw_re[rev][:, :, :, None] * bb_re[None] - pw_im[rev][:, :, :, None] * bb_im[None]
    w_im = pw_re[rev][:, :, :, None] * bb_im[None] + pw_im[rev][:, :, :, None] * bb_re[None]
    m_c = (jnp.einsum('gxqp,xst->gsqtp', kq, lag_onehot, precision=hi)
           .reshape(j, gl, s, p, s * p).transpose(0, 2, 1, 3, 4).reshape(j, s * gl * p, s * p))
    w_c = (jnp.stack([w_re, w_im]).reshape(2, s, j, gl, n, p).transpose(2, 1, 3, 5, 0, 4)
           .reshape(j, s * gl * p, 2 * n))
    v_c = (jnp.stack([ca_re[1:], -ca_im[1:]]).reshape(2, s, j, gl, p, n).transpose(2, 0, 3, 5, 1, 4)
           .reshape(j, 2 * gl * n, s * p))
    fl = s * gl * p
    c_io = np.arange(fl)
    c_st = np.arange(2 * gl * n)
    k_io = np.arange(s * p)
    k_st = np.arange(2 * n)
    spread_io = ((k_io[:, None] // p == c_io[None, :] // (gl * p)) & (k_io[:, None] % p == c_io[None, :] % p))
    spread_st = ((k_st[:, None] // n == c_st[None, :] // (gl * n)) & (k_st[:, None] % n == c_st[None, :] % n))
    grp_io = (c_io // p) % gl
    grp_st = (c_st // n) % gl

    def expand(compact, spread, row_grp, col_grp):
        full = jnp.einsum('jrk,kc->jrc', compact.astype(BF16), jnp.asarray(spread, BF16),
                          preferred_element_type=F32)
        return jnp.where(jnp.asarray(row_grp[:, None] == col_grp[None, :]), full, 0.0).astype(BF16)

    m = expand(m_c, spread_io, grp_io, grp_io)
    w = expand(w_c, spread_st, grp_io, grp_st)
    v = expand(v_c, spread_io, grp_st, grp_io)
    a_s_re = pw_re[s].reshape(1, g * n)
    a_s_im = pw_im[s].reshape(1, g * n)
    dtab = jnp.broadcast_to(d_skip.reshape(j, 1, 1, gl * p), (j, 1, s, gl * p)).reshape(j, 1, s * gl * p)
    return jnp.concatenate([m, v], axis=1), w, a_s_re, a_s_im, dtab


def _s5_flat(u_ref):
    return jnp.concatenate([u_ref[:, t, :] for t in range(u_ref.shape[1])], axis=1)


def _s5a_kernel(u_ref, w_ref, ire_ref, iim_ref):
    r = jnp.dot(_s5_flat(u_ref).astype(BF16), w_ref[0], preferred_element_type=F32)
    half = r.shape[1] // 2
    ire_ref[...] = r[:, :half]
    iim_ref[...] = r[:, half:]


def _s5scan_kernel(ire_ref, iim_ref, ar_ref, ai_ref, h0r_ref, h0i_ref,
                   hpr_ref, hpi_ref, hfr_ref, hfi_ref):
    nb, nc, _ = ire_ref.shape
    ar, ai = ar_ref[...], ai_ref[...]

    def body(c, carry):
        out = []
        for b in range(nb):
            hr, hi = carry[2 * b], carry[2 * b + 1]
            hpr_ref[b, pl.ds(c, 1), :] = hr
            hpi_ref[b, pl.ds(c, 1), :] = hi
            out.append(ar * hr - ai * hi + ire_ref[b, pl.ds(c, 1), :])
            out.append(ar * hi + ai * hr + iim_ref[b, pl.ds(c, 1), :])
        return tuple(out)

    init = []
    for b in range(nb):
        init += [h0r_ref[b], h0i_ref[b]]
    fin = lax.fori_loop(0, nc, body, tuple(init))
    for b in range(nb):
        hfr_ref[b] = fin[2 * b]
        hfi_ref[b] = fin[2 * b + 1]


def _s5b_kernel(u_ref, hpr_ref, hpi_ref, mv_ref, d_ref, y_ref):
    uf = _s5_flat(u_ref)
    lhs = jnp.concatenate([uf.astype(BF16), hpr_ref[...].astype(BF16), hpi_ref[...].astype(BF16)], axis=1)
    y = d_ref[0] * uf + jnp.dot(lhs, mv_ref[0], preferred_element_type=F32)
    lanes = y_ref.shape[2]
    for t in range(y_ref.shape[1]):
        y_ref[:, t, :] = y[:, t * lanes:(t + 1) * lanes]


def s5_apply(u, bsz, h0_re, h0_im, tables):
    mv, w, a_re, a_im, dtab = tables
    tokens, width = u.shape
    nj = w.shape[0]
    s = S5_CHUNK
    rows = tokens // s
    nc = rows // bsz
    lanes = a_re.shape[1]
    half = w.shape[2] // 2
    fl = w.shape[1]
    rt = min(rows, 1024)
    assert tokens % (s * bsz) == 0 and rows % rt == 0 and bsz % 4 == 0, (tokens, bsz)
    u3 = u.reshape(rows, s, width)
    u_spec = pl.BlockSpec((rt, s, S5_LANES), lambda j, r: (r, 0, j))
    st_spec = pl.BlockSpec((rt, half), lambda j, r: (r, j))
    tab_spec = pl.BlockSpec((1, fl, fl), lambda j, r: (j, 0, 0))
    inj_re, inj_im = pl.pallas_call(
        _s5a_kernel, grid=(nj, rows // rt),
        in_specs=[u_spec, tab_spec],
        out_specs=[st_spec, st_spec],
        out_shape=[jax.ShapeDtypeStruct((rows, lanes), F32)] * 2,
        compiler_params=_params(2), name="s5_chunk_in",
    )(u3, w)

    sb, lw = 4, (512 if nc > 16 else lanes)
    seq_spec = pl.BlockSpec((sb, nc, lw), lambda b, l: (b, 0, l))
    vec_spec = pl.BlockSpec((sb, 1, lw), lambda b, l: (b, 0, l))
    atab_spec = pl.BlockSpec((1, lw), lambda b, l: (0, l))
    hp_re, hp_im, hf_re, hf_im = pl.pallas_call(
        _s5scan_kernel, grid=(bsz // sb, lanes // lw),
        in_specs=[seq_spec, seq_spec, atab_spec, atab_spec, vec_spec, vec_spec],
        out_specs=[seq_spec, seq_spec, vec_spec, vec_spec],
        out_shape=[jax.ShapeDtypeStruct((bsz, nc, lanes), F32)] * 2
        + [jax.ShapeDtypeStruct((bsz, 1, lanes), F32)] * 2,
        compiler_params=_params(2), name="s5_scan",
    )(inj_re.reshape(bsz, nc, lanes), inj_im.reshape(bsz, nc, lanes), a_re, a_im,
      h0_re.reshape(bsz, 1, lanes), h0_im.reshape(bsz, 1, lanes))

    y3 = pl.pallas_call(
        _s5b_kernel, grid=(nj, rows // rt),
        in_specs=[u_spec, st_spec, st_spec, pl.BlockSpec((1, 2 * fl, fl), lambda j, r: (j, 0, 0)),
                  pl.BlockSpec((1, 1, fl), lambda j, r: (j, 0, 0))],
        out_specs=u_spec,
        out_shape=jax.ShapeDtypeStruct((rows, s, width), F32),
        compiler_params=_params(2), name="s5_chunk_out",
    )(u3, hp_re.reshape(rows, lanes), hp_im.reshape(rows, lanes), mv, dtab)
    g = lanes // SSM_STATE
    return y3.reshape(tokens, width), hf_re.reshape(bsz, g, SSM_STATE), hf_im.reshape(bsz, g, SSM_STATE)


def _retention_gammas():
    return 1.0 - np.exp2(-5.0 - np.arange(RET_HEADS, dtype=np.float64))


def retention_tables(tile, chunk, nb):
    gam = _retention_gammas()[:, None, None]
    i = np.arange(tile)[:, None]
    j = np.arange(tile)[None, :]
    same = (i // chunk) == (j // chunk)
    earlier = (j // chunk) < (i // chunk)
    dist = np.where(same, np.abs(i - j), np.where(earlier, i - j, 0))
    dmask = np.where(same | earlier, gam ** dist[None], 0.0)
    qw = np.broadcast_to((gam[:, :, 0] ** (np.arange(tile) + 1.0))[:, :, None], (RET_HEADS, tile, HEAD_DIM))
    kw = np.broadcast_to((gam[:, :, 0] ** (tile - 1.0 - np.arange(tile)))[:, :, None], (RET_HEADS, tile, HEAD_DIM))
    dmask = np.stack([np.kron(np.eye(nb), m) for m in dmask])
    return (jnp.asarray(dmask, F32), jnp.asarray(np.tile(qw, (1, nb, 1)), F32),
            jnp.asarray(np.tile(kw, (1, nb, 1)), F32), tuple(float(x) for x in _retention_gammas() ** tile))


def rope_tables(pos):
    half = HEAD_DIM // 2
    inv = jnp.exp(-math.log(ROPE_BASE) * 2.0 * jnp.arange(half, dtype=F32) / HEAD_DIM)
    ang = pos.astype(F32)[:, None] * inv[None, :]
    cos, sin = jnp.cos(ang), jnp.sin(ang)
    cosf = jnp.concatenate([cos, cos], axis=1)
    sinf = jnp.concatenate([-sin, sin], axis=1)
    return cosf, sinf


def _mixer_kernel(x_ref, zq_ref, xq_ref, gl_ref, y_ref, cq_ref, sq_ref, ck_ref, sk_ref,
                  dm_ref, qw_ref, kw_ref, vm_ref, am_ref, mk_ref, mv_ref, s0_ref, gn_ref,
                  wro_ref, wglu_ref, wso_ref, wxo_ref, wout_ref,
                  h_ref, sout_ref, s_scr, o_scr, xo_scr, glu_scr, mg_scr, *, nb, tl, tile_decay):
    hd = HEAD_DIM
    qk = RET_HEADS * hd
    n_mem = mk_ref.shape[1] // X_HEADS

    def own_blocks(a, width):
        if nb == 1:
            return a
        return jnp.concatenate([a[n * tl:(n + 1) * tl, n * width:(n + 1) * width] for n in range(nb)], axis=0)

    def spread_blocks(a, mask_ref):
        if nb == 1:
            return a
        return jnp.concatenate([a] * nb, axis=1) * mask_ref[...]

    @pl.when(pl.program_id(1) == 0)
    def _():
        for n in range(nb):
            for h in range(RET_HEADS):
                s_scr[h, :, n * hd:(n + 1) * hd] = s0_ref[n, h]

    cq, sq, ck, sk = cq_ref[...], sq_ref[...], ck_ref[...], sk_ref[...]
    for h in range(RET_HEADS):
        c0 = h * hd
        q = zq_ref[:, c0:c0 + hd].astype(F32)
        k = zq_ref[:, qk + c0:qk + c0 + hd].astype(F32)
        v = zq_ref[:, 2 * qk + c0:2 * qk + c0 + hd]
        g = zq_ref[:, 3 * qk + c0:3 * qk + c0 + hd].astype(F32)
        qr = q * cq + pltpu.roll(q, hd // 2, 1) * sq
        kr = k * ck + pltpu.roll(k, hd // 2, 1) * sk
        sc = lax.dot_general(qr.astype(BF16), kr.astype(BF16), NT_DIMS,
                             preferred_element_type=F32) * dm_ref[h]
        o = jnp.dot(sc.astype(BF16), v, preferred_element_type=F32)
        s_old = s_scr[h]
        o += own_blocks(jnp.dot((qr * qw_ref[h]).astype(BF16), s_old.astype(BF16),
                                preferred_element_type=F32), hd)
        kv = lax.dot_general((kr * kw_ref[h]).astype(BF16), spread_blocks(v, vm_ref), TN_DIMS,
                             preferred_element_type=F32)
        s_scr[h] = tile_decay[h] * s_old + kv
        d = o - jnp.mean(o, axis=-1, keepdims=True)
        on = d * lax.rsqrt(jnp.mean(d * d, axis=-1, keepdims=True) + EPS) * gn_ref[:, c0:c0 + hd]
        o_scr[:, c0:c0 + hd] = (on * g).astype(BF16)
        mem_rows = pl.ds(h, n_mem, stride=X_HEADS)
        mkh = mk_ref[:, mem_rows, :].astype(BF16).reshape(nb * n_mem, hd)
        mvh = mv_ref[:, mem_rows, :].astype(BF16).reshape(nb * n_mem, hd)
        s = own_blocks(lax.dot_general(xq_ref[:, c0:c0 + hd], mkh, NT_DIMS,
                                       preferred_element_type=F32), n_mem) * (hd ** -0.5)
        e = jnp.exp(s - jnp.max(s, axis=-1, keepdims=True))
        p = (e / jnp.sum(e, axis=-1, keepdims=True)).astype(BF16)
        xo_scr[:, c0:c0 + hd] = jnp.dot(spread_blocks(p, am_ref), mvh,
                                        preferred_element_type=F32).astype(BF16)

    cw = MIX_COLS
    dm = h_ref.shape[1]
    half = wglu_ref.shape[1] // 2
    yb = jax.nn.gelu(y_ref[...]).astype(BF16)
    for c0 in range(0, half, cw):
        ga = jnp.dot(yb, wglu_ref[:, c0:c0 + cw], preferred_element_type=F32)
        gb = jnp.dot(yb, wglu_ref[:, half + c0:half + c0 + cw], preferred_element_type=F32)
        glu_scr[:, c0:c0 + cw] = (ga * _sigmoid(gb)).astype(BF16)
    for c0 in range(0, dm, cw):
        cols = slice(c0, c0 + cw)
        ret = jnp.dot(o_scr[...], wro_ref[:, cols], preferred_element_type=F32)
        ssm = jnp.dot(glu_scr[...], wso_ref[:, cols], preferred_element_type=F32)
        xb = jnp.dot(xo_scr[...], wxo_ref[:, cols], preferred_element_type=F32)
        merged = (gl_ref[:, c0:c0 + cw].astype(F32) * ret
                  + gl_ref[:, dm + c0:dm + c0 + cw].astype(F32) * ssm
                  + gl_ref[:, 2 * dm + c0:2 * dm + c0 + cw].astype(F32) * xb)
        mg_scr[:, cols] = merged.astype(BF16)
    for c0 in range(0, dm, cw):
        cols = slice(c0, c0 + cw)
        h_ref[:, cols] = x_ref[:, cols] + jnp.dot(mg_scr[...], wout_ref[:, cols], preferred_element_type=F32)
    for n in range(nb):
        for h in range(RET_HEADS):
            sout_ref[n, h] = s_scr[h, :, n * hd:(n + 1) * hd]


def mixer(x, z, y_ssm, pos, mem_k, mem_v, s0, ret_gn, w_ret_o, w_ssm_glu, w_ssm_o, w_x_o, w_out, *, nb, tl):
    bsz, length, dm = x.shape
    chunk = min(CHUNK, length)
    nl = length // tl
    assert length % tl == 0 and bsz % nb == 0, (bsz, length, nb, tl)
    rows = nb * tl
    qk = RET_HEADS * HEAD_DIM
    sw = y_ssm.shape[1]
    xw = X_HEADS * HEAD_DIM
    gate_col = (4 * qk + sw + xw)
    assert gate_col % (3 * dm) == 0 and (4 * qk + sw) % xw == 0
    dmask, qw, kw, tile_decay = retention_tables(tl, chunk, nb)
    seq_of_row = np.arange(rows)[:, None] // tl
    own_v = jnp.asarray(seq_of_row == np.arange(nb * HEAD_DIM)[None, :] // HEAD_DIM, BF16)
    n_mem = mem_k.shape[1] // X_HEADS
    own_mem = jnp.asarray(seq_of_row == np.arange(nb * n_mem)[None, :] // n_mem, BF16)
    cosf, sinf = (jnp.broadcast_to(t.reshape(nl, 1, tl, HEAD_DIM), (nl, nb, tl, HEAD_DIM)).reshape(nl * rows, HEAD_DIM)
                  for t in rope_tables(pos))
    scale = HEAD_DIM ** -0.5
    row_map = lambda b, l: (b * nl + l, 0)
    tab_map = lambda b, l: (l, 0)
    st_spec = pl.BlockSpec((nb, RET_HEADS, HEAD_DIM, HEAD_DIM), lambda b, l: (b, 0, 0, 0))
    mem_spec = pl.BlockSpec((nb,) + mem_k.shape[1:], lambda b, l: (b,) + (0,) * (mem_k.ndim - 1))
    h, s_out = pl.pallas_call(
        functools.partial(_mixer_kernel, nb=nb, tl=tl, tile_decay=tile_decay),
        grid=(bsz // nb, nl),
        in_specs=[pl.BlockSpec((rows, dm), row_map),
                  pl.BlockSpec((rows, 4 * qk), row_map),
                  pl.BlockSpec((rows, xw), lambda b, l: (b * nl + l, (4 * qk + sw) // xw)),
                  pl.BlockSpec((rows, 3 * dm), lambda b, l: (b * nl + l, gate_col // (3 * dm))),
                  pl.BlockSpec((rows, sw), row_map),
                  pl.BlockSpec((rows, HEAD_DIM), tab_map), pl.BlockSpec((rows, HEAD_DIM), tab_map),
                  pl.BlockSpec((rows, HEAD_DIM), tab_map), pl.BlockSpec((rows, HEAD_DIM), tab_map),
                  _resident(dmask.shape), _resident(qw.shape), _resident(kw.shape),
                  _resident(own_v.shape), _resident(own_mem.shape),
                  mem_spec, mem_spec, st_spec, _resident((1, qk)),
                  _resident(w_ret_o.shape), _resident(w_ssm_glu.shape), _resident(w_ssm_o.shape),
                  _resident(w_x_o.shape), _resident(w_out.shape)],
        out_specs=[pl.BlockSpec((rows, dm), row_map), st_spec],
        out_shape=[jax.ShapeDtypeStruct((bsz * length, dm), F32),
                   jax.ShapeDtypeStruct(s0.shape, F32)],
        scratch_shapes=[pltpu.VMEM((RET_HEADS, HEAD_DIM, nb * HEAD_DIM), F32),
                        pltpu.VMEM((rows, qk), BF16), pltpu.VMEM((rows, xw), BF16),
                        pltpu.VMEM((rows, w_ssm_o.shape[0]), BF16), pltpu.VMEM((rows, dm), BF16)],
        compiler_params=_params(2), name="mixer",
    )(x.reshape(bsz * length, dm), z, z, z, y_ssm,
      cosf * scale, sinf * scale, cosf, sinf, dmask, qw, kw, own_v, own_mem,
      mem_k, mem_v, s0, ret_gn.reshape(1, qk), w_ret_o, w_ssm_glu, w_ssm_o, w_x_o, w_out)
    return h, s_out


def _router_kernel(hp_ref, hs_ref, nw_ref, wrt_ref, br_ref, tri_ref, low_ref,
                   xn_ref, gw_ref, crow_ref, cnt_ref, *, prompt_tiles):
    h = jnp.where(pl.program_id(0) < prompt_tiles, hp_ref[...], hs_ref[...])
    xn = _rms(h, nw_ref[...]).astype(BF16)
    xn_ref[...] = xn
    logits = lax.dot_general(wrt_ref[...], xn, NT_DIMS, preferred_element_type=F32) + br_ref[...]
    ne = logits.shape[0]
    iota = lax.broadcasted_iota(jnp.int32, logits.shape, 0)
    rest = logits
    sel = jnp.zeros(logits.shape, jnp.bool_)
    vals, idxs = [], []
    for _ in range(TOP_K):
        m = jnp.max(rest, axis=0, keepdims=True)
        ix = jnp.min(jnp.where(rest == m, iota, ne), axis=0, keepdims=True)
        hit = iota == ix
        vals.append(m)
        idxs.append(ix)
        sel = jnp.logical_or(sel, hit)
        rest = jnp.where(hit, -jnp.inf, rest)
    es = [jnp.exp(v - vals[0]) for v in vals]
    tot = es[0] + es[1] + es[2] + es[3]
    before = jnp.dot(sel.astype(BF16), tri_ref[...], preferred_element_type=F32)
    cnt = jnp.sum(sel.astype(F32), axis=1, keepdims=True)
    seg = jnp.floor((cnt + (ROW_ALIGN - 1.0)) * (1.0 / ROW_ALIGN)) * ROW_ALIGN
    start = jnp.dot(low_ref[...], jnp.broadcast_to(seg, before.shape), precision=lax.Precision.HIGHEST,
                    preferred_element_type=F32)
    place = start + before
    for k in range(TOP_K):
        gw_ref[k:k + 1, :] = es[k] / tot
        crow_ref[k:k + 1, :] = jnp.sum(jnp.where(iota == idxs[k], place, 0.0), axis=0,
                                       keepdims=True).astype(jnp.int32)
    cnt_ref[...] = jnp.broadcast_to(cnt, cnt_ref.shape)


def router(h_p, h_s, norm_ffn, w_router, b_router):
    dm = h_p.shape[1]
    ne = w_router.shape[1]
    tt = ROW_TILE
    npt, nst = h_p.shape[0] // tt, h_s.shape[0] // tt
    t = (npt + nst) * tt
    tri = jnp.asarray(np.triu(np.ones((tt, tt), np.float32), k=1), BF16)
    low = jnp.asarray(np.tril(np.ones((ne, ne), np.float32), k=-1))
    tok_spec = pl.BlockSpec((TOP_K, tt), lambda i: (0, i))
    return pl.pallas_call(
        functools.partial(_router_kernel, prompt_tiles=npt), grid=(npt + nst,),
        in_specs=[pl.BlockSpec((tt, dm), lambda i: (jnp.minimum(i, npt - 1), 0)),
                  pl.BlockSpec((tt, dm), lambda i: (jnp.maximum(i - npt, 0), 0)),
                  _resident((1, dm)), _resident((ne, dm)), _resident((ne, 1)), _resident((tt, tt)),
                  _resident((ne, ne))],
        out_specs=[pl.BlockSpec((tt, dm), lambda i: (i, 0)), tok_spec, tok_spec,
                   pl.BlockSpec((ne, 128), lambda i: (i, 0))],
        out_shape=[jax.ShapeDtypeStruct((t, dm), BF16),
                   jax.ShapeDtypeStruct((TOP_K, t), F32), jax.ShapeDtypeStruct((TOP_K, t), jnp.int32),
                   jax.ShapeDtypeStruct(((npt + nst) * ne, 128), F32)],
        compiler_params=_params(), name="router",
    )(h_p, h_s, norm_ffn.reshape(1, dm), w_router.T.astype(BF16), b_router.reshape(ne, 1), tri, low)


def _pack_bf16_pairs(x, exact=False):
    n = x.shape[1] // 2
    lo, hi = x[:, :n], x[:, n:]
    if not exact:
        lo, hi = lo.astype(BF16).astype(F32), hi.astype(BF16).astype(F32)
    lo = lax.bitcast_convert_type(lo, jnp.int32)
    hi = lax.bitcast_convert_type(hi, jnp.int32)
    return lax.shift_right_logical(lo, 16) | (hi & -65536)


def _unpack_bf16_pairs(u):
    lo = lax.bitcast_convert_type(lax.shift_left(u, 16), F32).astype(BF16)
    hi = lax.bitcast_convert_type(u & -65536, F32).astype(BF16)
    return lo, hi


def _split_count(n, fn):
    def quad(jq, carry):
        fn(4 * jq, 4)
        return carry

    lax.fori_loop(0, n // 4, quad, 0)

    @pl.when(n % 4 >= 2)
    def _():
        fn(n // 4 * 4, 2)

    @pl.when(n % 2 == 1)
    def _():
        fn(n // 2 * 2, 1)


def _chunk_loop(n, fn):
    def quad(jq, carry):
        for u in range(4):
            fn(4 * jq + u)
        return carry

    def single(j, carry):
        fn(j)
        return carry

    lax.fori_loop(0, n // 4, quad, 0)
    lax.fori_loop(n // 4 * 4, n, single, 0)


def _dispatch_kernel(dst_ref, tch_ref, zs_ref, zn_ref, used_ref,
                     xn_ref, crow_ref, xs_ref, cbuf, zbuf, sems, zsem, *, ne, bm):
    i = pl.program_id(0)
    nt = pl.num_programs(0)
    slot = lax.rem(i, 2)
    tt = xn_ref.shape[0]
    cr = cbuf.shape[1]
    ra = ROW_ALIGN
    n_blocks = xs_ref.shape[0] // bm

    def chunk_copy(sl, src_row, dst_row, rows):
        return pltpu.make_async_copy(cbuf.at[sl, pl.ds(src_row, rows)], xs_ref.at[pl.ds(dst_row, rows)],
                                     sems.at[sl])

    def wait_chunks(sl, n):
        _split_count(n, lambda j, m: chunk_copy(sl, 0, 0, ra * m).wait())

    def tail_copy(e, j):
        return pltpu.make_async_copy(zbuf.at[pl.ds(0, ra)],
                                     xs_ref.at[pl.ds(pl.multiple_of(zs_ref[e] + ra * j, ra), ra)], zsem)

    def block_copy(b):
        return pltpu.make_async_copy(zbuf, xs_ref.at[pl.ds(pl.multiple_of(b * bm, bm), bm)], zsem)

    def zero_fill(start):
        def per_expert(e, carry):
            def per_chunk(j, c2):
                (tail_copy(e, j).start() if start else tail_copy(e, j).wait())
                return c2
            lax.fori_loop(0, zn_ref[e], per_chunk, 0)
            return carry
        lax.fori_loop(0, ne, per_expert, 0)

        def per_block(b, carry):
            (block_copy(b).start() if start else block_copy(b).wait())
            return carry
        lax.fori_loop(used_ref[0], n_blocks, per_block, 0)

    @pl.when(i == 0)
    def _():
        zbuf[...] = jnp.zeros_like(zbuf)
        zero_fill(True)
        zero_fill(False)

    @pl.when(i >= 2)
    def _():
        wait_chunks(slot, tch_ref[jnp.maximum(i - 2, 0)])

    crow = crow_ref[...]
    rows = lax.broadcasted_iota(jnp.int32, (cr, tt), 0)
    hit = rows == crow[0:1, :]
    for k in range(1, TOP_K):
        hit = jnp.logical_or(hit, rows == crow[k:k + 1, :])
    packed = _pack_bf16_pairs(jnp.dot(jnp.where(hit, 1.0, 0.0).astype(BF16), xn_ref[...],
                                      preferred_element_type=F32), exact=True)
    cbuf[slot] = packed

    chunks_per_tile = cr // ra
    _chunk_loop(tch_ref[i], lambda c: chunk_copy(
        slot, pl.multiple_of(c * ra, ra), pl.multiple_of(dst_ref[i * chunks_per_tile + c], ra), ra).start())

    @pl.when(i == nt - 1)
    def _():
        wait_chunks(slot, tch_ref[i])
        wait_chunks(1 - slot, jnp.where(nt >= 2, tch_ref[jnp.maximum(i - 1, 0)], 0))


def dispatch(xn, crow, tables, n_rows, ne):
    t, dm = xn.shape
    tt = ROW_TILE
    bm = FFN_BLOCK
    cr = _compact_rows(ne)
    grid_spec = pltpu.PrefetchScalarGridSpec(
        num_scalar_prefetch=5, grid=(t // tt,),
        in_specs=[pl.BlockSpec((tt, dm), lambda i, *_: (i, 0)),
                  pl.BlockSpec((TOP_K, tt), lambda i, *_: (0, i))],
        out_specs=pl.BlockSpec(memory_space=pl.ANY),
        scratch_shapes=[pltpu.VMEM((2, cr, dm // 2), jnp.int32), pltpu.VMEM((bm, dm // 2), jnp.int32),
                        pltpu.SemaphoreType.DMA((2,)), pltpu.SemaphoreType.DMA(())])
    return pl.pallas_call(
        functools.partial(_dispatch_kernel, ne=ne, bm=bm), grid_spec=grid_spec,
        out_shape=jax.ShapeDtypeStruct((n_rows, dm // 2), jnp.int32),
        compiler_params=_params(), name="dispatch",
    )(*tables, xn, crow)


FFN_W_PIECES = (4, 2)


def _ffn_kernel(first_ref, nblk_ref, used_ref, next_ref, wslot_ref, lead_ref,
                xs_hbm, wgu_hbm, bgu_ref, wd_hbm, bd_ref, y_hbm,
                wgu_f32, wd_f32, wgu_bf, wd_bf, xbuf, ybuf, xsem, ysem, wsem, zsem, *, bm):
    e = pl.program_id(0)
    n = nblk_ref[e]
    b0 = first_ref[e]
    n_blocks = y_hbm.shape[0] // bm
    n_pieces = sum(FFN_W_PIECES)

    def block_rows(b):
        return pl.ds(pl.multiple_of(b * bm, bm), bm)

    used = used_ref[0]
    x_slots = xbuf.shape[0]

    def x_copy(g):
        sl = lax.rem(g, x_slots)
        return pltpu.make_async_copy(xs_hbm.at[block_rows(g)], xbuf.at[sl], xsem.at[sl])

    def y_copy(g):
        sl = lax.rem(g, 2)
        return pltpu.make_async_copy(ybuf.at[sl], y_hbm.at[block_rows(g)], ysem.at[sl])

    def w_piece(ex, sl, p):
        src, dst, q, parts = ((wgu_hbm, wgu_f32, p, FFN_W_PIECES[0]) if p < FFN_W_PIECES[0]
                              else (wd_hbm, wd_f32, p - FFN_W_PIECES[0], FFN_W_PIECES[1]))
        rows = src.shape[1] // parts
        return pltpu.make_async_copy(src.at[ex, pl.ds(q * rows, rows)], dst.at[sl, pl.ds(q * rows, rows)],
                                     wsem.at[sl, p])

    @pl.when(n > 0)
    def _():
        ws = wslot_ref[e]
        nxt = next_ref[e]

        @pl.when(lead_ref[0] == e)
        def _():
            for g in range(x_slots - 1):
                @pl.when(g < used)
                def _():
                    x_copy(g).start()
            for p in range(n_pieces):
                w_piece(e, ws, p).start()

        for p in range(n_pieces):
            w_piece(e, ws, p).wait()
        wgu_bf[...] = wgu_f32[ws].astype(BF16)
        wd_bf[...] = wd_f32[ws].astype(BF16)

        def block(j, carry):
            g = b0 + j
            x_copy(g).wait()

            @pl.when(g + x_slots - 1 < used)
            def _():
                x_copy(g + x_slots - 1).start()

            for p in range(n_pieces):
                @pl.when(jnp.logical_and(j == p, nxt >= 0))
                def _():
                    w_piece(nxt, 1 - ws, p).start()

            @pl.when(g >= 2)
            def _():
                y_copy(g - 2).wait()

            x = jnp.concatenate(_unpack_bf16_pairs(xbuf[lax.rem(g, x_slots)]), axis=1)
            hgu = jnp.dot(x, wgu_bf[...], preferred_element_type=F32) + bgu_ref[0]
            ff = hgu.shape[1] // 2
            gate = jnp.minimum(hgu[:, :ff], SWIGLU_LIMIT)
            up = jnp.clip(hgu[:, ff:], -SWIGLU_LIMIT, SWIGLU_LIMIT)
            act = (up + 1.0) * gate * _sigmoid(SWIGLU_ALPHA * gate)
            ybuf[lax.rem(g, 2)] = _pack_bf16_pairs(
                jnp.dot(act.astype(BF16), wd_bf[...], preferred_element_type=F32) + bd_ref[0])
            y_copy(g).start()
            return carry

        lax.fori_loop(0, n, block, 0)

        for p in range(n_pieces):
            @pl.when(jnp.logical_and(p >= n, nxt >= 0))
            def _():
                w_piece(nxt, 1 - ws, p).start()

    @pl.when(e == pl.num_programs(0) - 1)
    def _():
        for back in (2, 1):
            @pl.when(used >= back)
            def _():
                y_copy(used - back).wait()
        ybuf[0] = jnp.zeros(ybuf.shape[1:], ybuf.dtype)

        def zero_copy(b):
            return pltpu.make_async_copy(ybuf.at[0], y_hbm.at[block_rows(b)], zsem)

        def start(b, carry):
            zero_copy(b).start()
            return carry

        def wait(b, carry):
            zero_copy(b).wait()
            return carry

        lax.fori_loop(used_ref[0], n_blocks, start, 0)
        lax.fori_loop(used_ref[0], n_blocks, wait, 0)


def expert_ffn(xs, expert_tables, w_gate_up, b_gate_up, w_down, b_down):
    n_rows = xs.shape[0]
    ne, dm, ff2 = w_gate_up.shape
    bm = FFN_BLOCK
    grid_spec = pltpu.PrefetchScalarGridSpec(
        num_scalar_prefetch=6, grid=(ne,),
        in_specs=[pl.BlockSpec(memory_space=pl.ANY),
                  pl.BlockSpec(memory_space=pl.ANY),
                  pl.BlockSpec((1, 1, ff2), lambda e, *_: (e, 0, 0)),
                  pl.BlockSpec(memory_space=pl.ANY),
                  pl.BlockSpec((1, 1, dm), lambda e, *_: (e, 0, 0))],
        out_specs=pl.BlockSpec(memory_space=pl.ANY),
        scratch_shapes=[pltpu.VMEM((2, dm, ff2), F32), pltpu.VMEM((2, ff2 // 2, dm), F32),
                        pltpu.VMEM((dm, ff2), BF16), pltpu.VMEM((ff2 // 2, dm), BF16),
                        pltpu.VMEM((3, bm, dm // 2), jnp.int32), pltpu.VMEM((2, bm, dm // 2), jnp.int32),
                        pltpu.SemaphoreType.DMA((3,)), pltpu.SemaphoreType.DMA((2,)),
                        pltpu.SemaphoreType.DMA((2, sum(FFN_W_PIECES))), pltpu.SemaphoreType.DMA(())])
    return pl.pallas_call(
        functools.partial(_ffn_kernel, bm=bm), grid_spec=grid_spec,
        out_shape=jax.ShapeDtypeStruct((n_rows, dm // 2), jnp.int32),
        compiler_params=_params(), name="expert_ffn",
    )(*expert_tables, xs, w_gate_up, b_gate_up.reshape(ne, 1, ff2), w_down, b_down.reshape(ne, 1, dm))


def _combine_kernel(dst_ref, tch_ref,
                    crow_ref, gw_ref, hp_ref, hs_ref, fn_ref, yr_ref, yp_ref, ys_ref, ybuf, sems,
                    *, prompt_tiles):
    i = pl.program_id(0)
    nt = pl.num_programs(0)
    slot = lax.rem(i, 2)
    tt = hp_ref.shape[0]
    cr = ybuf.shape[1]
    ra = ROW_ALIGN

    def chunk_copy(sl, src_row, dst_row, rows):
        return pltpu.make_async_copy(yr_ref.at[pl.ds(src_row, rows)], ybuf.at[sl, pl.ds(dst_row, rows)],
                                     sems.at[sl])

    def fetch(tile, sl):
        chunks_per_tile = cr // ra
        _chunk_loop(tch_ref[tile], lambda c: chunk_copy(
            sl, pl.multiple_of(dst_ref[tile * chunks_per_tile + c], ra), pl.multiple_of(c * ra, ra), ra).start())

    @pl.when(i == 0)
    def _():
        ybuf[...] = jnp.zeros_like(ybuf)
        fetch(0, 0)

    @pl.when(i + 1 < nt)
    def _():
        fetch(i + 1, 1 - slot)

    _split_count(tch_ref[i], lambda j, m: chunk_copy(slot, 0, 0, ra * m).wait())

    y_lo, y_hi = _unpack_bf16_pairs(ybuf[slot])
    cols = lax.broadcasted_iota(jnp.int32, (tt, cr), 1)
    q = jnp.zeros((tt, cr), F32)
    for k in range(TOP_K):
        q = jnp.where(cols == crow_ref[:, k:k + 1], gw_ref[:, k:k + 1], q)
    qb = q.astype(BF16)
    moe = jnp.concatenate([jnp.dot(qb, y_lo, preferred_element_type=F32),
                           jnp.dot(qb, y_hi, preferred_element_type=F32)], axis=1)
    h = jnp.where(i < prompt_tiles, hp_ref[...], hs_ref[...])
    out = _rms(h + moe, fn_ref[...])

    @pl.when(i < prompt_tiles)
    def _():
        yp_ref[...] = out

    @pl.when(i >= prompt_tiles)
    def _():
        ys_ref[...] = out


def combine(crow_t, gw_t, h_p, h_s, final_norm, y_rows, tables, ne):
    dm = h_p.shape[1]
    tt = ROW_TILE
    npt, nst = h_p.shape[0] // tt, h_s.shape[0] // tt
    cr = _compact_rows(ne)
    p_map = lambda i, *_: (jnp.minimum(i, npt - 1), 0)
    s_map = lambda i, *_: (jnp.maximum(i - npt, 0), 0)
    grid_spec = pltpu.PrefetchScalarGridSpec(
        num_scalar_prefetch=2, grid=(npt + nst,),
        in_specs=[pl.BlockSpec((tt, TOP_K), lambda i, *_: (i, 0)),
                  pl.BlockSpec((tt, TOP_K), lambda i, *_: (i, 0)),
                  pl.BlockSpec((tt, dm), p_map), pl.BlockSpec((tt, dm), s_map),
                  pl.BlockSpec((1, dm), lambda i, *_: (0, 0)),
                  pl.BlockSpec(memory_space=pl.ANY)],
        out_specs=[pl.BlockSpec((tt, dm), p_map), pl.BlockSpec((tt, dm), s_map)],
        scratch_shapes=[pltpu.VMEM((2, cr, dm // 2), jnp.int32), pltpu.SemaphoreType.DMA((2,))])
    return pl.pallas_call(
        functools.partial(_combine_kernel, prompt_tiles=npt), grid_spec=grid_spec,
        out_shape=[jax.ShapeDtypeStruct(h_p.shape, F32), jax.ShapeDtypeStruct(h_s.shape, F32)],
        compiler_params=_params(), name="combine",
    )(*tables, crow_t, gw_t, h_p, h_s, final_norm.reshape(1, dm), y_rows)


def _compact_rows(ne):
    return -(-(TOP_K * ROW_TILE + ne * (ROW_ALIGN - 1)) // 128) * 128


def moe_and_final_norm(h_p, h_s, norm_ffn, w_router, b_router, w_gate_up, b_gate_up, w_down, b_down, final_norm):
    ne = w_router.shape[1]
    bm = FFN_BLOCK
    ra = ROW_ALIGN
    xn, gw, crow, cnt = router(h_p, h_s, norm_ffn, w_router, b_router)
    t = xn.shape[0]
    nt = t // ROW_TILE
    seg = -(-cnt[:, 0].astype(jnp.int32).reshape(nt, ne) // ra) * ra
    seg_before = jnp.cumsum(seg, axis=0) - seg
    rows_e = jnp.sum(seg, axis=0)
    padded = -(-rows_e // bm) * bm
    pad_ends = jnp.cumsum(padded)
    pad_starts = pad_ends - padded
    n_blocks = -(-(t * TOP_K + nt * ne * (ra - 1) + ne * (bm - 1)) // bm)
    n_used = pad_ends[-1] // bm
    experts = jnp.arange(ne, dtype=jnp.int32)
    chunks = seg // ra
    chunk_end = jnp.cumsum(chunks, axis=1)
    c_ids = jnp.arange(_compact_rows(ne) // ra, dtype=jnp.int32)
    owner = jnp.sum((chunk_end[:, None, :] <= c_ids[None, :, None]).astype(jnp.int32), axis=2)
    seg_shift = pad_starts[None, :] + seg_before - ra * (chunk_end - chunks)
    dst = ra * c_ids[None, :] + jnp.sum(
        jnp.where(owner[:, :, None] == experts[None, None, :], seg_shift[:, None, :], 0), axis=2)
    seg_tables = (dst.reshape(-1), chunk_end[:, -1])
    fill_tables = (pad_starts + rows_e, (padded - rows_e) // ra, n_used.reshape(1))
    to_i32 = lambda xs: tuple(x.astype(jnp.int32) for x in xs)
    xs = dispatch(xn, crow, to_i32(seg_tables + fill_tables), n_blocks * bm, ne)
    active = padded > 0
    later = jnp.where(active, experts, ne)
    next_active = jnp.concatenate([lax.cummin(later, reverse=True)[1:], jnp.full((1,), ne, jnp.int32)])
    expert_tables = (pad_starts // bm, padded // bm, n_used.reshape(1),
                     jnp.where(next_active < ne, next_active, -1),
                     (jnp.cumsum(active.astype(jnp.int32)) - active.astype(jnp.int32)) % 2,
                     jnp.min(later).reshape(1))
    y_rows = expert_ffn(xs, to_i32(expert_tables), w_gate_up, b_gate_up, w_down, b_down)
    return combine(crow.T, gw.T, h_p, h_s, final_norm, y_rows, to_i32(seg_tables), ne)


def kernel(x_prompt, x_sample, cache_mem_k, cache_mem_v, state_ret, state_ssm_re, state_ssm_im, mem_prompt, norm_mix, w_in, ret_gn, w_ret_o, ssm_lam_re, ssm_lam_im, ssm_log_dt, ssm_b_re, ssm_b_im, ssm_c_re, ssm_c_im, ssm_d, w_ssm_glu, w_ssm_o, mem_norm, w_mem_kv, w_x_o, w_out, norm_ffn, w_router, b_router, w_gate_up, b_gate_up, w_down, b_down, final_norm):
    assert norm_mix.shape[0] == 1, "single-layer step"
    bp, lp, dm = x_prompt.shape
    bs, ls, _ = x_sample.shape
    n_mem = mem_prompt.shape[1]
    xw = X_HEADS * HEAD_DIM
    qk = RET_HEADS * HEAD_DIM
    sw = ssm_d.shape[1]
    g = ssm_lam_re.shape[1]

    w_in_b = w_in[0].astype(BF16)
    tables = s5_tables(ssm_lam_re[0], ssm_lam_im[0], ssm_log_dt[0], ssm_b_re[0], ssm_b_im[0],
                       ssm_c_re[0], ssm_c_im[0], ssm_d[0])
    mix_w = (ret_gn[0], w_ret_o[0].astype(BF16), w_ssm_glu[0].astype(BF16), w_ssm_o[0].astype(BF16),
             w_x_o[0].astype(BF16), w_out[0].astype(BF16))

    kv = norm_matmul(mem_prompt.reshape(bp * n_mem, dm), mem_norm[0], w_mem_kv[0].astype(BF16), F32)
    mk_p = kv[:, :xw].reshape(bp, n_mem * X_HEADS, HEAD_DIM)
    mv_p = kv[:, xw:].reshape(bp, n_mem * X_HEADS, HEAD_DIM)

    def group(x, pos, mem_k, mem_v, s_ret, h_re, h_im, nb, tl):
        bsz, length, _ = x.shape
        z, u = norm_matmul(x.reshape(bsz * length, dm), norm_mix[0], w_in_b, BF16,
                           f32_cols=(4 * qk, 4 * qk + sw),
                           acts=((3 * qk, 4 * qk, "silu"), (4 * qk + sw + xw, w_in_b.shape[1], "sigmoid")))
        y, hf_re, hf_im = s5_apply(u, bsz, h_re, h_im, tables)
        h, s_new = mixer(x, z, y, pos, mem_k, mem_v, s_ret, *mix_w, nb=nb, tl=tl)
        return h, s_new, hf_re, hf_im

    zero_ret = jnp.zeros((bp, RET_HEADS, HEAD_DIM, HEAD_DIM), F32)
    zero_ssm = jnp.zeros((bp, g, SSM_STATE), F32)
    h_p, ret_p, sre_p, sim_p = group(x_prompt, jnp.arange(lp, dtype=jnp.int32), mk_p, mv_p,
                                     zero_ret, zero_ssm, zero_ssm, 1, ROW_TILE)
    h_s, ret_s, sre_s, sim_s = group(x_sample, PAST_LEN + jnp.arange(ls, dtype=jnp.int32),
                                     cache_mem_k[0].reshape(bs, n_mem * X_HEADS, HEAD_DIM),
                                     cache_mem_v[0].reshape(bs, n_mem * X_HEADS, HEAD_DIM),
                                     state_ret[0], state_ssm_re[0], state_ssm_im[0], ROW_TILE // ls, ls)

    y_p, y_s = moe_and_final_norm(h_p, h_s, norm_ffn[0], w_router[0], b_router[0],
                                  w_gate_up[0], b_gate_up[0], w_down[0], b_down[0], final_norm)
    return (y_p.reshape(bp, lp, dm), y_s.reshape(bs, ls, dm), ret_p[None], sre_p[None], sim_p[None],
            mk_p.reshape(1, bp, n_mem, X_HEADS, HEAD_DIM), mv_p.reshape(1, bp, n_mem, X_HEADS, HEAD_DIM),
            ret_s[None], sre_s[None], sim_s[None])
```

```python
import functools
import math

import jax
import jax.numpy as jnp
import numpy as np
from jax import lax
from jax.experimental import pallas as pl
from jax.experimental.pallas import tpu as pltpu

F32 = jnp.float32
BF16 = jnp.bfloat16

EPS = 1e-6
CHUNK = 64
PAST_LEN = 2048
ROPE_BASE = 10000.0
RET_HEADS = 4
X_HEADS = 4
HEAD_DIM = 128
SSM_GROUP = 16
SSM_STATE = 64
TOP_K = 4
SWIGLU_ALPHA = 1.702
SWIGLU_LIMIT = 7.0

VMEM_LIMIT = 52 * 1024 * 1024
S5_CHUNK = 8
S5_LANES = 128
ROW_TILE = 256
PROJ_TILE = 512
FFN_BLOCK = 256
ROW_ALIGN = 8
MIX_COLS = 512
NT_DIMS = (((1,), (1,)), ((), ()))
TN_DIMS = (((0,), (0,)), ((), ()))


def _params(n_axes=1):
    return pltpu.CompilerParams(dimension_semantics=("arbitrary",) * n_axes,
                                vmem_limit_bytes=VMEM_LIMIT)


def _resident(shape):
    nd = len(shape)
    return pl.BlockSpec(shape, lambda *_: (0,) * nd, pipeline_mode=pl.Buffered(1))


def _rms(x, w):
    return x * lax.rsqrt(jnp.mean(x * x, axis=-1, keepdims=True) + EPS) * w


def _sigmoid(x):
    return 0.5 * jnp.tanh(0.5 * x) + 0.5


_ACTIVATIONS = {"sigmoid": _sigmoid, "silu": lambda v: v * _sigmoid(v)}


def _norm_matmul_kernel(x_ref, nw_ref, w_ref, o_ref, *f32_refs, n_chunk, f32_cols, acts):
    xb = _rms(x_ref[...], nw_ref[...]).astype(BF16)
    for n0 in range(0, o_ref.shape[1], n_chunk):
        r = jnp.dot(xb, w_ref[:, n0:n0 + n_chunk], preferred_element_type=F32)
        if f32_cols is not None and n0 <= f32_cols[0] and f32_cols[1] <= n0 + n_chunk:
            f32_refs[0][...] = r[:, f32_cols[0] - n0:f32_cols[1] - n0]
        cuts = sorted({n0, n0 + n_chunk} | {c for lo, hi, _ in acts for c in (lo, hi) if n0 < c < n0 + n_chunk})
        for a, b in zip(cuts[:-1], cuts[1:]):
            piece = r[:, a - n0:b - n0]
            for lo, hi, kind in acts:
                if lo <= a and b <= hi:
                    piece = _ACTIVATIONS[kind](piece)
            o_ref[:, a:b] = piece.astype(o_ref.dtype)


def norm_matmul(x, nw, w, out_dtype, f32_cols=None, acts=()):
    t, d = x.shape
    n = w.shape[1]
    n_chunk = min(n, 1024)
    tm = PROJ_TILE
    assert t % tm == 0 and n % n_chunk == 0, (t, n)
    out_specs = [pl.BlockSpec((tm, n), lambda i: (i, 0))]
    out_shape = [jax.ShapeDtypeStruct((t, n), out_dtype)]
    if f32_cols is not None:
        lo, hi = f32_cols
        assert lo // n_chunk == (hi - 1) // n_chunk
        out_specs.append(pl.BlockSpec((tm, hi - lo), lambda i: (i, 0)))
        out_shape.append(jax.ShapeDtypeStruct((t, hi - lo), F32))
    out = pl.pallas_call(
        functools.partial(_norm_matmul_kernel, n_chunk=n_chunk, f32_cols=f32_cols, acts=tuple(acts)),
        grid=(t // tm,),
        in_specs=[pl.BlockSpec((tm, d), lambda i: (i, 0)), _resident((1, d)), _resident((d, n))],
        out_specs=out_specs, out_shape=out_shape,
        compiler_params=_params(), name="norm_matmul",
    )(x, nw.reshape(1, d), w)
    return out if f32_cols is not None else out[0]


def s5_tables(lam_re, lam_im, log_dt, b_re, b_im, c_re, c_im, d_skip):
    g, n, p = b_re.shape
    s = S5_CHUNK
    gl = S5_LANES // p
    j = g // gl
    hi = lax.Precision.HIGHEST
    dt = jnp.exp(log_dt)[:, None]
    a_re = jnp.exp(lam_re * dt) * jnp.cos(lam_im * dt)
    a_im = jnp.exp(lam_re * dt) * jnp.sin(lam_im * dt)
    den = lam_re * lam_re + lam_im * lam_im
    nr, ni = a_re - 1.0, a_im
    co_re = (nr * lam_re + ni * lam_im) / den
    co_im = (ni * lam_re - nr * lam_im) / den
    bb_re = co_re[..., None] * b_re - co_im[..., None] * b_im
    bb_im = co_re[..., None] * b_im + co_im[..., None] * b_re
    tau = jnp.arange(s + 1, dtype=F32)[:, None, None]
    pw_mag = jnp.exp(lam_re * dt * tau)
    pw_re = pw_mag * jnp.cos(lam_im * dt * tau)
    pw_im = pw_mag * jnp.sin(lam_im * dt * tau)
    ca_re = c_re[None] * pw_re[:, :, None, :] - c_im[None] * pw_im[:, :, None, :]
    ca_im = c_re[None] * pw_im[:, :, None, :] + c_im[None] * pw_re[:, :, None, :]
    bq_re, bq_im = bb_re.transpose(0, 2, 1), bb_im.transpose(0, 2, 1)
    kq = jnp.sum(ca_re[:s, :, :, None, :] * bq_re[None, :, None, :, :]
                 - ca_im[:s, :, :, None, :] * bq_im[None, :, None, :, :], axis=-1).transpose(1, 0, 3, 2)
    ts = np.arange(s)
    lag_onehot = (ts[None, None, :] - ts[None, :, None] == ts[:, None, None]).astype(np.float32)
    rev = s - 1 - ts
    w_re = pw_re[rev][:, :, :, None] * bb_re[None] - pw_im[rev][:, :, :, None] * bb_im[None]
    w_im = pw_re[rev][:, :, :, None] * bb_im[None] + pw_im[rev][:, :, :, None] * bb_re[None]
    m_c = (jnp.einsum('gxqp,xst->gsqtp', kq, lag_onehot, precision=hi)
           .reshape(j, gl, s, p, s * p).transpose(0, 2, 1, 3, 4).reshape(j, s * gl * p, s * p))
    w_c = (jnp.stack([w_re, w_im]).reshape(2, s, j, gl, n, p).transpose(2, 1, 3, 5, 0, 4)
           .reshape(j, s * gl * p, 2 * n))
    v_c = (jnp.stack([ca_re[1:], -ca_im[1:]]).reshape(2, s, j, gl, p, n).transpose(2, 0, 3, 5, 1, 4)
           .reshape(j, 2 * gl * n, s * p))
    fl = s * gl * p
    c_io = np.arange(fl)
    c_st = np.arange(2 * gl * n)
    k_io = np.arange(s * p)
    k_st = np.arange(2 * n)
    spread_io = ((k_io[:, None] // p == c_io[None, :] // (gl * p)) & (k_io[:, None] % p == c_io[None, :] % p))
    spread_st = ((k_st[:, None] // n == c_st[None, :] // (gl * n)) & (k_st[:, None] % n == c_st[None, :] % n))
    grp_io = (c_io // p) % gl
    grp_st = (c_st // n) % gl

    def expand(compact, spread, row_grp, col_grp):
        full = jnp.einsum('jrk,kc->jrc', compact.astype(BF16), jnp.asarray(spread, BF16),
                          preferred_element_type=F32)
        return jnp.where(jnp.asarray(row_grp[:, None] == col_grp[None, :]), full, 0.0).astype(BF16)

    m = expand(m_c, spread_io, grp_io, grp_io)
    w = expand(w_c, spread_st, grp_io, grp_st)
    v = expand(v_c, spread_io, grp_st, grp_io)
    a_s_re = pw_re[s].reshape(1, g * n)
    a_s_im = pw_im[s].reshape(1, g * n)
    dtab = jnp.broadcast_to(d_skip.reshape(j, 1, 1, gl * p), (j, 1, s, gl * p)).reshape(j, 1, s * gl * p)
    return jnp.concatenate([m, v], axis=1), w, a_s_re, a_s_im, dtab


def _s5_flat(u_ref):
    return jnp.concatenate([u_ref[:, t, :] for t in range(u_ref.shape[1])], axis=1)


def _s5a_kernel(u_ref, w_ref, ire_ref, iim_ref):
    r = jnp.dot(_s5_flat(u_ref).astype(BF16), w_ref[0], preferred_element_type=F32)
    half = r.shape[1] // 2
    ire_ref[...] = r[:, :half]
    iim_ref[...] = r[:, half:]


def _s5scan_kernel(ire_ref, iim_ref, ar_ref, ai_ref, h0r_ref, h0i_ref,
                   hpr_ref, hpi_ref, hfr_ref, hfi_ref):
    nb, nc, _ = ire_ref.shape
    ar, ai = ar_ref[...], ai_ref[...]

    def body(c, carry):
        out = []
        for b in range(nb):
            hr, hi = carry[2 * b], carry[2 * b + 1]
            hpr_ref[b, pl.ds(c, 1), :] = hr
            hpi_ref[b, pl.ds(c, 1), :] = hi
            out.append(ar * hr - ai * hi + ire_ref[b, pl.ds(c, 1), :])
            out.append(ar * hi + ai * hr + iim_ref[b, pl.ds(c, 1), :])
        return tuple(out)

    init = []
    for b in range(nb):
        init += [h0r_ref[b], h0i_ref[b]]
    fin = lax.fori_loop(0, nc, body, tuple(init))
    for b in range(nb):
        hfr_ref[b] = fin[2 * b]
        hfi_ref[b] = fin[2 * b + 1]


def _s5b_kernel(u_ref, hpr_ref, hpi_ref, mv_ref, d_ref, y_ref):
    uf = _s5_flat(u_ref)
    lhs = jnp.concatenate([uf.astype(BF16), hpr_ref[...].astype(BF16), hpi_ref[...].astype(BF16)], axis=1)
    y = d_ref[0] * uf + jnp.dot(lhs, mv_ref[0], preferred_element_type=F32)
    lanes = y_ref.shape[2]
    for t in range(y_ref.shape[1]):
        y_ref[:, t, :] = y[:, t * lanes:(t + 1) * lanes]


def s5_apply(u, bsz, h0_re, h0_im, tables):
    mv, w, a_re, a_im, dtab = tables
    tokens, width = u.shape
    nj = w.shape[0]
    s = S5_CHUNK
    rows = tokens // s
    nc = rows // bsz
    lanes = a_re.shape[1]
    half = w.shape[2] // 2
    fl = w.shape[1]
    rt = min(rows, 1024)
    assert tokens % (s * bsz) == 0 and rows % rt == 0 and bsz % 4 == 0, (tokens, bsz)
    u3 = u.reshape(rows, s, width)
    u_spec = pl.BlockSpec((rt, s, S5_LANES), lambda j, r: (r, 0, j))
    st_spec = pl.BlockSpec((rt, half), lambda j, r: (r, j))
    tab_spec = pl.BlockSpec((1, fl, fl), lambda j, r: (j, 0, 0))
    inj_re, inj_im = pl.pallas_call(
        _s5a_kernel, grid=(nj, rows // rt),
        in_specs=[u_spec, tab_spec],
        out_specs=[st_spec, st_spec],
        out_shape=[jax.ShapeDtypeStruct((rows, lanes), F32)] * 2,
        compiler_params=_params(2), name="s5_chunk_in",
    )(u3, w)

    sb, lw = 4, (512 if nc > 16 else lanes)
    seq_spec = pl.BlockSpec((sb, nc, lw), lambda b, l: (b, 0, l))
    vec_spec = pl.BlockSpec((sb, 1, lw), lambda b, l: (b, 0, l))
    atab_spec = pl.BlockSpec((1, lw), lambda b, l: (0, l))
    hp_re, hp_im, hf_re, hf_im = pl.pallas_call(
        _s5scan_kernel, grid=(bsz // sb, lanes // lw),
        in_specs=[seq_spec, seq_spec, atab_spec, atab_spec, vec_spec, vec_spec],
        out_specs=[seq_spec, seq_spec, vec_spec, vec_spec],
        out_shape=[jax.ShapeDtypeStruct((bsz, nc, lanes), F32)] * 2
        + [jax.ShapeDtypeStruct((bsz, 1, lanes), F32)] * 2,
        compiler_params=_params(2), name="s5_scan",
    )(inj_re.reshape(bsz, nc, lanes), inj_im.reshape(bsz, nc, lanes), a_re, a_im,
      h0_re.reshape(bsz, 1, lanes), h0_im.reshape(bsz, 1, lanes))

    y3 = pl.pallas_call(
        _s5b_kernel, grid=(nj, rows // rt),
        in_specs=[u_spec, st_spec, st_spec, pl.BlockSpec((1, 2 * fl, fl), lambda j, r: (j, 0, 0)),
                  pl.BlockSpec((1, 1, fl), lambda j, r: (j, 0, 0))],
        out_specs=u_spec,
        out_shape=jax.ShapeDtypeStruct((rows, s, width), F32),
        compiler_params=_params(2), name="s5_chunk_out",
    )(u3, hp_re.reshape(rows, lanes), hp_im.reshape(rows, lanes), mv, dtab)
    g = lanes // SSM_STATE
    return y3.reshape(tokens, width), hf_re.reshape(bsz, g, SSM_STATE), hf_im.reshape(bsz, g, SSM_STATE)


def _retention_gammas():
    return 1.0 - np.exp2(-5.0 - np.arange(RET_HEADS, dtype=np.float64))


def retention_tables(tile, chunk, nb):
    gam = _retention_gammas()[:, None, None]
    i = np.arange(tile)[:, None]
    j = np.arange(tile)[None, :]
    same = (i // chunk) == (j // chunk)
    earlier = (j // chunk) < (i // chunk)
    dist = np.where(same, np.abs(i - j), np.where(earlier, i - j, 0))
    dmask = np.where(same | earlier, gam ** dist[None], 0.0)
    qw = np.broadcast_to((gam[:, :, 0] ** (np.arange(tile) + 1.0))[:, :, None], (RET_HEADS, tile, HEAD_DIM))
    kw = np.broadcast_to((gam[:, :, 0] ** (tile - 1.0 - np.arange(tile)))[:, :, None], (RET_HEADS, tile, HEAD_DIM))
    dmask = np.stack([np.kron(np.eye(nb), m) for m in dmask])
    return (jnp.asarray(dmask, F32), jnp.asarray(np.tile(qw, (1, nb, 1)), F32),
            jnp.asarray(np.tile(kw, (1, nb, 1)), F32), tuple(float(x) for x in _retention_gammas() ** tile))


def rope_tables(pos):
    half = HEAD_DIM // 2
    inv = jnp.exp(-math.log(ROPE_BASE) * 2.0 * jnp.arange(half, dtype=F32) / HEAD_DIM)
    ang = pos.astype(F32)[:, None] * inv[None, :]
    cos, sin = jnp.cos(ang), jnp.sin(ang)
    cosf = jnp.concatenate([cos, cos], axis=1)
    sinf = jnp.concatenate([-sin, sin], axis=1)
    return cosf, sinf


def _mixer_kernel(x_ref, zq_ref, xq_ref, gl_ref, y_ref, cq_ref, sq_ref, ck_ref, sk_ref,
                  dm_ref, qw_ref, kw_ref, vm_ref, am_ref, mk_ref, mv_ref, s0_ref, gn_ref,
                  wro_ref, wglu_ref, wso_ref, wxo_ref, wout_ref,
                  h_ref, sout_ref, s_scr, o_scr, xo_scr, glu_scr, mg_scr, *, nb, tl, tile_decay):
    hd = HEAD_DIM
    qk = RET_HEADS * hd
    n_mem = mk_ref.shape[1] // X_HEADS

    def own_blocks(a, width):
        if nb == 1:
            return a
        return jnp.concatenate([a[n * tl:(n + 1) * tl, n * width:(n + 1) * width] for n in range(nb)], axis=0)

    def spread_blocks(a, mask_ref):
        if nb == 1:
            return a
        return jnp.concatenate([a] * nb, axis=1) * mask_ref[...]

    @pl.when(pl.program_id(1) == 0)
    def _():
        for n in range(nb):
            for h in range(RET_HEADS):
                s_scr[h, :, n * hd:(n + 1) * hd] = s0_ref[n, h]

    cq, sq, ck, sk = cq_ref[...], sq_ref[...], ck_ref[...], sk_ref[...]
    for h in range(RET_HEADS):
        c0 = h * hd
        q = zq_ref[:, c0:c0 + hd].astype(F32)
        k = zq_ref[:, qk + c0:qk + c0 + hd].astype(F32)
        v = zq_ref[:, 2 * qk + c0:2 * qk + c0 + hd]
        g = zq_ref[:, 3 * qk + c0:3 * qk + c0 + hd].astype(F32)
        qr = q * cq + pltpu.roll(q, hd // 2, 1) * sq
        kr = k * ck + pltpu.roll(k, hd // 2, 1) * sk
        sc = lax.dot_general(qr.astype(BF16), kr.astype(BF16), NT_DIMS,
                             preferred_element_type=F32) * dm_ref[h]
        o = jnp.dot(sc.astype(BF16), v, preferred_element_type=F32)
        s_old = s_scr[h]
        o += own_blocks(jnp.dot((qr * qw_ref[h]).astype(BF16), s_old.astype(BF16),
                                preferred_element_type=F32), hd)
        kv = lax.dot_general((kr * kw_ref[h]).astype(BF16), spread_blocks(v, vm_ref), TN_DIMS,
                             preferred_element_type=F32)
        s_scr[h] = tile_decay[h] * s_old + kv
        d = o - jnp.mean(o, axis=-1, keepdims=True)
        on = d * lax.rsqrt(jnp.mean(d * d, axis=-1, keepdims=True) + EPS) * gn_ref[:, c0:c0 + hd]
        o_scr[:, c0:c0 + hd] = (on * g).astype(BF16)
        mem_rows = pl.ds(h, n_mem, stride=X_HEADS)
        mkh = mk_ref[:, mem_rows, :].astype(BF16).reshape(nb * n_mem, hd)
        mvh = mv_ref[:, mem_rows, :].astype(BF16).reshape(nb * n_mem, hd)
        s = own_blocks(lax.dot_general(xq_ref[:, c0:c0 + hd], mkh, NT_DIMS,
                                       preferred_element_type=F32), n_mem) * (hd ** -0.5)
        e = jnp.exp(s - jnp.max(s, axis=-1, keepdims=True))
        p = (e / jnp.sum(e, axis=-1, keepdims=True)).astype(BF16)
        xo_scr[:, c0:c0 + hd] = jnp.dot(spread_blocks(p, am_ref), mvh,
                                        preferred_element_type=F32).astype(BF16)

    cw = MIX_COLS
    dm = h_ref.shape[1]
    half = wglu_ref.shape[1] // 2
    yb = jax.nn.gelu(y_ref[...]).astype(BF16)
    for c0 in range(0, half, cw):
        ga = jnp.dot(yb, wglu_ref[:, c0:c0 + cw], preferred_element_type=F32)
        gb = jnp.dot(yb, wglu_ref[:, half + c0:half + c0 + cw], preferred_element_type=F32)
        glu_scr[:, c0:c0 + cw] = (ga * _sigmoid(gb)).astype(BF16)
    for c0 in range(0, dm, cw):
        cols = slice(c0, c0 + cw)
        ret = jnp.dot(o_scr[...], wro_ref[:, cols], preferred_element_type=F32)
        ssm = jnp.dot(glu_scr[...], wso_ref[:, cols], preferred_element_type=F32)
        xb = jnp.dot(xo_scr[...], wxo_ref[:, cols], preferred_element_type=F32)
        merged = (gl_ref[:, c0:c0 + cw].astype(F32) * ret
                  + gl_ref[:, dm + c0:dm + c0 + cw].astype(F32) * ssm
                  + gl_ref[:, 2 * dm + c0:2 * dm + c0 + cw].astype(F32) * xb)
        mg_scr[:, cols] = merged.astype(BF16)
    for c0 in range(0, dm, cw):
        cols = slice(c0, c0 + cw)
        h_ref[:, cols] = x_ref[:, cols] + jnp.dot(mg_scr[...], wout_ref[:, cols], preferred_element_type=F32)
    for n in range(nb):
        for h in range(RET_HEADS):
            sout_ref[n, h] = s_scr[h, :, n * hd:(n + 1) * hd]


def mixer(x, z, y_ssm, pos, mem_k, mem_v, s0, ret_gn, w_ret_o, w_ssm_glu, w_ssm_o, w_x_o, w_out, *, nb, tl):
    bsz, length, dm = x.shape
    chunk = min(CHUNK, length)
    nl = length // tl
    assert length % tl == 0 and bsz % nb == 0, (bsz, length, nb, tl)
    rows = nb * tl
    qk = RET_HEADS * HEAD_DIM
    sw = y_ssm.shape[1]
    xw = X_HEADS * HEAD_DIM
    gate_col = (4 * qk + sw + xw)
    assert gate_col % (3 * dm) == 0 and (4 * qk + sw) % xw == 0
    dmask, qw, kw, tile_decay = retention_tables(tl, chunk, nb)
    seq_of_row = np.arange(rows)[:, None] // tl
    own_v = jnp.asarray(seq_of_row == np.arange(nb * HEAD_DIM)[None, :] // HEAD_DIM, BF16)
    n_mem = mem_k.shape[1] // X_HEADS
    own_mem = jnp.asarray(seq_of_row == np.arange(nb * n_mem)[None, :] // n_mem, BF16)
    cosf, sinf = (jnp.broadcast_to(t.reshape(nl, 1, tl, HEAD_DIM), (nl, nb, tl, HEAD_DIM)).reshape(nl * rows, HEAD_DIM)
                  for t in rope_tables(pos))
    scale = HEAD_DIM ** -0.5
    row_map = lambda b, l: (b * nl + l, 0)
    tab_map = lambda b, l: (l, 0)
    st_spec = pl.BlockSpec((nb, RET_HEADS, HEAD_DIM, HEAD_DIM), lambda b, l: (b, 0, 0, 0))
    mem_spec = pl.BlockSpec((nb,) + mem_k.shape[1:], lambda b, l: (b,) + (0,) * (mem_k.ndim - 1))
    h, s_out = pl.pallas_call(
        functools.partial(_mixer_kernel, nb=nb, tl=tl, tile_decay=tile_decay),
        grid=(bsz // nb, nl),
        in_specs=[pl.BlockSpec((rows, dm), row_map),
                  pl.BlockSpec((rows, 4 * qk), row_map),
                  pl.BlockSpec((rows, xw), lambda b, l: (b * nl + l, (4 * qk + sw) // xw)),
                  pl.BlockSpec((rows, 3 * dm), lambda b, l: (b * nl + l, gate_col // (3 * dm))),
                  pl.BlockSpec((rows, sw), row_map),
                  pl.BlockSpec((rows, HEAD_DIM), tab_map), pl.BlockSpec((rows, HEAD_DIM), tab_map),
                  pl.BlockSpec((rows, HEAD_DIM), tab_map), pl.BlockSpec((rows, HEAD_DIM), tab_map),
                  _resident(dmask.shape), _resident(qw.shape), _resident(kw.shape),
                  _resident(own_v.shape), _resident(own_mem.shape),
                  mem_spec, mem_spec, st_spec, _resident((1, qk)),
                  _resident(w_ret_o.shape), _resident(w_ssm_glu.shape), _resident(w_ssm_o.shape),
                  _resident(w_x_o.shape), _resident(w_out.shape)],
        out_specs=[pl.BlockSpec((rows, dm), row_map), st_spec],
        out_shape=[jax.ShapeDtypeStruct((bsz * length, dm), F32),
                   jax.ShapeDtypeStruct(s0.shape, F32)],
        scratch_shapes=[pltpu.VMEM((RET_HEADS, HEAD_DIM, nb * HEAD_DIM), F32),
                        pltpu.VMEM((rows, qk), BF16), pltpu.VMEM((rows, xw), BF16),
                        pltpu.VMEM((rows, w_ssm_o.shape[0]), BF16), pltpu.VMEM((rows, dm), BF16)],
        compiler_params=_params(2), name="mixer",
    )(x.reshape(bsz * length, dm), z, z, z, y_ssm,
      cosf * scale, sinf * scale, cosf, sinf, dmask, qw, kw, own_v, own_mem,
      mem_k, mem_v, s0, ret_gn.reshape(1, qk), w_ret_o, w_ssm_glu, w_ssm_o, w_x_o, w_out)
    return h, s_out


def _router_kernel(hp_ref, hs_ref, nw_ref, wrt_ref, br_ref, tri_ref, low_ref,
                   xn_ref, gw_ref, crow_ref, cnt_ref, *, prompt_tiles):
    h = jnp.where(pl.program_id(0) < prompt_tiles, hp_ref[...], hs_ref[...])
    xn = _rms(h, nw_ref[...]).astype(BF16)
    xn_ref[...] = xn
    logits = lax.dot_general(wrt_ref[...], xn, NT_DIMS, preferred_element_type=F32) + br_ref[...]
    ne = logits.shape[0]
    iota = lax.broadcasted_iota(jnp.int32, logits.shape, 0)
    rest = logits
    sel = jnp.zeros(logits.shape, jnp.bool_)
    vals, idxs = [], []
    for _ in range(TOP_K):
        m = jnp.max(rest, axis=0, keepdims=True)
        ix = jnp.min(jnp.where(rest == m, iota, ne), axis=0, keepdims=True)
        hit = iota == ix
        vals.append(m)
        idxs.append(ix)
        sel = jnp.logical_or(sel, hit)
        rest = jnp.where(hit, -jnp.inf, rest)
    es = [jnp.exp(v - vals[0]) for v in vals]
    tot = es[0] + es[1] + es[2] + es[3]
    before = jnp.dot(sel.astype(BF16), tri_ref[...], preferred_element_type=F32)
    cnt = jnp.sum(sel.astype(F32), axis=1, keepdims=True)
    seg = jnp.floor((cnt + (ROW_ALIGN - 1.0)) * (1.0 / ROW_ALIGN)) * ROW_ALIGN
    start = jnp.dot(low_ref[...], jnp.broadcast_to(seg, before.shape), precision=lax.Precision.HIGHEST,
                    preferred_element_type=F32)
    place = start + before
    for k in range(TOP_K):
        gw_ref[k:k + 1, :] = es[k] / tot
        crow_ref[k:k + 1, :] = jnp.sum(jnp.where(iota == idxs[k], place, 0.0), axis=0,
                                       keepdims=True).astype(jnp.int32)
    cnt_ref[...] = jnp.broadcast_to(cnt, cnt_ref.shape)


def router(h_p, h_s, norm_ffn, w_router, b_router):
    dm = h_p.shape[1]
    ne = w_router.shape[1]
    tt = ROW_TILE
    npt, nst = h_p.shape[0] // tt, h_s.shape[0] // tt
    t = (npt + nst) * tt
    tri = jnp.asarray(np.triu(np.ones((tt, tt), np.float32), k=1), BF16)
    low = jnp.asarray(np.tril(np.ones((ne, ne), np.float32), k=-1))
    tok_spec = pl.BlockSpec((TOP_K, tt), lambda i: (0, i))
    return pl.pallas_call(
        functools.partial(_router_kernel, prompt_tiles=npt), grid=(npt + nst,),
        in_specs=[pl.BlockSpec((tt, dm), lambda i: (jnp.minimum(i, npt - 1), 0)),
                  pl.BlockSpec((tt, dm), lambda i: (jnp.maximum(i - npt, 0), 0)),
                  _resident((1, dm)), _resident((ne, dm)), _resident((ne, 1)), _resident((tt, tt)),
                  _resident((ne, ne))],
        out_specs=[pl.BlockSpec((tt, dm), lambda i: (i, 0)), tok_spec, tok_spec,
                   pl.BlockSpec((ne, 128), lambda i: (i, 0))],
        out_shape=[jax.ShapeDtypeStruct((t, dm), BF16),
                   jax.ShapeDtypeStruct((TOP_K, t), F32), jax.ShapeDtypeStruct((TOP_K, t), jnp.int32),
                   jax.ShapeDtypeStruct(((npt + nst) * ne, 128), F32)],
        compiler_params=_params(), name="router",
    )(h_p, h_s, norm_ffn.reshape(1, dm), w_router.T.astype(BF16), b_router.reshape(ne, 1), tri, low)


def _pack_bf16_pairs(x, exact=False):
    n = x.shape[1] // 2
    lo, hi = x[:, :n], x[:, n:]
    if not exact:
        lo, hi = lo.astype(BF16).astype(F32), hi.astype(BF16).astype(F32)
    lo = lax.bitcast_convert_type(lo, jnp.int32)
    hi = lax.bitcast_convert_type(hi, jnp.int32)
    return lax.shift_right_logical(lo, 16) | (hi & -65536)


def _unpack_bf16_pairs(u):
    lo = lax.bitcast_convert_type(lax.shift_left(u, 16), F32).astype(BF16)
    hi = lax.bitcast_convert_type(u & -65536, F32).astype(BF16)
    return lo, hi


def _split_count(n, fn):
    def quad(jq, carry):
        fn(4 * jq, 4)
        return carry

    lax.fori_loop(0, n // 4, quad, 0)

    @pl.when(n % 4 >= 2)
    def _():
        fn(n // 4 * 4, 2)

    @pl.when(n % 2 == 1)
    def _():
        fn(n // 2 * 2, 1)


def _chunk_loop(n, fn):
    def quad(jq, carry):
        for u in range(4):
            fn(4 * jq + u)
        return carry

    def single(j, carry):
        fn(j)
        return carry

    lax.fori_loop(0, n // 4, quad, 0)
    lax.fori_loop(n // 4 * 4, n, single, 0)


def _dispatch_kernel(dst_ref, tch_ref, zs_ref, zn_ref, used_ref,
                     xn_ref, crow_ref, xs_ref, cbuf, zbuf, sems, zsem, *, ne, bm):
    i = pl.program_id(0)
    nt = pl.num_programs(0)
    slot = lax.rem(i, 2)
    tt = xn_ref.shape[0]
    cr = cbuf.shape[1]
    ra = ROW_ALIGN
    n_blocks = xs_ref.shape[0] // bm

    def chunk_copy(sl, src_row, dst_row, rows):
        return pltpu.make_async_copy(cbuf.at[sl, pl.ds(src_row, rows)], xs_ref.at[pl.ds(dst_row, rows)],
                                     sems.at[sl])

    def wait_chunks(sl, n):
        _split_count(n, lambda j, m: chunk_copy(sl, 0, 0, ra * m).wait())

    def tail_copy(e, j):
        return pltpu.make_async_copy(zbuf.at[pl.ds(0, ra)],
                                     xs_ref.at[pl.ds(pl.multiple_of(zs_ref[e] + ra * j, ra), ra)], zsem)

    def block_copy(b):
        return pltpu.make_async_copy(zbuf, xs_ref.at[pl.ds(pl.multiple_of(b * bm, bm), bm)], zsem)

    def zero_fill(start):
        def per_expert(e, carry):
            def per_chunk(j, c2):
                (tail_copy(e, j).start() if start else tail_copy(e, j).wait())
                return c2
            lax.fori_loop(0, zn_ref[e], per_chunk, 0)
            return carry
        lax.fori_loop(0, ne, per_expert, 0)

        def per_block(b, carry):
            (block_copy(b).start() if start else block_copy(b).wait())
            return carry
        lax.fori_loop(used_ref[0], n_blocks, per_block, 0)

    @pl.when(i == 0)
    def _():
        zbuf[...] = jnp.zeros_like(zbuf)
        zero_fill(True)
        zero_fill(False)

    @pl.when(i >= 2)
    def _():
        wait_chunks(slot, tch_ref[jnp.maximum(i - 2, 0)])

    crow = crow_ref[...]
    rows = lax.broadcasted_iota(jnp.int32, (cr, tt), 0)
    hit = rows == crow[0:1, :]
    for k in range(1, TOP_K):
        hit = jnp.logical_or(hit, rows == crow[k:k + 1, :])
    packed = _pack_bf16_pairs(jnp.dot(jnp.where(hit, 1.0, 0.0).astype(BF16), xn_ref[...],
                                      preferred_element_type=F32), exact=True)
    cbuf[slot] = packed

    chunks_per_tile = cr // ra
    _chunk_loop(tch_ref[i], lambda c: chunk_copy(
        slot, pl.multiple_of(c * ra, ra), pl.multiple_of(dst_ref[i * chunks_per_tile + c], ra), ra).start())

    @pl.when(i == nt - 1)
    def _():
        wait_chunks(slot, tch_ref[i])
        wait_chunks(1 - slot, jnp.where(nt >= 2, tch_ref[jnp.maximum(i - 1, 0)], 0))


def dispatch(xn, crow, tables, n_rows, ne):
    t, dm = xn.shape
    tt = ROW_TILE
    bm = FFN_BLOCK
    cr = _compact_rows(ne)
    grid_spec = pltpu.PrefetchScalarGridSpec(
        num_scalar_prefetch=5, grid=(t // tt,),
        in_specs=[pl.BlockSpec((tt, dm), lambda i, *_: (i, 0)),
                  pl.BlockSpec((TOP_K, tt), lambda i, *_: (0, i))],
        out_specs=pl.BlockSpec(memory_space=pl.ANY),
        scratch_shapes=[pltpu.VMEM((2, cr, dm // 2), jnp.int32), pltpu.VMEM((bm, dm // 2), jnp.int32),
                        pltpu.SemaphoreType.DMA((2,)), pltpu.SemaphoreType.DMA(())])
    return pl.pallas_call(
        functools.partial(_dispatch_kernel, ne=ne, bm=bm), grid_spec=grid_spec,
        out_shape=jax.ShapeDtypeStruct((n_rows, dm // 2), jnp.int32),
        compiler_params=_params(), name="dispatch",
    )(*tables, xn, crow)


FFN_W_PIECES = (4, 2)
FFN_HIDDEN_CHUNK = 512


def _ffn_kernel(first_ref, nblk_ref, used_ref, next_ref, wslot_ref, lead_ref,
                xs_hbm, wgu_hbm, bgu_ref, wd_hbm, bd_ref, y_hbm,
                wgu_f32, wd_f32, wgu_bf, wd_bf, xbuf, ybuf, xsem, ysem, wsem, zsem, *, bm):
    e = pl.program_id(0)
    n = nblk_ref[e]
    b0 = first_ref[e]
    n_blocks = y_hbm.shape[0] // bm
    n_pieces = sum(FFN_W_PIECES)

    def block_rows(b):
        return pl.ds(pl.multiple_of(b * bm, bm), bm)

    used = used_ref[0]
    x_slots = xbuf.shape[0]

    def x_copy(g):
        sl = lax.rem(g, x_slots)
        return pltpu.make_async_copy(xs_hbm.at[block_rows(g)], xbuf.at[sl], xsem.at[sl])

    def y_copy(g):
        sl = lax.rem(g, 2)
        return pltpu.make_async_copy(ybuf.at[sl], y_hbm.at[block_rows(g)], ysem.at[sl])

    def w_piece(ex, sl, p):
        src, dst, q, parts = ((wgu_hbm, wgu_f32, p, FFN_W_PIECES[0]) if p < FFN_W_PIECES[0]
                              else (wd_hbm, wd_f32, p - FFN_W_PIECES[0], FFN_W_PIECES[1]))
        rows = src.shape[1] // parts
        return pltpu.make_async_copy(src.at[ex, pl.ds(q * rows, rows)], dst.at[sl, pl.ds(q * rows, rows)],
                                     wsem.at[sl, p])

    @pl.when(n > 0)
    def _():
        ws = wslot_ref[e]
        nxt = next_ref[e]

        @pl.when(lead_ref[0] == e)
        def _():
            for g in range(x_slots - 1):
                @pl.when(g < used)
                def _():
                    x_copy(g).start()
            for p in range(n_pieces):
                w_piece(e, ws, p).start()

        for p in range(n_pieces):
            w_piece(e, ws, p).wait()
        wgu_bf[...] = wgu_f32[ws].astype(BF16)
        wd_bf[...] = wd_f32[ws].astype(BF16)

        def block(j, carry):
            g = b0 + j
            x_copy(g).wait()

            @pl.when(g + x_slots - 1 < used)
            def _():
                x_copy(g + x_slots - 1).start()

            for p in range(n_pieces):
                @pl.when(jnp.logical_and(j == p, nxt >= 0))
                def _():
                    w_piece(nxt, 1 - ws, p).start()

            @pl.when(g >= 2)
            def _():
                y_copy(g - 2).wait()

            x = jnp.concatenate(_unpack_bf16_pairs(xbuf[lax.rem(g, x_slots)]), axis=1)
            ff = wd_bf.shape[0]
            y = bd_ref[0]
            for c0 in range(0, ff, FFN_HIDDEN_CHUNK):
                c1 = c0 + FFN_HIDDEN_CHUNK
                gate = jnp.dot(x, wgu_bf[:, c0:c1], preferred_element_type=F32) + bgu_ref[0, :, c0:c1]
                up = jnp.dot(x, wgu_bf[:, ff + c0:ff + c1], preferred_element_type=F32) + bgu_ref[0, :, ff + c0:ff + c1]
                gate = jnp.minimum(gate, SWIGLU_LIMIT)
                up = jnp.clip(up, -SWIGLU_LIMIT, SWIGLU_LIMIT)
                act = (up + 1.0) * gate * _sigmoid(SWIGLU_ALPHA * gate)
                y = y + jnp.dot(act.astype(BF16), wd_bf[c0:c1, :], preferred_element_type=F32)
            ybuf[lax.rem(g, 2)] = _pack_bf16_pairs(y)
            y_copy(g).start()
            return carry

        lax.fori_loop(0, n, block, 0)

        for p in range(n_pieces):
            @pl.when(jnp.logical_and(p >= n, nxt >= 0))
            def _():
                w_piece(nxt, 1 - ws, p).start()

    @pl.when(e == pl.num_programs(0) - 1)
    def _():
        for back in (2, 1):
            @pl.when(used >= back)
            def _():
                y_copy(used - back).wait()
        ybuf[0] = jnp.zeros(ybuf.shape[1:], ybuf.dtype)

        def zero_copy(b):
            return pltpu.make_async_copy(ybuf.at[0], y_hbm.at[block_rows(b)], zsem)

        def start(b, carry):
            zero_copy(b).start()
            return carry

        def wait(b, carry):
            zero_copy(b).wait()
            return carry

        lax.fori_loop(used_ref[0], n_blocks, start, 0)
        lax.fori_loop(used_ref[0], n_blocks, wait, 0)


def expert_ffn(xs, expert_tables, w_gate_up, b_gate_up, w_down, b_down):
    n_rows = xs.shape[0]
    ne, dm, ff2 = w_gate_up.shape
    bm = FFN_BLOCK
    grid_spec = pltpu.PrefetchScalarGridSpec(
        num_scalar_prefetch=6, grid=(ne,),
        in_specs=[pl.BlockSpec(memory_space=pl.ANY),
                  pl.BlockSpec(memory_space=pl.ANY),
                  pl.BlockSpec((1, 1, ff2), lambda e, *_: (e, 0, 0)),
                  pl.BlockSpec(memory_space=pl.ANY),
                  pl.BlockSpec((1, 1, dm), lambda e, *_: (e, 0, 0))],
        out_specs=pl.BlockSpec(memory_space=pl.ANY),
        scratch_shapes=[pltpu.VMEM((2, dm, ff2), F32), pltpu.VMEM((2, ff2 // 2, dm), F32),
                        pltpu.VMEM((dm, ff2), BF16), pltpu.VMEM((ff2 // 2, dm), BF16),
                        pltpu.VMEM((3, bm, dm // 2), jnp.int32), pltpu.VMEM((2, bm, dm // 2), jnp.int32),
                        pltpu.SemaphoreType.DMA((3,)), pltpu.SemaphoreType.DMA((2,)),
                        pltpu.SemaphoreType.DMA((2, sum(FFN_W_PIECES))), pltpu.SemaphoreType.DMA(())])
    return pl.pallas_call(
        functools.partial(_ffn_kernel, bm=bm), grid_spec=grid_spec,
        out_shape=jax.ShapeDtypeStruct((n_rows, dm // 2), jnp.int32),
        compiler_params=_params(), name="expert_ffn",
    )(*expert_tables, xs, w_gate_up, b_gate_up.reshape(ne, 1, ff2), w_down, b_down.reshape(ne, 1, dm))


def _combine_kernel(dst_ref, tch_ref,
                    crow_ref, gw_ref, hp_ref, hs_ref, fn_ref, yr_ref, yp_ref, ys_ref, ybuf, sems,
                    *, prompt_tiles):
    i = pl.program_id(0)
    nt = pl.num_programs(0)
    slot = lax.rem(i, 2)
    tt = hp_ref.shape[0]
    cr = ybuf.shape[1]
    ra = ROW_ALIGN

    def chunk_copy(sl, src_row, dst_row, rows):
        return pltpu.make_async_copy(yr_ref.at[pl.ds(src_row, rows)], ybuf.at[sl, pl.ds(dst_row, rows)],
                                     sems.at[sl])

    def fetch(tile, sl):
        chunks_per_tile = cr // ra
        _chunk_loop(tch_ref[tile], lambda c: chunk_copy(
            sl, pl.multiple_of(dst_ref[tile * chunks_per_tile + c], ra), pl.multiple_of(c * ra, ra), ra).start())

    @pl.when(i == 0)
    def _():
        ybuf[...] = jnp.zeros_like(ybuf)
        fetch(0, 0)

    @pl.when(i + 1 < nt)
    def _():
        fetch(i + 1, 1 - slot)

    _split_count(tch_ref[i], lambda j, m: chunk_copy(slot, 0, 0, ra * m).wait())

    y_lo, y_hi = _unpack_bf16_pairs(ybuf[slot])
    cols = lax.broadcasted_iota(jnp.int32, (tt, cr), 1)
    q = jnp.zeros((tt, cr), F32)
    for k in range(TOP_K):
        q = jnp.where(cols == crow_ref[:, k:k + 1], gw_ref[:, k:k + 1], q)
    qb = q.astype(BF16)
    moe = jnp.concatenate([jnp.dot(qb, y_lo, preferred_element_type=F32),
                           jnp.dot(qb, y_hi, preferred_element_type=F32)], axis=1)
    h = jnp.where(i < prompt_tiles, hp_ref[...], hs_ref[...])
    out = _rms(h + moe, fn_ref[...])

    @pl.when(i < prompt_tiles)
    def _():
        yp_ref[...] = out

    @pl.when(i >= prompt_tiles)
    def _():
        ys_ref[...] = out


def combine(crow_t, gw_t, h_p, h_s, final_norm, y_rows, tables, ne):
    dm = h_p.shape[1]
    tt = ROW_TILE
    npt, nst = h_p.shape[0] // tt, h_s.shape[0] // tt
    cr = _compact_rows(ne)
    p_map = lambda i, *_: (jnp.minimum(i, npt - 1), 0)
    s_map = lambda i, *_: (jnp.maximum(i - npt, 0), 0)
    grid_spec = pltpu.PrefetchScalarGridSpec(
        num_scalar_prefetch=2, grid=(npt + nst,),
        in_specs=[pl.BlockSpec((tt, TOP_K), lambda i, *_: (i, 0)),
                  pl.BlockSpec((tt, TOP_K), lambda i, *_: (i, 0)),
                  pl.BlockSpec((tt, dm), p_map), pl.BlockSpec((tt, dm), s_map),
                  pl.BlockSpec((1, dm), lambda i, *_: (0, 0)),
                  pl.BlockSpec(memory_space=pl.ANY)],
        out_specs=[pl.BlockSpec((tt, dm), p_map), pl.BlockSpec((tt, dm), s_map)],
        scratch_shapes=[pltpu.VMEM((2, cr, dm // 2), jnp.int32), pltpu.SemaphoreType.DMA((2,))])
    return pl.pallas_call(
        functools.partial(_combine_kernel, prompt_tiles=npt), grid_spec=grid_spec,
        out_shape=[jax.ShapeDtypeStruct(h_p.shape, F32), jax.ShapeDtypeStruct(h_s.shape, F32)],
        compiler_params=_params(), name="combine",
    )(*tables, crow_t, gw_t, h_p, h_s, final_norm.reshape(1, dm), y_rows)


def _compact_rows(ne):
    return -(-(TOP_K * ROW_TILE + ne * (ROW_ALIGN - 1)) // 128) * 128


def moe_and_final_norm(h_p, h_s, norm_ffn, w_router, b_router, w_gate_up, b_gate_up, w_down, b_down, final_norm):
    ne = w_router.shape[1]
    bm = FFN_BLOCK
    ra = ROW_ALIGN
    xn, gw, crow, cnt = router(h_p, h_s, norm_ffn, w_router, b_router)
    t = xn.shape[0]
    nt = t // ROW_TILE
    seg = -(-cnt[:, 0].astype(jnp.int32).reshape(nt, ne) // ra) * ra
    seg_before = jnp.cumsum(seg, axis=0) - seg
    rows_e = jnp.sum(seg, axis=0)
    padded = -(-rows_e // bm) * bm
    pad_ends = jnp.cumsum(padded)
    pad_starts = pad_ends - padded
    n_blocks = -(-(t * TOP_K + nt * ne * (ra - 1) + ne * (bm - 1)) // bm)
    n_used = pad_ends[-1] // bm
    experts = jnp.arange(ne, dtype=jnp.int32)
    chunks = seg // ra
    chunk_end = jnp.cumsum(chunks, axis=1)
    c_ids = jnp.arange(_compact_rows(ne) // ra, dtype=jnp.int32)
    owner = jnp.sum((chunk_end[:, None, :] <= c_ids[None, :, None]).astype(jnp.int32), axis=2)
    seg_shift = pad_starts[None, :] + seg_before - ra * (chunk_end - chunks)
    dst = ra * c_ids[None, :] + jnp.sum(
        jnp.where(owner[:, :, None] == experts[None, None, :], seg_shift[:, None, :], 0), axis=2)
    seg_tables = (dst.reshape(-1), chunk_end[:, -1])
    fill_tables = (pad_starts + rows_e, (padded - rows_e) // ra, n_used.reshape(1))
    to_i32 = lambda xs: tuple(x.astype(jnp.int32) for x in xs)
    xs = dispatch(xn, crow, to_i32(seg_tables + fill_tables), n_blocks * bm, ne)
    active = padded > 0
    later = jnp.where(active, experts, ne)
    next_active = jnp.concatenate([lax.cummin(later, reverse=True)[1:], jnp.full((1,), ne, jnp.int32)])
    expert_tables = (pad_starts // bm, padded // bm, n_used.reshape(1),
                     jnp.where(next_active < ne, next_active, -1),
                     (jnp.cumsum(active.astype(jnp.int32)) - active.astype(jnp.int32)) % 2,
                     jnp.min(later).reshape(1))
    y_rows = expert_ffn(xs, to_i32(expert_tables), w_gate_up, b_gate_up, w_down, b_down)
    return combine(crow.T, gw.T, h_p, h_s, final_norm, y_rows, to_i32(seg_tables), ne)


def kernel(x_prompt, x_sample, cache_mem_k, cache_mem_v, state_ret, state_ssm_re, state_ssm_im, mem_prompt, norm_mix, w_in, ret_gn, w_ret_o, ssm_lam_re, ssm_lam_im, ssm_log_dt, ssm_b_re, ssm_b_im, ssm_c_re, ssm_c_im, ssm_d, w_ssm_glu, w_ssm_o, mem_norm, w_mem_kv, w_x_o, w_out, norm_ffn, w_router, b_router, w_gate_up, b_gate_up, w_down, b_down, final_norm):
    assert norm_mix.shape[0] == 1, "single-layer step"
    bp, lp, dm = x_prompt.shape
    bs, ls, _ = x_sample.shape
    n_mem = mem_prompt.shape[1]
    xw = X_HEADS * HEAD_DIM
    qk = RET_HEADS * HEAD_DIM
    sw = ssm_d.shape[1]
    g = ssm_lam_re.shape[1]

    w_in_b = w_in[0].astype(BF16)
    tables = s5_tables(ssm_lam_re[0], ssm_lam_im[0], ssm_log_dt[0], ssm_b_re[0], ssm_b_im[0],
                       ssm_c_re[0], ssm_c_im[0], ssm_d[0])
    mix_w = (ret_gn[0], w_ret_o[0].astype(BF16), w_ssm_glu[0].astype(BF16), w_ssm_o[0].astype(BF16),
             w_x_o[0].astype(BF16), w_out[0].astype(BF16))

    kv = norm_matmul(mem_prompt.reshape(bp * n_mem, dm), mem_norm[0], w_mem_kv[0].astype(BF16), F32)
    mk_p = kv[:, :xw].reshape(bp, n_mem * X_HEADS, HEAD_DIM)
    mv_p = kv[:, xw:].reshape(bp, n_mem * X_HEADS, HEAD_DIM)

    def group(x, pos, mem_k, mem_v, s_ret, h_re, h_im, nb, tl):
        bsz, length, _ = x.shape
        z, u = norm_matmul(x.reshape(bsz * length, dm), norm_mix[0], w_in_b, BF16,
                           f32_cols=(4 * qk, 4 * qk + sw),
                           acts=((3 * qk, 4 * qk, "silu"), (4 * qk + sw + xw, w_in_b.shape[1], "sigmoid")))
        y, hf_re, hf_im = s5_apply(u, bsz, h_re, h_im, tables)
        h, s_new = mixer(x, z, y, pos, mem_k, mem_v, s_ret, *mix_w, nb=nb, tl=tl)
        return h, s_new, hf_re, hf_im

    zero_ret = jnp.zeros((bp, RET_HEADS, HEAD_DIM, HEAD_DIM), F32)
    zero_ssm = jnp.zeros((bp, g, SSM_STATE), F32)
    h_p, ret_p, sre_p, sim_p = group(x_prompt, jnp.arange(lp, dtype=jnp.int32), mk_p, mv_p,
                                     zero_ret, zero_ssm, zero_ssm, 1, ROW_TILE)
    h_s, ret_s, sre_s, sim_s = group(x_sample, PAST_LEN + jnp.arange(ls, dtype=jnp.int32),
                                     cache_mem_k[0].reshape(bs, n_mem * X_HEADS, HEAD_DIM),
                                     cache_mem_v[0].reshape(bs, n_mem * X_HEADS, HEAD_DIM),
                                     state_ret[0], state_ssm_re[0], state_ssm_im[0], ROW_TILE // ls, ls)

    y_p, y_s = moe_and_final_norm(h_p, h_s, norm_ffn[0], w_router[0], b_router[0],
                                  w_gate_up[0], b_gate_up[0], w_down[0], b_down[0], final_norm)
    return (y_p.reshape(bp, lp, dm), y_s.reshape(bs, ls, dm), ret_p[None], sre_p[None], sim_p[None],
            mk_p.reshape(1, bp, n_mem, X_HEADS, HEAD_DIM), mv_p.reshape(1, bp, n_mem, X_HEADS, HEAD_DIM),
            ret_s[None], sre_s[None], sim_s[None])
```
